```python
import jax
import jax.numpy as jnp
from jax import lax
import numpy as np

D_MODEL = 2048
BATCH = 4
SEQ = 2048
DEPTH = 2

CHUNK = 64
D_PL = 256
D_FF = 5632
EPS = 1e-6
N_EVEN = (DEPTH + 1) // 2
N_ODD = DEPTH // 2

GLA_HEADS = 4
GLA_DK = D_MODEL // 4
GLA_DV = D_MODEL // 2
GLA_HEAD_K = GLA_DK // GLA_HEADS
GLA_HEAD_V = GLA_DV // GLA_HEADS
GLA_GATE_RANK = 16
GLA_GATE_TAU = 16.0
CONV_CH = D_MODEL // 2
CONV_WIDTH = 31
AB_SPLITS = [GLA_DK, GLA_DK, GLA_DV, GLA_DV, GLA_GATE_RANK, CONV_CH, CONV_CH]
AB_IN = sum(AB_SPLITS)
AB_OUT = GLA_DV + CONV_CH
ATT_HEADS = 16
ATT_HEAD_DIM = D_MODEL // ATT_HEADS
LEFT_CHUNKS = 8
BAND = (LEFT_CHUNKS + 1) * CHUNK
REL_CLIP = 128

kernel_name = "hybrid_gla_conv_chunkattn_macaron"


def rms_norm(x, g):
    xf = x.astype(jnp.float32)
    y = xf * lax.rsqrt(jnp.mean(xf * xf, axis=-1, keepdims=True) + EPS)
    return (y * g.astype(jnp.float32)).astype(x.dtype)


def layer_norm(x, g, b):
    xf = x.astype(jnp.float32)
    mu = jnp.mean(xf, axis=-1, keepdims=True)
    var = jnp.mean(jnp.square(xf - mu), axis=-1, keepdims=True)
    y = (xf - mu) * lax.rsqrt(var + EPS) * g.astype(jnp.float32) + b.astype(jnp.float32)
    return y.astype(x.dtype)


def swiglu_ffn(h, w_gate, w_up, w_down):
    return (jax.nn.silu(h @ w_gate) * (h @ w_up)) @ w_down


def gla_chunked(q, k, v, log_a):
    B, T, H, dk = q.shape
    dv = v.shape[-1]
    n = T // CHUNK

    def to_chunks(a):
        return a.astype(jnp.float32).reshape(B, n, CHUNK, H, a.shape[-1]).transpose(1, 0, 3, 2, 4)

    qc, kc, vc, gc = to_chunks(q), to_chunks(k), to_chunks(v), to_chunks(log_a)
    causal = jnp.tril(jnp.ones((CHUNK, CHUNK), dtype=bool))

    def step(S, inp):
        qj, kj, vj, gj = inp
        b = jnp.cumsum(gj, axis=2)
        diff = b[:, :, :, None, :] - b[:, :, None, :, :]
        decay = jnp.exp(jnp.where(causal[:, :, None], diff, -jnp.inf))
        A = jnp.einsum('bhtd,bhsd,bhtsd->bhts', qj, kj, decay)
        o = (jnp.einsum('bhtd,bhdv->bhtv', qj * jnp.exp(b), S)
             + jnp.einsum('bhts,bhsv->bhtv', A, vj))
        b_last = b[:, :, -1:, :]
        S = (S * jnp.exp(b_last[:, :, 0, :])[..., None]
             + jnp.einsum('bhsd,bhsv->bhdv', kj * jnp.exp(b_last - b), vj))
        return S, o

    S0 = jnp.zeros((B, H, dk, dv), jnp.float32)
    _, o = lax.scan(step, S0, (qc, kc, vc, gc))
    return o.transpose(1, 0, 3, 2, 4).reshape(B, T, H, dv)


def mixer_gla_conv(h, w_in, gla_gate_w, gla_gate_b, gla_norm_g,
                   conv_dw, conv_dw_b, conv_ln_g, conv_ln_b, w_out):
    B, T, _ = h.shape
    z = h @ w_in
    idx = np.cumsum(AB_SPLITS)[:-1].tolist()
    q, k, v, r, gz, ca, cb = jnp.split(z, idx, axis=-1)

    log_a = jax.nn.log_sigmoid((gz @ gla_gate_w + gla_gate_b).astype(jnp.float32)) / GLA_GATE_TAU
    q = q.reshape(B, T, GLA_HEADS, GLA_HEAD_K) * (GLA_HEAD_K ** -0.5)
    k = k.reshape(B, T, GLA_HEADS, GLA_HEAD_K)
    v = v.reshape(B, T, GLA_HEADS, GLA_HEAD_V)
    log_a = log_a.reshape(B, T, GLA_HEADS, GLA_HEAD_K)
    o = gla_chunked(q, k, v, log_a)
    o = rms_norm(o, gla_norm_g).reshape(B, T, GLA_DV)
    a_out = (o * jax.nn.silu(r.astype(jnp.float32))).astype(h.dtype)

    u = ca * jax.nn.sigmoid(cb)
    rhs = conv_dw.astype(u.dtype).reshape(CONV_WIDTH, 1, CONV_CH)
    u = lax.conv_general_dilated(u, rhs, window_strides=(1,),
                                 padding=[(CONV_WIDTH - 1, 0)],
                                 dimension_numbers=('NWC', 'WIO', 'NWC'),
                                 feature_group_count=CONV_CH)
    u = u + conv_dw_b
    b_out = jax.nn.silu(layer_norm(u, conv_ln_g, conv_ln_b))

    return jnp.concatenate([a_out, b_out], axis=-1) @ w_out


def mixer_chunk_attention(h, w_qkv, rel_bias, w_o):
    B, T, D = h.shape
    n = T // CHUNK
    pad = LEFT_CHUNKS * CHUNK
    q, k, v = jnp.split(h @ w_qkv, 3, axis=-1)
    q = q.reshape(B, n, CHUNK, ATT_HEADS, ATT_HEAD_DIM) * (ATT_HEAD_DIM ** -0.5)
    kp = jnp.pad(k.reshape(B, T, ATT_HEADS, ATT_HEAD_DIM), ((0, 0), (pad, 0), (0, 0), (0, 0)))
    vp = jnp.pad(v.reshape(B, T, ATT_HEADS, ATT_HEAD_DIM), ((0, 0), (pad, 0), (0, 0), (0, 0)))

    t_pos = jnp.arange(CHUNK)
    s_pos = jnp.arange(BAND)
    rel = t_pos[:, None] - s_pos[None, :] + pad
    bias = rel_bias[:, jnp.clip(rel, -REL_CLIP, REL_CLIP) + REL_CLIP].astype(jnp.float32)

    def one_chunk(j):
        qj = lax.dynamic_index_in_dim(q, j, axis=1, keepdims=False)
        kj = lax.dynamic_slice_in_dim(kp, j * CHUNK, BAND, axis=1)
        vj = lax.dynamic_slice_in_dim(vp, j * CHUNK, BAND, axis=1)
        sc = jnp.einsum('bthd,bshd->bhts', qj, kj).astype(jnp.float32) + bias
        valid = (j * CHUNK - pad + s_pos) >= 0
        sc = jnp.where(valid[None, None, None, :], sc, -jnp.inf)
        pr = jax.nn.softmax(sc, axis=-1).astype(vj.dtype)
        return jnp.einsum('bhts,bshd->bthd', pr, vj)

    o = lax.map(one_chunk, jnp.arange(n))
    o = o.transpose(1, 0, 2, 3, 4).reshape(B, T, D)
    return o @ w_o


def setup_inputs(seed: int = 0) -> dict:
    key = jax.random.key(seed)
    ks = jax.random.split(key, 24)
    f32 = jnp.float32

    def w(k, shape, fan_in):
        return jax.random.normal(k, shape, f32) * (fan_in ** -0.5)

    def gain(k, shape):
        return 1.0 + 0.05 * jax.random.normal(k, shape, f32)

    def small(k, shape, scale=0.01):
        return scale * jax.random.normal(k, shape, f32)

    return {
        "x": jax.random.normal(ks[0], (BATCH, SEQ, D_MODEL), f32),
        "p": jax.random.normal(ks[1], (DEPTH, BATCH, SEQ, D_PL), f32),
        "ffn_norm": gain(ks[2], (DEPTH, 2, D_MODEL)),
        "ffn_w_gate": w(ks[3], (DEPTH, 2, D_MODEL, D_FF), D_MODEL),
        "ffn_w_up": w(ks[4], (DEPTH, 2, D_MODEL, D_FF), D_MODEL),
        "ffn_w_down": w(ks[5], (DEPTH, 2, D_FF, D_MODEL), D_FF),
        "mix_norm": gain(ks[6], (DEPTH, D_MODEL)),
        "ab_w_in": w(ks[7], (N_EVEN, D_MODEL, AB_IN), D_MODEL),
        "gla_gate_w": w(ks[8], (N_EVEN, GLA_GATE_RANK, GLA_DK), GLA_GATE_RANK),
        "gla_gate_b": small(ks[9], (N_EVEN, GLA_DK), 0.1),
        "gla_norm_g": gain(ks[10], (N_EVEN, GLA_HEAD_V)),
        "conv_dw": w(ks[11], (N_EVEN, CONV_WIDTH, CONV_CH), CONV_WIDTH),
        "conv_dw_b": small(ks[12], (N_EVEN, CONV_CH)),
        "conv_ln_g": gain(ks[13], (N_EVEN, CONV_CH)),
        "conv_ln_b": small(ks[14], (N_EVEN, CONV_CH)),
        "ab_w_out": w(ks[15], (N_EVEN, AB_OUT, D_MODEL), AB_OUT),
        "att_w_qkv": w(ks[16], (N_ODD, D_MODEL, 3 * D_MODEL), D_MODEL),
        "att_rel_bias": small(ks[17], (N_ODD, ATT_HEADS, 2 * REL_CLIP + 1), 0.1),
        "att_w_o": w(ks[18], (N_ODD, D_MODEL, D_MODEL), D_MODEL),
        "pl_norm": gain(ks[19], (DEPTH, D_MODEL)),
        "pl_w_gate": w(ks[20], (DEPTH, D_MODEL, D_MODEL), D_MODEL),
        "pl_w_proj": w(ks[21], (DEPTH, D_PL, D_MODEL), D_PL),
        "final_norm": gain(ks[22], (D_MODEL,)),
    }


def reference(x, p, ffn_norm, ffn_w_gate, ffn_w_up, ffn_w_down, mix_norm,
              ab_w_in, gla_gate_w, gla_gate_b, gla_norm_g,
              conv_dw, conv_dw_b, conv_ln_g, conv_ln_b, ab_w_out,
              att_w_qkv, att_rel_bias, att_w_o,
              pl_norm, pl_w_gate, pl_w_proj, final_norm):
    for i in range(DEPTH):
        x = x + 0.5 * swiglu_ffn(rms_norm(x, ffn_norm[i, 0]),
                                 ffn_w_gate[i, 0], ffn_w_up[i, 0], ffn_w_down[i, 0])
        h = rms_norm(x, mix_norm[i])
        e = i // 2
        if i % 2 == 0:
            x = x + mixer_gla_conv(h, ab_w_in[e], gla_gate_w[e], gla_gate_b[e], gla_norm_g[e],
                                   conv_dw[e], conv_dw_b[e], conv_ln_g[e], conv_ln_b[e], ab_w_out[e])
        else:
            x = x + mixer_chunk_attention(h, att_w_qkv[e], att_rel_bias[e], att_w_o[e])
        x = x + 0.5 * swiglu_ffn(rms_norm(x, ffn_norm[i, 1]),
                                 ffn_w_gate[i, 1], ffn_w_up[i, 1], ffn_w_down[i, 1])
        gate = jax.nn.sigmoid(rms_norm(x, pl_norm[i]) @ pl_w_gate[i])
        x = x + gate * (p[i] @ pl_w_proj[i])
    return rms_norm(x, final_norm)
```

```python
import functools

import jax
import jax.numpy as jnp
from jax import lax
from jax.experimental import pallas as pl
from jax.experimental.pallas import tpu as pltpu

F32 = jnp.float32
BF16 = jnp.bfloat16

D_MODEL = 2048
BATCH = 4
SEQ = 2048
DEPTH = 2
TOKENS = BATCH * SEQ
D_PL = 256
D_FF = 5632
EPS = 1e-6

GLA_HEADS = 4
GLA_DK = 512
GLA_DV = 1024
GLA_HEAD_K = 128
GLA_HEAD_V = 256
GLA_GATE_RANK = 16
GLA_GATE_TAU = 16.0
GLA_CHUNK = 64
GLA_SUB = 16
CONV_CH = 1024
CONV_WIDTH = 31
AB_MAIN = 2 * GLA_DK + 2 * GLA_DV + 2 * CONV_CH
ATT_HEADS = 16
ATT_HEAD_DIM = 128
ATT_CHUNK = 64
LEFT_CHUNKS = 8
REL_CLIP = 128
NEG_BIG = -1e30

V7X_VMEM_BYTES = 64 * 1024 * 1024
LANES = 128
SUBLANES = 8
VMEM_LIMIT = V7X_VMEM_BYTES - 6 * 1024 * 1024

FFN_TM = 1024
FFN_TF = 512
FFN_TN = 512
MM_TM = 1024
MM_TN = 1024
PL_TM = 256
SEQ_TB = 512
CONV_HALO = 32
CONV_RC = 32
ATT_TQ = 256
RMS_ROWS = 256


def _cparams(sem):
    return pltpu.CompilerParams(dimension_semantics=sem, vmem_limit_bytes=VMEM_LIMIT)


def _rms_rows_to(dst_ref, x_ref, g_ref, rows):
    def body(c, carry):
        sl = pl.ds(pl.multiple_of(c * RMS_ROWS, RMS_ROWS), RMS_ROWS)
        x = x_ref[sl, :]
        ms = jnp.mean(x * x, axis=-1, keepdims=True)
        dst_ref[sl, :] = (x * lax.rsqrt(ms + EPS) * g_ref[...]).astype(dst_ref.dtype)
        return carry
    lax.fori_loop(0, rows // RMS_ROWS, body, 0)


def _ffn_kernel(x_ref, g_ref, wg_ref, wu_ref, wd_ref, o_ref, h_ref):
    f = pl.program_id(1)

    @pl.when(f == 0)
    def _():
        _rms_rows_to(h_ref, x_ref, g_ref, FFN_TM)

    h = h_ref[...]
    gate = jnp.dot(h, wg_ref[...], preferred_element_type=F32)
    up = jnp.dot(h, wu_ref[...], preferred_element_type=F32)
    a = (0.5 * gate * jax.nn.sigmoid(gate) * up).astype(BF16)

    for n in range(D_MODEL // FFN_TN):
        cs = slice(n * FFN_TN, (n + 1) * FFN_TN)
        y = jnp.dot(a, wd_ref[:, cs], preferred_element_type=F32)

        @pl.when(f == 0)
        def _():
            o_ref[:, cs] = x_ref[:, cs] + y

        @pl.when(f > 0)
        def _():
            o_ref[:, cs] += y


def _ffn(x, g, wg, wu, wd):
    m = x.shape[0]
    return pl.pallas_call(
        _ffn_kernel,
        grid=(m // FFN_TM, D_FF // FFN_TF),
        in_specs=[
            pl.BlockSpec((FFN_TM, D_MODEL), lambda i, f: (i, 0)),
            pl.BlockSpec((1, D_MODEL), lambda i, f: (0, 0)),
            pl.BlockSpec((D_MODEL, FFN_TF), lambda i, f: (0, f)),
            pl.BlockSpec((D_MODEL, FFN_TF), lambda i, f: (0, f)),
            pl.BlockSpec((FFN_TF, D_MODEL), lambda i, f: (f, 0)),
        ],
        out_specs=pl.BlockSpec((FFN_TM, D_MODEL), lambda i, f: (i, 0)),
        out_shape=jax.ShapeDtypeStruct((m, D_MODEL), F32),
        scratch_shapes=[pltpu.VMEM((FFN_TM, D_MODEL), BF16)],
        compiler_params=_cparams(("parallel", "arbitrary")),
        name="ffn",
    )(x, g, wg, wu, wd)


def _rms_proj_kernel(x_ref, g_ref, w_ref, cs_ref, o_ref, h_ref):
    @pl.when(pl.program_id(1) == 0)
    def _():
        _rms_rows_to(h_ref, x_ref, g_ref, MM_TM)

    acc = jnp.dot(h_ref[...], w_ref[...], preferred_element_type=F32)
    o_ref[...] = (acc * cs_ref[...]).astype(o_ref.dtype)


def _rms_proj(x, g, w, colscale, out_dtype):
    m, n = x.shape[0], w.shape[1]
    return pl.pallas_call(
        _rms_proj_kernel,
        grid=(m // MM_TM, n // MM_TN),
        in_specs=[
            pl.BlockSpec((MM_TM, D_MODEL), lambda i, j: (i, 0)),
            pl.BlockSpec((1, D_MODEL), lambda i, j: (0, 0)),
            pl.BlockSpec((D_MODEL, MM_TN), lambda i, j: (0, j)),
            pl.BlockSpec((1, MM_TN), lambda i, j: (0, j)),
        ],
        out_specs=pl.BlockSpec((MM_TM, MM_TN), lambda i, j: (i, j)),
        out_shape=jax.ShapeDtypeStruct((m, n), out_dtype),
        scratch_shapes=[pltpu.VMEM((MM_TM, D_MODEL), BF16)],
        compiler_params=_cparams(("parallel", "arbitrary")),
        name="rms_proj",
    )(x, g, w, colscale)


def _out_proj_kernel(n_lhs, *refs):
    lhs_refs = refs[:n_lhs]
    w_refs = refs[n_lhs:2 * n_lhs]
    x_ref, o_ref = refs[2 * n_lhs], refs[2 * n_lhs + 1]
    acc = x_ref[...]
    for a_ref, w_ref in zip(lhs_refs, w_refs):
        acc = acc + jnp.dot(a_ref[...], w_ref[...], preferred_element_type=F32)
    o_ref[...] = acc


def _out_proj(lhs_list, w_list, x):
    m = x.shape[0]
    n_lhs = len(lhs_list)
    in_specs = [pl.BlockSpec((MM_TM, a.shape[1]), lambda i, j: (i, 0)) for a in lhs_list]
    in_specs += [pl.BlockSpec((w.shape[0], MM_TN), lambda i, j: (0, j)) for w in w_list]
    in_specs += [pl.BlockSpec((MM_TM, MM_TN), lambda i, j: (i, j))]
    return pl.pallas_call(
        functools.partial(_out_proj_kernel, n_lhs),
        grid=(m // MM_TM, D_MODEL // MM_TN),
        in_specs=in_specs,
        out_specs=pl.BlockSpec((MM_TM, MM_TN), lambda i, j: (i, j)),
        out_shape=jax.ShapeDtypeStruct((m, D_MODEL), F32),
        compiler_params=_cparams(("parallel", "parallel")),
        name="out_proj",
    )(*lhs_list, *w_list, x)


def _pl_embed_kernel(final, x_ref, p_ref, g_ref, wg_ref, wp_ref, fg_ref, o_ref, h_ref):
    _rms_rows_to(h_ref, x_ref, g_ref, PL_TM)
    gate = jax.nn.sigmoid(jnp.dot(h_ref[...], wg_ref[...], preferred_element_type=F32))
    proj = jnp.dot(p_ref[...].astype(BF16), wp_ref[...], preferred_element_type=F32)
    y = x_ref[...] + gate * proj
    if final:
        ms = jnp.mean(y * y, axis=-1, keepdims=True)
        y = y * lax.rsqrt(ms + EPS) * fg_ref[...]
    o_ref[...] = y


def _pl_embed(x, p, g, wg, wp, final_g, final):
    m = x.shape[0]
    return pl.pallas_call(
        functools.partial(_pl_embed_kernel, final),
        grid=(m // PL_TM,),
        in_specs=[
            pl.BlockSpec((PL_TM, D_MODEL), lambda i: (i, 0)),
            pl.BlockSpec((PL_TM, D_PL), lambda i: (i, 0)),
            pl.BlockSpec((1, D_MODEL), lambda i: (0, 0)),
            pl.BlockSpec((D_MODEL, D_MODEL), lambda i: (0, 0)),
            pl.BlockSpec((D_PL, D_MODEL), lambda i: (0, 0)),
            pl.BlockSpec((1, D_MODEL), lambda i: (0, 0)),
        ],
        out_specs=pl.BlockSpec((PL_TM, D_MODEL), lambda i: (i, 0)),
        out_shape=jax.ShapeDtypeStruct((m, D_MODEL), F32),
        scratch_shapes=[pltpu.VMEM((PL_TM, D_MODEL), BF16)],
        compiler_params=_cparams(("parallel",)),
        name="pl_embed",
    )(x, p, g, wg, wp, final_g)


def _gla_kernel(q_ref, k_ref, v_ref, r_ref, gz_ref, gw_ref, gb_ref, ng_ref, o_ref,
                b_ref, s_ref):
    C, SB = GLA_CHUNK, GLA_SUB
    n_chunks = SEQ_TB // C

    @pl.when(pl.program_id(1) == 0)
    def _():
        s_ref[...] = jnp.zeros_like(s_ref)

    lin = jnp.dot(gz_ref[...].astype(BF16), gw_ref[...], preferred_element_type=F32) + gb_ref[...]
    log_a = -(jnp.maximum(-lin, 0.0) + jnp.log1p(jnp.exp(-jnp.abs(lin)))) / GLA_GATE_TAU
    tri = (lax.broadcasted_iota(jnp.int32, (C, C), 0)
           >= lax.broadcasted_iota(jnp.int32, (C, C), 1)).astype(F32)
    for c in range(n_chunks):
        b_ref[c * C:(c + 1) * C, :] = jnp.dot(
            tri, log_a[c * C:(c + 1) * C, :], preferred_element_type=F32,
            precision=lax.Precision.HIGHEST)

    lane = lax.broadcasted_iota(jnp.int32, (SB, C), 1)
    trow = lax.broadcasted_iota(jnp.int32, (SB, C), 0)
    nt = (((1,), (1,)), ((), ()))
    tn = (((0,), (0,)), ((), ()))

    def chunk_body(c, carry):
        r0 = pl.multiple_of(c * C, C)
        rows = pl.ds(r0, C)
        for h in range(GLA_HEADS):
            hk = slice(h * GLA_HEAD_K, (h + 1) * GLA_HEAD_K)
            hv = slice(h * GLA_HEAD_V, (h + 1) * GLA_HEAD_V)
            q = q_ref[rows, hk] * (GLA_HEAD_K ** -0.5)
            k = k_ref[rows, hk]
            v = v_ref[rows, hv].astype(BF16)
            b = b_ref[rows, hk]
            b_last = b[C - 1:C, :]

            st = s_ref[h]
            inter = lax.dot_general((q * jnp.exp(b)).astype(BF16), st.astype(BF16), nt,
                                    preferred_element_type=F32)

            blocks = []
            for i in range(C // SB):
                s0 = i * SB
                q_i = q[s0:s0 + SB]
                b_i = b[s0:s0 + SB]
                acc = jnp.zeros((SB, C), F32)
                if i > 0:
                    p_i = b[s0 - 1:s0, :]
                    qt = (q_i * jnp.exp(b_i - p_i)).astype(BF16)
                    kt = (k * jnp.exp(jnp.minimum(p_i - b, 0.0))).astype(BF16)
                    off = lax.dot_general(qt, kt, nt, preferred_element_type=F32)
                    acc = jnp.where(lane < s0, off, 0.0)
                for s in range(SB):
                    b_s = b[s0 + s:s0 + s + 1, :]
                    k_s = k[s0 + s:s0 + s + 1, :]
                    e = jnp.exp(jnp.minimum(b_i - b_s, 0.0))
                    col = jnp.sum(q_i * (k_s * e), axis=1, keepdims=True)
                    acc = acc + jnp.where((lane == s0 + s) & (trow >= s), col, 0.0)
                blocks.append(acc)
            a_mat = jnp.concatenate(blocks, axis=0)
            o = inter + jnp.dot(a_mat.astype(BF16), v, preferred_element_type=F32)

            k_dec = (k * jnp.exp(b_last - b)).astype(BF16)
            s_ref[h] = st * jnp.exp(b_last) + lax.dot_general(
                v, k_dec, tn, preferred_element_type=F32)

            on = o * lax.rsqrt(jnp.mean(o * o, axis=-1, keepdims=True) + EPS) * ng_ref[...]
            r = r_ref[rows, hv]
            o_ref[rows, hv] = (on * (r * jax.nn.sigmoid(r))).astype(o_ref.dtype)
        return carry

    lax.fori_loop(0, n_chunks, chunk_body, 0)


def _gla(z, gate_w, gate_b, norm_g):
    nb = SEQ // SEQ_TB
    row = lambda b, i: b * nb + i
    return pl.pallas_call(
        _gla_kernel,
        grid=(BATCH, nb),
        in_specs=[
            pl.BlockSpec((SEQ_TB, GLA_DK), lambda b, i: (row(b, i), 0)),
            pl.BlockSpec((SEQ_TB, GLA_DK), lambda b, i: (row(b, i), 1)),
            pl.BlockSpec((SEQ_TB, GLA_DV), lambda b, i: (row(b, i), 1)),
            pl.BlockSpec((SEQ_TB, GLA_DV), lambda b, i: (row(b, i), 2)),
            pl.BlockSpec((SEQ_TB, LANES), lambda b, i: (row(b, i), AB_MAIN // LANES)),
            pl.BlockSpec((LANES, GLA_DK), lambda b, i: (0, 0)),
            pl.BlockSpec((1, GLA_DK), lambda b, i: (0, 0)),
            pl.BlockSpec((1, GLA_HEAD_V), lambda b, i: (0, 0)),
        ],
        out_specs=pl.BlockSpec((SEQ_TB, GLA_DV), lambda b, i: (row(b, i), 0)),
        out_shape=jax.ShapeDtypeStruct((TOKENS, GLA_DV), BF16),
        scratch_shapes=[pltpu.VMEM((SEQ_TB, GLA_DK), F32),
                        pltpu.VMEM((GLA_HEADS, GLA_HEAD_V, GLA_HEAD_K), F32)],
        compiler_params=_cparams(("parallel", "arbitrary")),
        name="gla",
    )(z, z, z, z, z, gate_w, gate_b, norm_g)


def _conv_kernel(ca_ref, cb_ref, ha_ref, hb_ref, w_ref, wb_ref, lg_ref, lb_ref, o_ref,
                 sh_ref, y_ref):
    H = CONV_HALO
    n_sh = SUBLANES
    keep = (pl.program_id(1) > 0).astype(F32)

    u_main = ca_ref[...] * jax.nn.sigmoid(cb_ref[...])
    u_halo = ha_ref[...] * jax.nn.sigmoid(hb_ref[...]) * keep
    for r in range(n_sh):
        sh_ref[r, 0:H - r, :] = u_halo[r:H, :]
        sh_ref[r, H - r:H - r + SEQ_TB, :] = u_main

    first = H - (CONV_WIDTH - 1)
    ct_w = 512

    def row_body(rc, carry):
        t0 = pl.multiple_of(rc * CONV_RC, CONV_RC)
        for ct in range(CONV_CH // ct_w):
            cs = slice(ct * ct_w, (ct + 1) * ct_w)
            acc = jnp.broadcast_to(wb_ref[:, cs], (CONV_RC, ct_w))
            for j in range(CONV_WIDTH):
                off = first + j
                r, base = off % n_sh, off - off % n_sh
                acc = acc + w_ref[j:j + 1, cs] * sh_ref[r, pl.ds(t0 + base, CONV_RC), cs]
            y_ref[pl.ds(t0, CONV_RC), cs] = acc
        return carry

    lax.fori_loop(0, SEQ_TB // CONV_RC, row_body, 0)

    def ln_body(rc, carry):
        sl = pl.ds(pl.multiple_of(rc * 64, 64), 64)
        y = y_ref[sl, :]
        mu = jnp.mean(y, axis=-1, keepdims=True)
        var = jnp.mean(jnp.square(y - mu), axis=-1, keepdims=True)
        t = (y - mu) * lax.rsqrt(var + EPS) * lg_ref[...] + lb_ref[...]
        o_ref[sl, :] = (t * jax.nn.sigmoid(t)).astype(o_ref.dtype)
        return carry

    lax.fori_loop(0, SEQ_TB // 64, ln_body, 0)


def _conv(z, w, wb, ln_g, ln_b):
    nb = SEQ // SEQ_TB
    hb = SEQ_TB // CONV_HALO
    ca_col = (2 * GLA_DK + 2 * GLA_DV) // CONV_CH
    row = lambda b, i: b * nb + i
    halo = lambda b, i: jnp.maximum(row(b, i) * hb - 1, 0)
    return pl.pallas_call(
        _conv_kernel,
        grid=(BATCH, nb),
        in_specs=[
            pl.BlockSpec((SEQ_TB, CONV_CH), lambda b, i: (row(b, i), ca_col)),
            pl.BlockSpec((SEQ_TB, CONV_CH), lambda b, i: (row(b, i), ca_col + 1)),
            pl.BlockSpec((CONV_HALO, CONV_CH), lambda b, i: (halo(b, i), ca_col)),
            pl.BlockSpec((CONV_HALO, CONV_CH), lambda b, i: (halo(b, i), ca_col + 1)),
            pl.BlockSpec((CONV_WIDTH, CONV_CH), lambda b, i: (0, 0)),
            pl.BlockSpec((1, CONV_CH), lambda b, i: (0, 0)),
            pl.BlockSpec((1, CONV_CH), lambda b, i: (0, 0)),
            pl.BlockSpec((1, CONV_CH), lambda b, i: (0, 0)),
        ],
        out_specs=pl.BlockSpec((SEQ_TB, CONV_CH), lambda b, i: (row(b, i), 0)),
        out_shape=jax.ShapeDtypeStruct((TOKENS, CONV_CH), BF16),
        scratch_shapes=[pltpu.VMEM((SUBLANES, SEQ_TB + CONV_HALO, CONV_CH), F32),
                        pltpu.VMEM((SEQ_TB, CONV_CH), F32)],
        compiler_params=_cparams(("parallel", "parallel")),
        name="conv",
    )(z, z, z, z, w, wb, ln_g, ln_b)


def _attn_kernel(q_ref, k0_ref, k1_ref, k2_ref, v0_ref, v1_ref, v2_ref,
                 bias1_ref, bias2_ref, far_ref, o_ref):
    i = pl.program_id(1)
    pen0 = jnp.where(i >= 2, 0.0, NEG_BIG)
    pen1 = jnp.where(i >= 1, 0.0, NEG_BIG)
    qc = lax.broadcasted_iota(jnp.int32, (ATT_TQ, ATT_TQ), 0) // ATT_CHUNK
    kc = lax.broadcasted_iota(jnp.int32, (ATT_TQ, ATT_TQ), 1) // ATT_CHUNK
    band0 = kc >= qc
    nt = (((1,), (1,)), ((), ()))

    for h in range(ATT_HEADS):
        hs = slice(h * ATT_HEAD_DIM, (h + 1) * ATT_HEAD_DIM)
        q = q_ref[:, hs]
        s0 = lax.dot_general(q, k0_ref[:, hs], nt, preferred_element_type=F32)
        s1 = lax.dot_general(q, k1_ref[:, hs], nt, preferred_element_type=F32)
        s2 = lax.dot_general(q, k2_ref[:, hs], nt, preferred_element_type=F32)
        s0 = jnp.where(band0, s0 + far_ref[h], NEG_BIG) + pen0
        s1 = s1 + bias1_ref[h] + pen1
        s2 = s2 + bias2_ref[h]
        m = jnp.maximum(jnp.maximum(jnp.max(s0, axis=-1, keepdims=True),
                                    jnp.max(s1, axis=-1, keepdims=True)),
                        jnp.max(s2, axis=-1, keepdims=True))
        p0 = jnp.exp(s0 - m)
        p1 = jnp.exp(s1 - m)
        p2 = jnp.exp(s2 - m)
        l = (jnp.sum(p0, axis=-1, keepdims=True) + jnp.sum(p1, axis=-1, keepdims=True)
             + jnp.sum(p2, axis=-1, keepdims=True))
        o = (jnp.dot(p0.astype(BF16), v0_ref[:, hs], preferred_element_type=F32)
             + jnp.dot(p1.astype(BF16), v1_ref[:, hs], preferred_element_type=F32)
             + jnp.dot(p2.astype(BF16), v2_ref[:, hs], preferred_element_type=F32))
        o_ref[:, hs] = (o / l).astype(o_ref.dtype)


def _attention(qkv, bias1, bias2, far):
    nb = SEQ // ATT_TQ
    row = lambda b, i: b * nb + i
    back = lambda d: (lambda b, i: (b * nb + jnp.maximum(i - d, 0)))
    spec = lambda rowfn, col: pl.BlockSpec((ATT_TQ, D_MODEL), lambda b, i: (rowfn(b, i), col))
    table = pl.BlockSpec((ATT_HEADS, ATT_TQ, ATT_TQ), lambda b, i: (0, 0, 0))
    return pl.pallas_call(
        _attn_kernel,
        grid=(BATCH, nb),
        in_specs=[
            spec(row, 0),
            spec(back(2), 1), spec(back(1), 1), spec(row, 1),
            spec(back(2), 2), spec(back(1), 2), spec(row, 2),
            table, table,
            pl.BlockSpec(memory_space=pltpu.SMEM),
        ],
        out_specs=pl.BlockSpec((ATT_TQ, D_MODEL), lambda b, i: (row(b, i), 0)),
        out_shape=jax.ShapeDtypeStruct((TOKENS, D_MODEL), BF16),
        compiler_params=_cparams(("parallel", "parallel")),
        name="attention",
    )(qkv, qkv, qkv, qkv, qkv, qkv, qkv, bias1, bias2, far)


def _attention_bias_tables(rel_bias):
    r = jnp.arange(ATT_TQ)[:, None]
    c = jnp.arange(ATT_TQ)[None, :]

    def table(j):
        rel = (2 - j) * ATT_TQ + r - c
        return rel_bias[:, jnp.clip(rel, -REL_CLIP, REL_CLIP) + REL_CLIP].astype(F32)

    band2 = (c // ATT_CHUNK) <= (r // ATT_CHUNK)
    bias1 = table(1)
    bias2 = jnp.where(band2[None], table(2), NEG_BIG)
    far = rel_bias[:, 2 * REL_CLIP].astype(F32)
    return bias1, bias2, far


def kernel(x, p, ffn_norm, ffn_w_gate, ffn_w_up, ffn_w_down, mix_norm, ab_w_in, gla_gate_w, gla_gate_b, gla_norm_g, conv_dw, conv_dw_b, conv_ln_g, conv_ln_b, ab_w_out, att_w_qkv, att_rel_bias, att_w_o, pl_norm, pl_w_gate, pl_w_proj, final_norm):
    xs = x.reshape(TOKENS, D_MODEL)
    ps = p.reshape(DEPTH, TOKENS, D_PL)
    row = lambda a: a.reshape(1, -1).astype(F32)

    def ffn(xs, i, s):
        return _ffn(xs, row(ffn_norm[i, s]), ffn_w_gate[i, s].astype(BF16),
                    ffn_w_up[i, s].astype(BF16), ffn_w_down[i, s].astype(BF16))

    for i in range(DEPTH):
        e = i // 2
        xs = ffn(xs, i, 0)
        if i % 2 == 0:
            w_in = ab_w_in[e]
            gz_lo = 2 * GLA_DK + 2 * GLA_DV
            gz_hi = gz_lo + GLA_GATE_RANK
            w_main = jnp.concatenate([w_in[:, :gz_lo], w_in[:, gz_hi:]], axis=1).astype(BF16)
            w_gz = jnp.pad(w_in[:, gz_lo:gz_hi], ((0, 0), (0, LANES - GLA_GATE_RANK))).astype(BF16)
            w_all = jnp.concatenate(
                [w_main, jnp.pad(w_gz, ((0, 0), (0, MM_TN - LANES)))], axis=1)
            z_all = _rms_proj(xs, row(mix_norm[i]), w_all,
                              jnp.ones((1, w_all.shape[1]), F32), F32)
            gate_w = jnp.pad(gla_gate_w[e], ((0, LANES - GLA_GATE_RANK), (0, 0))).astype(BF16)
            a_out = _gla(z_all, gate_w, row(gla_gate_b[e]), row(gla_norm_g[e]))
            b_out = _conv(z_all, conv_dw[e].astype(F32), row(conv_dw_b[e]),
                          row(conv_ln_g[e]), row(conv_ln_b[e]))
            w_out = ab_w_out[e].astype(BF16)
            xs = _out_proj([a_out, b_out], [w_out[:GLA_DV], w_out[GLA_DV:]], xs)
        else:
            colscale = jnp.concatenate([jnp.full((1, D_MODEL), ATT_HEAD_DIM ** -0.5, F32),
                                        jnp.ones((1, 2 * D_MODEL), F32)], axis=1)
            qkv = _rms_proj(xs, row(mix_norm[i]), att_w_qkv[e].astype(BF16), colscale, BF16)
            bias1, bias2, far = _attention_bias_tables(att_rel_bias[e])
            o = _attention(qkv, bias1, bias2, far)
            xs = _out_proj([o], [att_w_o[e].astype(BF16)], xs)
        xs = ffn(xs, i, 1)
        xs = _pl_embed(xs, ps[i], row(pl_norm[i]), pl_w_gate[i].astype(BF16),
                       pl_w_proj[i].astype(BF16), row(final_norm), i == DEPTH - 1)
    return xs.reshape(BATCH, SEQ, D_MODEL)
```

```python
import functools

import jax
import jax.numpy as jnp
from jax import lax
from jax.experimental import pallas as pl
from jax.experimental.pallas import tpu as pltpu

F32 = jnp.float32
BF16 = jnp.bfloat16

D_MODEL = 2048
BATCH = 4
SEQ = 2048
DEPTH = 2
TOKENS = BATCH * SEQ
D_PL = 256
D_FF = 5632
EPS = 1e-6

GLA_HEADS = 4
GLA_DK = 512
GLA_DV = 1024
GLA_HEAD_K = 128
GLA_HEAD_V = 256
GLA_GATE_RANK = 16
GLA_GATE_TAU = 16.0
GLA_CHUNK = 64
GLA_SUB = 16
CONV_CH = 1024
CONV_WIDTH = 31
AB_MAIN = 2 * GLA_DK + 2 * GLA_DV + 2 * CONV_CH
ATT_HEADS = 16
ATT_HEAD_DIM = 128
ATT_CHUNK = 64
LEFT_CHUNKS = 8
REL_CLIP = 128
NEG_BIG = -1e30

V7X_VMEM_BYTES = 64 * 1024 * 1024
LANES = 128
SUBLANES = 8
VMEM_LIMIT = V7X_VMEM_BYTES - 6 * 1024 * 1024

FFN_TM = 1024
FFN_TF = 512
FFN_TN = 512
MM_TM = 1024
MM_TN = 1024
PL_TM = 256
SEQ_TB = 512
CONV_HALO = 32
CONV_RC = 32
ATT_TQ = 256
RMS_ROWS = 256


def _cparams(sem):
    return pltpu.CompilerParams(dimension_semantics=sem, vmem_limit_bytes=VMEM_LIMIT)


def _rms_rows_to(dst_ref, x_ref, g_ref, rows):
    def body(c, carry):
        sl = pl.ds(pl.multiple_of(c * RMS_ROWS, RMS_ROWS), RMS_ROWS)
        x = x_ref[sl, :]
        ms = jnp.mean(x * x, axis=-1, keepdims=True)
        dst_ref[sl, :] = (x * lax.rsqrt(ms + EPS) * g_ref[...]).astype(dst_ref.dtype)
        return carry
    lax.fori_loop(0, rows // RMS_ROWS, body, 0)


def _ffn_kernel(x_ref, g_ref, wg_ref, wu_ref, wd_ref, o_ref, h_ref):
    f = pl.program_id(1)

    @pl.when(f == 0)
    def _():
        _rms_rows_to(h_ref, x_ref, g_ref, FFN_TM)
        o_ref[...] = x_ref[...]

    h = h_ref[...]
    gate = jnp.dot(h, wg_ref[...], preferred_element_type=F32)
    up = jnp.dot(h, wu_ref[...], preferred_element_type=F32)
    a = (0.5 * gate * jax.nn.sigmoid(gate) * up).astype(BF16)

    for n in range(D_MODEL // FFN_TN):
        cs = slice(n * FFN_TN, (n + 1) * FFN_TN)
        o_ref[:, cs] += jnp.dot(a, wd_ref[:, cs], preferred_element_type=F32)


def _ffn(x, g, wg, wu, wd, layer, half):
    m = x.shape[0]
    return pl.pallas_call(
        _ffn_kernel,
        grid=(m // FFN_TM, D_FF // FFN_TF),
        in_specs=[
            pl.BlockSpec((FFN_TM, D_MODEL), lambda i, f: (i, 0)),
            pl.BlockSpec((1, D_MODEL), lambda i, f: (0, 0)),
            pl.BlockSpec((None, None, D_MODEL, FFN_TF), lambda i, f: (layer, half, 0, f)),
            pl.BlockSpec((None, None, D_MODEL, FFN_TF), lambda i, f: (layer, half, 0, f)),
            pl.BlockSpec((None, None, FFN_TF, D_MODEL), lambda i, f: (layer, half, f, 0)),
        ],
        out_specs=pl.BlockSpec((FFN_TM, D_MODEL), lambda i, f: (i, 0)),
        out_shape=jax.ShapeDtypeStruct((m, D_MODEL), F32),
        scratch_shapes=[pltpu.VMEM((FFN_TM, D_MODEL), BF16)],
        compiler_params=_cparams(("parallel", "arbitrary")),
        name="ffn",
    )(x, g, wg, wu, wd)


def _rms_proj_kernel(x_ref, g_ref, w_ref, cs_ref, o_ref, h_ref):
    @pl.when(pl.program_id(1) == 0)
    def _():
        _rms_rows_to(h_ref, x_ref, g_ref, MM_TM)

    acc = jnp.dot(h_ref[...], w_ref[...], preferred_element_type=F32)
    o_ref[...] = (acc * cs_ref[...]).astype(o_ref.dtype)


def _rms_proj(x, g, w, colscale, out_dtype):
    m, n = x.shape[0], w.shape[1]
    return pl.pallas_call(
        _rms_proj_kernel,
        grid=(m // MM_TM, n // MM_TN),
        in_specs=[
            pl.BlockSpec((MM_TM, D_MODEL), lambda i, j: (i, 0)),
            pl.BlockSpec((1, D_MODEL), lambda i, j: (0, 0)),
            pl.BlockSpec((D_MODEL, MM_TN), lambda i, j: (0, j)),
            pl.BlockSpec((1, MM_TN), lambda i, j: (0, j)),
        ],
        out_specs=pl.BlockSpec((MM_TM, MM_TN), lambda i, j: (i, j)),
        out_shape=jax.ShapeDtypeStruct((m, n), out_dtype),
        scratch_shapes=[pltpu.VMEM((MM_TM, D_MODEL), BF16)],
        compiler_params=_cparams(("parallel", "arbitrary")),
        name="rms_proj",
    )(x, g, w, colscale)


def _out_proj_kernel(n_lhs, *refs):
    lhs_refs = refs[:n_lhs]
    w_refs = refs[n_lhs:2 * n_lhs]
    x_ref, o_ref = refs[2 * n_lhs], refs[2 * n_lhs + 1]
    acc = x_ref[...]
    for a_ref, w_ref in zip(lhs_refs, w_refs):
        acc = acc + jnp.dot(a_ref[...], w_ref[...], preferred_element_type=F32)
    o_ref[...] = acc


def _out_proj(lhs_list, w_list, x):
    m = x.shape[0]
    n_lhs = len(lhs_list)
    in_specs = [pl.BlockSpec((MM_TM, a.shape[1]), lambda i, j: (i, 0)) for a in lhs_list]
    in_specs += [pl.BlockSpec((w.shape[0], MM_TN), lambda i, j: (0, j)) for w in w_list]
    in_specs += [pl.BlockSpec((MM_TM, MM_TN), lambda i, j: (i, j))]
    return pl.pallas_call(
        functools.partial(_out_proj_kernel, n_lhs),
        grid=(m // MM_TM, D_MODEL // MM_TN),
        in_specs=in_specs,
        out_specs=pl.BlockSpec((MM_TM, MM_TN), lambda i, j: (i, j)),
        out_shape=jax.ShapeDtypeStruct((m, D_MODEL), F32),
        compiler_params=_cparams(("parallel", "parallel")),
        name="out_proj",
    )(*lhs_list, *w_list, x)


def _pl_embed_kernel(final, x_ref, p_ref, g_ref, wg_ref, wp_ref, fg_ref, o_ref, h_ref):
    _rms_rows_to(h_ref, x_ref, g_ref, PL_TM)
    gate = jax.nn.sigmoid(jnp.dot(h_ref[...], wg_ref[...], preferred_element_type=F32))
    proj = jnp.dot(p_ref[...].astype(BF16), wp_ref[...], preferred_element_type=F32)
    y = x_ref[...] + gate * proj
    if final:
        ms = jnp.mean(y * y, axis=-1, keepdims=True)
        y = y * lax.rsqrt(ms + EPS) * fg_ref[...]
    o_ref[...] = y


def _pl_embed(x, p, g, wg, wp, final_g, layer, final):
    m = x.shape[0]
    return pl.pallas_call(
        functools.partial(_pl_embed_kernel, final),
        grid=(m // PL_TM,),
        in_specs=[
            pl.BlockSpec((PL_TM, D_MODEL), lambda i: (i, 0)),
            pl.BlockSpec((None, PL_TM, D_PL), lambda i: (layer, i, 0)),
            pl.BlockSpec((1, D_MODEL), lambda i: (0, 0)),
            pl.BlockSpec((None, D_MODEL, D_MODEL), lambda i: (layer, 0, 0)),
            pl.BlockSpec((None, D_PL, D_MODEL), lambda i: (layer, 0, 0)),
            pl.BlockSpec((1, D_MODEL), lambda i: (0, 0)),
        ],
        out_specs=pl.BlockSpec((PL_TM, D_MODEL), lambda i: (i, 0)),
        out_shape=jax.ShapeDtypeStruct((m, D_MODEL), F32),
        scratch_shapes=[pltpu.VMEM((PL_TM, D_MODEL), BF16)],
        compiler_params=_cparams(("parallel",)),
        name="pl_embed",
    )(x, p, g, wg, wp, final_g)


def _gla_kernel(q_ref, k_ref, v_ref, r_ref, gz_ref, gw_ref, gb_ref, ng_ref, o_ref,
                b_ref, s_ref):
    C, SB = GLA_CHUNK, GLA_SUB
    n_chunks = SEQ_TB // C

    @pl.when(pl.program_id(1) == 0)
    def _():
        s_ref[...] = jnp.zeros_like(s_ref)

    lin = jnp.dot(gz_ref[...].astype(BF16), gw_ref[...], preferred_element_type=F32) + gb_ref[...]
    log_a = -(jnp.maximum(-lin, 0.0) + jnp.log1p(jnp.exp(-jnp.abs(lin)))) / GLA_GATE_TAU
    tri = (lax.broadcasted_iota(jnp.int32, (C, C), 0)
           >= lax.broadcasted_iota(jnp.int32, (C, C), 1)).astype(F32)
    for c in range(n_chunks):
        b_ref[c * C:(c + 1) * C, :] = jnp.dot(
            tri, log_a[c * C:(c + 1) * C, :], preferred_element_type=F32,
            precision=lax.Precision.HIGHEST)

    lane = lax.broadcasted_iota(jnp.int32, (SB, C), 1)
    trow = lax.broadcasted_iota(jnp.int32, (SB, C), 0)
    nt = (((1,), (1,)), ((), ()))
    tn = (((0,), (0,)), ((), ()))

    def chunk_body(c, carry):
        r0 = pl.multiple_of(c * C, C)
        rows = pl.ds(r0, C)
        for h in range(GLA_HEADS):
            hk = slice(h * GLA_HEAD_K, (h + 1) * GLA_HEAD_K)
            hv = slice(h * GLA_HEAD_V, (h + 1) * GLA_HEAD_V)
            q = q_ref[rows, hk] * (GLA_HEAD_K ** -0.5)
            k = k_ref[rows, hk]
            v = v_ref[rows, hv].astype(BF16)
            b = b_ref[rows, hk]
            b_last = b[C - 1:C, :]

            st = s_ref[h]
            inter = lax.dot_general((q * jnp.exp(b)).astype(BF16), st.astype(BF16), nt,
                                    preferred_element_type=F32)

            blocks = []
            for i in range(C // SB):
                s0 = i * SB
                q_i = q[s0:s0 + SB]
                b_i = b[s0:s0 + SB]
                acc = jnp.zeros((SB, C), F32)
                if i > 0:
                    p_i = b[s0 - 1:s0, :]
                    qt = (q_i * jnp.exp(b_i - p_i)).astype(BF16)
                    kt = (k * jnp.exp(jnp.minimum(p_i - b, 0.0))).astype(BF16)
                    off = lax.dot_general(qt, kt, nt, preferred_element_type=F32)
                    acc = jnp.where(lane < s0, off, 0.0)
                for s in range(SB):
                    b_s = b[s0 + s:s0 + s + 1, :]
                    k_s = k[s0 + s:s0 + s + 1, :]
                    e = jnp.exp(jnp.minimum(b_i - b_s, 0.0))
                    col = jnp.sum(q_i * (k_s * e), axis=1, keepdims=True)
                    acc = acc + jnp.where((lane == s0 + s) & (trow >= s), col, 0.0)
                blocks.append(acc)
            a_mat = jnp.concatenate(blocks, axis=0)
            o = inter + jnp.dot(a_mat.astype(BF16), v, preferred_element_type=F32)

            k_dec = (k * jnp.exp(b_last - b)).astype(BF16)
            s_ref[h] = st * jnp.exp(b_last) + lax.dot_general(
                v, k_dec, tn, preferred_element_type=F32)

            on = o * lax.rsqrt(jnp.mean(o * o, axis=-1, keepdims=True) + EPS) * ng_ref[...]
            r = r_ref[rows, hv]
            o_ref[rows, hv] = (on * (r * jax.nn.sigmoid(r))).astype(o_ref.dtype)
        return carry

    lax.fori_loop(0, n_chunks, chunk_body, 0)


def _gla(z, gate_w, gate_b, norm_g):
    nb = SEQ // SEQ_TB
    row = lambda b, i: b * nb + i
    return pl.pallas_call(
        _gla_kernel,
        grid=(BATCH, nb),
        in_specs=[
            pl.BlockSpec((SEQ_TB, GLA_DK), lambda b, i: (row(b, i), 0)),
            pl.BlockSpec((SEQ_TB, GLA_DK), lambda b, i: (row(b, i), 1)),
            pl.BlockSpec((SEQ_TB, GLA_DV), lambda b, i: (row(b, i), 1)),
            pl.BlockSpec((SEQ_TB, GLA_DV), lambda b, i: (row(b, i), 2)),
            pl.BlockSpec((SEQ_TB, LANES), lambda b, i: (row(b, i), AB_MAIN // LANES)),
            pl.BlockSpec((LANES, GLA_DK), lambda b, i: (0, 0)),
            pl.BlockSpec((1, GLA_DK), lambda b, i: (0, 0)),
            pl.BlockSpec((1, GLA_HEAD_V), lambda b, i: (0, 0)),
        ],
        out_specs=pl.BlockSpec((SEQ_TB, GLA_DV), lambda b, i: (row(b, i), 0)),
        out_shape=jax.ShapeDtypeStruct((TOKENS, GLA_DV), BF16),
        scratch_shapes=[pltpu.VMEM((SEQ_TB, GLA_DK), F32),
                        pltpu.VMEM((GLA_HEADS, GLA_HEAD_V, GLA_HEAD_K), F32)],
        compiler_params=_cparams(("parallel", "arbitrary")),
        name="gla",
    )(z, z, z, z, z, gate_w, gate_b, norm_g)


def _conv_kernel(ca_ref, cb_ref, ha_ref, hb_ref, w_ref, wb_ref, lg_ref, lb_ref, o_ref,
                 sh_ref, y_ref):
    H = CONV_HALO
    n_sh = SUBLANES
    keep = (pl.program_id(1) > 0).astype(F32)

    u_main = ca_ref[...] * jax.nn.sigmoid(cb_ref[...])
    u_halo = ha_ref[...] * jax.nn.sigmoid(hb_ref[...]) * keep
    for r in range(n_sh):
        sh_ref[r, 0:H - r, :] = u_halo[r:H, :]
        sh_ref[r, H - r:H - r + SEQ_TB, :] = u_main

    first = H - (CONV_WIDTH - 1)
    ct_w = 512

    def row_body(rc, carry):
        t0 = pl.multiple_of(rc * CONV_RC, CONV_RC)
        for ct in range(CONV_CH // ct_w):
            cs = slice(ct * ct_w, (ct + 1) * ct_w)
            acc = jnp.broadcast_to(wb_ref[:, cs], (CONV_RC, ct_w))
            for j in range(CONV_WIDTH):
                off = first + j
                r, base = off % n_sh, off - off % n_sh
                acc = acc + w_ref[j:j + 1, cs] * sh_ref[r, pl.ds(t0 + base, CONV_RC), cs]
            y_ref[pl.ds(t0, CONV_RC), cs] = acc
        return carry

    lax.fori_loop(0, SEQ_TB // CONV_RC, row_body, 0)

    def ln_body(rc, carry):
        sl = pl.ds(pl.multiple_of(rc * 64, 64), 64)
        y = y_ref[sl, :]
        mu = jnp.mean(y, axis=-1, keepdims=True)
        var = jnp.mean(jnp.square(y - mu), axis=-1, keepdims=True)
        t = (y - mu) * lax.rsqrt(var + EPS) * lg_ref[...] + lb_ref[...]
        o_ref[sl, :] = (t * jax.nn.sigmoid(t)).astype(o_ref.dtype)
        return carry

    lax.fori_loop(0, SEQ_TB // 64, ln_body, 0)


def _conv(z, w, wb, ln_g, ln_b):
    nb = SEQ // SEQ_TB
    hb = SEQ_TB // CONV_HALO
    ca_col = (2 * GLA_DK + 2 * GLA_DV) // CONV_CH
    row = lambda b, i: b * nb + i
    halo = lambda b, i: jnp.maximum(row(b, i) * hb - 1, 0)
    return pl.pallas_call(
        _conv_kernel,
        grid=(BATCH, nb),
        in_specs=[
            pl.BlockSpec((SEQ_TB, CONV_CH), lambda b, i: (row(b, i), ca_col)),
            pl.BlockSpec((SEQ_TB, CONV_CH), lambda b, i: (row(b, i), ca_col + 1)),
            pl.BlockSpec((CONV_HALO, CONV_CH), lambda b, i: (halo(b, i), ca_col)),
            pl.BlockSpec((CONV_HALO, CONV_CH), lambda b, i: (halo(b, i), ca_col + 1)),
            pl.BlockSpec((CONV_WIDTH, CONV_CH), lambda b, i: (0, 0)),
            pl.BlockSpec((1, CONV_CH), lambda b, i: (0, 0)),
            pl.BlockSpec((1, CONV_CH), lambda b, i: (0, 0)),
            pl.BlockSpec((1, CONV_CH), lambda b, i: (0, 0)),
        ],
        out_specs=pl.BlockSpec((SEQ_TB, CONV_CH), lambda b, i: (row(b, i), 0)),
        out_shape=jax.ShapeDtypeStruct((TOKENS, CONV_CH), BF16),
        scratch_shapes=[pltpu.VMEM((SUBLANES, SEQ_TB + CONV_HALO, CONV_CH), F32),
                        pltpu.VMEM((SEQ_TB, CONV_CH), F32)],
        compiler_params=_cparams(("parallel", "parallel")),
        name="conv",
    )(z, z, z, z, w, wb, ln_g, ln_b)


def _attn_kernel(q_ref, k0_ref, k1_ref, k2_ref, v0_ref, v1_ref, v2_ref,
                 bias1_ref, bias2_ref, far_ref, o_ref):
    i = pl.program_id(1)
    pen0 = jnp.where(i >= 2, 0.0, NEG_BIG)
    pen1 = jnp.where(i >= 1, 0.0, NEG_BIG)
    qc = lax.broadcasted_iota(jnp.int32, (ATT_TQ, ATT_TQ), 0) // ATT_CHUNK
    kc = lax.broadcasted_iota(jnp.int32, (ATT_TQ, ATT_TQ), 1) // ATT_CHUNK
    band0 = kc >= qc
    nt = (((1,), (1,)), ((), ()))

    for h in range(ATT_HEADS):
        hs = slice(h * ATT_HEAD_DIM, (h + 1) * ATT_HEAD_DIM)
        q = q_ref[:, hs]
        s0 = lax.dot_general(q, k0_ref[:, hs], nt, preferred_element_type=F32)
        s1 = lax.dot_general(q, k1_ref[:, hs], nt, preferred_element_type=F32)
        s2 = lax.dot_general(q, k2_ref[:, hs], nt, preferred_element_type=F32)
        s0 = jnp.where(band0, s0 + far_ref[h], NEG_BIG) + pen0
        s1 = s1 + bias1_ref[h] + pen1
        s2 = s2 + bias2_ref[h]
        m = jnp.maximum(jnp.maximum(jnp.max(s0, axis=-1, keepdims=True),
                                    jnp.max(s1, axis=-1, keepdims=True)),
                        jnp.max(s2, axis=-1, keepdims=True))
        p0 = jnp.exp(s0 - m)
        p1 = jnp.exp(s1 - m)
        p2 = jnp.exp(s2 - m)
        l = (jnp.sum(p0, axis=-1, keepdims=True) + jnp.sum(p1, axis=-1, keepdims=True)
             + jnp.sum(p2, axis=-1, keepdims=True))
        o = (jnp.dot(p0.astype(BF16), v0_ref[:, hs], preferred_element_type=F32)
             + jnp.dot(p1.astype(BF16), v1_ref[:, hs], preferred_element_type=F32)
             + jnp.dot(p2.astype(BF16), v2_ref[:, hs], preferred_element_type=F32))
        o_ref[:, hs] = (o / l).astype(o_ref.dtype)


def _attention(qkv, bias1, bias2, far):
    nb = SEQ // ATT_TQ
    row = lambda b, i: b * nb + i
    back = lambda d: (lambda b, i: (b * nb + jnp.maximum(i - d, 0)))
    spec = lambda rowfn, col: pl.BlockSpec((ATT_TQ, D_MODEL), lambda b, i: (rowfn(b, i), col))
    table = pl.BlockSpec((ATT_HEADS, ATT_TQ, ATT_TQ), lambda b, i: (0, 0, 0))
    return pl.pallas_call(
        _attn_kernel,
        grid=(BATCH, nb),
        in_specs=[
            spec(row, 0),
            spec(back(2), 1), spec(back(1), 1), spec(row, 1),
            spec(back(2), 2), spec(back(1), 2), spec(row, 2),
            table, table,
            pl.BlockSpec(memory_space=pltpu.SMEM),
        ],
        out_specs=pl.BlockSpec((ATT_TQ, D_MODEL), lambda b, i: (row(b, i), 0)),
        out_shape=jax.ShapeDtypeStruct((TOKENS, D_MODEL), BF16),
        compiler_params=_cparams(("parallel", "parallel")),
        name="attention",
    )(qkv, qkv, qkv, qkv, qkv, qkv, qkv, bias1, bias2, far)


def _attention_bias_tables(rel_bias):
    def table(j):
        base = (2 - j) * ATT_TQ
        first = min(base - (ATT_TQ - 1), -REL_CLIP)
        last = max(base + (ATT_TQ - 1), REL_CLIP)
        ext = jnp.pad(rel_bias.astype(F32),
                      ((0, 0), (-REL_CLIP - first, last - REL_CLIP)), mode="edge")
        rev = ext[:, ::-1]
        rows = [rev[:, last - base - r:last - base - r + ATT_TQ] for r in range(ATT_TQ)]
        return jnp.stack(rows, axis=1)

    r = jnp.arange(ATT_TQ)[:, None]
    c = jnp.arange(ATT_TQ)[None, :]
    band2 =(c // ATT_CHUNK) <= (r // ATT_CHUNK)
    bias1 = table(1)
    bias2 = jnp.where(band2[None], table(2), NEG_BIG)
    far = rel_bias[:, 2 * REL_CLIP].astype(F32)
    return bias1, bias2, far


def kernel(x, p, ffn_norm, ffn_w_gate, ffn_w_up, ffn_w_down, mix_norm, ab_w_in, gla_gate_w, gla_gate_b, gla_norm_g, conv_dw, conv_dw_b, conv_ln_g, conv_ln_b, ab_w_out, att_w_qkv, att_rel_bias, att_w_o, pl_norm, pl_w_gate, pl_w_proj, final_norm):
    xs = x.reshape(TOKENS, D_MODEL)
    ps = p.reshape(DEPTH, TOKENS, D_PL)
    row = lambda a: a.reshape(1, -1).astype(F32)

    wg_all = ffn_w_gate.astype(BF16)
    wu_all = ffn_w_up.astype(BF16)
    wd_all = ffn_w_down.astype(BF16)
    pl_wg_all = pl_w_gate.astype(BF16)
    pl_wp_all = pl_w_proj.astype(BF16)

    def ffn(xs, i, s):
        return _ffn(xs, row(ffn_norm[i, s]), wg_all, wu_all, wd_all, i, s)

    for i in range(DEPTH):
        e = i // 2
        xs = ffn(xs, i, 0)
        if i % 2 == 0:
            w_in = ab_w_in[e]
            gz_lo = 2 * GLA_DK + 2 * GLA_DV
            gz_hi = gz_lo + GLA_GATE_RANK
            w_main = jnp.concatenate([w_in[:, :gz_lo], w_in[:, gz_hi:]], axis=1).astype(BF16)
            w_gz = jnp.pad(w_in[:, gz_lo:gz_hi], ((0, 0), (0, LANES - GLA_GATE_RANK))).astype(BF16)
            w_all = jnp.concatenate(
                [w_main, jnp.pad(w_gz, ((0, 0), (0, MM_TN - LANES)))], axis=1)
            z_all = _rms_proj(xs, row(mix_norm[i]), w_all,
                              jnp.ones((1, w_all.shape[1]), F32), F32)
            gate_w = jnp.pad(gla_gate_w[e], ((0, LANES - GLA_GATE_RANK), (0, 0))).astype(BF16)
            a_out = _gla(z_all, gate_w, row(gla_gate_b[e]), row(gla_norm_g[e]))
            b_out = _conv(z_all, conv_dw[e].astype(F32), row(conv_dw_b[e]),
                          row(conv_ln_g[e]), row(conv_ln_b[e]))
            w_out = ab_w_out[e].astype(BF16)
            xs = _out_proj([a_out, b_out], [w_out[:GLA_DV], w_out[GLA_DV:]], xs)
        else:
            colscale = jnp.concatenate([jnp.full((1, D_MODEL), ATT_HEAD_DIM ** -0.5, F32),
                                        jnp.ones((1, 2 * D_MODEL), F32)], axis=1)
            qkv = _rms_proj(xs, row(mix_norm[i]), att_w_qkv[e].astype(BF16), colscale, BF16)
            bias1, bias2, far = _attention_bias_tables(att_rel_bias[e])
            o = _attention(qkv, bias1, bias2, far)
            xs = _out_proj([o], [att_w_o[e].astype(BF16)], xs)
        xs = ffn(xs, i, 1)
        xs = _pl_embed(xs, ps, row(pl_norm[i]), pl_wg_all, pl_wp_all, row(final_norm),
                       i, i == DEPTH - 1)
    return xs.reshape(BATCH, SEQ, D_MODEL)
```

```python
import functools

import jax
import jax.numpy as jnp
from jax import lax
from jax.experimental import pallas as pl
from jax.experimental.pallas import tpu as pltpu

F32 = jnp.float32
BF16 = jnp.bfloat16

D_MODEL = 2048
BATCH = 4
SEQ = 2048
DEPTH = 2
TOKENS = BATCH * SEQ
D_PL = 256
D_FF = 5632
EPS = 1e-6

GLA_HEADS = 4
GLA_DK = 512
GLA_DV = 1024
GLA_HEAD_K = 128
GLA_HEAD_V = 256
GLA_GATE_RANK = 16
GLA_GATE_TAU = 16.0
GLA_CHUNK = 64
GLA_SUB = 16
CONV_CH = 1024
CONV_WIDTH = 31
AB_MAIN = 2 * GLA_DK + 2 * GLA_DV + 2 * CONV_CH
ATT_HEADS = 16
ATT_HEAD_DIM = 128
ATT_CHUNK = 64
LEFT_CHUNKS = 8
REL_CLIP = 128
NEG_BIG = -1e30

V7X_VMEM_BYTES = 64 * 1024 * 1024
LANES = 128
SUBLANES = 8
VMEM_LIMIT = V7X_VMEM_BYTES - 6 * 1024 * 1024

FFN_TM = 1024
FFN_TF = 256
FFN_TN = 512
MM_TM = 1024
MM_TN = 1024
PL_TM = 256
SEQ_TB = 512
CONV_HALO = 32
CONV_RC = 32
ATT_TQ = 256
RMS_ROWS = 256


def _cparams(sem):
    return pltpu.CompilerParams(dimension_semantics=sem, vmem_limit_bytes=VMEM_LIMIT)


def _rms_rows_to(dst_ref, x_ref, g_ref, rows):
    def body(c, carry):
        sl = pl.ds(pl.multiple_of(c * RMS_ROWS, RMS_ROWS), RMS_ROWS)
        x = x_ref[sl, :]
        ms = jnp.mean(x * x, axis=-1, keepdims=True)
        dst_ref[sl, :] = (x * lax.rsqrt(ms + EPS) * g_ref[...]).astype(dst_ref.dtype)
        return carry
    lax.fori_loop(0, rows // RMS_ROWS, body, 0)


def _ffn_kernel(x_ref, g_ref, wg_ref, wu_ref, wd_ref, o_ref, h_ref):
    f = pl.program_id(1)

    @pl.when(f == 0)
    def _():
        _rms_rows_to(h_ref, x_ref, g_ref, FFN_TM)
        o_ref[...] = x_ref[...]

    h = h_ref[...]
    gate = jnp.dot(h, wg_ref[...].astype(BF16), preferred_element_type=F32)
    up = jnp.dot(h, wu_ref[...].astype(BF16), preferred_element_type=F32)
    a = (0.5 * gate * jax.nn.sigmoid(gate) * up).astype(BF16)

    for n in range(D_MODEL // FFN_TN):
        cs = slice(n * FFN_TN, (n + 1) * FFN_TN)
        o_ref[:, cs] += jnp.dot(a, wd_ref[:, cs].astype(BF16), preferred_element_type=F32)


def _ffn(x, g, wg, wu, wd, layer, half):
    m = x.shape[0]
    return pl.pallas_call(
        _ffn_kernel,
        grid=(m // FFN_TM, D_FF // FFN_TF),
        in_specs=[
            pl.BlockSpec((FFN_TM, D_MODEL), lambda i, f: (i, 0)),
            pl.BlockSpec((1, D_MODEL), lambda i, f: (0, 0)),
            pl.BlockSpec((None, None, D_MODEL, FFN_TF), lambda i, f: (layer, half, 0, f)),
            pl.BlockSpec((None, None, D_MODEL, FFN_TF), lambda i, f: (layer, half, 0, f)),
            pl.BlockSpec((None, None, FFN_TF, D_MODEL), lambda i, f: (layer, half, f, 0)),
        ],
        out_specs=pl.BlockSpec((FFN_TM, D_MODEL), lambda i, f: (i, 0)),
        out_shape=jax.ShapeDtypeStruct((m, D_MODEL), F32),
        scratch_shapes=[pltpu.VMEM((FFN_TM, D_MODEL), BF16)],
        compiler_params=_cparams(("parallel", "arbitrary")),
        name="ffn",
    )(x, g, wg, wu, wd)


def _rms_proj_kernel(has_extra, *refs):
    if has_extra:
        x_ref, g_ref, w_ref, cs_ref, we_ref, o_ref, oe_ref, h_ref = refs
    else:
        x_ref, g_ref, w_ref, cs_ref, o_ref, h_ref = refs

    @pl.when(pl.program_id(1) == 0)
    def _():
        _rms_rows_to(h_ref, x_ref, g_ref, MM_TM)
        if has_extra:
            oe_ref[...] = jnp.dot(h_ref[...], we_ref[...].astype(BF16),
                                  preferred_element_type=F32)

    acc = jnp.dot(h_ref[...], w_ref[...].astype(BF16), preferred_element_type=F32)
    o_ref[...] = (acc * cs_ref[...]).astype(o_ref.dtype)


def _rms_proj(x, g, w, colscale, out_dtype, w_extra=None):
    m, n = x.shape[0], w.shape[1]
    has_extra = w_extra is not None
    in_specs = [
        pl.BlockSpec((MM_TM, D_MODEL), lambda i, j: (i, 0)),
        pl.BlockSpec((1, D_MODEL), lambda i, j: (0, 0)),
        pl.BlockSpec((D_MODEL, MM_TN), lambda i, j: (0, j)),
        pl.BlockSpec((1, MM_TN), lambda i, j: (0, j)),
    ]
    out_specs = pl.BlockSpec((MM_TM, MM_TN), lambda i, j: (i, j))
    out_shape = jax.ShapeDtypeStruct((m, n), out_dtype)
    args = [x, g, w, colscale]
    if has_extra:
        in_specs.append(pl.BlockSpec((D_MODEL, LANES), lambda i, j: (0, 0)))
        out_specs = [out_specs, pl.BlockSpec((MM_TM, LANES), lambda i, j: (i, 0))]
        out_shape = [out_shape, jax.ShapeDtypeStruct((m, LANES), F32)]
        args.append(w_extra)
    return pl.pallas_call(
        functools.partial(_rms_proj_kernel, has_extra),
        grid=(m // MM_TM, n // MM_TN),
        in_specs=in_specs,
        out_specs=out_specs,
        out_shape=out_shape,
        scratch_shapes=[pltpu.VMEM((MM_TM, D_MODEL), BF16)],
        compiler_params=_cparams(("parallel", "arbitrary")),
        name="rms_proj",
    )(*args)


def _out_proj_kernel(n_lhs, *refs):
    lhs_refs = refs[:n_lhs]
    w_refs = refs[n_lhs:2 * n_lhs]
    x_ref, o_ref = refs[2 * n_lhs], refs[2 * n_lhs + 1]
    acc = x_ref[...]
    for a_ref, w_ref in zip(lhs_refs, w_refs):
        acc = acc + jnp.dot(a_ref[...], w_ref[...].astype(BF16), preferred_element_type=F32)
    o_ref[...] = acc


def _out_proj(lhs_list, w, x):
    m = x.shape[0]
    n_lhs = len(lhs_list)
    kw = lhs_list[0].shape[1]
    in_specs = [pl.BlockSpec((MM_TM, kw), lambda i, j: (i, 0)) for _ in lhs_list]
    in_specs += [pl.BlockSpec((kw, MM_TN), lambda i, j, t=t: (t, j)) for t in range(n_lhs)]
    in_specs += [pl.BlockSpec((MM_TM, MM_TN), lambda i, j: (i, j))]
    return pl.pallas_call(
        functools.partial(_out_proj_kernel, n_lhs),
        grid=(m // MM_TM, D_MODEL // MM_TN),
        in_specs=in_specs,
        out_specs=pl.BlockSpec((MM_TM, MM_TN), lambda i, j: (i, j)),
        out_shape=jax.ShapeDtypeStruct((m, D_MODEL), F32),
        compiler_params=_cparams(("parallel", "parallel")),
        name="out_proj",
    )(*lhs_list, *([w] * n_lhs), x)


def _pl_embed_kernel(final, x_ref, p_ref, g_ref, wg_ref, wp_ref, fg_ref, o_ref, h_ref):
    _rms_rows_to(h_ref, x_ref, g_ref, PL_TM)
    gate = jax.nn.sigmoid(jnp.dot(h_ref[...], wg_ref[...], preferred_element_type=F32))
    proj = jnp.dot(p_ref[...].astype(BF16), wp_ref[...], preferred_element_type=F32)
    y = x_ref[...] + gate * proj
    if final:
        ms = jnp.mean(y * y, axis=-1, keepdims=True)
        y = y * lax.rsqrt(ms + EPS) * fg_ref[...]
    o_ref[...] = y


def _pl_embed(x, p, g, wg, wp, final_g, layer, final):
    m = x.shape[0]
    return pl.pallas_call(
        functools.partial(_pl_embed_kernel, final),
        grid=(m // PL_TM,),
        in_specs=[
            pl.BlockSpec((PL_TM, D_MODEL), lambda i: (i, 0)),
            pl.BlockSpec((None, PL_TM, D_PL), lambda i: (layer, i, 0)),
            pl.BlockSpec((1, D_MODEL), lambda i: (0, 0)),
            pl.BlockSpec((None, D_MODEL, D_MODEL), lambda i: (layer, 0, 0)),
            pl.BlockSpec((None, D_PL, D_MODEL), lambda i: (layer, 0, 0)),
            pl.BlockSpec((1, D_MODEL), lambda i: (0, 0)),
        ],
        out_specs=pl.BlockSpec((PL_TM, D_MODEL), lambda i: (i, 0)),
        out_shape=jax.ShapeDtypeStruct((m, D_MODEL), F32),
        scratch_shapes=[pltpu.VMEM((PL_TM, D_MODEL), BF16)],
        compiler_params=_cparams(("parallel",)),
        name="pl_embed",
    )(x, p, g, wg, wp, final_g)


def _gla_kernel(q_ref, k_ref, v_ref, r_ref, gz_ref, gw_ref, gb_ref, ng_ref, o_ref,
                b_ref, s_ref):
    C, SB = GLA_CHUNK, GLA_SUB
    n_chunks = SEQ_TB // C

    @pl.when(pl.program_id(1) == 0)
    def _():
        s_ref[...] = jnp.zeros_like(s_ref)

    lin = jnp.dot(gz_ref[...].astype(BF16), gw_ref[...], preferred_element_type=F32) + gb_ref[...]
    log_a = -(jnp.maximum(-lin, 0.0) + jnp.log1p(jnp.exp(-jnp.abs(lin)))) / GLA_GATE_TAU
    tri = (lax.broadcasted_iota(jnp.int32, (C, C), 0)
           >= lax.broadcasted_iota(jnp.int32, (C, C), 1)).astype(F32)
    for c in range(n_chunks):
        b_ref[c * C:(c + 1) * C, :] = jnp.dot(
            tri, log_a[c * C:(c + 1) * C, :], preferred_element_type=F32,
            precision=lax.Precision.HIGHEST)

    lane = lax.broadcasted_iota(jnp.int32, (SB, C), 1)
    trow = lax.broadcasted_iota(jnp.int32, (SB, C), 0)
    nt = (((1,), (1,)), ((), ()))
    tn = (((0,), (0,)), ((), ()))

    def chunk_body(c, carry):
        r0 = pl.multiple_of(c * C, C)
        rows = pl.ds(r0, C)
        for h in range(GLA_HEADS):
            hk = slice(h * GLA_HEAD_K, (h + 1) * GLA_HEAD_K)
            hv = slice(h * GLA_HEAD_V, (h + 1) * GLA_HEAD_V)
            q = q_ref[rows, hk] * (GLA_HEAD_K ** -0.5)
            k = k_ref[rows, hk]
            v = v_ref[rows, hv].astype(BF16)
            b = b_ref[rows, hk]
            b_last = b[C - 1:C, :]

            st = s_ref[h]
            inter = lax.dot_general((q * jnp.exp(b)).astype(BF16), st.astype(BF16), nt,
                                    preferred_element_type=F32)

            blocks = []
            for i in range(C // SB):
                s0 = i * SB
                q_i = q[s0:s0 + SB]
                b_i = b[s0:s0 + SB]
                acc = jnp.zeros((SB, C), F32)
                if i > 0:
                    p_i = b[s0 - 1:s0, :]
                    qt = (q_i * jnp.exp(b_i - p_i)).astype(BF16)
                    kt = (k * jnp.exp(jnp.minimum(p_i - b, 0.0))).astype(BF16)
                    off = lax.dot_general(qt, kt, nt, preferred_element_type=F32)
                    acc = jnp.where(lane < s0, off, 0.0)
                for s in range(SB):
                    b_s = b[s0 + s:s0 + s + 1, :]
                    k_s = k[s0 + s:s0 + s + 1, :]
                    e = jnp.exp(jnp.minimum(b_i - b_s, 0.0))
                    col = jnp.sum(q_i * (k_s * e), axis=1, keepdims=True)
                    acc = acc + jnp.where((lane == s0 + s) & (trow >= s), col, 0.0)
                blocks.append(acc)
            a_mat = jnp.concatenate(blocks, axis=0)
            o = inter + jnp.dot(a_mat.astype(BF16), v, preferred_element_type=F32)

            k_dec = (k * jnp.exp(b_last - b)).astype(BF16)
            s_ref[h] = st * jnp.exp(b_last) + lax.dot_general(
                v, k_dec, tn, preferred_element_type=F32)

            on = o * lax.rsqrt(jnp.mean(o * o, axis=-1, keepdims=True) + EPS) * ng_ref[...]
            r = r_ref[rows, hv]
            o_ref[rows, hv] = (on * (r * jax.nn.sigmoid(r))).astype(o_ref.dtype)
        return carry

    lax.fori_loop(0, n_chunks, chunk_body, 0)


def _gla(z, gz, gate_w, gate_b, norm_g):
    nb = SEQ // SEQ_TB
    row = lambda b, i: b * nb + i
    return pl.pallas_call(
        _gla_kernel,
        grid=(BATCH, nb),
        in_specs=[
            pl.BlockSpec((SEQ_TB, GLA_DK), lambda b, i: (row(b, i), 0)),
            pl.BlockSpec((SEQ_TB, GLA_DK), lambda b, i: (row(b, i), 1)),
            pl.BlockSpec((SEQ_TB, GLA_DV), lambda b, i: (row(b, i), 1)),
            pl.BlockSpec((SEQ_TB, GLA_DV), lambda b, i: (row(b, i), 2)),
            pl.BlockSpec((SEQ_TB, LANES), lambda b, i: (row(b, i), 0)),
            pl.BlockSpec((LANES, GLA_DK), lambda b, i: (0, 0)),
            pl.BlockSpec((1, GLA_DK), lambda b, i: (0, 0)),
            pl.BlockSpec((1, GLA_HEAD_V), lambda b, i: (0, 0)),
        ],
        out_specs=pl.BlockSpec((SEQ_TB, GLA_DV), lambda b, i: (row(b, i), 0)),
        out_shape=jax.ShapeDtypeStruct((TOKENS, GLA_DV), BF16),
        scratch_shapes=[pltpu.VMEM((SEQ_TB, GLA_DK), F32),
                        pltpu.VMEM((GLA_HEADS, GLA_HEAD_V, GLA_HEAD_K), F32)],
        compiler_params=_cparams(("parallel", "arbitrary")),
        name="gla",
    )(z, z, z, z, gz, gate_w, gate_b, norm_g)


def _conv_kernel(ca_ref, cb_ref, ha_ref, hb_ref, w_ref, wb_ref, lg_ref, lb_ref, o_ref,
                 sh_ref, y_ref):
    H = CONV_HALO
    n_sh = SUBLANES
    keep = (pl.program_id(1) > 0).astype(F32)

    u_main = ca_ref[...] * jax.nn.sigmoid(cb_ref[...])
    u_halo = ha_ref[...] * jax.nn.sigmoid(hb_ref[...]) * keep
    for r in range(n_sh):
        sh_ref[r, 0:H - r, :] = u_halo[r:H, :]
        sh_ref[r, H - r:H - r + SEQ_TB, :] = u_main

    first = H - (CONV_WIDTH - 1)
    ct_w = 512

    def row_body(rc, carry):
        t0 = pl.multiple_of(rc * CONV_RC, CONV_RC)
        for ct in range(CONV_CH // ct_w):
            cs = slice(ct * ct_w, (ct + 1) * ct_w)
            acc = jnp.broadcast_to(wb_ref[:, cs], (CONV_RC, ct_w))
            for j in range(CONV_WIDTH):
                off = first + j
                r, base = off % n_sh, off - off % n_sh
                acc = acc + w_ref[j:j + 1, cs] * sh_ref[r, pl.ds(t0 + base, CONV_RC), cs]
            y_ref[pl.ds(t0, CONV_RC), cs] = acc
        return carry

    lax.fori_loop(0, SEQ_TB // CONV_RC, row_body, 0)

    def ln_body(rc, carry):
        sl = pl.ds(pl.multiple_of(rc * 64, 64), 64)
        y = y_ref[sl, :]
        mu = jnp.mean(y, axis=-1, keepdims=True)
        var = jnp.mean(jnp.square(y - mu), axis=-1, keepdims=True)
        t = (y - mu) * lax.rsqrt(var + EPS) * lg_ref[...] + lb_ref[...]
        o_ref[sl, :] = (t * jax.nn.sigmoid(t)).astype(o_ref.dtype)
        return carry

    lax.fori_loop(0, SEQ_TB // 64, ln_body, 0)


def _conv(z, w, wb, ln_g, ln_b):
    nb = SEQ // SEQ_TB
    hb = SEQ_TB // CONV_HALO
    ca_col = (2 * GLA_DK + 2 * GLA_DV) // CONV_CH
    row = lambda b, i: b * nb + i
    halo = lambda b, i: jnp.maximum(row(b, i) * hb - 1, 0)
    return pl.pallas_call(
        _conv_kernel,
        grid=(BATCH, nb),
        in_specs=[
            pl.BlockSpec((SEQ_TB, CONV_CH), lambda b, i: (row(b, i), ca_col)),
            pl.BlockSpec((SEQ_TB, CONV_CH), lambda b, i: (row(b, i), ca_col + 1)),
            pl.BlockSpec((CONV_HALO, CONV_CH), lambda b, i: (halo(b, i), ca_col)),
            pl.BlockSpec((CONV_HALO, CONV_CH), lambda b, i: (halo(b, i), ca_col + 1)),
            pl.BlockSpec((CONV_WIDTH, CONV_CH), lambda b, i: (0, 0)),
            pl.BlockSpec((1, CONV_CH), lambda b, i: (0, 0)),
            pl.BlockSpec((1, CONV_CH), lambda b, i: (0, 0)),
            pl.BlockSpec((1, CONV_CH), lambda b, i: (0, 0)),
        ],
        out_specs=pl.BlockSpec((SEQ_TB, CONV_CH), lambda b, i: (row(b, i), 0)),
        out_shape=jax.ShapeDtypeStruct((TOKENS, CONV_CH), BF16),
        scratch_shapes=[pltpu.VMEM((SUBLANES, SEQ_TB + CONV_HALO, CONV_CH), F32),
                        pltpu.VMEM((SEQ_TB, CONV_CH), F32)],
        compiler_params=_cparams(("parallel", "parallel")),
        name="conv",
    )(z, z, z, z, w, wb, ln_g, ln_b)


def _attn_kernel(q_ref, k0_ref, k1_ref, k2_ref, v0_ref, v1_ref, v2_ref,
                 bias1_ref, bias2_ref, far_ref, o_ref):
    i = pl.program_id(1)
    pen0 = jnp.where(i >= 2, 0.0, NEG_BIG)
    pen1 = jnp.where(i >= 1, 0.0, NEG_BIG)
    qc = lax.broadcasted_iota(jnp.int32, (ATT_TQ, ATT_TQ), 0) // ATT_CHUNK
    kc = lax.broadcasted_iota(jnp.int32, (ATT_TQ, ATT_TQ), 1) // ATT_CHUNK
    band0 = kc >= qc
    nt = (((1,), (1,)), ((), ()))

    for h in range(ATT_HEADS):
        hs = slice(h * ATT_HEAD_DIM, (h + 1) * ATT_HEAD_DIM)
        q = q_ref[:, hs]
        s0 = lax.dot_general(q, k0_ref[:, hs], nt, preferred_element_type=F32)
        s1 = lax.dot_general(q, k1_ref[:, hs], nt, preferred_element_type=F32)
        s2 = lax.dot_general(q, k2_ref[:, hs], nt, preferred_element_type=F32)
        s0 = jnp.where(band0, s0 + far_ref[h], NEG_BIG) + pen0
        s1 = s1 + bias1_ref[h] + pen1
        s2 = s2 + bias2_ref[h]
        m = jnp.maximum(jnp.maximum(jnp.max(s0, axis=-1, keepdims=True),
                                    jnp.max(s1, axis=-1, keepdims=True)),
                        jnp.max(s2, axis=-1, keepdims=True))
        p0 = jnp.exp(s0 - m)
        p1 = jnp.exp(s1 - m)
        p2 = jnp.exp(s2 - m)
        l = (jnp.sum(p0, axis=-1, keepdims=True) + jnp.sum(p1, axis=-1, keepdims=True)
             + jnp.sum(p2, axis=-1, keepdims=True))
        o = (jnp.dot(p0.astype(BF16), v0_ref[:, hs], preferred_element_type=F32)
             + jnp.dot(p1.astype(BF16), v1_ref[:, hs], preferred_element_type=F32)
             + jnp.dot(p2.astype(BF16), v2_ref[:, hs], preferred_element_type=F32))
        o_ref[:, hs] = (o / l).astype(o_ref.dtype)


def _attention(qkv, bias1, bias2, far):
    nb = SEQ // ATT_TQ
    row = lambda b, i: b * nb + i
    back = lambda d: (lambda b, i: (b * nb + jnp.maximum(i - d, 0)))
    spec = lambda rowfn, col: pl.BlockSpec((ATT_TQ, D_MODEL), lambda b, i: (rowfn(b, i), col))
    table = pl.BlockSpec((ATT_HEADS, ATT_TQ, ATT_TQ), lambda b, i: (0, 0, 0))
    return pl.pallas_call(
        _attn_kernel,
        grid=(BATCH, nb),
        in_specs=[
            spec(row, 0),
            spec(back(2), 1), spec(back(1), 1), spec(row, 1),
            spec(back(2), 2), spec(back(1), 2), spec(row, 2),
            table, table,
            pl.BlockSpec(memory_space=pltpu.SMEM),
        ],
        out_specs=pl.BlockSpec((ATT_TQ, D_MODEL), lambda b, i: (row(b, i), 0)),
        out_shape=jax.ShapeDtypeStruct((TOKENS, D_MODEL), BF16),
        compiler_params=_cparams(("parallel", "parallel")),
        name="attention",
    )(qkv, qkv, qkv, qkv, qkv, qkv, qkv, bias1, bias2, far)


def _attention_bias_tables(rel_bias):
    def table(j):
        base = (2 - j) * ATT_TQ
        first = min(base - (ATT_TQ - 1), -REL_CLIP)
        last = max(base + (ATT_TQ - 1), REL_CLIP)
        ext = jnp.pad(rel_bias.astype(F32),
                      ((0, 0), (-REL_CLIP - first, last - REL_CLIP)), mode="edge")
        off = base - (ATT_TQ - 1) - first
        period = 2 * ATT_TQ
        row = jnp.pad(ext[:, off:off + period - 1], ((0, 0), (0, 1)))
        rep = jnp.tile(row, (1, ATT_TQ + 1))[:, :ATT_TQ * (period + 1)]
        hank = rep.reshape(-1, ATT_TQ, period + 1)[:, :, :ATT_TQ]
        return hank[:, :, ::-1]

    r = jnp.arange(ATT_TQ)[:, None]
    c = jnp.arange(ATT_TQ)[None, :]
    band2 =(c // ATT_CHUNK) <= (r // ATT_CHUNK)
    bias1 = table(1)
    bias2 = jnp.where(band2[None], table(2), NEG_BIG)
    far = rel_bias[:, 2 * REL_CLIP].astype(F32)
    return bias1, bias2, far


def kernel(x, p, ffn_norm, ffn_w_gate, ffn_w_up, ffn_w_down, mix_norm, ab_w_in, gla_gate_w, gla_gate_b, gla_norm_g, conv_dw, conv_dw_b, conv_ln_g, conv_ln_b, ab_w_out, att_w_qkv, att_rel_bias, att_w_o, pl_norm, pl_w_gate, pl_w_proj, final_norm):
    xs = x.reshape(TOKENS, D_MODEL)
    ps = p.reshape(DEPTH, TOKENS, D_PL)
    row = lambda a: a.reshape(1, -1).astype(F32)

    pl_wg_all = pl_w_gate.astype(BF16)
    pl_wp_all = pl_w_proj.astype(BF16)

    def ffn(xs, i, s):
        return _ffn(xs, row(ffn_norm[i, s]), ffn_w_gate, ffn_w_up, ffn_w_down, i, s)

    for i in range(DEPTH):
        e = i // 2
        xs = ffn(xs, i, 0)
        if i % 2 == 0:
            w_in = ab_w_in[e]
            gz_lo = 2 * GLA_DK + 2 * GLA_DV
            gz_hi = gz_lo + GLA_GATE_RANK
            w_main = jnp.concatenate([w_in[:, :gz_lo], w_in[:, gz_hi:]], axis=1).astype(BF16)
            w_gz = jnp.pad(w_in[:, gz_lo:gz_hi], ((0, 0), (0, LANES - GLA_GATE_RANK))).astype(BF16)
            z_all, gz = _rms_proj(xs, row(mix_norm[i]), w_main,
                                  jnp.ones((1, AB_MAIN), F32), F32, w_extra=w_gz)
            gate_w = jnp.pad(gla_gate_w[e], ((0, LANES - GLA_GATE_RANK), (0, 0))).astype(BF16)
            a_out = _gla(z_all, gz, gate_w, row(gla_gate_b[e]), row(gla_norm_g[e]))
            b_out = _conv(z_all, conv_dw[e].astype(F32), row(conv_dw_b[e]),
                          row(conv_ln_g[e]), row(conv_ln_b[e]))
            xs = _out_proj([a_out, b_out], ab_w_out[e], xs)
        else:
            colscale = jnp.concatenate([jnp.full((1, D_MODEL), ATT_HEAD_DIM ** -0.5, F32),
                                        jnp.ones((1, 2 * D_MODEL), F32)], axis=1)
            qkv = _rms_proj(xs, row(mix_norm[i]), att_w_qkv[e], colscale, BF16)
            bias1, bias2, far = _attention_bias_tables(att_rel_bias[e])
            o = _attention(qkv, bias1, bias2, far)
            xs = _out_proj([o], att_w_o[e], xs)
        xs = ffn(xs, i, 1)
        xs = _pl_embed(xs, ps, row(pl_norm[i]), pl_wg_all, pl_wp_all, row(final_norm),
                       i, i == DEPTH - 1)
    return xs.reshape(BATCH, SEQ, D_MODEL)
```

```python
import functools

import jax
import jax.numpy as jnp
from jax import lax
from jax.experimental import pallas as pl
from jax.experimental.pallas import tpu as pltpu

F32 = jnp.float32
BF16 = jnp.bfloat16

D_MODEL = 2048
BATCH = 4
SEQ = 2048
DEPTH = 2
TOKENS = BATCH * SEQ
D_PL = 256
D_FF = 5632
EPS = 1e-6

GLA_HEADS = 4
GLA_DK = 512
GLA_DV = 1024
GLA_HEAD_K = 128
GLA_HEAD_V = 256
GLA_GATE_RANK = 16
GLA_GATE_TAU = 16.0
GLA_CHUNK = 64
GLA_SUB = 16
CONV_CH = 1024
CONV_WIDTH = 31
AB_MAIN = 2 * GLA_DK + 2 * GLA_DV + 2 * CONV_CH
ATT_HEADS = 16
ATT_HEAD_DIM = 128
ATT_CHUNK = 64
LEFT_CHUNKS = 8
REL_CLIP = 128
NEG_BIG = -1e30
LOG2E = 1.4426950408889634

V7X_VMEM_BYTES = 64 * 1024 * 1024
LANES = 128
SUBLANES = 8
VMEM_LIMIT = V7X_VMEM_BYTES - 4 * 1024 * 1024

FFN_TM = 1024
FFN_TF = 512
FFN_TN = 512
MM_TM = 1024
MM_TN = 1024
PL_TM = 256
OP_TM = 512
SEQ_TB = 512
CONV_HALO = 32
CONV_RC = 32
ATT_TQ = 256
RMS_ROWS = 256


def _cparams(sem):
    return pltpu.CompilerParams(dimension_semantics=sem, vmem_limit_bytes=VMEM_LIMIT)


def _rms_rows_to(dst_ref, x_ref, g_ref, rows):
    def body(c, carry):
        sl = pl.ds(pl.multiple_of(c * RMS_ROWS, RMS_ROWS), RMS_ROWS)
        x = x_ref[sl, :]
        ms = jnp.mean(x * x, axis=-1, keepdims=True)
        dst_ref[sl, :] = (x * lax.rsqrt(ms + EPS) * g_ref[...]).astype(dst_ref.dtype)
        return carry
    lax.fori_loop(0, rows // RMS_ROWS, body, 0)


def _ffn_kernel(x_ref, g_ref, wg_ref, wu_ref, wd_ref, o_ref, h_ref):
    f = pl.program_id(1)

    @pl.when(f == 0)
    def _():
        _rms_rows_to(h_ref, x_ref, g_ref, FFN_TM)
        o_ref[...] = x_ref[...]

    h = h_ref[...]
    gate = jnp.dot(h, wg_ref[...].astype(BF16), preferred_element_type=F32)
    up = jnp.dot(h, wu_ref[...].astype(BF16), preferred_element_type=F32)
    a = (0.5 * gate * jax.nn.sigmoid(gate) * up).astype(BF16)

    for n in range(D_MODEL // FFN_TN):
        cs = slice(n * FFN_TN, (n + 1) * FFN_TN)
        o_ref[:, cs] += jnp.dot(a, wd_ref[:, cs].astype(BF16), preferred_element_type=F32)


def _ffn(x, g, wg, wu, wd, layer, half):
    m = x.shape[0]
    return pl.pallas_call(
        _ffn_kernel,
        grid=(m // FFN_TM, D_FF // FFN_TF),
        in_specs=[
            pl.BlockSpec((FFN_TM, D_MODEL), lambda i, f: (i, 0), pipeline_mode=pl.Buffered(1)),
            pl.BlockSpec((1, D_MODEL), lambda i, f: (0, 0)),
            pl.BlockSpec((None, None, D_MODEL, FFN_TF), lambda i, f: (layer, half, 0, f)),
            pl.BlockSpec((None, None, D_MODEL, FFN_TF), lambda i, f: (layer, half, 0, f)),
            pl.BlockSpec((None, None, FFN_TF, D_MODEL), lambda i, f: (layer, half, f, 0)),
        ],
        out_specs=pl.BlockSpec((FFN_TM, D_MODEL), lambda i, f: (i, 0)),
        out_shape=jax.ShapeDtypeStruct((m, D_MODEL), F32),
        scratch_shapes=[pltpu.VMEM((FFN_TM, D_MODEL), BF16)],
        compiler_params=_cparams(("parallel", "arbitrary")),
        name="ffn",
    )(x, g, wg, wu, wd)


def _rms_proj_kernel(has_extra, *refs):
    if has_extra:
        x_ref, g_ref, w_ref, cs_ref, we_ref, o_ref, oe_ref, h_ref = refs
    else:
        x_ref, g_ref, w_ref, cs_ref, o_ref, h_ref = refs

    @pl.when(pl.program_id(1) == 0)
    def _():
        _rms_rows_to(h_ref, x_ref, g_ref, MM_TM)
        if has_extra:
            oe_ref[...] = jnp.dot(h_ref[...], we_ref[...].astype(BF16),
                                  preferred_element_type=F32)

    acc = jnp.dot(h_ref[...], w_ref[...].astype(BF16), preferred_element_type=F32)
    o_ref[...] = (acc * cs_ref[...]).astype(o_ref.dtype)


def _rms_proj(x, g, w, colscale, out_dtype, w_extra=None):
    m, n = x.shape[0], w.shape[1]
    has_extra = w_extra is not None
    in_specs = [
        pl.BlockSpec((MM_TM, D_MODEL), lambda i, j: (i, 0)),
        pl.BlockSpec((1, D_MODEL), lambda i, j: (0, 0)),
        pl.BlockSpec((D_MODEL, MM_TN), lambda i, j: (0, j)),
        pl.BlockSpec((1, MM_TN), lambda i, j: (0, j)),
    ]
    out_specs = pl.BlockSpec((MM_TM, MM_TN), lambda i, j: (i, j))
    out_shape = jax.ShapeDtypeStruct((m, n), out_dtype)
    args = [x, g, w, colscale]
    if has_extra:
        in_specs.append(pl.BlockSpec((D_MODEL, LANES), lambda i, j: (0, 0)))
        out_specs = [out_specs, pl.BlockSpec((MM_TM, LANES), lambda i, j: (i, 0))]
        out_shape = [out_shape, jax.ShapeDtypeStruct((m, LANES), F32)]
        args.append(w_extra)
    return pl.pallas_call(
        functools.partial(_rms_proj_kernel, has_extra),
        grid=(m // MM_TM, n // MM_TN),
        in_specs=in_specs,
        out_specs=out_specs,
        out_shape=out_shape,
        scratch_shapes=[pltpu.VMEM((MM_TM, D_MODEL), BF16)],
        compiler_params=_cparams(("parallel", "arbitrary")),
        name="rms_proj",
    )(*args)


def _out_proj_kernel(n_lhs, *refs):
    lhs_refs = refs[:n_lhs]
    w_refs = refs[n_lhs:2 * n_lhs]
    x_ref, o_ref = refs[2 * n_lhs], refs[2 * n_lhs + 1]
    acc = x_ref[...]
    for a_ref, w_ref in zip(lhs_refs, w_refs):
        acc = acc + jnp.dot(a_ref[...], w_ref[...].astype(BF16), preferred_element_type=F32)
    o_ref[...] = acc


def _out_proj(lhs_list, w, x):
    m = x.shape[0]
    n_lhs = len(lhs_list)
    kw = lhs_list[0].shape[1]
    in_specs = [pl.BlockSpec((OP_TM, kw), lambda i: (i, 0)) for _ in lhs_list]
    in_specs += [pl.BlockSpec((kw, D_MODEL), lambda i, t=t: (t, 0)) for t in range(n_lhs)]
    in_specs += [pl.BlockSpec((OP_TM, D_MODEL), lambda i: (i, 0))]
    return pl.pallas_call(
        functools.partial(_out_proj_kernel, n_lhs),
        grid=(m // OP_TM,),
        in_specs=in_specs,
        out_specs=pl.BlockSpec((OP_TM, D_MODEL), lambda i: (i, 0)),
        out_shape=jax.ShapeDtypeStruct((m, D_MODEL), F32),
        compiler_params=_cparams(("parallel",)),
        name="out_proj",
    )(*lhs_list, *([w] * n_lhs), x)


def _pl_embed_kernel(final, x_ref, p_ref, g_ref, wg_ref, wp_ref, fg_ref, o_ref, h_ref):
    _rms_rows_to(h_ref, x_ref, g_ref, PL_TM)
    gate = jax.nn.sigmoid(jnp.dot(h_ref[...], wg_ref[...], preferred_element_type=F32))
    proj = jnp.dot(p_ref[...].astype(BF16), wp_ref[...], preferred_element_type=F32)
    y = x_ref[...] + gate * proj
    if final:
        ms = jnp.mean(y * y, axis=-1, keepdims=True)
        y = y * lax.rsqrt(ms + EPS) * fg_ref[...]
    o_ref[...] = y


def _pl_embed(x, p, g, wg, wp, final_g, layer, final):
    m = x.shape[0]
    return pl.pallas_call(
        functools.partial(_pl_embed_kernel, final),
        grid=(m // PL_TM,),
        in_specs=[
            pl.BlockSpec((PL_TM, D_MODEL), lambda i: (i, 0)),
            pl.BlockSpec((None, PL_TM, D_PL), lambda i: (layer, i, 0)),
            pl.BlockSpec((1, D_MODEL), lambda i: (0, 0)),
            pl.BlockSpec((None, D_MODEL, D_MODEL), lambda i: (layer, 0, 0)),
            pl.BlockSpec((None, D_PL, D_MODEL), lambda i: (layer, 0, 0)),
            pl.BlockSpec((1, D_MODEL), lambda i: (0, 0)),
        ],
        out_specs=pl.BlockSpec((PL_TM, D_MODEL), lambda i: (i, 0)),
        out_shape=jax.ShapeDtypeStruct((m, D_MODEL), F32),
        scratch_shapes=[pltpu.VMEM((PL_TM, D_MODEL), BF16)],
        compiler_params=_cparams(("parallel",)),
        name="pl_embed",
    )(x, p, g, wg, wp, final_g)


def _gla_kernel(q_ref, k_ref, v_ref, r_ref, gz_ref, gw_ref, gb_ref, ng_ref, o_ref,
                b_ref, s_ref):
    C, SB = GLA_CHUNK, GLA_SUB
    n_chunks = SEQ_TB // C

    @pl.when(pl.program_id(1) == 0)
    def _():
        s_ref[...] = jnp.zeros_like(s_ref)

    lin = jnp.dot(gz_ref[...].astype(BF16), gw_ref[...], preferred_element_type=F32) + gb_ref[...]
    log_a = -(jnp.maximum(-lin, 0.0) + jnp.log1p(jnp.exp(-jnp.abs(lin)))) / GLA_GATE_TAU
    tri = (lax.broadcasted_iota(jnp.int32, (C, C), 0)
           >= lax.broadcasted_iota(jnp.int32, (C, C), 1)).astype(F32)
    for c in range(n_chunks):
        b_ref[c * C:(c + 1) * C, :] = jnp.dot(
            tri, log_a[c * C:(c + 1) * C, :], preferred_element_type=F32,
            precision=lax.Precision.HIGHEST)

    lane = lax.broadcasted_iota(jnp.int32, (SB, C), 1)
    trow = lax.broadcasted_iota(jnp.int32, (SB, C), 0)
    nt = (((1,), (1,)), ((), ()))
    tn = (((0,), (0,)), ((), ()))

    def chunk_body(c, carry):
        r0 = pl.multiple_of(c * C, C)
        rows = pl.ds(r0, C)
        for h in range(GLA_HEADS):
            hk = slice(h * GLA_HEAD_K, (h + 1) * GLA_HEAD_K)
            hv = slice(h * GLA_HEAD_V, (h + 1) * GLA_HEAD_V)
            q = q_ref[rows, hk] * (GLA_HEAD_K ** -0.5)
            k = k_ref[rows, hk]
            v = v_ref[rows, hv].astype(BF16)
            b = b_ref[rows, hk]
            b_last = b[C - 1:C, :]

            st = s_ref[h]
            inter = lax.dot_general((q * jnp.exp(b)).astype(BF16), st.astype(BF16), nt,
                                    preferred_element_type=F32)

            blocks = []
            for i in range(C // SB):
                s0 = i * SB
                q_i = q[s0:s0 + SB]
                b_i = b[s0:s0 + SB]
                acc = jnp.zeros((SB, C), F32)
                if i > 0:
                    p_i = b[s0 - 1:s0, :]
                    qt = (q_i * jnp.exp(b_i - p_i)).astype(BF16)
                    kt = (k * jnp.exp(jnp.minimum(p_i - b, 0.0))).astype(BF16)
                    off = lax.dot_general(qt, kt, nt, preferred_element_type=F32)
                    acc = jnp.where(lane < s0, off, 0.0)
                for s in range(SB):
                    b_s = b[s0 + s:s0 + s + 1, :]
                    k_s = k[s0 + s:s0 + s + 1, :]
                    e = jnp.exp(jnp.minimum(b_i - b_s, 0.0))
                    col = jnp.sum(q_i * (k_s * e), axis=1, keepdims=True)
                    acc = acc + jnp.where((lane == s0 + s) & (trow >= s), col, 0.0)
                blocks.append(acc)
            a_mat = jnp.concatenate(blocks, axis=0)
            o = inter + jnp.dot(a_mat.astype(BF16), v, preferred_element_type=F32)

            k_dec = (k * jnp.exp(b_last - b)).astype(BF16)
            s_ref[h] = st * jnp.exp(b_last) + lax.dot_general(
                v, k_dec, tn, preferred_element_type=F32)

            on = o * lax.rsqrt(jnp.mean(o * o, axis=-1, keepdims=True) + EPS) * ng_ref[...]
            r = r_ref[rows, hv]
            o_ref[rows, hv] = (on * (r * jax.nn.sigmoid(r))).astype(o_ref.dtype)
        return carry

    lax.fori_loop(0, n_chunks, chunk_body, 0)


def _gla(z, gz, gate_w, gate_b, norm_g):
    nb = SEQ // SEQ_TB
    row = lambda b, i: b * nb + i
    return pl.pallas_call(
        _gla_kernel,
        grid=(BATCH, nb),
        in_specs=[
            pl.BlockSpec((SEQ_TB, GLA_DK), lambda b, i: (row(b, i), 0)),
            pl.BlockSpec((SEQ_TB, GLA_DK), lambda b, i: (row(b, i), 1)),
            pl.BlockSpec((SEQ_TB, GLA_DV), lambda b, i: (row(b, i), 1)),
            pl.BlockSpec((SEQ_TB, GLA_DV), lambda b, i: (row(b, i), 2)),
            pl.BlockSpec((SEQ_TB, LANES), lambda b, i: (row(b, i), 0)),
            pl.BlockSpec((LANES, GLA_DK), lambda b, i: (0, 0)),
            pl.BlockSpec((1, GLA_DK), lambda b, i: (0, 0)),
            pl.BlockSpec((1, GLA_HEAD_V), lambda b, i: (0, 0)),
        ],
        out_specs=pl.BlockSpec((SEQ_TB, GLA_DV), lambda b, i: (row(b, i), 0)),
        out_shape=jax.ShapeDtypeStruct((TOKENS, GLA_DV), BF16),
        scratch_shapes=[pltpu.VMEM((SEQ_TB, GLA_DK), F32),
                        pltpu.VMEM((GLA_HEADS, GLA_HEAD_V, GLA_HEAD_K), F32)],
        compiler_params=_cparams(("parallel", "arbitrary")),
        name="gla",
    )(z, z, z, z, gz, gate_w, gate_b, norm_g)


def _conv_kernel(ca_ref, cb_ref, ha_ref, hb_ref, w_ref, wb_ref, lg_ref, lb_ref, o_ref,
                 sh_ref, y_ref):
    H = CONV_HALO
    n_sh = SUBLANES
    keep = (pl.program_id(1) > 0).astype(F32)

    u_main = ca_ref[...] * jax.nn.sigmoid(cb_ref[...])
    u_halo = ha_ref[...] * jax.nn.sigmoid(hb_ref[...]) * keep
    for r in range(n_sh):
        sh_ref[r, 0:H - r, :] = u_halo[r:H, :]
        sh_ref[r, H - r:H - r + SEQ_TB, :] = u_main

    first = H - (CONV_WIDTH - 1)
    ct_w = 512

    def row_body(rc, carry):
        t0 = pl.multiple_of(rc * CONV_RC, CONV_RC)
        for ct in range(CONV_CH // ct_w):
            cs = slice(ct * ct_w, (ct + 1) * ct_w)
            acc = jnp.broadcast_to(wb_ref[:, cs], (CONV_RC, ct_w))
            for j in range(CONV_WIDTH):
                off = first + j
                r, base = off % n_sh, off - off % n_sh
                acc = acc + w_ref[j:j + 1, cs] * sh_ref[r, pl.ds(t0 + base, CONV_RC), cs]
            y_ref[pl.ds(t0, CONV_RC), cs] = acc
        return carry

    lax.fori_loop(0, SEQ_TB // CONV_RC, row_body, 0)

    def ln_body(rc, carry):
        sl = pl.ds(pl.multiple_of(rc * 64, 64), 64)
        y = y_ref[sl, :]
        mu = jnp.mean(y, axis=-1, keepdims=True)
        var = jnp.mean(jnp.square(y - mu), axis=-1, keepdims=True)
        t = (y - mu) * lax.rsqrt(var + EPS) * lg_ref[...] + lb_ref[...]
        o_ref[sl, :] = (t * jax.nn.sigmoid(t)).astype(o_ref.dtype)
        return carry

    lax.fori_loop(0, SEQ_TB // 64, ln_body, 0)


def _conv(z, w, wb, ln_g, ln_b):
    nb = SEQ // SEQ_TB
    hb = SEQ_TB // CONV_HALO
    ca_col = (2 * GLA_DK + 2 * GLA_DV) // CONV_CH
    row = lambda b, i: b * nb + i
    halo = lambda b, i: jnp.maximum(row(b, i) * hb - 1, 0)
    return pl.pallas_call(
        _conv_kernel,
        grid=(BATCH, nb),
        in_specs=[
            pl.BlockSpec((SEQ_TB, CONV_CH), lambda b, i: (row(b, i), ca_col)),
            pl.BlockSpec((SEQ_TB, CONV_CH), lambda b, i: (row(b, i), ca_col + 1)),
            pl.BlockSpec((CONV_HALO, CONV_CH), lambda b, i: (halo(b, i), ca_col)),
            pl.BlockSpec((CONV_HALO, CONV_CH), lambda b, i: (halo(b, i), ca_col + 1)),
            pl.BlockSpec((CONV_WIDTH, CONV_CH), lambda b, i: (0, 0)),
            pl.BlockSpec((1, CONV_CH), lambda b, i: (0, 0)),
            pl.BlockSpec((1, CONV_CH), lambda b, i: (0, 0)),
            pl.BlockSpec((1, CONV_CH), lambda b, i: (0, 0)),
        ],
        out_specs=pl.BlockSpec((SEQ_TB, CONV_CH), lambda b, i: (row(b, i), 0)),
        out_shape=jax.ShapeDtypeStruct((TOKENS, CONV_CH), BF16),
        scratch_shapes=[pltpu.VMEM((SUBLANES, SEQ_TB + CONV_HALO, CONV_CH), F32),
                        pltpu.VMEM((SEQ_TB, CONV_CH), F32)],
        compiler_params=_cparams(("parallel", "parallel")),
        name="conv",
    )(z, z, z, z, w, wb, ln_g, ln_b)


ATT_HALF = ATT_TQ // 2


def _attn_kernel(q_ref, k0_ref, k1_ref, k2_ref, v0_ref, v1_ref, v2_ref,
                 m0_ref, ta_ref, tb_ref, o_ref):
    i = pl.program_id(1)
    pen = [jnp.where(i >= 2, 0.0, NEG_BIG), jnp.where(i >= 1, 0.0, NEG_BIG), None]
    k_refs = (k0_ref, k1_ref, k2_ref)
    v_refs = (v0_ref, v1_ref, v2_ref)
    nt = (((1,), (1,)), ((), ()))
    tn = (((0,), (0,)), ((), ()))
    H = ATT_HALF

    def scores(h):
        hs = slice(h * ATT_HEAD_DIM, (h + 1) * ATT_HEAD_DIM)
        q = q_ref[:, hs]
        tiles = {}
        for j in range(3):
            st = lax.dot_general(k_refs[j][:, hs], q, nt, preferred_element_type=F32)
            for kh in range(2):
                for a in range(2):
                    n = 2 * j + kh - a
                    if n < 0 or n > 4:
                        continue
                    t = st[kh * H:(kh + 1) * H, a * H:(a + 1) * H]
                    if n == 0:
                        t = t + m0_ref[...]
                    elif n == 3:
                        t = t + ta_ref[h]
                    elif n == 4:
                        t = t + tb_ref[h]
                    tiles[(j, kh, a)] = t
        maxima = []
        for a in range(2):
            mx = None
            for key in tiles:
                if key[2] == a:
                    cur = jnp.max(tiles[key], axis=0, keepdims=True)
                    if pen[key[0]] is not None:
                        cur = cur + pen[key[0]]
                    mx = cur if mx is None else jnp.maximum(mx, cur)
            maxima.append(mx)
        return tiles, maxima

    def finish(h, tiles, maxima):
        hs = slice(h * ATT_HEAD_DIM, (h + 1) * ATT_HEAD_DIM)
        probs = {}
        inv_l = []
        for a in range(2):
            mine = [key for key in tiles if key[2] == a]
            shift = [maxima[a] if pj is None else maxima[a] - pj for pj in pen]
            tot = None
            for key in mine:
                p = jnp.exp2(tiles[key] - shift[key[0]])
                probs[key] = p.astype(BF16)
                cur = jnp.sum(p, axis=0, keepdims=True)
                tot = cur if tot is None else tot + cur
            inv_l.append(1.0 / tot)

        zero = jnp.zeros((H, H), BF16)
        ot = None
        for j in range(3):
            pt = jnp.concatenate(
                [jnp.concatenate([probs.get((j, kh, a), zero) for a in range(2)], axis=1)
                 for kh in range(2)], axis=0)
            cur = lax.dot_general(v_refs[j][:, hs], pt, tn, preferred_element_type=F32)
            ot = cur if ot is None else ot + cur
        ot = ot * jnp.concatenate(inv_l, axis=1)
        o_ref[:, hs] = ot.T.astype(o_ref.dtype)

    pending = scores(0)
    for h in range(ATT_HEADS):
        current = pending
        if h + 1 < ATT_HEADS:
            pending = scores(h + 1)
        finish(h, *current)


def _attention(qkv, m0, ta, tb):
    nb = SEQ // ATT_TQ
    row = lambda b, i: b * nb + i
    back = lambda d: (lambda b, i: (b * nb + jnp.maximum(i - d, 0)))
    spec = lambda rowfn, col: pl.BlockSpec((ATT_TQ, D_MODEL), lambda b, i: (rowfn(b, i), col))
    table = pl.BlockSpec((ATT_HEADS, ATT_HALF, ATT_HALF), lambda b, i: (0, 0, 0))
    return pl.pallas_call(
        _attn_kernel,
        grid=(BATCH, nb),
        in_specs=[
            spec(row, 0),
            spec(back(2), 1), spec(back(1), 1), spec(row, 1),
            spec(back(2), 2), spec(back(1), 2), spec(row, 2),
            pl.BlockSpec((ATT_HALF, ATT_HALF), lambda b, i: (0, 0)),
            table, table,
        ],
        out_specs=pl.BlockSpec((ATT_TQ, D_MODEL), lambda b, i: (row(b, i), 0)),
        out_shape=jax.ShapeDtypeStruct((TOKENS, D_MODEL), BF16),
        compiler_params=_cparams(("parallel", "parallel")),
        name="attention",
    )(qkv, qkv, qkv, qkv, qkv, qkv, qkv, m0, ta, tb)


def _attention_bias_tables(rel_bias):
    H = ATT_HALF
    assert H == REL_CLIP
    rb = rel_bias.astype(F32)
    rel = (rb - rb[:, 2 * REL_CLIP:]) * LOG2E

    def toeplitz(g):
        flat = jnp.tile(g, (1, H))[:, :H * (2 * H - 1)]
        return flat.reshape(-1, H, 2 * H - 1)[:, :, :H]

    ta = toeplitz(jnp.concatenate([jnp.zeros((ATT_HEADS, H), F32),
                                   rel[:, REL_CLIP:2 * REL_CLIP]], axis=1))
    tb = toeplitz(jnp.concatenate([rel[:, REL_CLIP:2 * REL_CLIP], rel[:, :REL_CLIP]], axis=1))
    kc = jnp.arange(H)[:, None] // ATT_CHUNK
    qc = jnp.arange(H)[None, :] // ATT_CHUNK
    tb = jnp.where((kc <= qc)[None], tb, NEG_BIG)
    m0 = jnp.where(kc >= qc, 0.0, NEG_BIG).astype(F32)
    return m0, ta, tb


def kernel(x, p, ffn_norm, ffn_w_gate, ffn_w_up, ffn_w_down, mix_norm, ab_w_in, gla_gate_w, gla_gate_b, gla_norm_g, conv_dw, conv_dw_b, conv_ln_g, conv_ln_b, ab_w_out, att_w_qkv, att_rel_bias, att_w_o, pl_norm, pl_w_gate, pl_w_proj, final_norm):
    xs = x.reshape(TOKENS, D_MODEL)
    ps = p.reshape(DEPTH, TOKENS, D_PL)
    row = lambda a: a.reshape(1, -1).astype(F32)

    pl_wg_all = pl_w_gate.astype(BF16)
    pl_wp_all = pl_w_proj.astype(BF16)

    def ffn(xs, i, s):
        return _ffn(xs, row(ffn_norm[i, s]), ffn_w_gate, ffn_w_up, ffn_w_down, i, s)

    for i in range(DEPTH):
        e = i // 2
        xs = ffn(xs, i, 0)
        if i % 2 == 0:
            w_in = ab_w_in[e]
            gz_lo = 2 * GLA_DK + 2 * GLA_DV
            gz_hi = gz_lo + GLA_GATE_RANK
            w_main = jnp.concatenate([w_in[:, :gz_lo], w_in[:, gz_hi:]], axis=1).astype(BF16)
            w_gz = jnp.pad(w_in[:, gz_lo:gz_hi], ((0, 0), (0, LANES - GLA_GATE_RANK))).astype(BF16)
            z_all, gz = _rms_proj(xs, row(mix_norm[i]), w_main,
                                  jnp.ones((1, AB_MAIN), F32), F32, w_extra=w_gz)
            gate_w = jnp.pad(gla_gate_w[e], ((0, LANES - GLA_GATE_RANK), (0, 0))).astype(BF16)
            a_out = _gla(z_all, gz, gate_w, row(gla_gate_b[e]), row(gla_norm_g[e]))
            b_out = _conv(z_all, conv_dw[e].astype(F32), row(conv_dw_b[e]),
                          row(conv_ln_g[e]), row(conv_ln_b[e]))
            xs = _out_proj([a_out, b_out], ab_w_out[e].astype(BF16), xs)
        else:
            colscale = jnp.concatenate([jnp.full((1, D_MODEL), ATT_HEAD_DIM ** -0.5 * LOG2E, F32),
                                        jnp.ones((1, 2 * D_MODEL), F32)], axis=1)
            qkv = _rms_proj(xs, row(mix_norm[i]), att_w_qkv[e], colscale, BF16)
            m0, ta, tb = _attention_bias_tables(att_rel_bias[e])
            o = _attention(qkv, m0, ta, tb)
            xs = _out_proj([o], att_w_o[e].astype(BF16), xs)
        xs = ffn(xs, i, 1)
        xs = _pl_embed(xs, ps, row(pl_norm[i]), pl_wg_all, pl_wp_all, row(final_norm),
                       i, i == DEPTH - 1)
    return xs.reshape(BATCH, SEQ, D_MODEL)
```

```python
import functools

import jax
import jax.numpy as jnp
from jax import lax
from jax.experimental import pallas as pl
from jax.experimental.pallas import tpu as pltpu

F32 = jnp.float32
BF16 = jnp.bfloat16

D_MODEL = 2048
BATCH = 4
SEQ = 2048
DEPTH = 2
TOKENS = BATCH * SEQ
D_PL = 256
D_FF = 5632
EPS = 1e-6

GLA_HEADS = 4
GLA_DK = 512
GLA_DV = 1024
GLA_HEAD_K = 128
GLA_HEAD_V = 256
GLA_GATE_RANK = 16
GLA_GATE_TAU = 16.0
GLA_CHUNK = 64
GLA_SUB = 16
CONV_CH = 1024
CONV_WIDTH = 31
AB_MAIN = 2 * GLA_DK + 2 * GLA_DV + 2 * CONV_CH
ATT_HEADS = 16
ATT_HEAD_DIM = 128
ATT_CHUNK = 64
LEFT_CHUNKS = 8
REL_CLIP = 128
NEG_BIG = -1e30
LOG2E = 1.4426950408889634

V7X_VMEM_BYTES = 64 * 1024 * 1024
LANES = 128
SUBLANES = 8
VMEM_LIMIT = V7X_VMEM_BYTES - 4 * 1024 * 1024

FFN_TM = 1024
FFN_TF = 512
FFN_TF_FIRST = 256
FFN_TN = 512
MM_TM = 1024
MM_TN = 1024
PL_TM = 256
OP_TM = 512
SEQ_TB = 512
CONV_HALO = 32
CONV_RC = 32
ATT_TQ = 256
RMS_ROWS = 256


def _cparams(sem):
    return pltpu.CompilerParams(dimension_semantics=sem, vmem_limit_bytes=VMEM_LIMIT)


def _rms_rows_to(dst_ref, x_ref, g_ref, rows):
    def body(c, carry):
        sl = pl.ds(pl.multiple_of(c * RMS_ROWS, RMS_ROWS), RMS_ROWS)
        x = x_ref[sl, :]
        ms = jnp.mean(x * x, axis=-1, keepdims=True)
        dst_ref[sl, :] = (x * lax.rsqrt(ms + EPS) * g_ref[...]).astype(dst_ref.dtype)
        return carry
    lax.fori_loop(0, rows // RMS_ROWS, body, 0)


def _ffn_kernel(emit_bf16, *refs):
    if emit_bf16:
        x_ref, g_ref, wg_ref, wu_ref, wd_ref, o_ref, wgb_ref, wub_ref, wdb_ref, h_ref = refs
        wgb_ref[...] = wg_ref[...].astype(BF16)
        wub_ref[...] = wu_ref[...].astype(BF16)
        wdb_ref[...] = wd_ref[...].astype(BF16)
        wg_ref, wu_ref, wd_ref = wgb_ref, wub_ref, wdb_ref
    else:
        x_ref, g_ref, wg_ref, wu_ref, wd_ref, _, o_ref, h_ref = refs
    f = pl.program_id(1)

    @pl.when(f == 0)
    def _():
        _rms_rows_to(h_ref, x_ref, g_ref, FFN_TM)
        o_ref[...] = x_ref[...]

    h = h_ref[...]
    gate = jnp.dot(h, wg_ref[...], preferred_element_type=F32)
    up = jnp.dot(h, wu_ref[...], preferred_element_type=F32)
    a = (0.5 * gate * jax.nn.sigmoid(gate) * up).astype(BF16)

    for n in range(D_MODEL // FFN_TN):
        cs = slice(n * FFN_TN, (n + 1) * FFN_TN)
        o_ref[:, cs] += jnp.dot(a, wd_ref[:, cs], preferred_element_type=F32)


def _ffn(x, g, wg, wu, wd, layer, half):
    m = x.shape[0]
    scratch = [pltpu.VMEM((FFN_TM, D_MODEL), BF16)]
    params = _cparams(("parallel", "arbitrary"))
    tf = FFN_TF_FIRST
    first, wgb, wub, wdb = pl.pallas_call(
        functools.partial(_ffn_kernel, True),
        grid=(1, D_FF // tf),
        in_specs=[
            pl.BlockSpec((FFN_TM, D_MODEL), lambda i, f: (0, 0), pipeline_mode=pl.Buffered(1)),
            pl.BlockSpec((1, D_MODEL), lambda i, f: (0, 0)),
            pl.BlockSpec((None, None, D_MODEL, tf), lambda i, f: (layer, half, 0, f)),
            pl.BlockSpec((None, None, D_MODEL, tf), lambda i, f: (layer, half, 0, f)),
            pl.BlockSpec((None, None, tf, D_MODEL), lambda i, f: (layer, half, f, 0)),
        ],
        out_specs=[
            pl.BlockSpec((FFN_TM, D_MODEL), lambda i, f: (0, 0)),
            pl.BlockSpec((D_MODEL, tf), lambda i, f: (0, f)),
            pl.BlockSpec((D_MODEL, tf), lambda i, f: (0, f)),
            pl.BlockSpec((tf, D_MODEL), lambda i, f: (f, 0)),
        ],
        out_shape=[
            jax.ShapeDtypeStruct((m, D_MODEL), F32),
            jax.ShapeDtypeStruct((D_MODEL, D_FF), BF16),
            jax.ShapeDtypeStruct((D_MODEL, D_FF), BF16),
            jax.ShapeDtypeStruct((D_FF, D_MODEL), BF16),
        ],
        scratch_shapes=scratch,
        compiler_params=params,
        name="ffn_first",
    )(x, g, wg, wu, wd)
    tf = FFN_TF
    return pl.pallas_call(
        functools.partial(_ffn_kernel, False),
        grid=(m // FFN_TM - 1, D_FF // tf),
        in_specs=[
            pl.BlockSpec((FFN_TM, D_MODEL), lambda i, f: (i + 1, 0)),
            pl.BlockSpec((1, D_MODEL), lambda i, f: (0, 0)),
            pl.BlockSpec((D_MODEL, tf), lambda i, f: (0, f)),
            pl.BlockSpec((D_MODEL, tf), lambda i, f: (0, f)),
            pl.BlockSpec((tf, D_MODEL), lambda i, f: (f, 0)),
            pl.BlockSpec(memory_space=pl.ANY),
        ],
        out_specs=pl.BlockSpec((FFN_TM, D_MODEL), lambda i, f: (i + 1, 0)),
        out_shape=jax.ShapeDtypeStruct((m, D_MODEL), F32),
        input_output_aliases={5: 0},
        scratch_shapes=scratch,
        compiler_params=params,
        name="ffn_rest",
    )(x, g, wgb, wub, wdb, first)


def _rms_proj_kernel(seg_tiles, has_extra, *refs):
    n_seg = len(seg_tiles)
    x_ref, g_ref = refs[:2]
    w_refs = refs[2:2 + n_seg]
    cs_ref = refs[2 + n_seg]
    if has_extra:
        we_ref, o_ref, oe_ref, h_ref = refs[3 + n_seg:]
    else:
        o_ref, h_ref = refs[3 + n_seg:]
    j = pl.program_id(1)

    @pl.when(j == 0)
    def _():
        _rms_rows_to(h_ref, x_ref, g_ref, MM_TM)
        if has_extra:
            oe_ref[...] = jnp.dot(h_ref[...], we_ref[...].astype(BF16),
                                  preferred_element_type=F32)

    start = 0
    for w_ref, tiles in zip(w_refs, seg_tiles):
        @pl.when((j >= start) & (j < start + tiles))
        def _(w_ref=w_ref):
            acc = jnp.dot(h_ref[...], w_ref[...].astype(BF16), preferred_element_type=F32)
            o_ref[...] = (acc * cs_ref[...]).astype(o_ref.dtype)
        start += tiles


def _rms_proj(x, g, w_list, colscale, out_dtype, w_extra=None):
    m = x.shape[0]
    seg_tiles = tuple(t for _, t in w_list)
    n_tiles = sum(seg_tiles)
    has_extra = w_extra is not None
    in_specs = [
        pl.BlockSpec((MM_TM, D_MODEL), lambda i, j: (i, 0)),
        pl.BlockSpec((1, D_MODEL), lambda i, j: (0, 0)),
    ]
    start = 0
    for _, tiles in w_list:
        in_specs.append(pl.BlockSpec(
            (D_MODEL, MM_TN),
            lambda i, j, s=start, t=tiles: (0, jnp.clip(j - s, 0, t - 1))))
        start += tiles
    in_specs.append(pl.BlockSpec((1, MM_TN), lambda i, j: (0, j)))
    out_specs = pl.BlockSpec((MM_TM, MM_TN), lambda i, j: (i, j))
    out_shape = jax.ShapeDtypeStruct((m, n_tiles * MM_TN), out_dtype)
    args = [x, g] + [w for w, _ in w_list] + [colscale]
    if has_extra:
        in_specs.append(pl.BlockSpec((D_MODEL, LANES), lambda i, j: (0, 0)))
        out_specs = [out_specs, pl.BlockSpec((MM_TM, LANES), lambda i, j: (i, 0))]
        out_shape = [out_shape, jax.ShapeDtypeStruct((m, LANES), F32)]
        args.append(w_extra)
    return pl.pallas_call(
        functools.partial(_rms_proj_kernel, seg_tiles, has_extra),
        grid=(m // MM_TM, n_tiles),
        in_specs=in_specs,
        out_specs=out_specs,
        out_shape=out_shape,
        scratch_shapes=[pltpu.VMEM((MM_TM, D_MODEL), BF16)],
        compiler_params=_cparams(("parallel", "arbitrary")),
        name="rms_proj",
    )(*args)


def _out_proj_kernel(n_lhs, *refs):
    lhs_refs = refs[:n_lhs]
    w_refs = refs[n_lhs:2 * n_lhs]
    x_ref, o_ref = refs[2 * n_lhs], refs[2 * n_lhs + 1]
    acc = x_ref[...]
    for a_ref, w_ref in zip(lhs_refs, w_refs):
        acc = acc + jnp.dot(a_ref[...], w_ref[...].astype(BF16), preferred_element_type=F32)
    o_ref[...] = acc


def _out_proj(lhs_list, w, x):
    m = x.shape[0]
    n_lhs = len(lhs_list)
    kw = lhs_list[0].shape[1]
    in_specs = [pl.BlockSpec((OP_TM, kw), lambda i: (i, 0)) for _ in lhs_list]
    in_specs += [pl.BlockSpec((kw, D_MODEL), lambda i, t=t: (t, 0)) for t in range(n_lhs)]
    in_specs += [pl.BlockSpec((OP_TM, D_MODEL), lambda i: (i, 0))]
    return pl.pallas_call(
        functools.partial(_out_proj_kernel, n_lhs),
        grid=(m // OP_TM,),
        in_specs=in_specs,
        out_specs=pl.BlockSpec((OP_TM, D_MODEL), lambda i: (i, 0)),
        out_shape=jax.ShapeDtypeStruct((m, D_MODEL), F32),
        compiler_params=_cparams(("parallel",)),
        name="out_proj",
    )(*lhs_list, *([w] * n_lhs), x)


def _pl_embed_kernel(final, x_ref, p_ref, g_ref, wg_ref, wp_ref, fg_ref, o_ref, h_ref):
    _rms_rows_to(h_ref, x_ref, g_ref, PL_TM)
    gate = jax.nn.sigmoid(jnp.dot(h_ref[...], wg_ref[...], preferred_element_type=F32))
    proj = jnp.dot(p_ref[...].astype(BF16), wp_ref[...], preferred_element_type=F32)
    y = x_ref[...] + gate * proj
    if final:
        ms = jnp.mean(y * y, axis=-1, keepdims=True)
        y = y * lax.rsqrt(ms + EPS) * fg_ref[...]
    o_ref[...] = y


def _pl_embed(x, p, g, wg, wp, final_g, layer, final):
    m = x.shape[0]
    return pl.pallas_call(
        functools.partial(_pl_embed_kernel, final),
        grid=(m // PL_TM,),
        in_specs=[
            pl.BlockSpec((PL_TM, D_MODEL), lambda i: (i, 0)),
            pl.BlockSpec((None, PL_TM, D_PL), lambda i: (layer, i, 0)),
            pl.BlockSpec((1, D_MODEL), lambda i: (0, 0)),
            pl.BlockSpec((None, D_MODEL, D_MODEL), lambda i: (layer, 0, 0)),
            pl.BlockSpec((None, D_PL, D_MODEL), lambda i: (layer, 0, 0)),
            pl.BlockSpec((1, D_MODEL), lambda i: (0, 0)),
        ],
        out_specs=pl.BlockSpec((PL_TM, D_MODEL), lambda i: (i, 0)),
        out_shape=jax.ShapeDtypeStruct((m, D_MODEL), F32),
        scratch_shapes=[pltpu.VMEM((PL_TM, D_MODEL), BF16)],
        compiler_params=_cparams(("parallel",)),
        name="pl_embed",
    )(x, p, g, wg, wp, final_g)


def _gla_kernel(q_ref, k_ref, v_ref, r_ref, gz_ref, gw_ref, gb_ref, ng_ref, o_ref,
                b_ref, s_ref):
    C, SB = GLA_CHUNK, GLA_SUB
    n_chunks = SEQ_TB // C

    @pl.when(pl.program_id(1) == 0)
    def _():
        s_ref[...] = jnp.zeros_like(s_ref)

    lin = jnp.dot(gz_ref[...].astype(BF16), gw_ref[...], preferred_element_type=F32) + gb_ref[...]
    log_a = -(jnp.maximum(-lin, 0.0) + jnp.log1p(jnp.exp(-jnp.abs(lin)))) * (LOG2E / GLA_GATE_TAU)
    tri = (lax.broadcasted_iota(jnp.int32, (C, C), 0)
           >= lax.broadcasted_iota(jnp.int32, (C, C), 1)).astype(F32)
    for c in range(n_chunks):
        b_ref[c * C:(c + 1) * C, :] = jnp.dot(
            tri, log_a[c * C:(c + 1) * C, :], preferred_element_type=F32,
            precision=lax.Precision.HIGHEST)

    lane = lax.broadcasted_iota(jnp.int32, (SB, C), 1)
    trow = lax.broadcasted_iota(jnp.int32, (SB, C), 0)
    nt = (((1,), (1,)), ((), ()))
    tn = (((0,), (0,)), ((), ()))

    def chunk_body(c, carry):
        r0 = pl.multiple_of(c * C, C)
        rows = pl.ds(r0, C)
        for h in range(GLA_HEADS):
            hk = slice(h * GLA_HEAD_K, (h + 1) * GLA_HEAD_K)
            hv = slice(h * GLA_HEAD_V, (h + 1) * GLA_HEAD_V)
            q = q_ref[rows, hk] * (GLA_HEAD_K ** -0.5)
            k = k_ref[rows, hk]
            v = v_ref[rows, hv].astype(BF16)
            b = b_ref[rows, hk]
            b_last = b[C - 1:C, :]

            st = s_ref[h]
            inter = lax.dot_general((q * jnp.exp2(b)).astype(BF16), st.astype(BF16), nt,
                                    preferred_element_type=F32)

            blocks = []
            for i in range(C // SB):
                s0 = i * SB
                q_i = q[s0:s0 + SB]
                b_i = b[s0:s0 + SB]
                acc = jnp.zeros((SB, C), F32)
                if i > 0:
                    p_i = b[s0 - 1:s0, :]
                    qt = (q_i * jnp.exp2(b_i - p_i)).astype(BF16)
                    kt = (k * jnp.exp2(p_i - b)).astype(BF16)
                    off = lax.dot_general(qt, kt, nt, preferred_element_type=F32)
                    acc = jnp.where(lane < s0, off, 0.0)
                for s in range(SB):
                    b_s = b[s0 + s:s0 + s + 1, :]
                    k_s = k[s0 + s:s0 + s + 1, :]
                    e = jnp.exp2(b_i - b_s)
                    col = jnp.sum(q_i * (k_s * e), axis=1, keepdims=True)
                    acc = jnp.where((lane == s0 + s) & (trow >= s), col, acc)
                blocks.append(acc)
            a_mat = jnp.concatenate(blocks, axis=0)
            o = inter + jnp.dot(a_mat.astype(BF16), v, preferred_element_type=F32)

            k_dec = (k * jnp.exp2(b_last - b)).astype(BF16)
            s_ref[h] = st * jnp.exp2(b_last) + lax.dot_general(
                v, k_dec, tn, preferred_element_type=F32)

            on = o * lax.rsqrt(jnp.mean(o * o, axis=-1, keepdims=True) + EPS) * ng_ref[...]
            r = r_ref[rows, hv]
            o_ref[rows, hv] = (on * (r * jax.nn.sigmoid(r))).astype(o_ref.dtype)
        return carry

    lax.fori_loop(0, n_chunks, chunk_body, 0)


def _gla(z, gz, gate_w, gate_b, norm_g):
    nb = SEQ // SEQ_TB
    row = lambda b, i: b * nb + i
    return pl.pallas_call(
        _gla_kernel,
        grid=(BATCH, nb),
        in_specs=[
            pl.BlockSpec((SEQ_TB, GLA_DK), lambda b, i: (row(b, i), 0)),
            pl.BlockSpec((SEQ_TB, GLA_DK), lambda b, i: (row(b, i), 1)),
            pl.BlockSpec((SEQ_TB, GLA_DV), lambda b, i: (row(b, i), 1)),
            pl.BlockSpec((SEQ_TB, GLA_DV), lambda b, i: (row(b, i), 2)),
            pl.BlockSpec((SEQ_TB, LANES), lambda b, i: (row(b, i), 0)),
            pl.BlockSpec((LANES, GLA_DK), lambda b, i: (0, 0)),
            pl.BlockSpec((1, GLA_DK), lambda b, i: (0, 0)),
            pl.BlockSpec((1, GLA_HEAD_V), lambda b, i: (0, 0)),
        ],
        out_specs=pl.BlockSpec((SEQ_TB, GLA_DV), lambda b, i: (row(b, i), 0)),
        out_shape=jax.ShapeDtypeStruct((TOKENS, GLA_DV), BF16),
        scratch_shapes=[pltpu.VMEM((SEQ_TB, GLA_DK), F32),
                        pltpu.VMEM((GLA_HEADS, GLA_HEAD_V, GLA_HEAD_K), F32)],
        compiler_params=_cparams(("parallel", "arbitrary")),
        name="gla",
    )(z, z, z, z, gz, gate_w, gate_b, norm_g)


def _conv_kernel(ca_ref, cb_ref, ha_ref, hb_ref, w_ref, wb_ref, lg_ref, lb_ref, o_ref,
                 sh_ref, y_ref):
    H = CONV_HALO
    n_sh = SUBLANES
    keep = (pl.program_id(1) > 0).astype(F32)

    u_main = ca_ref[...] * jax.nn.sigmoid(cb_ref[...])
    u_halo = ha_ref[...] * jax.nn.sigmoid(hb_ref[...]) * keep
    for r in range(n_sh):
        sh_ref[r, 0:H - r, :] = u_halo[r:H, :]
        sh_ref[r, H - r:H - r + SEQ_TB, :] = u_main

    first = H - (CONV_WIDTH - 1)
    ct_w = 256

    def row_body(rc, carry):
        t0 = pl.multiple_of(rc * CONV_RC, CONV_RC)
        for ct in range(CONV_CH // ct_w):
            cs = slice(ct * ct_w, (ct + 1) * ct_w)
            acc = jnp.broadcast_to(wb_ref[:, cs], (CONV_RC, ct_w))
            for j in range(CONV_WIDTH):
                off = first + j
                r, base = off % n_sh, off - off % n_sh
                acc = acc + w_ref[j:j + 1, cs] * sh_ref[r, pl.ds(t0 + base, CONV_RC), cs]
            y_ref[pl.ds(t0, CONV_RC), cs] = acc
        return carry

    lax.fori_loop(0, SEQ_TB // CONV_RC, row_body, 0)

    def ln_body(rc, carry):
        sl = pl.ds(pl.multiple_of(rc * 64, 64), 64)
        y = y_ref[sl, :]
        mu = jnp.mean(y, axis=-1, keepdims=True)
        var = jnp.mean(jnp.square(y - mu), axis=-1, keepdims=True)
        t = (y - mu) * lax.rsqrt(var + EPS) * lg_ref[...] + lb_ref[...]
        o_ref[sl, :] = (t * jax.nn.sigmoid(t)).astype(o_ref.dtype)
        return carry

    lax.fori_loop(0, SEQ_TB // 64, ln_body, 0)


def _conv(z, w, wb, ln_g, ln_b):
    nb = SEQ // SEQ_TB
    hb = SEQ_TB // CONV_HALO
    ca_col = (2 * GLA_DK + 2 * GLA_DV) // CONV_CH
    row = lambda b, i: b * nb + i
    halo = lambda b, i: jnp.maximum(row(b, i) * hb - 1, 0)
    return pl.pallas_call(
        _conv_kernel,
        grid=(BATCH, nb),
        in_specs=[
            pl.BlockSpec((SEQ_TB, CONV_CH), lambda b, i: (row(b, i), ca_col)),
            pl.BlockSpec((SEQ_TB, CONV_CH), lambda b, i: (row(b, i), ca_col + 1)),
            pl.BlockSpec((CONV_HALO, CONV_CH), lambda b, i: (halo(b, i), ca_col)),
            pl.BlockSpec((CONV_HALO, CONV_CH), lambda b, i: (halo(b, i), ca_col + 1)),
            pl.BlockSpec((CONV_WIDTH, CONV_CH), lambda b, i: (0, 0)),
            pl.BlockSpec((1, CONV_CH), lambda b, i: (0, 0)),
            pl.BlockSpec((1, CONV_CH), lambda b, i: (0, 0)),
            pl.BlockSpec((1, CONV_CH), lambda b, i: (0, 0)),
        ],
        out_specs=pl.BlockSpec((SEQ_TB, CONV_CH), lambda b, i: (row(b, i), 0)),
        out_shape=jax.ShapeDtypeStruct((TOKENS, CONV_CH), BF16),
        scratch_shapes=[pltpu.VMEM((SUBLANES, SEQ_TB + CONV_HALO, CONV_CH), F32),
                        pltpu.VMEM((SEQ_TB, CONV_CH), F32)],
        compiler_params=_cparams(("parallel", "parallel")),
        name="conv",
    )(z, z, z, z, w, wb, ln_g, ln_b)


ATT_HALF = ATT_TQ // 2


def _attn_kernel(q_ref, k0_ref, k1_ref, k2_ref, v0_ref, v1_ref, v2_ref,
                 m0_ref, ta_ref, tb_ref, o_ref):
    i = pl.program_id(1)
    pen = [jnp.where(i >= 2, 0.0, NEG_BIG), jnp.where(i >= 1, 0.0, NEG_BIG), None]
    k_refs = (k0_ref, k1_ref, k2_ref)
    v_refs = (v0_ref, v1_ref, v2_ref)
    nt = (((1,), (1,)), ((), ()))
    tn = (((0,), (0,)), ((), ()))
    H = ATT_HALF

    def scores(h):
        hs = slice(h * ATT_HEAD_DIM, (h + 1) * ATT_HEAD_DIM)
        q = q_ref[:, hs]
        tiles = {}
        for j in range(3):
            st = lax.dot_general(k_refs[j][:, hs], q, nt, preferred_element_type=F32)
            for kh in range(2):
                for a in range(2):
                    n = 2 * j + kh - a
                    if n < 0 or n > 4:
                        continue
                    t = st[kh * H:(kh + 1) * H, a * H:(a + 1) * H]
                    if n == 0:
                        t = t + m0_ref[...]
                    elif n == 3:
                        t = t + ta_ref[h]
                    elif n == 4:
                        t = t + tb_ref[h]
                    tiles[(j, kh, a)] = t
        maxima = []
        for a in range(2):
            mx = None
            for key in tiles:
                if key[2] == a:
                    cur = jnp.max(tiles[key], axis=0, keepdims=True)
                    if pen[key[0]] is not None:
                        cur = cur + pen[key[0]]
                    mx = cur if mx is None else jnp.maximum(mx, cur)
            maxima.append(mx)
        return tiles, maxima

    def finish(h, tiles, maxima):
        hs = slice(h * ATT_HEAD_DIM, (h + 1) * ATT_HEAD_DIM)
        probs = {}
        inv_l = []
        for a in range(2):
            mine = [key for key in tiles if key[2] == a]
            shift = [maxima[a] if pj is None else maxima[a] - pj for pj in pen]
            tot = None
            for key in mine:
                p = jnp.exp2(tiles[key] - shift[key[0]])
                probs[key] = p.astype(BF16)
                cur = jnp.sum(p, axis=0, keepdims=True)
                tot = cur if tot is None else tot + cur
            inv_l.append(1.0 / tot)

        zero = jnp.zeros((H, H), BF16)
        ot = None
        for j in range(3):
            pt = jnp.concatenate(
                [jnp.concatenate([probs.get((j, kh, a), zero) for a in range(2)], axis=1)
                 for kh in range(2)], axis=0)
            cur = lax.dot_general(v_refs[j][:, hs], pt, tn, preferred_element_type=F32)
            ot = cur if ot is None else ot + cur
        ot = ot * jnp.concatenate(inv_l, axis=1)
        o_ref[:, hs] = ot.T.astype(o_ref.dtype)

    pending = scores(0)
    for h in range(ATT_HEADS):
        current = pending
        if h + 1 < ATT_HEADS:
            pending = scores(h + 1)
        finish(h, *current)


def _attention(qkv, m0, ta, tb):
    nb = SEQ // ATT_TQ
    row = lambda b, i: b * nb + i
    back = lambda d: (lambda b, i: (b * nb + jnp.maximum(i - d, 0)))
    spec = lambda rowfn, col: pl.BlockSpec((ATT_TQ, D_MODEL), lambda b, i: (rowfn(b, i), col))
    table = pl.BlockSpec((ATT_HEADS, ATT_HALF, ATT_HALF), lambda b, i: (0, 0, 0))
    return pl.pallas_call(
        _attn_kernel,
        grid=(BATCH, nb),
        in_specs=[
            spec(row, 0),
            spec(back(2), 1), spec(back(1), 1), spec(row, 1),
            spec(back(2), 2), spec(back(1), 2), spec(row, 2),
            pl.BlockSpec((ATT_HALF, ATT_HALF), lambda b, i: (0, 0)),
            table, table,
        ],
        out_specs=pl.BlockSpec((ATT_TQ, D_MODEL), lambda b, i: (row(b, i), 0)),
        out_shape=jax.ShapeDtypeStruct((TOKENS, D_MODEL), BF16),
        compiler_params=_cparams(("parallel", "parallel")),
        name="attention",
    )(qkv, qkv, qkv, qkv, qkv, qkv, qkv, m0, ta, tb)


def _attention_bias_tables(rel_bias):
    H = ATT_HALF
    assert H == REL_CLIP
    rb = rel_bias.astype(F32)
    rel = (rb - rb[:, 2 * REL_CLIP:]) * LOG2E

    def toeplitz(g):
        flat = jnp.tile(g, (1, H))[:, :H * (2 * H - 1)]
        return flat.reshape(-1, H, 2 * H - 1)[:, :, :H]

    ta = toeplitz(jnp.concatenate([jnp.zeros((ATT_HEADS, H), F32),
                                   rel[:, REL_CLIP:2 * REL_CLIP]], axis=1))
    tb = toeplitz(jnp.concatenate([rel[:, REL_CLIP:2 * REL_CLIP], rel[:, :REL_CLIP]], axis=1))
    kc = jnp.arange(H)[:, None] // ATT_CHUNK
    qc = jnp.arange(H)[None, :] // ATT_CHUNK
    tb = jnp.where((kc <= qc)[None], tb, NEG_BIG)
    m0 = jnp.where(kc >= qc, 0.0, NEG_BIG).astype(F32)
    return m0, ta, tb


def kernel(x, p, ffn_norm, ffn_w_gate, ffn_w_up, ffn_w_down, mix_norm, ab_w_in, gla_gate_w, gla_gate_b, gla_norm_g, conv_dw, conv_dw_b, conv_ln_g, conv_ln_b, ab_w_out, att_w_qkv, att_rel_bias, att_w_o, pl_norm, pl_w_gate, pl_w_proj, final_norm):
    xs = x.reshape(TOKENS, D_MODEL)
    ps = p.reshape(DEPTH, TOKENS, D_PL)
    row = lambda a: a.reshape(1, -1).astype(F32)

    pl_wg_all = pl_w_gate.astype(BF16)
    pl_wp_all = pl_w_proj.astype(BF16)

    def ffn(xs, i, s):
        return _ffn(xs, row(ffn_norm[i, s]), ffn_w_gate, ffn_w_up, ffn_w_down, i, s)

    for i in range(DEPTH):
        e = i // 2
        xs = ffn(xs, i, 0)
        if i % 2 == 0:
            w_in = ab_w_in[e]
            gz_lo = 2 * GLA_DK + 2 * GLA_DV
            gz_hi = gz_lo + GLA_GATE_RANK
            w_qkvr = w_in[:, :gz_lo].astype(BF16)
            w_conv = w_in[:, gz_hi:].astype(BF16)
            w_gz = jnp.pad(w_in[:, gz_lo:gz_hi], ((0, 0), (0, LANES - GLA_GATE_RANK))).astype(BF16)
            z_all, gz = _rms_proj(
                xs, row(mix_norm[i]),
                [(w_qkvr, gz_lo // MM_TN), (w_conv, 2 * CONV_CH // MM_TN)],
                jnp.ones((1, AB_MAIN), F32), F32, w_extra=w_gz)
            gate_w = jnp.pad(gla_gate_w[e], ((0, LANES - GLA_GATE_RANK), (0, 0))).astype(BF16)
            a_out = _gla(z_all, gz, gate_w, row(gla_gate_b[e]), row(gla_norm_g[e]))
            b_out = _conv(z_all, conv_dw[e].astype(F32), row(conv_dw_b[e]),
                          row(conv_ln_g[e]), row(conv_ln_b[e]))
            xs = _out_proj([a_out, b_out], ab_w_out[e].astype(BF16), xs)
        else:
            colscale = jnp.concatenate([jnp.full((1, D_MODEL), ATT_HEAD_DIM ** -0.5 * LOG2E, F32),
                                        jnp.ones((1, 2 * D_MODEL), F32)], axis=1)
            qkv = _rms_proj(xs, row(mix_norm[i]), [(att_w_qkv[e], 3 * D_MODEL // MM_TN)],
                            colscale, BF16)
            m0, ta, tb = _attention_bias_tables(att_rel_bias[e])
            o = _attention(qkv, m0, ta, tb)
            xs = _out_proj([o], att_w_o[e].astype(BF16), xs)
        xs = ffn(xs, i, 1)
        xs = _pl_embed(xs, ps, row(pl_norm[i]), pl_wg_all, pl_wp_all, row(final_norm),
                       i, i == DEPTH - 1)
    return xs.reshape(BATCH, SEQ, D_MODEL)
```

```python
import functools

import jax
import jax.numpy as jnp
from jax import lax
from jax.experimental import pallas as pl
from jax.experimental.pallas import tpu as pltpu

F32 = jnp.float32
BF16 = jnp.bfloat16

D_MODEL = 2048
BATCH = 4
SEQ = 2048
DEPTH = 2
TOKENS = BATCH * SEQ
D_PL = 256
D_FF = 5632
EPS = 1e-6

GLA_HEADS = 4
GLA_DK = 512
GLA_DV = 1024
GLA_HEAD_K = 128
GLA_HEAD_V = 256
GLA_GATE_RANK = 16
GLA_GATE_TAU = 16.0
GLA_CHUNK = 64
GLA_SUB = 16
GLA_SCALED_KEY_MAX = 2.0 ** 40
GLA_CHUNKS_PER_STEP = 2
CONV_CH = 1024
CONV_WIDTH = 31
AB_MAIN = 2 * GLA_DK + 2 * GLA_DV + 2 * CONV_CH
ATT_HEADS = 16
ATT_HEAD_DIM = 128
ATT_CHUNK = 64
LEFT_CHUNKS = 8
REL_CLIP = 128
NEG_BIG = -1e30
LOG2E = 1.4426950408889634

V7X_VMEM_BYTES = 64 * 1024 * 1024
LANES = 128
SUBLANES = 8
VMEM_LIMIT = V7X_VMEM_BYTES - 4 * 1024 * 1024

FFN_TM = 1024
FFN_TF = 512
FFN_TF_FIRST = 256
FFN_TN = 512
MM_TM = 1024
MM_TN = 1024
PL_TM = 256
OP_TM = 512
SEQ_TB = 512
CONV_HALO = 32
CONV_RC = 128
ATT_TQ = 256
RMS_ROWS = 256


def _cparams(sem):
    return pltpu.CompilerParams(dimension_semantics=sem, vmem_limit_bytes=VMEM_LIMIT)


def _rms_rows_to(dst_ref, x_ref, g_ref, rows):
    def body(c, carry):
        sl = pl.ds(pl.multiple_of(c * RMS_ROWS, RMS_ROWS), RMS_ROWS)
        x = x_ref[sl, :]
        ms = jnp.mean(x * x, axis=-1, keepdims=True)
        dst_ref[sl, :] = (x * lax.rsqrt(ms + EPS) * g_ref[...]).astype(dst_ref.dtype)
        return carry
    lax.fori_loop(0, rows // RMS_ROWS, body, 0)


def _ffn_kernel(emit_bf16, *refs):
    if emit_bf16:
        x_ref, g_ref, wg_ref, wu_ref, wd_ref, o_ref, wgb_ref, wub_ref, wdb_ref, h_ref = refs
        wgb_ref[...] = wg_ref[...].astype(BF16)
        wub_ref[...] = wu_ref[...].astype(BF16)
        wdb_ref[...] = wd_ref[...].astype(BF16)
        wg_ref, wu_ref, wd_ref = wgb_ref, wub_ref, wdb_ref
    else:
        x_ref, g_ref, wg_ref, wu_ref, wd_ref, _, o_ref, h_ref = refs
    f = pl.program_id(1)

    @pl.when(f == 0)
    def _():
        _rms_rows_to(h_ref, x_ref, g_ref, FFN_TM)
        o_ref[...] = x_ref[...]

    h = h_ref[...]
    gate = jnp.dot(h, wg_ref[...], preferred_element_type=F32)
    up = jnp.dot(h, wu_ref[...], preferred_element_type=F32)
    a = (0.5 * gate * jax.nn.sigmoid(gate) * up).astype(BF16)

    for n in range(D_MODEL // FFN_TN):
        cs = slice(n * FFN_TN, (n + 1) * FFN_TN)
        o_ref[:, cs] += jnp.dot(a, wd_ref[:, cs], preferred_element_type=F32)


def _ffn(x, g, wg, wu, wd, layer, half):
    m = x.shape[0]
    scratch = [pltpu.VMEM((FFN_TM, D_MODEL), BF16)]
    params = _cparams(("parallel", "arbitrary"))
    tf = FFN_TF_FIRST
    first, wgb, wub, wdb = pl.pallas_call(
        functools.partial(_ffn_kernel, True),
        grid=(1, D_FF // tf),
        in_specs=[
            pl.BlockSpec((FFN_TM, D_MODEL), lambda i, f: (0, 0), pipeline_mode=pl.Buffered(1)),
            pl.BlockSpec((1, D_MODEL), lambda i, f: (0, 0)),
            pl.BlockSpec((None, None, D_MODEL, tf), lambda i, f: (layer, half, 0, f)),
            pl.BlockSpec((None, None, D_MODEL, tf), lambda i, f: (layer, half, 0, f)),
            pl.BlockSpec((None, None, tf, D_MODEL), lambda i, f: (layer, half, f, 0)),
        ],
        out_specs=[
            pl.BlockSpec((FFN_TM, D_MODEL), lambda i, f: (0, 0)),
            pl.BlockSpec((D_MODEL, tf), lambda i, f: (0, f)),
            pl.BlockSpec((D_MODEL, tf), lambda i, f: (0, f)),
            pl.BlockSpec((tf, D_MODEL), lambda i, f: (f, 0)),
        ],
        out_shape=[
            jax.ShapeDtypeStruct((m, D_MODEL), F32),
            jax.ShapeDtypeStruct((D_MODEL, D_FF), BF16),
            jax.ShapeDtypeStruct((D_MODEL, D_FF), BF16),
            jax.ShapeDtypeStruct((D_FF, D_MODEL), BF16),
        ],
        scratch_shapes=scratch,
        compiler_params=params,
        name="ffn_first",
    )(x, g, wg, wu, wd)
    tf = FFN_TF
    return pl.pallas_call(
        functools.partial(_ffn_kernel, False),
        grid=(m // FFN_TM - 1, D_FF // tf),
        in_specs=[
            pl.BlockSpec((FFN_TM, D_MODEL), lambda i, f: (i + 1, 0)),
            pl.BlockSpec((1, D_MODEL), lambda i, f: (0, 0)),
            pl.BlockSpec((D_MODEL, tf), lambda i, f: (0, f)),
            pl.BlockSpec((D_MODEL, tf), lambda i, f: (0, f)),
            pl.BlockSpec((tf, D_MODEL), lambda i, f: (f, 0)),
            pl.BlockSpec(memory_space=pl.ANY),
        ],
        out_specs=pl.BlockSpec((FFN_TM, D_MODEL), lambda i, f: (i + 1, 0)),
        out_shape=jax.ShapeDtypeStruct((m, D_MODEL), F32),
        input_output_aliases={5: 0},
        scratch_shapes=scratch,
        compiler_params=params,
        name="ffn_rest",
    )(x, g, wgb, wub, wdb, first)


def _rms_proj_kernel(seg_tiles, has_extra, *refs):
    n_seg = len(seg_tiles)
    x_ref, g_ref = refs[:2]
    w_refs = refs[2:2 + n_seg]
    cs_ref = refs[2 + n_seg]
    if has_extra:
        we_ref, o_ref, oe_ref, h_ref = refs[3 + n_seg:]
    else:
        o_ref, h_ref = refs[3 + n_seg:]
    j = pl.program_id(1)

    @pl.when(j == 0)
    def _():
        _rms_rows_to(h_ref, x_ref, g_ref, MM_TM)
        if has_extra:
            oe_ref[...] = jnp.dot(h_ref[...], we_ref[...].astype(BF16),
                                  preferred_element_type=F32)

    start = 0
    for w_ref, tiles in zip(w_refs, seg_tiles):
        @pl.when((j >= start) & (j < start + tiles))
        def _(w_ref=w_ref):
            acc = jnp.dot(h_ref[...], w_ref[...].astype(BF16), preferred_element_type=F32)
            o_ref[...] = (acc * cs_ref[...]).astype(o_ref.dtype)
        start += tiles


def _rms_proj(x, g, w_list, colscale, out_dtype, w_extra=None):
    m = x.shape[0]
    seg_tiles = tuple(t for _, t in w_list)
    n_tiles = sum(seg_tiles)
    has_extra = w_extra is not None
    in_specs = [
        pl.BlockSpec((MM_TM, D_MODEL), lambda i, j: (i, 0)),
        pl.BlockSpec((1, D_MODEL), lambda i, j: (0, 0)),
    ]
    start = 0
    for _, tiles in w_list:
        in_specs.append(pl.BlockSpec(
            (D_MODEL, MM_TN),
            lambda i, j, s=start, t=tiles: (0, jnp.clip(j - s, 0, t - 1))))
        start += tiles
    in_specs.append(pl.BlockSpec((1, MM_TN), lambda i, j: (0, j)))
    out_specs = pl.BlockSpec((MM_TM, MM_TN), lambda i, j: (i, j))
    out_shape = jax.ShapeDtypeStruct((m, n_tiles * MM_TN), out_dtype)
    args = [x, g] + [w for w, _ in w_list] + [colscale]
    if has_extra:
        in_specs.append(pl.BlockSpec((D_MODEL, LANES), lambda i, j: (0, 0)))
        out_specs = [out_specs, pl.BlockSpec((MM_TM, LANES), lambda i, j: (i, 0))]
        out_shape = [out_shape, jax.ShapeDtypeStruct((m, LANES), F32)]
        args.append(w_extra)
    return pl.pallas_call(
        functools.partial(_rms_proj_kernel, seg_tiles, has_extra),
        grid=(m // MM_TM, n_tiles),
        in_specs=in_specs,
        out_specs=out_specs,
        out_shape=out_shape,
        scratch_shapes=[pltpu.VMEM((MM_TM, D_MODEL), BF16)],
        compiler_params=_cparams(("parallel", "arbitrary")),
        name="rms_proj",
    )(*args)


def _out_proj_kernel(n_lhs, *refs):
    lhs_refs = refs[:n_lhs]
    w_refs = refs[n_lhs:2 * n_lhs]
    x_ref, o_ref = refs[2 * n_lhs], refs[2 * n_lhs + 1]
    acc = x_ref[...]
    for a_ref, w_ref in zip(lhs_refs, w_refs):
        acc = acc + jnp.dot(a_ref[...], w_ref[...].astype(BF16), preferred_element_type=F32)
    o_ref[...] = acc


def _out_proj(lhs_list, w, x):
    m = x.shape[0]
    n_lhs = len(lhs_list)
    kw = lhs_list[0].shape[1]
    in_specs = [pl.BlockSpec((OP_TM, kw), lambda i: (i, 0)) for _ in lhs_list]
    in_specs += [pl.BlockSpec((kw, D_MODEL), lambda i, t=t: (t, 0)) for t in range(n_lhs)]
    in_specs += [pl.BlockSpec((OP_TM, D_MODEL), lambda i: (i, 0))]
    return pl.pallas_call(
        functools.partial(_out_proj_kernel, n_lhs),
        grid=(m // OP_TM,),
        in_specs=in_specs,
        out_specs=pl.BlockSpec((OP_TM, D_MODEL), lambda i: (i, 0)),
        out_shape=jax.ShapeDtypeStruct((m, D_MODEL), F32),
        compiler_params=_cparams(("parallel",)),
        name="out_proj",
    )(*lhs_list, *([w] * n_lhs), x)


def _pl_embed_kernel(final, x_ref, p_ref, g_ref, wg_ref, wp_ref, fg_ref, o_ref, h_ref):
    _rms_rows_to(h_ref, x_ref, g_ref, PL_TM)
    gate = jax.nn.sigmoid(jnp.dot(h_ref[...], wg_ref[...], preferred_element_type=F32))
    proj = jnp.dot(p_ref[...].astype(BF16), wp_ref[...], preferred_element_type=F32)
    y = x_ref[...] + gate * proj
    if final:
        ms = jnp.mean(y * y, axis=-1, keepdims=True)
        y = y * lax.rsqrt(ms + EPS) * fg_ref[...]
    o_ref[...] = y


def _pl_embed(x, p, g, wg, wp, final_g, layer, final):
    m = x.shape[0]
    return pl.pallas_call(
        functools.partial(_pl_embed_kernel, final),
        grid=(m // PL_TM,),
        in_specs=[
            pl.BlockSpec((PL_TM, D_MODEL), lambda i: (i, 0)),
            pl.BlockSpec((None, PL_TM, D_PL), lambda i: (layer, i, 0)),
            pl.BlockSpec((1, D_MODEL), lambda i: (0, 0)),
            pl.BlockSpec((None, D_MODEL, D_MODEL), lambda i: (layer, 0, 0)),
            pl.BlockSpec((None, D_PL, D_MODEL), lambda i: (layer, 0, 0)),
            pl.BlockSpec((1, D_MODEL), lambda i: (0, 0)),
        ],
        out_specs=pl.BlockSpec((PL_TM, D_MODEL), lambda i: (i, 0)),
        out_shape=jax.ShapeDtypeStruct((m, D_MODEL), F32),
        scratch_shapes=[pltpu.VMEM((PL_TM, D_MODEL), BF16)],
        compiler_params=_cparams(("parallel",)),
        name="pl_embed",
    )(x, p, g, wg, wp, final_g)


def _gla_kernel(q_ref, k_ref, v_ref, r_ref, gz_ref, gw_ref, gb_ref, ng_ref, o_ref,
                b_ref, kh_ref, s_ref):
    C, SB = GLA_CHUNK, GLA_SUB
    n_chunks = SEQ_TB // C

    @pl.when(pl.program_id(1) == 0)
    def _():
        s_ref[...] = jnp.zeros_like(s_ref)

    lin = jnp.dot(gz_ref[...].astype(BF16), gw_ref[...], preferred_element_type=F32) + gb_ref[...]
    log_a = -(jnp.maximum(-lin, 0.0) + jnp.log1p(jnp.exp(-jnp.abs(lin)))) * (LOG2E / GLA_GATE_TAU)
    tri = (lax.broadcasted_iota(jnp.int32, (C, C), 0)
           >= lax.broadcasted_iota(jnp.int32, (C, C), 1)).astype(F32)
    for c in range(n_chunks):
        b_ref[c * C:(c + 1) * C, :] = jnp.dot(
            tri, log_a[c * C:(c + 1) * C, :], preferred_element_type=F32,
            precision=lax.Precision.HIGHEST)
    kh_ref[...] = k_ref[...].astype(F32) * jnp.exp2(-b_ref[...])
    scaled_ok = jnp.max(jnp.abs(kh_ref[...])) <= GLA_SCALED_KEY_MAX

    lane = lax.broadcasted_iota(jnp.int32, (SB, C), 1)
    trow = lax.broadcasted_iota(jnp.int32, (SB, C), 0)
    causal = (lax.broadcasted_iota(jnp.int32, (C, C), 0)
              >= lax.broadcasted_iota(jnp.int32, (C, C), 1))
    nt = (((1,), (1,)), ((), ()))
    tn = (((0,), (0,)), ((), ()))

    def guarded_products(q, k, b):
        blocks = []
        for i in range(C // SB):
            s0 = i * SB
            q_i = q[s0:s0 + SB]
            b_i = b[s0:s0 + SB]
            acc = jnp.zeros((SB, C), F32)
            if i > 0:
                p_i = b[s0 - 1:s0, :]
                qt = (q_i * jnp.exp2(b_i - p_i)).astype(BF16)
                kt = (k * jnp.exp2(p_i - b)).astype(BF16)
                off = lax.dot_general(qt, kt, nt, preferred_element_type=F32)
                acc = jnp.where(lane < s0, off, 0.0)
            for s in range(SB):
                b_s = b[s0 + s:s0 + s + 1, :]
                k_s = k[s0 + s:s0 + s + 1, :]
                e = jnp.exp2(b_i - b_s)
                col = jnp.sum(q_i * (k_s * e), axis=1, keepdims=True)
                acc = jnp.where((lane == s0 + s) & (trow >= s), col, acc)
            blocks.append(acc)
        return jnp.concatenate(blocks, axis=0)

    def chunk_body(scaled, per_step, it, carry):
        units = []
        for u in range(per_step):
            rows = pl.ds(pl.multiple_of((it * per_step + u) * C, C), C)
            for h in range(GLA_HEADS):
                units.append((u, h, rows))
        hk = lambda h: slice(h * GLA_HEAD_K, (h + 1) * GLA_HEAD_K)
        hv = lambda h: slice(h * GLA_HEAD_V, (h + 1) * GLA_HEAD_V)

        part = {}
        for u, h, rows in units:
            q = q_ref[rows, hk(h)].astype(F32) * (GLA_HEAD_K ** -0.5)
            v = v_ref[rows, hv(h)].astype(BF16)
            b = b_ref[rows, hk(h)]
            b_last = b[C - 1:C, :]
            qb = (q * jnp.exp2(b)).astype(BF16)
            if scaled:
                kh = kh_ref[rows, hk(h)]
                a_mat = jnp.where(causal, lax.dot_general(qb, kh.astype(BF16), nt,
                                                          preferred_element_type=F32), 0.0)
                k_dec = (kh * jnp.exp2(b_last)).astype(BF16)
            else:
                k = k_ref[rows, hk(h)].astype(F32)
                a_mat = guarded_products(q, k, b)
                k_dec = (k * jnp.exp2(b_last - b)).astype(BF16)
            upd = lax.dot_general(v, k_dec, tn, preferred_element_type=F32)
            local = jnp.dot(a_mat.astype(BF16), v, preferred_element_type=F32)
            part[(u, h)] = (qb, jnp.exp2(b_last), upd, local)

        state = {}
        for h in range(GLA_HEADS):
            st = s_ref[h]
            for u in range(per_step):
                state[(u, h)] = st
                _, decay, upd, _ = part[(u, h)]
                st = st * decay + upd
            s_ref[h] = st

        for u, h, rows in units:
            qb, _, _, local = part[(u, h)]
            o = local + lax.dot_general(qb, state[(u, h)].astype(BF16), nt,
                                        preferred_element_type=F32)
            on = o * lax.rsqrt(jnp.mean(o * o, axis=-1, keepdims=True) + EPS) * ng_ref[...]
            r = r_ref[rows, hv(h)].astype(F32)
            o_ref[rows, hv(h)] = (on * (r * jax.nn.sigmoid(r))).astype(o_ref.dtype)
        return carry

    def run(scaled):
        per_step = GLA_CHUNKS_PER_STEP if scaled else 1
        lax.fori_loop(0, n_chunks // per_step,
                      functools.partial(chunk_body, scaled, per_step), 0)

    lax.cond(scaled_ok, lambda: run(True), lambda: run(False))


def _gla(z, gz, gate_w, gate_b, norm_g):
    nb = SEQ // SEQ_TB
    row = lambda b, i: b * nb + i
    return pl.pallas_call(
        _gla_kernel,
        grid=(BATCH, nb),
        in_specs=[
            pl.BlockSpec((SEQ_TB, GLA_DK), lambda b, i: (row(b, i), 0)),
            pl.BlockSpec((SEQ_TB, GLA_DK), lambda b, i: (row(b, i), 1)),
            pl.BlockSpec((SEQ_TB, GLA_DV), lambda b, i: (row(b, i), 1)),
            pl.BlockSpec((SEQ_TB, GLA_DV), lambda b, i: (row(b, i), 2)),
            pl.BlockSpec((SEQ_TB, LANES), lambda b, i: (row(b, i), 0)),
            pl.BlockSpec((LANES, GLA_DK), lambda b, i: (0, 0)),
            pl.BlockSpec((1, GLA_DK), lambda b, i: (0, 0)),
            pl.BlockSpec((1, GLA_HEAD_V), lambda b, i: (0, 0)),
        ],
        out_specs=pl.BlockSpec((SEQ_TB, GLA_DV), lambda b, i: (row(b, i), 0)),
        out_shape=jax.ShapeDtypeStruct((TOKENS, GLA_DV), BF16),
        scratch_shapes=[pltpu.VMEM((SEQ_TB, GLA_DK), F32),
                        pltpu.VMEM((SEQ_TB, GLA_DK), F32),
                        pltpu.VMEM((GLA_HEADS, GLA_HEAD_V, GLA_HEAD_K), F32)],
        compiler_params=_cparams(("parallel", "arbitrary")),
        name="gla",
    )(z, z, z, z, gz, gate_w, gate_b, norm_g)


def _conv_kernel(ca_ref, cb_ref, ha_ref, hb_ref, w_ref, wb_ref, lg_ref, lb_ref, o_ref,
                 sh_ref, y_ref):
    H = CONV_HALO
    n_sh = SUBLANES
    keep = (pl.program_id(1) > 0).astype(F32)

    u_main = ca_ref[...].astype(F32) * jax.nn.sigmoid(cb_ref[...].astype(F32))
    u_halo = ha_ref[...].astype(F32) * jax.nn.sigmoid(hb_ref[...].astype(F32)) * keep
    for r in range(n_sh):
        sh_ref[r, 0:H - r, :] = u_halo[r:H, :]
        sh_ref[r, H - r:H - r + SEQ_TB, :] = u_main

    first = H - (CONV_WIDTH - 1)
    ct_w = 128

    groups = CONV_RC // SUBLANES

    def row_body(cs, bias, rc, carry):
        t0 = pl.multiple_of(rc * CONV_RC, CONV_RC)
        accs = [bias, None]
        for r in range(n_sh):
            offs = [o for o in range(first, first + CONV_WIDTH) if o % n_sh == r]
            lo, hi = offs[0] - r, offs[-1] - r
            slab = sh_ref[r, pl.ds(t0 + lo, CONV_RC + hi - lo), cs]
            for off in offs:
                a = off - r - lo
                win = slab[a:a + CONV_RC].reshape(groups, SUBLANES, ct_w)
                term = w_ref[off - first, :, cs][None] * win
                accs[r % 2] = term if accs[r % 2] is None else accs[r % 2] + term
        y_ref[pl.ds(t0, CONV_RC), cs] = (accs[0] + accs[1]).reshape(CONV_RC, ct_w)
        return carry

    for ct in range(CONV_CH // ct_w):
        cs = slice(ct * ct_w, (ct + 1) * ct_w)
        bias = jnp.broadcast_to(wb_ref[:, cs][None], (groups, SUBLANES, ct_w))
        lax.fori_loop(0, SEQ_TB // CONV_RC, functools.partial(row_body, cs, bias), 0)

    def ln_body(rc, carry):
        sl = pl.ds(pl.multiple_of(rc * 64, 64), 64)
        y = y_ref[sl, :]
        mu = jnp.mean(y, axis=-1, keepdims=True)
        var = jnp.mean(jnp.square(y - mu), axis=-1, keepdims=True)
        t = (y - mu) * lax.rsqrt(var + EPS) * lg_ref[...] + lb_ref[...]
        o_ref[sl, :] = (t * jax.nn.sigmoid(t)).astype(o_ref.dtype)
        return carry

    lax.fori_loop(0, SEQ_TB // 64, ln_body, 0)


def _conv(z, w, wb, ln_g, ln_b):
    nb = SEQ // SEQ_TB
    hb = SEQ_TB // CONV_HALO
    ca_col = (2 * GLA_DK + 2 * GLA_DV) // CONV_CH
    row = lambda b, i: b * nb + i
    halo = lambda b, i: jnp.maximum(row(b, i) * hb - 1, 0)
    return pl.pallas_call(
        _conv_kernel,
        grid=(BATCH, nb),
        in_specs=[
            pl.BlockSpec((SEQ_TB, CONV_CH), lambda b, i: (row(b, i), ca_col)),
            pl.BlockSpec((SEQ_TB, CONV_CH), lambda b, i: (row(b, i), ca_col + 1)),
            pl.BlockSpec((CONV_HALO, CONV_CH), lambda b, i: (halo(b, i), ca_col)),
            pl.BlockSpec((CONV_HALO, CONV_CH), lambda b, i: (halo(b, i), ca_col + 1)),
            pl.BlockSpec((CONV_WIDTH, SUBLANES, CONV_CH), lambda b, i: (0, 0, 0)),
            pl.BlockSpec((1, CONV_CH), lambda b, i: (0, 0)),
            pl.BlockSpec((1, CONV_CH), lambda b, i: (0, 0)),
            pl.BlockSpec((1, CONV_CH), lambda b, i: (0, 0)),
        ],
        out_specs=pl.BlockSpec((SEQ_TB, CONV_CH), lambda b, i: (row(b, i), 0)),
        out_shape=jax.ShapeDtypeStruct((TOKENS, CONV_CH), BF16),
        scratch_shapes=[pltpu.VMEM((SUBLANES, SEQ_TB + CONV_HALO, CONV_CH), F32),
                        pltpu.VMEM((SEQ_TB, CONV_CH), F32)],
        compiler_params=_cparams(("parallel", "parallel")),
        name="conv",
    )(z, z, z, z, w, wb, ln_g, ln_b)


ATT_HALF = ATT_TQ // 2


def _attn_kernel(q_ref, k0_ref, k1_ref, k2_ref, v0_ref, v1_ref, v2_ref,
                 m0_ref, ta_ref, tb_ref, o_ref):
    i = pl.program_id(1)
    pen = [jnp.where(i >= 2, 0.0, NEG_BIG), jnp.where(i >= 1, 0.0, NEG_BIG), None]
    k_refs = (k0_ref, k1_ref, k2_ref)
    v_refs = (v0_ref, v1_ref, v2_ref)
    nt = (((1,), (1,)), ((), ()))
    tn = (((0,), (0,)), ((), ()))
    H = ATT_HALF

    def scores(h):
        hs = slice(h * ATT_HEAD_DIM, (h + 1) * ATT_HEAD_DIM)
        q = q_ref[:, hs]
        tiles = {}
        for j in range(3):
            st = lax.dot_general(k_refs[j][:, hs], q, nt, preferred_element_type=F32)
            for kh in range(2):
                for a in range(2):
                    n = 2 * j + kh - a
                    if n < 0 or n > 4:
                        continue
                    t = st[kh * H:(kh + 1) * H, a * H:(a + 1) * H]
                    if n == 0:
                        t = t + m0_ref[...]
                    elif n == 3:
                        t = t + ta_ref[h]
                    elif n == 4:
                        t = t + tb_ref[h]
                    tiles[(j, kh, a)] = t
        maxima = []
        for a in range(2):
            mx = None
            for key in tiles:
                if key[2] == a:
                    cur = jnp.max(tiles[key], axis=0, keepdims=True)
                    if pen[key[0]] is not None:
                        cur = cur + pen[key[0]]
                    mx = cur if mx is None else jnp.maximum(mx, cur)
            maxima.append(mx)
        return tiles, maxima

    def finish(h, tiles, maxima):
        hs = slice(h * ATT_HEAD_DIM, (h + 1) * ATT_HEAD_DIM)
        probs = {}
        inv_l = []
        for a in range(2):
            mine = [key for key in tiles if key[2] == a]
            shift = [maxima[a] if pj is None else maxima[a] - pj for pj in pen]
            tot = None
            for key in mine:
                p = jnp.exp2(tiles[key] - shift[key[0]])
                probs[key] = p.astype(BF16)
                cur = jnp.sum(p, axis=0, keepdims=True)
                tot = cur if tot is None else tot + cur
            inv_l.append(1.0 / tot)

        zero = jnp.zeros((H, H), BF16)
        ot = None
        for j in range(3):
            pt = jnp.concatenate(
                [jnp.concatenate([probs.get((j, kh, a), zero) for a in range(2)], axis=1)
                 for kh in range(2)], axis=0)
            cur = lax.dot_general(v_refs[j][:, hs], pt, tn, preferred_element_type=F32)
            ot = cur if ot is None else ot + cur
        ot = ot * jnp.concatenate(inv_l, axis=1)
        o_ref[:, hs] = ot.T.astype(o_ref.dtype)

    pending = scores(0)
    for h in range(ATT_HEADS):
        current = pending
        if h + 1 < ATT_HEADS:
            pending = scores(h + 1)
        finish(h, *current)


def _attention(qkv, m0, ta, tb):
    nb = SEQ // ATT_TQ
    row = lambda b, i: b * nb + i
    back = lambda d: (lambda b, i: (b * nb + jnp.maximum(i - d, 0)))
    spec = lambda rowfn, col: pl.BlockSpec((ATT_TQ, D_MODEL), lambda b, i: (rowfn(b, i), col))
    table = pl.BlockSpec((ATT_HEADS, ATT_HALF, ATT_HALF), lambda b, i: (0, 0, 0))
    return pl.pallas_call(
        _attn_kernel,
        grid=(BATCH, nb),
        in_specs=[
            spec(row, 0),
            spec(back(2), 1), spec(back(1), 1), spec(row, 1),
            spec(back(2), 2), spec(back(1), 2), spec(row, 2),
            pl.BlockSpec((ATT_HALF, ATT_HALF), lambda b, i: (0, 0)),
            table, table,
        ],
        out_specs=pl.BlockSpec((ATT_TQ, D_MODEL), lambda b, i: (row(b, i), 0)),
        out_shape=jax.ShapeDtypeStruct((TOKENS, D_MODEL), BF16),
        compiler_params=_cparams(("parallel", "parallel")),
        name="attention",
    )(qkv, qkv, qkv, qkv, qkv, qkv, qkv, m0, ta, tb)


def _attention_bias_tables(rel_bias):
    H = ATT_HALF
    assert H == REL_CLIP
    rb = rel_bias.astype(F32)
    rel = (rb - rb[:, 2 * REL_CLIP:]) * LOG2E

    def toeplitz(g):
        flat = jnp.tile(g, (1, H))[:, :H * (2 * H - 1)]
        return flat.reshape(-1, H, 2 * H - 1)[:, :, :H]

    ta = toeplitz(jnp.concatenate([jnp.zeros((ATT_HEADS, H), F32),
                                   rel[:, REL_CLIP:2 * REL_CLIP]], axis=1))
    tb = toeplitz(jnp.concatenate([rel[:, REL_CLIP:2 * REL_CLIP], rel[:, :REL_CLIP]], axis=1))
    kc = jnp.arange(H)[:, None] // ATT_CHUNK
    qc = jnp.arange(H)[None, :] // ATT_CHUNK
    tb = jnp.where((kc <= qc)[None], tb, NEG_BIG)
    m0 = jnp.where(kc >= qc, 0.0, NEG_BIG).astype(F32)
    return m0, ta, tb


def kernel(x, p, ffn_norm, ffn_w_gate, ffn_w_up, ffn_w_down, mix_norm, ab_w_in, gla_gate_w, gla_gate_b, gla_norm_g, conv_dw, conv_dw_b, conv_ln_g, conv_ln_b, ab_w_out, att_w_qkv, att_rel_bias, att_w_o, pl_norm, pl_w_gate, pl_w_proj, final_norm):
    xs = x.reshape(TOKENS, D_MODEL)
    ps = p.reshape(DEPTH, TOKENS, D_PL)
    row = lambda a: a.reshape(1, -1).astype(F32)

    pl_wg_all = pl_w_gate.astype(BF16)
    pl_wp_all = pl_w_proj.astype(BF16)

    def ffn(xs, i, s):
        return _ffn(xs, row(ffn_norm[i, s]), ffn_w_gate, ffn_w_up, ffn_w_down, i, s)

    for i in range(DEPTH):
        e = i // 2
        xs = ffn(xs, i, 0)
        if i % 2 == 0:
            w_in = ab_w_in[e]
            gz_lo = 2 * GLA_DK + 2 * GLA_DV
            gz_hi = gz_lo + GLA_GATE_RANK
            w_qkvr = w_in[:, :gz_lo].astype(BF16)
            w_conv = w_in[:, gz_hi:].astype(BF16)
            w_gz = jnp.pad(w_in[:, gz_lo:gz_hi], ((0, 0), (0, LANES - GLA_GATE_RANK))).astype(BF16)
            z_all, gz = _rms_proj(
                xs, row(mix_norm[i]),
                [(w_qkvr, gz_lo // MM_TN), (w_conv, 2 * CONV_CH // MM_TN)],
                jnp.ones((1, AB_MAIN), F32), BF16, w_extra=w_gz)
            gate_w = jnp.pad(gla_gate_w[e], ((0, LANES - GLA_GATE_RANK), (0, 0))).astype(BF16)
            a_out = _gla(z_all, gz, gate_w, row(gla_gate_b[e]), row(gla_norm_g[e]))
            taps = jnp.broadcast_to(conv_dw[e].astype(F32)[:, None, :],
                                    (CONV_WIDTH, SUBLANES, CONV_CH))
            b_out = _conv(z_all, taps, row(conv_dw_b[e]), row(conv_ln_g[e]), row(conv_ln_b[e]))
            xs = _out_proj([a_out, b_out], ab_w_out[e].astype(BF16), xs)
        else:
            colscale = jnp.concatenate([jnp.full((1, D_MODEL), ATT_HEAD_DIM ** -0.5 * LOG2E, F32),
                                        jnp.ones((1, 2 * D_MODEL), F32)], axis=1)
            qkv = _rms_proj(xs, row(mix_norm[i]), [(att_w_qkv[e], 3 * D_MODEL // MM_TN)],
                            colscale, BF16)
            m0, ta, tb = _attention_bias_tables(att_rel_bias[e])
            o = _attention(qkv, m0, ta, tb)
            xs = _out_proj([o], att_w_o[e].astype(BF16), xs)
        xs = ffn(xs, i, 1)
        xs = _pl_embed(xs, ps, row(pl_norm[i]), pl_wg_all, pl_wp_all, row(final_norm),
                       i, i == DEPTH - 1)
    return xs.reshape(BATCH, SEQ, D_MODEL)
```

```python
import functools

import jax
import jax.numpy as jnp
from jax import lax
from jax.experimental import pallas as pl
from jax.experimental.pallas import tpu as pltpu

F32 = jnp.float32
BF16 = jnp.bfloat16

D_MODEL = 2048
BATCH = 4
SEQ = 2048
DEPTH = 2
TOKENS = BATCH * SEQ
D_PL = 256
D_FF = 5632
EPS = 1e-6

GLA_HEADS = 4
GLA_DK = 512
GLA_DV = 1024
GLA_HEAD_K = 128
GLA_HEAD_V = 256
GLA_GATE_RANK = 16
GLA_GATE_TAU = 16.0
GLA_CHUNK = 64
GLA_SUB = 16
GLA_SCALED_KEY_MAX = 2.0 ** 40
GLA_CHUNKS_PER_STEP = 2
CONV_CH = 1024
CONV_WIDTH = 31
AB_MAIN = 2 * GLA_DK + 2 * GLA_DV + 2 * CONV_CH
ATT_HEADS = 16
ATT_HEAD_DIM = 128
ATT_CHUNK = 64
LEFT_CHUNKS = 8
REL_CLIP = 128
NEG_BIG = -1e30
LOG2E = 1.4426950408889634

V7X_VMEM_BYTES = 64 * 1024 * 1024
LANES = 128
SUBLANES = 8
VMEM_LIMIT = V7X_VMEM_BYTES - 4 * 1024 * 1024

FFN_TM = 1024
FFN_TF = 512
FFN_TF_FIRST = 256
FFN_TN = 512
MM_TM = 1024
MM_TN = 1024
PL_TM = 512
OP_TM = 512
SEQ_TB = 512
CONV_HALO = 32
CONV_RC = 128
ATT_TQ = 256
RMS_ROWS = 256


def _cparams(sem):
    return pltpu.CompilerParams(dimension_semantics=sem, vmem_limit_bytes=VMEM_LIMIT)


def _rms_rows_to(dst_ref, x_ref, g_ref, rows):
    def body(c, carry):
        sl = pl.ds(pl.multiple_of(c * RMS_ROWS, RMS_ROWS), RMS_ROWS)
        x = x_ref[sl, :]
        ms = jnp.mean(x * x, axis=-1, keepdims=True)
        dst_ref[sl, :] = (x * lax.rsqrt(ms + EPS) * g_ref[...]).astype(dst_ref.dtype)
        return carry
    lax.fori_loop(0, rows // RMS_ROWS, body, 0)


def _ffn_step(f, x_ref, g_ref, wg_ref, wu_ref, wd_ref, o_ref, h_ref):
    @pl.when(f == 0)
    def _():
        _rms_rows_to(h_ref, x_ref, g_ref, FFN_TM)
        o_ref[...] = x_ref[...]

    h = h_ref[...]
    gate = jnp.dot(h, wg_ref[...], preferred_element_type=F32)
    up = jnp.dot(h, wu_ref[...], preferred_element_type=F32)
    a = (0.5 * gate * jax.nn.sigmoid(gate) * up).astype(BF16)

    for n in range(D_MODEL // FFN_TN):
        cs = slice(n * FFN_TN, (n + 1) * FFN_TN)
        o_ref[:, cs] += jnp.dot(a, wd_ref[:, cs], preferred_element_type=F32)


def _ffn_first_kernel(x_ref, g_ref, wg_ref, wu_ref, wd_ref, o_ref, wgb_ref, wub_ref, wdb_ref,
                      h_ref):
    wgb_ref[...] = wg_ref[...].astype(BF16)
    wub_ref[...] = wu_ref[...].astype(BF16)
    wdb_ref[...] = wd_ref[...].astype(BF16)
    _ffn_step(pl.program_id(1), x_ref, g_ref, wgb_ref, wub_ref, wdb_ref, o_ref, h_ref)


def _ffn_rest_kernel(n_f, x_ref, g_ref, wg_ref, wu_ref, wd_ref, first_ref, o_ref, h_ref, sem):
    s = pl.program_id(0)

    @pl.when(s == 0)
    def _():
        copy = pltpu.make_async_copy(first_ref, o_ref, sem)
        copy.start()
        copy.wait()

    @pl.when(s > 0)
    def _():
        _ffn_step((s - 1) % n_f, x_ref, g_ref, wg_ref, wu_ref, wd_ref, o_ref, h_ref)


def _ffn(x, g, wg, wu, wd, layer, half):
    m = x.shape[0]
    scratch = [pltpu.VMEM((FFN_TM, D_MODEL), BF16)]
    tf = FFN_TF_FIRST
    first, wgb, wub, wdb = pl.pallas_call(
        _ffn_first_kernel,
        grid=(1, D_FF // tf),
        in_specs=[
            pl.BlockSpec((FFN_TM, D_MODEL), lambda i, f: (0, 0), pipeline_mode=pl.Buffered(1)),
            pl.BlockSpec((1, D_MODEL), lambda i, f: (0, 0)),
            pl.BlockSpec((None, None, D_MODEL, tf), lambda i, f: (layer, half, 0, f)),
            pl.BlockSpec((None, None, D_MODEL, tf), lambda i, f: (layer, half, 0, f)),
            pl.BlockSpec((None, None, tf, D_MODEL), lambda i, f: (layer, half, f, 0)),
        ],
        out_specs=[
            pl.BlockSpec((FFN_TM, D_MODEL), lambda i, f: (0, 0)),
            pl.BlockSpec((D_MODEL, tf), lambda i, f: (0, f)),
            pl.BlockSpec((D_MODEL, tf), lambda i, f: (0, f)),
            pl.BlockSpec((tf, D_MODEL), lambda i, f: (f, 0)),
        ],
        out_shape=[
            jax.ShapeDtypeStruct((FFN_TM, D_MODEL), F32),
            jax.ShapeDtypeStruct((D_MODEL, D_FF), BF16),
            jax.ShapeDtypeStruct((D_MODEL, D_FF), BF16),
            jax.ShapeDtypeStruct((D_FF, D_MODEL), BF16),
        ],
        scratch_shapes=scratch,
        compiler_params=_cparams(("parallel", "arbitrary")),
        name="ffn_first",
    )(x, g, wg, wu, wd)
    tf = FFN_TF
    n_f = D_FF // tf
    tile = lambda s: jnp.where(s == 0, 0, 1 + (s - 1) // n_f)
    f_of = lambda s: jnp.where(s == 0, 0, (s - 1) % n_f)
    return pl.pallas_call(
        functools.partial(_ffn_rest_kernel, n_f),
        grid=(1 + (m // FFN_TM - 1) * n_f,),
        in_specs=[
            pl.BlockSpec((FFN_TM, D_MODEL), lambda s: (jnp.maximum(tile(s), 1), 0)),
            pl.BlockSpec((1, D_MODEL), lambda s: (0, 0)),
            pl.BlockSpec((D_MODEL, tf), lambda s: (0, f_of(s))),
            pl.BlockSpec((D_MODEL, tf), lambda s: (0, f_of(s))),
            pl.BlockSpec((tf, D_MODEL), lambda s: (f_of(s), 0)),
            pl.BlockSpec(memory_space=pl.ANY),
        ],
        out_specs=pl.BlockSpec((FFN_TM, D_MODEL), lambda s: (tile(s), 0)),
        out_shape=jax.ShapeDtypeStruct((m, D_MODEL), F32),
        scratch_shapes=scratch + [pltpu.SemaphoreType.DMA(())],
        compiler_params=_cparams(("arbitrary",)),
        name="ffn_rest",
    )(x, g, wgb, wub, wdb, first)


def _rms_proj_kernel(seg_tiles, has_extra, *refs):
    n_seg = len(seg_tiles)
    x_ref, g_ref = refs[:2]
    w_refs = refs[2:2 + n_seg]
    cs_ref = refs[2 + n_seg]
    if has_extra:
        we_ref, o_ref, oe_ref, h_ref = refs[3 + n_seg:]
    else:
        o_ref, h_ref = refs[3 + n_seg:]
    j = pl.program_id(1)

    @pl.when(j == 0)
    def _():
        _rms_rows_to(h_ref, x_ref, g_ref, MM_TM)
        if has_extra:
            oe_ref[...] = jnp.dot(h_ref[...], we_ref[...].astype(BF16),
                                  preferred_element_type=F32)

    start = 0
    for w_ref, tiles in zip(w_refs, seg_tiles):
        @pl.when((j >= start) & (j < start + tiles))
        def _(w_ref=w_ref):
            acc = jnp.dot(h_ref[...], w_ref[...].astype(BF16), preferred_element_type=F32)
            o_ref[...] = (acc * cs_ref[...]).astype(o_ref.dtype)
        start += tiles


def _rms_proj(x, g, w_list, colscale, out_dtype, w_extra=None):
    m = x.shape[0]
    seg_tiles = tuple(t for _, t in w_list)
    n_tiles = sum(seg_tiles)
    has_extra = w_extra is not None
    in_specs = [
        pl.BlockSpec((MM_TM, D_MODEL), lambda i, j: (i, 0)),
        pl.BlockSpec((1, D_MODEL), lambda i, j: (0, 0)),
    ]
    start = 0
    for _, tiles in w_list:
        in_specs.append(pl.BlockSpec(
            (D_MODEL, MM_TN),
            lambda i, j, s=start, t=tiles: (0, jnp.clip(j - s, 0, t - 1))))
        start += tiles
    in_specs.append(pl.BlockSpec((1, MM_TN), lambda i, j: (0, j)))
    out_specs = pl.BlockSpec((MM_TM, MM_TN), lambda i, j: (i, j))
    out_shape = jax.ShapeDtypeStruct((m, n_tiles * MM_TN), out_dtype)
    args = [x, g] + [w for w, _ in w_list] + [colscale]
    if has_extra:
        in_specs.append(pl.BlockSpec((D_MODEL, LANES), lambda i, j: (0, 0)))
        out_specs = [out_specs, pl.BlockSpec((MM_TM, LANES), lambda i, j: (i, 0))]
        out_shape = [out_shape, jax.ShapeDtypeStruct((m, LANES), F32)]
        args.append(w_extra)
    return pl.pallas_call(
        functools.partial(_rms_proj_kernel, seg_tiles, has_extra),
        grid=(m // MM_TM, n_tiles),
        in_specs=in_specs,
        out_specs=out_specs,
        out_shape=out_shape,
        scratch_shapes=[pltpu.VMEM((MM_TM, D_MODEL), BF16)],
        compiler_params=_cparams(("parallel", "arbitrary")),
        name="rms_proj",
    )(*args)


def _out_proj_kernel(n_lhs, *refs):
    lhs_refs = refs[:n_lhs]
    w_refs = refs[n_lhs:2 * n_lhs]
    x_ref, o_ref = refs[2 * n_lhs], refs[2 * n_lhs + 1]
    acc = x_ref[...]
    for a_ref, w_ref in zip(lhs_refs, w_refs):
        acc = acc + jnp.dot(a_ref[...], w_ref[...].astype(BF16), preferred_element_type=F32)
    o_ref[...] = acc


def _out_proj(lhs_list, w, x):
    m = x.shape[0]
    n_lhs = len(lhs_list)
    kw = lhs_list[0].shape[1]
    in_specs = [pl.BlockSpec((OP_TM, kw), lambda i: (i, 0)) for _ in lhs_list]
    in_specs += [pl.BlockSpec((kw, D_MODEL), lambda i, t=t: (t, 0)) for t in range(n_lhs)]
    in_specs += [pl.BlockSpec((OP_TM, D_MODEL), lambda i: (i, 0))]
    return pl.pallas_call(
        functools.partial(_out_proj_kernel, n_lhs),
        grid=(m // OP_TM,),
        in_specs=in_specs,
        out_specs=pl.BlockSpec((OP_TM, D_MODEL), lambda i: (i, 0)),
        out_shape=jax.ShapeDtypeStruct((m, D_MODEL), F32),
        compiler_params=_cparams(("parallel",)),
        name="out_proj",
    )(*lhs_list, *([w] * n_lhs), x)


def _pl_embed_kernel(final, x_ref, p_ref, g_ref, wg_ref, wp_ref, fg_ref, o_ref, h_ref):
    _rms_rows_to(h_ref, x_ref, g_ref, PL_TM)
    gate = jax.nn.sigmoid(jnp.dot(h_ref[...], wg_ref[...], preferred_element_type=F32))
    proj = jnp.dot(p_ref[...].astype(BF16), wp_ref[...], preferred_element_type=F32)
    y = x_ref[...] + gate * proj
    if final:
        ms = jnp.mean(y * y, axis=-1, keepdims=True)
        y = y * lax.rsqrt(ms + EPS) * fg_ref[...]
    o_ref[...] = y


def _pl_embed(x, p, g, wg, wp, final_g, layer, final):
    m = x.shape[0]
    return pl.pallas_call(
        functools.partial(_pl_embed_kernel, final),
        grid=(m // PL_TM,),
        in_specs=[
            pl.BlockSpec((PL_TM, D_MODEL), lambda i: (i, 0)),
            pl.BlockSpec((None, PL_TM, D_PL), lambda i: (layer, i, 0)),
            pl.BlockSpec((1, D_MODEL), lambda i: (0, 0)),
            pl.BlockSpec((None, D_MODEL, D_MODEL), lambda i: (layer, 0, 0),
                         pipeline_mode=pl.Buffered(1)),
            pl.BlockSpec((None, D_PL, D_MODEL), lambda i: (layer, 0, 0),
                         pipeline_mode=pl.Buffered(1)),
            pl.BlockSpec((1, D_MODEL), lambda i: (0, 0)),
        ],
        out_specs=pl.BlockSpec((PL_TM, D_MODEL), lambda i: (i, 0)),
        out_shape=jax.ShapeDtypeStruct((m, D_MODEL), F32),
        scratch_shapes=[pltpu.VMEM((PL_TM, D_MODEL), BF16)],
        compiler_params=_cparams(("parallel",)),
        name="pl_embed",
    )(x, p, g, wg, wp, final_g)


def _gla_kernel(q_ref, k_ref, v_ref, r_ref, gz_ref, gw_ref, gb_ref, ng_ref, o_ref,
                b_ref, kh_ref, s_ref):
    C, SB = GLA_CHUNK, GLA_SUB
    n_chunks = SEQ_TB // C

    @pl.when(pl.program_id(1) == 0)
    def _():
        s_ref[...] = jnp.zeros_like(s_ref)

    lin = jnp.dot(gz_ref[...].astype(BF16), gw_ref[...], preferred_element_type=F32) + gb_ref[...]
    log_a = -(jnp.maximum(-lin, 0.0) + jnp.log1p(jnp.exp(-jnp.abs(lin)))) * (LOG2E / GLA_GATE_TAU)
    tri = (lax.broadcasted_iota(jnp.int32, (C, C), 0)
           >= lax.broadcasted_iota(jnp.int32, (C, C), 1)).astype(F32)
    for c in range(n_chunks):
        b_ref[c * C:(c + 1) * C, :] = jnp.dot(
            tri, log_a[c * C:(c + 1) * C, :], preferred_element_type=F32,
            precision=lax.Precision.HIGHEST)
    kh_ref[...] = k_ref[...].astype(F32) * jnp.exp2(-b_ref[...])
    scaled_ok = jnp.max(jnp.abs(kh_ref[...])) <= GLA_SCALED_KEY_MAX

    lane = lax.broadcasted_iota(jnp.int32, (SB, C), 1)
    trow = lax.broadcasted_iota(jnp.int32, (SB, C), 0)
    causal = (lax.broadcasted_iota(jnp.int32, (C, C), 0)
              >= lax.broadcasted_iota(jnp.int32, (C, C), 1))
    nt = (((1,), (1,)), ((), ()))
    tn = (((0,), (0,)), ((), ()))

    def guarded_products(q, k, b):
        blocks = []
        for i in range(C // SB):
            s0 = i * SB
            q_i = q[s0:s0 + SB]
            b_i = b[s0:s0 + SB]
            acc = jnp.zeros((SB, C), F32)
            if i > 0:
                p_i = b[s0 - 1:s0, :]
                qt = (q_i * jnp.exp2(b_i - p_i)).astype(BF16)
                kt = (k * jnp.exp2(p_i - b)).astype(BF16)
                off = lax.dot_general(qt, kt, nt, preferred_element_type=F32)
                acc = jnp.where(lane < s0, off, 0.0)
            for s in range(SB):
                b_s = b[s0 + s:s0 + s + 1, :]
                k_s = k[s0 + s:s0 + s + 1, :]
                e = jnp.exp2(b_i - b_s)
                col = jnp.sum(q_i * (k_s * e), axis=1, keepdims=True)
                acc = jnp.where((lane == s0 + s) & (trow >= s), col, acc)
            blocks.append(acc)
        return jnp.concatenate(blocks, axis=0)

    def chunk_body(scaled, per_step, it, carry):
        units = []
        for u in range(per_step):
            rows = pl.ds(pl.multiple_of((it * per_step + u) * C, C), C)
            for h in range(GLA_HEADS):
                units.append((u, h, rows))
        hk = lambda h: slice(h * GLA_HEAD_K, (h + 1) * GLA_HEAD_K)
        hv = lambda h: slice(h * GLA_HEAD_V, (h + 1) * GLA_HEAD_V)

        part = {}
        for u, h, rows in units:
            q = q_ref[rows, hk(h)].astype(F32) * (GLA_HEAD_K ** -0.5)
            v = v_ref[rows, hv(h)].astype(BF16)
            b = b_ref[rows, hk(h)]
            b_last = b[C - 1:C, :]
            qb = (q * jnp.exp2(b)).astype(BF16)
            if scaled:
                kh = kh_ref[rows, hk(h)]
                a_mat = jnp.where(causal, lax.dot_general(qb, kh.astype(BF16), nt,
                                                          preferred_element_type=F32), 0.0)
                k_dec = (kh * jnp.exp2(b_last)).astype(BF16)
            else:
                k = k_ref[rows, hk(h)].astype(F32)
                a_mat = guarded_products(q, k, b)
                k_dec = (k * jnp.exp2(b_last - b)).astype(BF16)
            upd = lax.dot_general(v, k_dec, tn, preferred_element_type=F32)
            local = jnp.dot(a_mat.astype(BF16), v, preferred_element_type=F32)
            part[(u, h)] = (qb, jnp.exp2(b_last), upd, local)

        state = {}
        for h in range(GLA_HEADS):
            st = s_ref[h]
            for u in range(per_step):
                state[(u, h)] = st
                _, decay, upd, _ = part[(u, h)]
                st = st * decay + upd
            s_ref[h] = st

        for u, h, rows in units:
            qb, _, _, local = part[(u, h)]
            o = local + lax.dot_general(qb, state[(u, h)].astype(BF16), nt,
                                        preferred_element_type=F32)
            on = o * lax.rsqrt(jnp.mean(o * o, axis=-1, keepdims=True) + EPS) * ng_ref[...]
            r = r_ref[rows, hv(h)].astype(F32)
            o_ref[rows, hv(h)] = (on * (r * jax.nn.sigmoid(r))).astype(o_ref.dtype)
        return carry

    def run(scaled):
        per_step = GLA_CHUNKS_PER_STEP if scaled else 1
        lax.fori_loop(0, n_chunks // per_step,
                      functools.partial(chunk_body, scaled, per_step), 0)

    lax.cond(scaled_ok, lambda: run(True), lambda: run(False))


def _gla(z, gz, gate_w, gate_b, norm_g):
    nb = SEQ // SEQ_TB
    row = lambda b, i: b * nb + i
    return pl.pallas_call(
        _gla_kernel,
        grid=(BATCH, nb),
        in_specs=[
            pl.BlockSpec((SEQ_TB, GLA_DK), lambda b, i: (row(b, i), 0)),
            pl.BlockSpec((SEQ_TB, GLA_DK), lambda b, i: (row(b, i), 1)),
            pl.BlockSpec((SEQ_TB, GLA_DV), lambda b, i: (row(b, i), 1)),
            pl.BlockSpec((SEQ_TB, GLA_DV), lambda b, i: (row(b, i), 2)),
            pl.BlockSpec((SEQ_TB, LANES), lambda b, i: (row(b, i), 0)),
            pl.BlockSpec((LANES, GLA_DK), lambda b, i: (0, 0)),
            pl.BlockSpec((1, GLA_DK), lambda b, i: (0, 0)),
            pl.BlockSpec((1, GLA_HEAD_V), lambda b, i: (0, 0)),
        ],
        out_specs=pl.BlockSpec((SEQ_TB, GLA_DV), lambda b, i: (row(b, i), 0)),
        out_shape=jax.ShapeDtypeStruct((TOKENS, GLA_DV), BF16),
        scratch_shapes=[pltpu.VMEM((SEQ_TB, GLA_DK), F32),
                        pltpu.VMEM((SEQ_TB, GLA_DK), F32),
                        pltpu.VMEM((GLA_HEADS, GLA_HEAD_V, GLA_HEAD_K), F32)],
        compiler_params=_cparams(("parallel", "arbitrary")),
        name="gla",
    )(z, z, z, z, gz, gate_w, gate_b, norm_g)


def _conv_kernel(ca_ref, cb_ref, ha_ref, hb_ref, w_ref, wb_ref, lg_ref, lb_ref, o_ref,
                 sh_ref, y_ref):
    H = CONV_HALO
    n_sh = SUBLANES
    keep = (pl.program_id(1) > 0).astype(F32)

    u_main = ca_ref[...].astype(F32) * jax.nn.sigmoid(cb_ref[...].astype(F32))
    u_halo = ha_ref[...].astype(F32) * jax.nn.sigmoid(hb_ref[...].astype(F32)) * keep
    for r in range(n_sh):
        sh_ref[r, 0:H - r, :] = u_halo[r:H, :]
        sh_ref[r, H - r:H - r + SEQ_TB, :] = u_main

    first = H - (CONV_WIDTH - 1)
    ct_w = 128

    groups = CONV_RC // SUBLANES

    def row_body(cs, bias, rc, carry):
        t0 = pl.multiple_of(rc * CONV_RC, CONV_RC)
        accs = [bias, None]
        for r in range(n_sh):
            offs = [o for o in range(first, first + CONV_WIDTH) if o % n_sh == r]
            lo, hi = offs[0] - r, offs[-1] - r
            slab = sh_ref[r, pl.ds(t0 + lo, CONV_RC + hi - lo), cs]
            for off in offs:
                a = off - r - lo
                win = slab[a:a + CONV_RC].reshape(groups, SUBLANES, ct_w)
                term = w_ref[off - first, :, cs][None] * win
                accs[r % 2] = term if accs[r % 2] is None else accs[r % 2] + term
        y_ref[pl.ds(t0, CONV_RC), cs] = (accs[0] + accs[1]).reshape(CONV_RC, ct_w)
        return carry

    for ct in range(CONV_CH // ct_w):
        cs = slice(ct * ct_w, (ct + 1) * ct_w)
        bias = jnp.broadcast_to(wb_ref[:, cs][None], (groups, SUBLANES, ct_w))
        lax.fori_loop(0, SEQ_TB // CONV_RC, functools.partial(row_body, cs, bias), 0)

    def ln_body(rc, carry):
        sl = pl.ds(pl.multiple_of(rc * 64, 64), 64)
        y = y_ref[sl, :]
        mu = jnp.mean(y, axis=-1, keepdims=True)
        var = jnp.mean(jnp.square(y - mu), axis=-1, keepdims=True)
        t = (y - mu) * lax.rsqrt(var + EPS) * lg_ref[...] + lb_ref[...]
        o_ref[sl, :] = (t * jax.nn.sigmoid(t)).astype(o_ref.dtype)
        return carry

    lax.fori_loop(0, SEQ_TB // 64, ln_body, 0, unroll=2)


def _conv(z, w, wb, ln_g, ln_b):
    nb = SEQ // SEQ_TB
    hb = SEQ_TB // CONV_HALO
    ca_col = (2 * GLA_DK + 2 * GLA_DV) // CONV_CH
    row = lambda b, i: b * nb + i
    halo = lambda b, i: jnp.maximum(row(b, i) * hb - 1, 0)
    return pl.pallas_call(
        _conv_kernel,
        grid=(BATCH, nb),
        in_specs=[
            pl.BlockSpec((SEQ_TB, CONV_CH), lambda b, i: (row(b, i), ca_col)),
            pl.BlockSpec((SEQ_TB, CONV_CH), lambda b, i: (row(b, i), ca_col + 1)),
            pl.BlockSpec((CONV_HALO, CONV_CH), lambda b, i: (halo(b, i), ca_col)),
            pl.BlockSpec((CONV_HALO, CONV_CH), lambda b, i: (halo(b, i), ca_col + 1)),
            pl.BlockSpec((CONV_WIDTH, SUBLANES, CONV_CH), lambda b, i: (0, 0, 0)),
            pl.BlockSpec((1, CONV_CH), lambda b, i: (0, 0)),
            pl.BlockSpec((1, CONV_CH), lambda b, i: (0, 0)),
            pl.BlockSpec((1, CONV_CH), lambda b, i: (0, 0)),
        ],
        out_specs=pl.BlockSpec((SEQ_TB, CONV_CH), lambda b, i: (row(b, i), 0)),
        out_shape=jax.ShapeDtypeStruct((TOKENS, CONV_CH), BF16),
        scratch_shapes=[pltpu.VMEM((SUBLANES, SEQ_TB + CONV_HALO, CONV_CH), F32),
                        pltpu.VMEM((SEQ_TB, CONV_CH), F32)],
        compiler_params=_cparams(("parallel", "parallel")),
        name="conv",
    )(z, z, z, z, w, wb, ln_g, ln_b)


ATT_HALF = ATT_TQ // 2


def _attn_kernel(q_ref, k0_ref, k1_ref, k2_ref, v0_ref, v1_ref, v2_ref,
                 m0_ref, ta_ref, tb_ref, o_ref):
    i = pl.program_id(1)
    pen = [jnp.where(i >= 2, 0.0, NEG_BIG), jnp.where(i >= 1, 0.0, NEG_BIG), None]
    k_refs = (k0_ref, k1_ref, k2_ref)
    v_refs = (v0_ref, v1_ref, v2_ref)
    nt = (((1,), (1,)), ((), ()))
    tn = (((0,), (0,)), ((), ()))
    H = ATT_HALF

    def scores(h):
        hs = slice(h * ATT_HEAD_DIM, (h + 1) * ATT_HEAD_DIM)
        q = q_ref[:, hs]
        tiles = {}
        for j in range(3):
            st = lax.dot_general(k_refs[j][:, hs], q, nt, preferred_element_type=F32)
            for kh in range(2):
                for a in range(2):
                    n = 2 * j + kh - a
                    if n < 0 or n > 4:
                        continue
                    t = st[kh * H:(kh + 1) * H, a * H:(a + 1) * H]
                    if n == 0:
                        t = t + m0_ref[...]
                    elif n == 3:
                        t = t + ta_ref[h]
                    elif n == 4:
                        t = t + tb_ref[h]
                    tiles[(j, kh, a)] = t
        maxima = []
        for a in range(2):
            mx = None
            for key in tiles:
                if key[2] == a:
                    cur = jnp.max(tiles[key], axis=0, keepdims=True)
                    if pen[key[0]] is not None:
                        cur = cur + pen[key[0]]
                    mx = cur if mx is None else jnp.maximum(mx, cur)
            maxima.append(mx)
        return tiles, maxima

    def finish(h, tiles, maxima):
        hs = slice(h * ATT_HEAD_DIM, (h + 1) * ATT_HEAD_DIM)
        probs = {}
        inv_l = []
        for a in range(2):
            mine = [key for key in tiles if key[2] == a]
            shift = [maxima[a] if pj is None else maxima[a] - pj for pj in pen]
            tot = None
            for key in mine:
                p = jnp.exp2(tiles[key] - shift[key[0]])
                probs[key] = p.astype(BF16)
                cur = jnp.sum(p, axis=0, keepdims=True)
                tot = cur if tot is None else tot + cur
            inv_l.append(1.0 / tot)

        zero = jnp.zeros((H, H), BF16)
        ot = None
        for j in range(3):
            pt = jnp.concatenate(
                [jnp.concatenate([probs.get((j, kh, a), zero) for a in range(2)], axis=1)
                 for kh in range(2)], axis=0)
            cur = lax.dot_general(v_refs[j][:, hs], pt, tn, preferred_element_type=F32)
            ot = cur if ot is None else ot + cur
        ot = ot * jnp.concatenate(inv_l, axis=1)
        o_ref[:, hs] = ot.T.astype(o_ref.dtype)

    pending = scores(0)
    for h in range(ATT_HEADS):
        current = pending
        if h + 1 < ATT_HEADS:
            pending = scores(h + 1)
        finish(h, *current)


def _attention(qkv, m0, ta, tb):
    nb = SEQ // ATT_TQ
    row = lambda b, i: b * nb + i
    back = lambda d: (lambda b, i: (b * nb + jnp.maximum(i - d, 0)))
    spec = lambda rowfn, col: pl.BlockSpec((ATT_TQ, D_MODEL), lambda b, i: (rowfn(b, i), col))
    table = pl.BlockSpec((ATT_HEADS, ATT_HALF, ATT_HALF), lambda b, i: (0, 0, 0))
    return pl.pallas_call(
        _attn_kernel,
        grid=(BATCH, nb),
        in_specs=[
            spec(row, 0),
            spec(back(2), 1), spec(back(1), 1), spec(row, 1),
            spec(back(2), 2), spec(back(1), 2), spec(row, 2),
            pl.BlockSpec((ATT_HALF, ATT_HALF), lambda b, i: (0, 0)),
            table, table,
        ],
        out_specs=pl.BlockSpec((ATT_TQ, D_MODEL), lambda b, i: (row(b, i), 0)),
        out_shape=jax.ShapeDtypeStruct((TOKENS, D_MODEL), BF16),
        compiler_params=_cparams(("parallel", "parallel")),
        name="attention",
    )(qkv, qkv, qkv, qkv, qkv, qkv, qkv, m0, ta, tb)


def _attention_bias_tables(rel_bias):
    H = ATT_HALF
    assert H == REL_CLIP
    rb = rel_bias.astype(F32)
    rel = (rb - rb[:, 2 * REL_CLIP:]) * LOG2E

    def toeplitz(g):
        flat = jnp.tile(g, (1, H))[:, :H * (2 * H - 1)]
        return flat.reshape(-1, H, 2 * H - 1)[:, :, :H]

    ta = toeplitz(jnp.concatenate([jnp.zeros((ATT_HEADS, H), F32),
                                   rel[:, REL_CLIP:2 * REL_CLIP]], axis=1))
    tb = toeplitz(jnp.concatenate([rel[:, REL_CLIP:2 * REL_CLIP], rel[:, :REL_CLIP]], axis=1))
    kc = jnp.arange(H)[:, None] // ATT_CHUNK
    qc = jnp.arange(H)[None, :] // ATT_CHUNK
    tb = jnp.where((kc <= qc)[None], tb, NEG_BIG)
    m0 = jnp.where(kc >= qc, 0.0, NEG_BIG).astype(F32)
    return m0, ta, tb


def kernel(x, p, ffn_norm, ffn_w_gate, ffn_w_up, ffn_w_down, mix_norm, ab_w_in, gla_gate_w, gla_gate_b, gla_norm_g, conv_dw, conv_dw_b, conv_ln_g, conv_ln_b, ab_w_out, att_w_qkv, att_rel_bias, att_w_o, pl_norm, pl_w_gate, pl_w_proj, final_norm):
    xs = x.reshape(TOKENS, D_MODEL)
    ps = p.reshape(DEPTH, TOKENS, D_PL)
    row = lambda a: a.reshape(1, -1).astype(F32)

    pl_wg_all = pl_w_gate.astype(BF16)
    pl_wp_all = pl_w_proj.astype(BF16)

    def ffn(xs, i, s):
        return _ffn(xs, row(ffn_norm[i, s]), ffn_w_gate, ffn_w_up, ffn_w_down, i, s)

    for i in range(DEPTH):
        e = i // 2
        xs = ffn(xs, i, 0)
        if i % 2 == 0:
            w_in = ab_w_in[e]
            gz_lo = 2 * GLA_DK + 2 * GLA_DV
            gz_hi = gz_lo + GLA_GATE_RANK
            w_conv = w_in[:, gz_hi:].astype(BF16)
            w_gz = jnp.pad(w_in[:, gz_lo:gz_hi], ((0, 0), (0, LANES - GLA_GATE_RANK))).astype(BF16)
            z_all, gz = _rms_proj(
                xs, row(mix_norm[i]),
                [(w_in, gz_lo // MM_TN), (w_conv, 2 * CONV_CH // MM_TN)],
                jnp.ones((1, AB_MAIN), F32), BF16, w_extra=w_gz)
            gate_w = jnp.pad(gla_gate_w[e], ((0, LANES - GLA_GATE_RANK), (0, 0))).astype(BF16)
            a_out = _gla(z_all, gz, gate_w, row(gla_gate_b[e]), row(gla_norm_g[e]))
            taps = jnp.broadcast_to(conv_dw[e].astype(F32)[:, None, :],
                                    (CONV_WIDTH, SUBLANES, CONV_CH))
            b_out = _conv(z_all, taps, row(conv_dw_b[e]), row(conv_ln_g[e]), row(conv_ln_b[e]))
            xs = _out_proj([a_out, b_out], ab_w_out[e].astype(BF16), xs)
        else:
            colscale = jnp.concatenate([jnp.full((1, D_MODEL), ATT_HEAD_DIM ** -0.5 * LOG2E, F32),
                                        jnp.ones((1, 2 * D_MODEL), F32)], axis=1)
            qkv = _rms_proj(xs, row(mix_norm[i]), [(att_w_qkv[e], 3 * D_MODEL // MM_TN)],
                            colscale, BF16)
            m0, ta, tb = _attention_bias_tables(att_rel_bias[e])
            o = _attention(qkv, m0, ta, tb)
            xs = _out_proj([o], att_w_o[e].astype(BF16), xs)
        xs = ffn(xs, i, 1)
        xs = _pl_embed(xs, ps, row(pl_norm[i]), pl_wg_all, pl_wp_all, row(final_norm),
                       i, i == DEPTH - 1)
    return xs.reshape(BATCH, SEQ, D_MODEL)
```

```python
import functools

import jax
import jax.numpy as jnp
from jax import lax
from jax.experimental import pallas as pl
from jax.experimental.pallas import tpu as pltpu

F32 = jnp.float32
BF16 = jnp.bfloat16

D_MODEL = 2048
BATCH = 4
SEQ = 2048
DEPTH = 2
TOKENS = BATCH * SEQ
D_PL = 256
D_FF = 5632
EPS = 1e-6

GLA_HEADS = 4
GLA_DK = 512
GLA_DV = 1024
GLA_HEAD_K = 128
GLA_HEAD_V = 256
GLA_GATE_RANK = 16
GLA_GATE_TAU = 16.0
GLA_CHUNK = 64
GLA_SUB = 16
GLA_SCALED_KEY_MAX = 2.0 ** 40
GLA_CHUNKS_PER_STEP = 2
CONV_CH = 1024
CONV_WIDTH = 31
AB_MAIN = 2 * GLA_DK + 2 * GLA_DV + 2 * CONV_CH
ATT_HEADS = 16
ATT_HEAD_DIM = 128
ATT_CHUNK = 64
LEFT_CHUNKS = 8
REL_CLIP = 128
NEG_BIG = -1e30
LOG2E = 1.4426950408889634

V7X_VMEM_BYTES = 64 * 1024 * 1024
LANES = 128
SUBLANES = 8
VMEM_LIMIT = V7X_VMEM_BYTES - 4 * 1024 * 1024

FFN_TM = 1024
FFN_TF = 512
FFN_TF_FIRST = 256
FFN_TN = 512
MM_TM = 1024
MM_TN = 1024
PL_TM = 256
OP_TM = 512
SEQ_TB = 512
CONV_HALO = 32
CONV_RC = 128
ATT_TQ = 256
RMS_ROWS = 256


def _cparams(sem):
    return pltpu.CompilerParams(dimension_semantics=sem, vmem_limit_bytes=VMEM_LIMIT)


def _rms_rows_to(dst_ref, x_ref, g_ref, rows):
    def body(c, carry):
        sl = pl.ds(pl.multiple_of(c * RMS_ROWS, RMS_ROWS), RMS_ROWS)
        x = x_ref[sl, :]
        ms = jnp.mean(x * x, axis=-1, keepdims=True)
        dst_ref[sl, :] = (x * lax.rsqrt(ms + EPS) * g_ref[...]).astype(dst_ref.dtype)
        return carry
    lax.fori_loop(0, rows // RMS_ROWS, body, 0)


def _ffn_step(f, x_ref, g_ref, wg_ref, wu_ref, wd_ref, o_ref, h_ref):
    @pl.when(f == 0)
    def _():
        _rms_rows_to(h_ref, x_ref, g_ref, FFN_TM)
        o_ref[...] = x_ref[...]

    h = h_ref[...]
    gate = jnp.dot(h, wg_ref[...], preferred_element_type=F32)
    up = jnp.dot(h, wu_ref[...], preferred_element_type=F32)
    a = (0.5 * gate * jax.nn.sigmoid(gate) * up).astype(BF16)

    for n in range(D_MODEL // FFN_TN):
        cs = slice(n * FFN_TN, (n + 1) * FFN_TN)
        o_ref[:, cs] += jnp.dot(a, wd_ref[:, cs], preferred_element_type=F32)


def _ffn_first_kernel(x_ref, g_ref, wg_ref, wu_ref, wd_ref, o_ref, wgb_ref, wub_ref, wdb_ref,
                      h_ref):
    wgb_ref[...] = wg_ref[...].astype(BF16)
    wub_ref[...] = wu_ref[...].astype(BF16)
    wdb_ref[...] = wd_ref[...].astype(BF16)
    _ffn_step(pl.program_id(1), x_ref, g_ref, wgb_ref, wub_ref, wdb_ref, o_ref, h_ref)


def _ffn_rest_kernel(n_f, x_ref, g_ref, wg_ref, wu_ref, wd_ref, first_ref, o_ref, h_ref, sem):
    s = pl.program_id(0)

    @pl.when(s == 0)
    def _():
        copy = pltpu.make_async_copy(first_ref, o_ref, sem)
        copy.start()
        copy.wait()

    @pl.when(s > 0)
    def _():
        _ffn_step((s - 1) % n_f, x_ref, g_ref, wg_ref, wu_ref, wd_ref, o_ref, h_ref)


def _ffn(x, g, wg, wu, wd, layer, half):
    m = x.shape[0]
    scratch = [pltpu.VMEM((FFN_TM, D_MODEL), BF16)]
    tf = FFN_TF_FIRST
    first, wgb, wub, wdb = pl.pallas_call(
        _ffn_first_kernel,
        grid=(1, D_FF // tf),
        in_specs=[
            pl.BlockSpec((FFN_TM, D_MODEL), lambda i, f: (0, 0), pipeline_mode=pl.Buffered(1)),
            pl.BlockSpec((1, D_MODEL), lambda i, f: (0, 0)),
            pl.BlockSpec((None, None, D_MODEL, tf), lambda i, f: (layer, half, 0, f)),
            pl.BlockSpec((None, None, D_MODEL, tf), lambda i, f: (layer, half, 0, f)),
            pl.BlockSpec((None, None, tf, D_MODEL), lambda i, f: (layer, half, f, 0)),
        ],
        out_specs=[
            pl.BlockSpec((FFN_TM, D_MODEL), lambda i, f: (0, 0)),
            pl.BlockSpec((D_MODEL, tf), lambda i, f: (0, f)),
            pl.BlockSpec((D_MODEL, tf), lambda i, f: (0, f)),
            pl.BlockSpec((tf, D_MODEL), lambda i, f: (f, 0)),
        ],
        out_shape=[
            jax.ShapeDtypeStruct((FFN_TM, D_MODEL), F32),
            jax.ShapeDtypeStruct((D_MODEL, D_FF), BF16),
            jax.ShapeDtypeStruct((D_MODEL, D_FF), BF16),
            jax.ShapeDtypeStruct((D_FF, D_MODEL), BF16),
        ],
        scratch_shapes=scratch,
        compiler_params=_cparams(("parallel", "arbitrary")),
        name="ffn_first",
    )(x, g, wg, wu, wd)
    tf = FFN_TF
    n_f = D_FF // tf
    tile = lambda s: jnp.where(s == 0, 0, 1 + (s - 1) // n_f)
    f_of = lambda s: jnp.where(s == 0, 0, (s - 1) % n_f)
    return pl.pallas_call(
        functools.partial(_ffn_rest_kernel, n_f),
        grid=(1 + (m // FFN_TM - 1) * n_f,),
        in_specs=[
            pl.BlockSpec((FFN_TM, D_MODEL), lambda s: (jnp.maximum(tile(s), 1), 0)),
            pl.BlockSpec((1, D_MODEL), lambda s: (0, 0)),
            pl.BlockSpec((D_MODEL, tf), lambda s: (0, f_of(s))),
            pl.BlockSpec((D_MODEL, tf), lambda s: (0, f_of(s))),
            pl.BlockSpec((tf, D_MODEL), lambda s: (f_of(s), 0)),
            pl.BlockSpec(memory_space=pl.ANY),
        ],
        out_specs=pl.BlockSpec((FFN_TM, D_MODEL), lambda s: (tile(s), 0)),
        out_shape=jax.ShapeDtypeStruct((m, D_MODEL), F32),
        scratch_shapes=scratch + [pltpu.SemaphoreType.DMA(())],
        compiler_params=_cparams(("arbitrary",)),
        name="ffn_rest",
    )(x, g, wgb, wub, wdb, first)


def _rms_proj_kernel(seg_tiles, has_extra, *refs):
    n_seg = len(seg_tiles)
    x_ref, g_ref = refs[:2]
    w_refs = refs[2:2 + n_seg]
    cs_ref = refs[2 + n_seg]
    if has_extra:
        we_ref, o_ref, oe_ref, h_ref = refs[3 + n_seg:]
    else:
        o_ref, h_ref = refs[3 + n_seg:]
    j = pl.program_id(1)

    @pl.when(j == 0)
    def _():
        _rms_rows_to(h_ref, x_ref, g_ref, MM_TM)
        if has_extra:
            oe_ref[...] = jnp.dot(h_ref[...], we_ref[...].astype(BF16),
                                  preferred_element_type=F32)

    start = 0
    for w_ref, tiles in zip(w_refs, seg_tiles):
        @pl.when((j >= start) & (j < start + tiles))
        def _(w_ref=w_ref):
            acc = jnp.dot(h_ref[...], w_ref[...].astype(BF16), preferred_element_type=F32)
            o_ref[...] = (acc * cs_ref[...]).astype(o_ref.dtype)
        start += tiles


def _rms_proj(x, g, w_list, colscale, out_dtype, w_extra=None):
    m = x.shape[0]
    seg_tiles = tuple(t for _, t in w_list)
    n_tiles = sum(seg_tiles)
    has_extra = w_extra is not None
    in_specs = [
        pl.BlockSpec((MM_TM, D_MODEL), lambda i, j: (i, 0)),
        pl.BlockSpec((1, D_MODEL), lambda i, j: (0, 0)),
    ]
    start = 0
    for _, tiles in w_list:
        in_specs.append(pl.BlockSpec(
            (D_MODEL, MM_TN),
            lambda i, j, s=start, t=tiles: (0, jnp.clip(j - s, 0, t - 1))))
        start += tiles
    in_specs.append(pl.BlockSpec((1, MM_TN), lambda i, j: (0, j)))
    out_specs = pl.BlockSpec((MM_TM, MM_TN), lambda i, j: (i, j))
    out_shape = jax.ShapeDtypeStruct((m, n_tiles * MM_TN), out_dtype)
    args = [x, g] + [w for w, _ in w_list] + [colscale]
    if has_extra:
        in_specs.append(pl.BlockSpec((D_MODEL, LANES), lambda i, j: (0, 0)))
        out_specs = [out_specs, pl.BlockSpec((MM_TM, LANES), lambda i, j: (i, 0))]
        out_shape = [out_shape, jax.ShapeDtypeStruct((m, LANES), F32)]
        args.append(w_extra)
    return pl.pallas_call(
        functools.partial(_rms_proj_kernel, seg_tiles, has_extra),
        grid=(m // MM_TM, n_tiles),
        in_specs=in_specs,
        out_specs=out_specs,
        out_shape=out_shape,
        scratch_shapes=[pltpu.VMEM((MM_TM, D_MODEL), BF16)],
        compiler_params=_cparams(("parallel", "arbitrary")),
        name="rms_proj",
    )(*args)


def _out_proj_kernel(n_lhs, *refs):
    lhs_refs = refs[:n_lhs]
    w_refs = refs[n_lhs:2 * n_lhs]
    x_ref, o_ref = refs[2 * n_lhs], refs[2 * n_lhs + 1]
    acc = x_ref[...]
    for a_ref, w_ref in zip(lhs_refs, w_refs):
        acc = acc + jnp.dot(a_ref[...], w_ref[...].astype(BF16), preferred_element_type=F32)
    o_ref[...] = acc


def _out_proj(lhs_list, w, x):
    m = x.shape[0]
    n_lhs = len(lhs_list)
    kw = lhs_list[0].shape[1]
    in_specs = [pl.BlockSpec((OP_TM, kw), lambda i: (i, 0)) for _ in lhs_list]
    in_specs += [pl.BlockSpec((kw, D_MODEL), lambda i, t=t: (t, 0)) for t in range(n_lhs)]
    in_specs += [pl.BlockSpec((OP_TM, D_MODEL), lambda i: (i, 0))]
    return pl.pallas_call(
        functools.partial(_out_proj_kernel, n_lhs),
        grid=(m // OP_TM,),
        in_specs=in_specs,
        out_specs=pl.BlockSpec((OP_TM, D_MODEL), lambda i: (i, 0)),
        out_shape=jax.ShapeDtypeStruct((m, D_MODEL), F32),
        compiler_params=_cparams(("parallel",)),
        name="out_proj",
    )(*lhs_list, *([w] * n_lhs), x)


def _pl_embed_kernel(final, x_ref, p_ref, g_ref, wg_ref, wp_ref, fg_ref, o_ref, h_ref):
    _rms_rows_to(h_ref, x_ref, g_ref, PL_TM)
    gate = jax.nn.sigmoid(jnp.dot(h_ref[...], wg_ref[...], preferred_element_type=F32))
    proj = jnp.dot(p_ref[...].astype(BF16), wp_ref[...], preferred_element_type=F32)
    y = x_ref[...] + gate * proj
    if final:
        ms = jnp.mean(y * y, axis=-1, keepdims=True)
        y = y * lax.rsqrt(ms + EPS) * fg_ref[...]
    o_ref[...] = y


def _pl_embed(x, p, g, wg, wp, final_g, layer, final):
    m = x.shape[0]
    return pl.pallas_call(
        functools.partial(_pl_embed_kernel, final),
        grid=(m // PL_TM,),
        in_specs=[
            pl.BlockSpec((PL_TM, D_MODEL), lambda i: (i, 0)),
            pl.BlockSpec((None, PL_TM, D_PL), lambda i: (layer, i, 0)),
            pl.BlockSpec((1, D_MODEL), lambda i: (0, 0)),
            pl.BlockSpec((None, D_MODEL, D_MODEL), lambda i: (layer, 0, 0),
                         pipeline_mode=pl.Buffered(1)),
            pl.BlockSpec((None, D_PL, D_MODEL), lambda i: (layer, 0, 0),
                         pipeline_mode=pl.Buffered(1)),
            pl.BlockSpec((1, D_MODEL), lambda i: (0, 0)),
        ],
        out_specs=pl.BlockSpec((PL_TM, D_MODEL), lambda i: (i, 0)),
        out_shape=jax.ShapeDtypeStruct((m, D_MODEL), F32),
        scratch_shapes=[pltpu.VMEM((PL_TM, D_MODEL), BF16)],
        compiler_params=_cparams(("parallel",)),
        name="pl_embed",
    )(x, p, g, wg, wp, final_g)


def _gla_kernel(q_ref, k_ref, v_ref, r_ref, gz_ref, gw_ref, gb_ref, ng_ref, o_ref,
                b_ref, kh_ref, s_ref):
    C, SB = GLA_CHUNK, GLA_SUB
    n_chunks = SEQ_TB // C

    @pl.when(pl.program_id(1) == 0)
    def _():
        s_ref[...] = jnp.zeros_like(s_ref)

    lin = jnp.dot(gz_ref[...].astype(BF16), gw_ref[...], preferred_element_type=F32) + gb_ref[...]
    log_a = -(jnp.maximum(-lin, 0.0) + jnp.log1p(jnp.exp(-jnp.abs(lin)))) * (LOG2E / GLA_GATE_TAU)
    tri = (lax.broadcasted_iota(jnp.int32, (C, C), 0)
           >= lax.broadcasted_iota(jnp.int32, (C, C), 1)).astype(F32)
    for c in range(n_chunks):
        b_ref[c * C:(c + 1) * C, :] = jnp.dot(
            tri, log_a[c * C:(c + 1) * C, :], preferred_element_type=F32,
            precision=lax.Precision.HIGHEST)
    kh_ref[...] = k_ref[...].astype(F32) * jnp.exp2(-b_ref[...])
    scaled_ok = jnp.max(jnp.abs(kh_ref[...])) <= GLA_SCALED_KEY_MAX

    lane = lax.broadcasted_iota(jnp.int32, (SB, C), 1)
    trow = lax.broadcasted_iota(jnp.int32, (SB, C), 0)
    causal = (lax.broadcasted_iota(jnp.int32, (C, C), 0)
              >= lax.broadcasted_iota(jnp.int32, (C, C), 1))
    nt = (((1,), (1,)), ((), ()))
    tn = (((0,), (0,)), ((), ()))

    def guarded_products(q, k, b):
        blocks = []
        for i in range(C // SB):
            s0 = i * SB
            q_i = q[s0:s0 + SB]
            b_i = b[s0:s0 + SB]
            acc = jnp.zeros((SB, C), F32)
            if i > 0:
                p_i = b[s0 - 1:s0, :]
                qt = (q_i * jnp.exp2(b_i - p_i)).astype(BF16)
                kt = (k * jnp.exp2(p_i - b)).astype(BF16)
                off = lax.dot_general(qt, kt, nt, preferred_element_type=F32)
                acc = jnp.where(lane < s0, off, 0.0)
            for s in range(SB):
                b_s = b[s0 + s:s0 + s + 1, :]
                k_s = k[s0 + s:s0 + s + 1, :]
                e = jnp.exp2(b_i - b_s)
                col = jnp.sum(q_i * (k_s * e), axis=1, keepdims=True)
                acc = jnp.where((lane == s0 + s) & (trow >= s), col, acc)
            blocks.append(acc)
        return jnp.concatenate(blocks, axis=0)

    def chunk_body(scaled, per_step, it, carry):
        units = []
        for u in range(per_step):
            rows = pl.ds(pl.multiple_of((it * per_step + u) * C, C), C)
            for h in range(GLA_HEADS):
                units.append((u, h, rows))
        hk = lambda h: slice(h * GLA_HEAD_K, (h + 1) * GLA_HEAD_K)
        hv = lambda h: slice(h * GLA_HEAD_V, (h + 1) * GLA_HEAD_V)

        part = {}
        for u, h, rows in units:
            q = q_ref[rows, hk(h)].astype(F32) * (GLA_HEAD_K ** -0.5)
            v = v_ref[rows, hv(h)].astype(BF16)
            b = b_ref[rows, hk(h)]
            b_last = b[C - 1:C, :]
            qb = (q * jnp.exp2(b)).astype(BF16)
            if scaled:
                kh = kh_ref[rows, hk(h)]
                a_mat = jnp.where(causal, lax.dot_general(qb, kh.astype(BF16), nt,
                                                          preferred_element_type=F32), 0.0)
                k_dec = (kh * jnp.exp2(b_last)).astype(BF16)
            else:
                k = k_ref[rows, hk(h)].astype(F32)
                a_mat = guarded_products(q, k, b)
                k_dec = (k * jnp.exp2(b_last - b)).astype(BF16)
            upd = lax.dot_general(v, k_dec, tn, preferred_element_type=F32)
            local = jnp.dot(a_mat.astype(BF16), v, preferred_element_type=F32)
            part[(u, h)] = (qb, jnp.exp2(b_last), upd, local)

        state = {}
        for h in range(GLA_HEADS):
            st = s_ref[h]
            for u in range(per_step):
                state[(u, h)] = st
                _, decay, upd, _ = part[(u, h)]
                st = st * decay + upd
            s_ref[h] = st

        for u, h, rows in units:
            qb, _, _, local = part[(u, h)]
            o = local + lax.dot_general(qb, state[(u, h)].astype(BF16), nt,
                                        preferred_element_type=F32)
            on = o * lax.rsqrt(jnp.mean(o * o, axis=-1, keepdims=True) + EPS) * ng_ref[...]
            r = r_ref[rows, hv(h)].astype(F32)
            o_ref[rows, hv(h)] = (on * (r * jax.nn.sigmoid(r))).astype(o_ref.dtype)
        return carry

    def run(scaled):
        per_step = GLA_CHUNKS_PER_STEP if scaled else 1
        lax.fori_loop(0, n_chunks // per_step,
                      functools.partial(chunk_body, scaled, per_step), 0)

    lax.cond(scaled_ok, lambda: run(True), lambda: run(False))


def _gla(z, gz, gate_w, gate_b, norm_g):
    nb = SEQ // SEQ_TB
    row = lambda b, i: b * nb + i
    return pl.pallas_call(
        _gla_kernel,
        grid=(BATCH, nb),
        in_specs=[
            pl.BlockSpec((SEQ_TB, GLA_DK), lambda b, i: (row(b, i), 0)),
            pl.BlockSpec((SEQ_TB, GLA_DK), lambda b, i: (row(b, i), 1)),
            pl.BlockSpec((SEQ_TB, GLA_DV), lambda b, i: (row(b, i), 1)),
            pl.BlockSpec((SEQ_TB, GLA_DV), lambda b, i: (row(b, i), 2)),
            pl.BlockSpec((SEQ_TB, LANES), lambda b, i: (row(b, i), 0)),
            pl.BlockSpec((LANES, GLA_DK), lambda b, i: (0, 0)),
            pl.BlockSpec((1, GLA_DK), lambda b, i: (0, 0)),
            pl.BlockSpec((1, GLA_HEAD_V), lambda b, i: (0, 0)),
        ],
        out_specs=pl.BlockSpec((SEQ_TB, GLA_DV), lambda b, i: (row(b, i), 0)),
        out_shape=jax.ShapeDtypeStruct((TOKENS, GLA_DV), BF16),
        scratch_shapes=[pltpu.VMEM((SEQ_TB, GLA_DK), F32),
                        pltpu.VMEM((SEQ_TB, GLA_DK), F32),
                        pltpu.VMEM((GLA_HEADS, GLA_HEAD_V, GLA_HEAD_K), F32)],
        compiler_params=_cparams(("parallel", "arbitrary")),
        name="gla",
    )(z, z, z, z, gz, gate_w, gate_b, norm_g)


def _conv_kernel(ca_ref, cb_ref, ha_ref, hb_ref, w_ref, wb_ref, lg_ref, lb_ref, o_ref,
                 sh_ref, y_ref):
    H = CONV_HALO
    n_sh = SUBLANES
    keep = (pl.program_id(1) > 0).astype(F32)

    u_main = ca_ref[...].astype(F32) * jax.nn.sigmoid(cb_ref[...].astype(F32))
    u_halo = ha_ref[...].astype(F32) * jax.nn.sigmoid(hb_ref[...].astype(F32)) * keep
    for r in range(n_sh):
        sh_ref[r, 0:H - r, :] = u_halo[r:H, :]
        sh_ref[r, H - r:H - r + SEQ_TB, :] = u_main

    first = H - (CONV_WIDTH - 1)
    ct_w = 128

    groups = CONV_RC // SUBLANES

    def row_body(cs, bias, rc, carry):
        t0 = pl.multiple_of(rc * CONV_RC, CONV_RC)
        accs = [bias, None]
        for r in range(n_sh):
            offs = [o for o in range(first, first + CONV_WIDTH) if o % n_sh == r]
            lo, hi = offs[0] - r, offs[-1] - r
            slab = sh_ref[r, pl.ds(t0 + lo, CONV_RC + hi - lo), cs]
            for off in offs:
                a = off - r - lo
                win = slab[a:a + CONV_RC].reshape(groups, SUBLANES, ct_w)
                term = w_ref[off - first, :, cs][None] * win
                accs[r % 2] = term if accs[r % 2] is None else accs[r % 2] + term
        y_ref[pl.ds(t0, CONV_RC), cs] = (accs[0] + accs[1]).reshape(CONV_RC, ct_w)
        return carry

    for ct in range(CONV_CH // ct_w):
        cs = slice(ct * ct_w, (ct + 1) * ct_w)
        bias = jnp.broadcast_to(wb_ref[:, cs][None], (groups, SUBLANES, ct_w))
        lax.fori_loop(0, SEQ_TB // CONV_RC, functools.partial(row_body, cs, bias), 0)

    def ln_body(rc, carry):
        sl = pl.ds(pl.multiple_of(rc * 64, 64), 64)
        y = y_ref[sl, :]
        mu = jnp.mean(y, axis=-1, keepdims=True)
        var = jnp.mean(jnp.square(y - mu), axis=-1, keepdims=True)
        t = (y - mu) * lax.rsqrt(var + EPS) * lg_ref[...] + lb_ref[...]
        o_ref[sl, :] = (t * jax.nn.sigmoid(t)).astype(o_ref.dtype)
        return carry

    lax.fori_loop(0, SEQ_TB // 64, ln_body, 0, unroll=2)


def _conv(z, w, wb, ln_g, ln_b):
    nb = SEQ // SEQ_TB
    hb = SEQ_TB // CONV_HALO
    ca_col = (2 * GLA_DK + 2 * GLA_DV) // CONV_CH
    row = lambda b, i: b * nb + i
    halo = lambda b, i: jnp.maximum(row(b, i) * hb - 1, 0)
    return pl.pallas_call(
        _conv_kernel,
        grid=(BATCH, nb),
        in_specs=[
            pl.BlockSpec((SEQ_TB, CONV_CH), lambda b, i: (row(b, i), ca_col)),
            pl.BlockSpec((SEQ_TB, CONV_CH), lambda b, i: (row(b, i), ca_col + 1)),
            pl.BlockSpec((CONV_HALO, CONV_CH), lambda b, i: (halo(b, i), ca_col)),
            pl.BlockSpec((CONV_HALO, CONV_CH), lambda b, i: (halo(b, i), ca_col + 1)),
            pl.BlockSpec((CONV_WIDTH, SUBLANES, CONV_CH), lambda b, i: (0, 0, 0)),
            pl.BlockSpec((1, CONV_CH), lambda b, i: (0, 0)),
            pl.BlockSpec((1, CONV_CH), lambda b, i: (0, 0)),
            pl.BlockSpec((1, CONV_CH), lambda b, i: (0, 0)),
        ],
        out_specs=pl.BlockSpec((SEQ_TB, CONV_CH), lambda b, i: (row(b, i), 0)),
        out_shape=jax.ShapeDtypeStruct((TOKENS, CONV_CH), BF16),
        scratch_shapes=[pltpu.VMEM((SUBLANES, SEQ_TB + CONV_HALO, CONV_CH), F32),
                        pltpu.VMEM((SEQ_TB, CONV_CH), F32)],
        compiler_params=_cparams(("parallel", "parallel")),
        name="conv",
    )(z, z, z, z, w, wb, ln_g, ln_b)


ATT_HALF = ATT_TQ // 2


def _attn_kernel(q_ref, k0_ref, k1_ref, k2_ref, v0_ref, v1_ref, v2_ref,
                 m0_ref, ta_ref, tb_ref, o_ref):
    i = pl.program_id(1)
    pen = [jnp.where(i >= 2, 0.0, NEG_BIG), jnp.where(i >= 1, 0.0, NEG_BIG), None]
    k_refs = (k0_ref, k1_ref, k2_ref)
    v_refs = (v0_ref, v1_ref, v2_ref)
    nt = (((1,), (1,)), ((), ()))
    tn = (((0,), (0,)), ((), ()))
    H = ATT_HALF

    def scores(h):
        hs = slice(h * ATT_HEAD_DIM, (h + 1) * ATT_HEAD_DIM)
        q = q_ref[:, hs]
        tiles = {}
        for j in range(3):
            st = lax.dot_general(k_refs[j][:, hs], q, nt, preferred_element_type=F32)
            for kh in range(2):
                for a in range(2):
                    n = 2 * j + kh - a
                    if n < 0 or n > 4:
                        continue
                    t = st[kh * H:(kh + 1) * H, a * H:(a + 1) * H]
                    if n == 0:
                        t = t + m0_ref[...]
                    elif n == 3:
                        t = t + ta_ref[h]
                    elif n == 4:
                        t = t + tb_ref[h]
                    tiles[(j, kh, a)] = t
        maxima = []
        for a in range(2):
            mx = None
            for key in tiles:
                if key[2] == a:
                    cur = jnp.max(tiles[key], axis=0, keepdims=True)
                    if pen[key[0]] is not None:
                        cur = cur + pen[key[0]]
                    mx = cur if mx is None else jnp.maximum(mx, cur)
            maxima.append(mx)
        return tiles, maxima

    def weights(tiles, maxima):
        probs = {}
        inv_l = []
        for a in range(2):
            mine = [key for key in tiles if key[2] == a]
            shift = [maxima[a] if pj is None else maxima[a] - pj for pj in pen]
            tot = None
            for key in mine:
                p = jnp.exp2(tiles[key] - shift[key[0]])
                probs[key] = p.astype(BF16)
                cur = jnp.sum(p, axis=0, keepdims=True)
                tot = cur if tot is None else tot + cur
            inv_l.append(1.0 / tot)
        return probs, inv_l

    def values(h, probs, inv_l):
        hs = slice(h * ATT_HEAD_DIM, (h + 1) * ATT_HEAD_DIM)
        zero = jnp.zeros((H, H), BF16)
        ot = None
        for j in range(3):
            pt = jnp.concatenate(
                [jnp.concatenate([probs.get((j, kh, a), zero) for a in range(2)], axis=1)
                 for kh in range(2)], axis=0)
            cur = lax.dot_general(v_refs[j][:, hs], pt, tn, preferred_element_type=F32)
            ot = cur if ot is None else ot + cur
        ot = ot * jnp.concatenate(inv_l, axis=1)
        o_ref[:, hs] = ot.T.astype(o_ref.dtype)

    scored, weighted = {}, {}
    for t in range(ATT_HEADS + 2):
        if t < ATT_HEADS:
            scored[t] = scores(t)
        if 0 <= t - 1 < ATT_HEADS:
            weighted[t - 1] = weights(*scored.pop(t - 1))
        if 0 <= t - 2 < ATT_HEADS:
            values(t - 2, *weighted.pop(t - 2))


def _attention(qkv, m0, ta, tb):
    nb = SEQ // ATT_TQ
    row = lambda b, i: b * nb + i
    back = lambda d: (lambda b, i: (b * nb + jnp.maximum(i - d, 0)))
    spec = lambda rowfn, col: pl.BlockSpec((ATT_TQ, D_MODEL), lambda b, i: (rowfn(b, i), col))
    table = pl.BlockSpec((ATT_HEADS, ATT_HALF, ATT_HALF), lambda b, i: (0, 0, 0))
    return pl.pallas_call(
        _attn_kernel,
        grid=(BATCH, nb),
        in_specs=[
            spec(row, 0),
            spec(back(2), 1), spec(back(1), 1), spec(row, 1),
            spec(back(2), 2), spec(back(1), 2), spec(row, 2),
            pl.BlockSpec((ATT_HALF, ATT_HALF), lambda b, i: (0, 0)),
            table, table,
        ],
        out_specs=pl.BlockSpec((ATT_TQ, D_MODEL), lambda b, i: (row(b, i), 0)),
        out_shape=jax.ShapeDtypeStruct((TOKENS, D_MODEL), BF16),
        compiler_params=_cparams(("parallel", "parallel")),
        name="attention",
    )(qkv, qkv, qkv, qkv, qkv, qkv, qkv, m0, ta, tb)


def _attention_bias_tables(rel_bias):
    H = ATT_HALF
    assert H == REL_CLIP
    rb = rel_bias.astype(F32)
    rel = (rb - rb[:, 2 * REL_CLIP:]) * LOG2E

    def toeplitz(g):
        flat = jnp.tile(g, (1, H))[:, :H * (2 * H - 1)]
        return flat.reshape(-1, H, 2 * H - 1)[:, :, :H]

    ta = toeplitz(jnp.concatenate([jnp.zeros((ATT_HEADS, H), F32),
                                   rel[:, REL_CLIP:2 * REL_CLIP]], axis=1))
    tb = toeplitz(jnp.concatenate([rel[:, REL_CLIP:2 * REL_CLIP], rel[:, :REL_CLIP]], axis=1))
    kc = jnp.arange(H)[:, None] // ATT_CHUNK
    qc = jnp.arange(H)[None, :] // ATT_CHUNK
    tb = jnp.where((kc <= qc)[None], tb, NEG_BIG)
    m0 = jnp.where(kc >= qc, 0.0, NEG_BIG).astype(F32)
    return m0, ta, tb


def kernel(x, p, ffn_norm, ffn_w_gate, ffn_w_up, ffn_w_down, mix_norm, ab_w_in, gla_gate_w, gla_gate_b, gla_norm_g, conv_dw, conv_dw_b, conv_ln_g, conv_ln_b, ab_w_out, att_w_qkv, att_rel_bias, att_w_o, pl_norm, pl_w_gate, pl_w_proj, final_norm):
    xs = x.reshape(TOKENS, D_MODEL)
    ps = p.reshape(DEPTH, TOKENS, D_PL)
    row = lambda a: a.reshape(1, -1).astype(F32)

    pl_wg_all = pl_w_gate.astype(BF16)
    pl_wp_all = pl_w_proj.astype(BF16)

    def ffn(xs, i, s):
        return _ffn(xs, row(ffn_norm[i, s]), ffn_w_gate, ffn_w_up, ffn_w_down, i, s)

    for i in range(DEPTH):
        e = i // 2
        xs = ffn(xs, i, 0)
        if i % 2 == 0:
            w_in = ab_w_in[e]
            gz_lo = 2 * GLA_DK + 2 * GLA_DV
            gz_hi = gz_lo + GLA_GATE_RANK
            w_bf = w_in.astype(BF16)
            w_conv = w_bf[:, gz_hi:]
            w_gz = jnp.pad(w_bf[:, gz_lo:gz_hi], ((0, 0), (0, LANES - GLA_GATE_RANK)))
            z_all, gz = _rms_proj(
                xs, row(mix_norm[i]),
                [(w_bf, gz_lo // MM_TN), (w_conv, 2 * CONV_CH // MM_TN)],
                jnp.ones((1, AB_MAIN), F32), BF16, w_extra=w_gz)
            gate_w = jnp.pad(gla_gate_w[e], ((0, LANES - GLA_GATE_RANK), (0, 0))).astype(BF16)
            a_out = _gla(z_all, gz, gate_w, row(gla_gate_b[e]), row(gla_norm_g[e]))
            taps = jnp.broadcast_to(conv_dw[e].astype(F32)[:, None, :],
                                    (CONV_WIDTH, SUBLANES, CONV_CH))
            b_out = _conv(z_all, taps, row(conv_dw_b[e]), row(conv_ln_g[e]), row(conv_ln_b[e]))
            xs = _out_proj([a_out, b_out], ab_w_out[e].astype(BF16), xs)
        else:
            colscale = jnp.concatenate([jnp.full((1, D_MODEL), ATT_HEAD_DIM ** -0.5 * LOG2E, F32),
                                        jnp.ones((1, 2 * D_MODEL), F32)], axis=1)
            qkv = _rms_proj(xs, row(mix_norm[i]), [(att_w_qkv[e], 3 * D_MODEL // MM_TN)],
                            colscale, BF16)
            m0, ta, tb = _attention_bias_tables(att_rel_bias[e])
            o = _attention(qkv, m0, ta, tb)
            xs = _out_proj([o], att_w_o[e].astype(BF16), xs)
        xs = ffn(xs, i, 1)
        xs = _pl_embed(xs, ps, row(pl_norm[i]), pl_wg_all, pl_wp_all, row(final_norm),
                       i, i == DEPTH - 1)
    return xs.reshape(BATCH, SEQ, D_MODEL)
```

```python
import functools

import jax
import jax.numpy as jnp
from jax import lax
from jax.experimental import pallas as pl
from jax.experimental.pallas import tpu as pltpu

F32 = jnp.float32
BF16 = jnp.bfloat16

D_MODEL = 2048
BATCH = 4
SEQ = 2048
DEPTH = 2
TOKENS = BATCH * SEQ
D_PL = 256
D_FF = 5632
EPS = 1e-6

GLA_HEADS = 4
GLA_DK = 512
GLA_DV = 1024
GLA_HEAD_K = 128
GLA_HEAD_V = 256
GLA_GATE_RANK = 16
GLA_GATE_TAU = 16.0
GLA_CHUNK = 64
GLA_SUB = 16
GLA_SCALED_KEY_MAX = 2.0 ** 40
GLA_CHUNKS_PER_STEP = 2
CONV_CH = 1024
CONV_WIDTH = 31
AB_MAIN = 2 * GLA_DK + 2 * GLA_DV + 2 * CONV_CH
ATT_HEADS = 16
ATT_HEAD_DIM = 128
ATT_CHUNK = 64
LEFT_CHUNKS = 8
REL_CLIP = 128
NEG_BIG = -1e30
LOG2E = 1.4426950408889634

V7X_VMEM_BYTES = 64 * 1024 * 1024
LANES = 128
SUBLANES = 8
VMEM_LIMIT = V7X_VMEM_BYTES - 4 * 1024 * 1024

FFN_TM = 1024
FFN_TF = 512
FFN_TF_FIRST = 256
FFN_TN = 512
MM_TM = 1024
MM_TN = 1024
PL_TM = 256
OP_TM = 512
SEQ_TB = 512
CONV_HALO = 32
CONV_RC = 128
ATT_TQ = 256
RMS_ROWS = 256


def _cparams(sem):
    return pltpu.CompilerParams(dimension_semantics=sem, vmem_limit_bytes=VMEM_LIMIT)


def _rms_rows_to(dst_ref, x_ref, g_ref, rows):
    def body(c, carry):
        sl = pl.ds(pl.multiple_of(c * RMS_ROWS, RMS_ROWS), RMS_ROWS)
        x = x_ref[sl, :]
        ms = jnp.mean(x * x, axis=-1, keepdims=True)
        dst_ref[sl, :] = (x * lax.rsqrt(ms + EPS) * g_ref[...]).astype(dst_ref.dtype)
        return carry
    lax.fori_loop(0, rows // RMS_ROWS, body, 0)


def _ffn_step(f, x_ref, g_ref, wg_ref, wu_ref, wd_ref, o_ref, h_ref):
    @pl.when(f == 0)
    def _():
        _rms_rows_to(h_ref, x_ref, g_ref, FFN_TM)
        o_ref[...] = x_ref[...]

    h = h_ref[...]
    gate = jnp.dot(h, wg_ref[...], preferred_element_type=F32)
    up = jnp.dot(h, wu_ref[...], preferred_element_type=F32)
    a = (0.5 * gate * jax.nn.sigmoid(gate) * up).astype(BF16)

    for n in range(D_MODEL // FFN_TN):
        cs = slice(n * FFN_TN, (n + 1) * FFN_TN)
        o_ref[:, cs] += jnp.dot(a, wd_ref[:, cs], preferred_element_type=F32)


def _ffn_first_kernel(x_ref, g_ref, wg_ref, wu_ref, wd_ref, o_ref, wgb_ref, wub_ref, wdb_ref,
                      h_ref):
    wgb_ref[...] = wg_ref[...].astype(BF16)
    wub_ref[...] = wu_ref[...].astype(BF16)
    wdb_ref[...] = wd_ref[...].astype(BF16)
    _ffn_step(pl.program_id(1), x_ref, g_ref, wgb_ref, wub_ref, wdb_ref, o_ref, h_ref)


def _ffn_rest_kernel(n_f, x_ref, g_ref, wg_ref, wu_ref, wd_ref, first_ref, o_ref, h_ref, sem):
    s = pl.program_id(0)

    @pl.when(s == 0)
    def _():
        copy = pltpu.make_async_copy(first_ref, o_ref, sem)
        copy.start()
        copy.wait()

    @pl.when(s > 0)
    def _():
        _ffn_step((s - 1) % n_f, x_ref, g_ref, wg_ref, wu_ref, wd_ref, o_ref, h_ref)


def _ffn(x, g, wg, wu, wd, layer, half):
    m = x.shape[0]
    scratch = [pltpu.VMEM((FFN_TM, D_MODEL), BF16)]
    tf = FFN_TF_FIRST
    first, wgb, wub, wdb = pl.pallas_call(
        _ffn_first_kernel,
        grid=(1, D_FF // tf),
        in_specs=[
            pl.BlockSpec((FFN_TM, D_MODEL), lambda i, f: (0, 0), pipeline_mode=pl.Buffered(1)),
            pl.BlockSpec((1, D_MODEL), lambda i, f: (0, 0)),
            pl.BlockSpec((None, None, D_MODEL, tf), lambda i, f: (layer, half, 0, f)),
            pl.BlockSpec((None, None, D_MODEL, tf), lambda i, f: (layer, half, 0, f)),
            pl.BlockSpec((None, None, tf, D_MODEL), lambda i, f: (layer, half, f, 0)),
        ],
        out_specs=[
            pl.BlockSpec((FFN_TM, D_MODEL), lambda i, f: (0, 0)),
            pl.BlockSpec((D_MODEL, tf), lambda i, f: (0, f)),
            pl.BlockSpec((D_MODEL, tf), lambda i, f: (0, f)),
            pl.BlockSpec((tf, D_MODEL), lambda i, f: (f, 0)),
        ],
        out_shape=[
            jax.ShapeDtypeStruct((FFN_TM, D_MODEL), F32),
            jax.ShapeDtypeStruct((D_MODEL, D_FF), BF16),
            jax.ShapeDtypeStruct((D_MODEL, D_FF), BF16),
            jax.ShapeDtypeStruct((D_FF, D_MODEL), BF16),
        ],
        scratch_shapes=scratch,
        compiler_params=_cparams(("parallel", "arbitrary")),
        name="ffn_first",
    )(x, g, wg, wu, wd)
    tf = FFN_TF
    n_f = D_FF // tf
    tile = lambda s: jnp.where(s == 0, 0, 1 + (s - 1) // n_f)
    f_of = lambda s: jnp.where(s == 0, 0, (s - 1) % n_f)
    return pl.pallas_call(
        functools.partial(_ffn_rest_kernel, n_f),
        grid=(1 + (m // FFN_TM - 1) * n_f,),
        in_specs=[
            pl.BlockSpec((FFN_TM, D_MODEL), lambda s: (jnp.maximum(tile(s), 1), 0)),
            pl.BlockSpec((1, D_MODEL), lambda s: (0, 0)),
            pl.BlockSpec((D_MODEL, tf), lambda s: (0, f_of(s))),
            pl.BlockSpec((D_MODEL, tf), lambda s: (0, f_of(s))),
            pl.BlockSpec((tf, D_MODEL), lambda s: (f_of(s), 0)),
            pl.BlockSpec(memory_space=pl.ANY),
        ],
        out_specs=pl.BlockSpec((FFN_TM, D_MODEL), lambda s: (tile(s), 0)),
        out_shape=jax.ShapeDtypeStruct((m, D_MODEL), F32),
        scratch_shapes=scratch + [pltpu.SemaphoreType.DMA(())],
        compiler_params=_cparams(("arbitrary",)),
        name="ffn_rest",
    )(x, g, wgb, wub, wdb, first)


def _rms_proj_kernel(seg_tiles, has_extra, *refs):
    n_seg = len(seg_tiles)
    x_ref, g_ref = refs[:2]
    w_refs = refs[2:2 + n_seg]
    cs_ref = refs[2 + n_seg]
    if has_extra:
        we_ref, o_ref, oe_ref, h_ref = refs[3 + n_seg:]
    else:
        o_ref, h_ref = refs[3 + n_seg:]
    j = pl.program_id(1)

    @pl.when(j == 0)
    def _():
        _rms_rows_to(h_ref, x_ref, g_ref, MM_TM)
        if has_extra:
            oe_ref[...] = jnp.dot(h_ref[...], we_ref[...].astype(BF16),
                                  preferred_element_type=F32)

    start = 0
    for w_ref, tiles in zip(w_refs, seg_tiles):
        @pl.when((j >= start) & (j < start + tiles))
        def _(w_ref=w_ref):
            acc = jnp.dot(h_ref[...], w_ref[...].astype(BF16), preferred_element_type=F32)
            o_ref[...] = (acc * cs_ref[...]).astype(o_ref.dtype)
        start += tiles


def _rms_proj(x, g, w_list, colscale, out_dtype, w_extra=None):
    m = x.shape[0]
    seg_tiles = tuple(t for _, t in w_list)
    n_tiles = sum(seg_tiles)
    has_extra = w_extra is not None
    in_specs = [
        pl.BlockSpec((MM_TM, D_MODEL), lambda i, j: (i, 0)),
        pl.BlockSpec((1, D_MODEL), lambda i, j: (0, 0)),
    ]
    start = 0
    for _, tiles in w_list:
        in_specs.append(pl.BlockSpec(
            (D_MODEL, MM_TN),
            lambda i, j, s=start, t=tiles: (0, jnp.clip(j - s, 0, t - 1))))
        start += tiles
    in_specs.append(pl.BlockSpec((1, MM_TN), lambda i, j: (0, j)))
    out_specs = pl.BlockSpec((MM_TM, MM_TN), lambda i, j: (i, j))
    out_shape = jax.ShapeDtypeStruct((m, n_tiles * MM_TN), out_dtype)
    args = [x, g] + [w for w, _ in w_list] + [colscale]
    if has_extra:
        in_specs.append(pl.BlockSpec((D_MODEL, LANES), lambda i, j: (0, 0)))
        out_specs = [out_specs, pl.BlockSpec((MM_TM, LANES), lambda i, j: (i, 0))]
        out_shape = [out_shape, jax.ShapeDtypeStruct((m, LANES), F32)]
        args.append(w_extra)
    return pl.pallas_call(
        functools.partial(_rms_proj_kernel, seg_tiles, has_extra),
        grid=(m // MM_TM, n_tiles),
        in_specs=in_specs,
        out_specs=out_specs,
        out_shape=out_shape,
        scratch_shapes=[pltpu.VMEM((MM_TM, D_MODEL), BF16)],
        compiler_params=_cparams(("parallel", "arbitrary")),
        name="rms_proj",
    )(*args)


def _out_proj_kernel(n_lhs, *refs):
    lhs_refs = refs[:n_lhs]
    w_refs = refs[n_lhs:2 * n_lhs]
    x_ref, o_ref = refs[2 * n_lhs], refs[2 * n_lhs + 1]
    wb_refs = refs[2 * n_lhs + 2:]

    @pl.when(pl.program_id(0) == 0)
    def _():
        for w_ref, wb_ref in zip(w_refs, wb_refs):
            wb_ref[...] = w_ref[...].astype(BF16)

    acc = x_ref[...]
    for a_ref, wb_ref in zip(lhs_refs, wb_refs):
        acc = acc + jnp.dot(a_ref[...], wb_ref[...], preferred_element_type=F32)
    o_ref[...] = acc


def _out_proj(lhs_list, w, x):
    m = x.shape[0]
    n_lhs = len(lhs_list)
    kw = lhs_list[0].shape[1]
    in_specs = [pl.BlockSpec((OP_TM, kw), lambda i: (i, 0)) for _ in lhs_list]
    in_specs += [pl.BlockSpec((kw, D_MODEL), lambda i, t=t: (t, 0), pipeline_mode=pl.Buffered(1))
                 for t in range(n_lhs)]
    in_specs += [pl.BlockSpec((OP_TM, D_MODEL), lambda i: (i, 0))]
    return pl.pallas_call(
        functools.partial(_out_proj_kernel, n_lhs),
        grid=(m // OP_TM,),
        in_specs=in_specs,
        out_specs=pl.BlockSpec((OP_TM, D_MODEL), lambda i: (i, 0)),
        out_shape=jax.ShapeDtypeStruct((m, D_MODEL), F32),
        scratch_shapes=[pltpu.VMEM((kw, D_MODEL), BF16) for _ in lhs_list],
        compiler_params=_cparams(("arbitrary",)),
        name="out_proj",
    )(*lhs_list, *([w] * n_lhs), x)


def _pl_embed_kernel(final, x_ref, p_ref, g_ref, wg_ref, wp_ref, fg_ref, o_ref,
                     h_ref, wgb_ref, wpb_ref):
    @pl.when(pl.program_id(0) == 0)
    def _():
        wgb_ref[...] = wg_ref[...].astype(BF16)
        wpb_ref[...] = wp_ref[...].astype(BF16)

    _rms_rows_to(h_ref, x_ref, g_ref, PL_TM)
    gate = jax.nn.sigmoid(jnp.dot(h_ref[...], wgb_ref[...], preferred_element_type=F32))
    proj = jnp.dot(p_ref[...].astype(BF16), wpb_ref[...], preferred_element_type=F32)
    y = x_ref[...] + gate * proj
    if final:
        ms = jnp.mean(y * y, axis=-1, keepdims=True)
        y = y * lax.rsqrt(ms + EPS) * fg_ref[...]
    o_ref[...] = y


def _pl_embed(x, p, g, wg, wp, final_g, layer, final):
    m = x.shape[0]
    return pl.pallas_call(
        functools.partial(_pl_embed_kernel, final),
        grid=(m // PL_TM,),
        in_specs=[
            pl.BlockSpec((PL_TM, D_MODEL), lambda i: (i, 0)),
            pl.BlockSpec((None, PL_TM, D_PL), lambda i: (layer, i, 0)),
            pl.BlockSpec((1, D_MODEL), lambda i: (0, 0)),
            pl.BlockSpec((None, D_MODEL, D_MODEL), lambda i: (layer, 0, 0),
                         pipeline_mode=pl.Buffered(1)),
            pl.BlockSpec((None, D_PL, D_MODEL), lambda i: (layer, 0, 0),
                         pipeline_mode=pl.Buffered(1)),
            pl.BlockSpec((1, D_MODEL), lambda i: (0, 0)),
        ],
        out_specs=pl.BlockSpec((PL_TM, D_MODEL), lambda i: (i, 0)),
        out_shape=jax.ShapeDtypeStruct((m, D_MODEL), F32),
        scratch_shapes=[pltpu.VMEM((PL_TM, D_MODEL), BF16),
                        pltpu.VMEM((D_MODEL, D_MODEL), BF16),
                        pltpu.VMEM((D_PL, D_MODEL), BF16)],
        compiler_params=_cparams(("arbitrary",)),
        name="pl_embed",
    )(x, p, g, wg, wp, final_g)


def _gla_kernel(q_ref, k_ref, v_ref, r_ref, gz_ref, gw_ref, gb_ref, ng_ref, o_ref,
                b_ref, kh_ref, s_ref):
    C, SB = GLA_CHUNK, GLA_SUB
    n_chunks = SEQ_TB // C

    @pl.when(pl.program_id(1) == 0)
    def _():
        s_ref[...] = jnp.zeros_like(s_ref)

    lin = jnp.dot(gz_ref[...].astype(BF16), gw_ref[...], preferred_element_type=F32) + gb_ref[...]
    log_a = -(jnp.maximum(-lin, 0.0) + jnp.log1p(jnp.exp(-jnp.abs(lin)))) * (LOG2E / GLA_GATE_TAU)
    tri = (lax.broadcasted_iota(jnp.int32, (C, C), 0)
           >= lax.broadcasted_iota(jnp.int32, (C, C), 1)).astype(F32)
    for c in range(n_chunks):
        b_ref[c * C:(c + 1) * C, :] = jnp.dot(
            tri, log_a[c * C:(c + 1) * C, :], preferred_element_type=F32,
            precision=lax.Precision.HIGHEST)
    kh_ref[...] = k_ref[...].astype(F32) * jnp.exp2(-b_ref[...])
    scaled_ok = jnp.max(jnp.abs(kh_ref[...])) <= GLA_SCALED_KEY_MAX

    lane = lax.broadcasted_iota(jnp.int32, (SB, C), 1)
    trow = lax.broadcasted_iota(jnp.int32, (SB, C), 0)
    causal = (lax.broadcasted_iota(jnp.int32, (C, C), 0)
              >= lax.broadcasted_iota(jnp.int32, (C, C), 1))
    nt = (((1,), (1,)), ((), ()))
    tn = (((0,), (0,)), ((), ()))

    def guarded_products(q, k, b):
        blocks = []
        for i in range(C // SB):
            s0 = i * SB
            q_i = q[s0:s0 + SB]
            b_i = b[s0:s0 + SB]
            acc = jnp.zeros((SB, C), F32)
            if i > 0:
                p_i = b[s0 - 1:s0, :]
                qt = (q_i * jnp.exp2(b_i - p_i)).astype(BF16)
                kt = (k * jnp.exp2(p_i - b)).astype(BF16)
                off = lax.dot_general(qt, kt, nt, preferred_element_type=F32)
                acc = jnp.where(lane < s0, off, 0.0)
            for s in range(SB):
                b_s = b[s0 + s:s0 + s + 1, :]
                k_s = k[s0 + s:s0 + s + 1, :]
                e = jnp.exp2(b_i - b_s)
                col = jnp.sum(q_i * (k_s * e), axis=1, keepdims=True)
                acc = jnp.where((lane == s0 + s) & (trow >= s), col, acc)
            blocks.append(acc)
        return jnp.concatenate(blocks, axis=0)

    def chunk_body(scaled, per_step, it, carry):
        units = []
        for u in range(per_step):
            rows = pl.ds(pl.multiple_of((it * per_step + u) * C, C), C)
            for h in range(GLA_HEADS):
                units.append((u, h, rows))
        hk = lambda h: slice(h * GLA_HEAD_K, (h + 1) * GLA_HEAD_K)
        hv = lambda h: slice(h * GLA_HEAD_V, (h + 1) * GLA_HEAD_V)

        part = {}
        for u, h, rows in units:
            q = q_ref[rows, hk(h)].astype(F32) * (GLA_HEAD_K ** -0.5)
            v = v_ref[rows, hv(h)].astype(BF16)
            b = b_ref[rows, hk(h)]
            b_last = b[C - 1:C, :]
            qb = (q * jnp.exp2(b)).astype(BF16)
            if scaled:
                kh = kh_ref[rows, hk(h)]
                a_mat = jnp.where(causal, lax.dot_general(qb, kh.astype(BF16), nt,
                                                          preferred_element_type=F32), 0.0)
                k_dec = (kh * jnp.exp2(b_last)).astype(BF16)
            else:
                k = k_ref[rows, hk(h)].astype(F32)
                a_mat = guarded_products(q, k, b)
                k_dec = (k * jnp.exp2(b_last - b)).astype(BF16)
            upd = lax.dot_general(v, k_dec, tn, preferred_element_type=F32)
            local = jnp.dot(a_mat.astype(BF16), v, preferred_element_type=F32)
            part[(u, h)] = (qb, jnp.exp2(b_last), upd, local)

        state = {}
        for h in range(GLA_HEADS):
            st = s_ref[h]
            for u in range(per_step):
                state[(u, h)] = st
                _, decay, upd, _ = part[(u, h)]
                st = st * decay + upd
            s_ref[h] = st

        for u, h, rows in units:
            qb, _, _, local = part[(u, h)]
            o = local + lax.dot_general(qb, state[(u, h)].astype(BF16), nt,
                                        preferred_element_type=F32)
            on = o * lax.rsqrt(jnp.mean(o * o, axis=-1, keepdims=True) + EPS) * ng_ref[...]
            r = r_ref[rows, hv(h)].astype(F32)
            o_ref[rows, hv(h)] = (on * (r * jax.nn.sigmoid(r))).astype(o_ref.dtype)
        return carry

    def run(scaled):
        per_step = GLA_CHUNKS_PER_STEP if scaled else 1
        lax.fori_loop(0, n_chunks // per_step,
                      functools.partial(chunk_body, scaled, per_step), 0)

    lax.cond(scaled_ok, lambda: run(True), lambda: run(False))


def _gla(z, gz, gate_w, gate_b, norm_g):
    nb = SEQ // SEQ_TB
    row = lambda b, i: b * nb + i
    return pl.pallas_call(
        _gla_kernel,
        grid=(BATCH, nb),
        in_specs=[
            pl.BlockSpec((SEQ_TB, GLA_DK), lambda b, i: (row(b, i), 0)),
            pl.BlockSpec((SEQ_TB, GLA_DK), lambda b, i: (row(b, i), 1)),
            pl.BlockSpec((SEQ_TB, GLA_DV), lambda b, i: (row(b, i), 1)),
            pl.BlockSpec((SEQ_TB, GLA_DV), lambda b, i: (row(b, i), 2)),
            pl.BlockSpec((SEQ_TB, LANES), lambda b, i: (row(b, i), 0)),
            pl.BlockSpec((LANES, GLA_DK), lambda b, i: (0, 0)),
            pl.BlockSpec((1, GLA_DK), lambda b, i: (0, 0)),
            pl.BlockSpec((1, GLA_HEAD_V), lambda b, i: (0, 0)),
        ],
        out_specs=pl.BlockSpec((SEQ_TB, GLA_DV), lambda b, i: (row(b, i), 0)),
        out_shape=jax.ShapeDtypeStruct((TOKENS, GLA_DV), BF16),
        scratch_shapes=[pltpu.VMEM((SEQ_TB, GLA_DK), F32),
                        pltpu.VMEM((SEQ_TB, GLA_DK), F32),
                        pltpu.VMEM((GLA_HEADS, GLA_HEAD_V, GLA_HEAD_K), F32)],
        compiler_params=_cparams(("parallel", "arbitrary")),
        name="gla",
    )(z, z, z, z, gz, gate_w, gate_b, norm_g)


def _conv_kernel(ca_ref, cb_ref, ha_ref, hb_ref, w_ref, wb_ref, lg_ref, lb_ref, o_ref,
                 sh_ref, y_ref):
    H = CONV_HALO
    n_sh = SUBLANES
    keep = (pl.program_id(1) > 0).astype(F32)

    u_main = ca_ref[...].astype(F32) * jax.nn.sigmoid(cb_ref[...].astype(F32))
    u_halo = ha_ref[...].astype(F32) * jax.nn.sigmoid(hb_ref[...].astype(F32)) * keep
    for r in range(n_sh):
        sh_ref[r, 0:H - r, :] = u_halo[r:H, :]
        sh_ref[r, H - r:H - r + SEQ_TB, :] = u_main

    first = H - (CONV_WIDTH - 1)
    ct_w = 128

    groups = CONV_RC // SUBLANES

    def row_body(cs, bias, rc, carry):
        t0 = pl.multiple_of(rc * CONV_RC, CONV_RC)
        accs = [bias, None]
        for r in range(n_sh):
            offs = [o for o in range(first, first + CONV_WIDTH) if o % n_sh == r]
            lo, hi = offs[0] - r, offs[-1] - r
            slab = sh_ref[r, pl.ds(t0 + lo, CONV_RC + hi - lo), cs]
            for off in offs:
                a = off - r - lo
                win = slab[a:a + CONV_RC].reshape(groups, SUBLANES, ct_w)
                term = w_ref[off - first, :, cs][None] * win
                accs[r % 2] = term if accs[r % 2] is None else accs[r % 2] + term
        y_ref[pl.ds(t0, CONV_RC), cs] = (accs[0] + accs[1]).reshape(CONV_RC, ct_w)
        return carry

    for ct in range(CONV_CH // ct_w):
        cs = slice(ct * ct_w, (ct + 1) * ct_w)
        bias = jnp.broadcast_to(wb_ref[:, cs][None], (groups, SUBLANES, ct_w))
        lax.fori_loop(0, SEQ_TB // CONV_RC, functools.partial(row_body, cs, bias), 0)

    def ln_body(rc, carry):
        sl = pl.ds(pl.multiple_of(rc * 64, 64), 64)
        y = y_ref[sl, :]
        mu = jnp.mean(y, axis=-1, keepdims=True)
        var = jnp.mean(jnp.square(y - mu), axis=-1, keepdims=True)
        t = (y - mu) * lax.rsqrt(var + EPS) * lg_ref[...] + lb_ref[...]
        o_ref[sl, :] = (t * jax.nn.sigmoid(t)).astype(o_ref.dtype)
        return carry

    lax.fori_loop(0, SEQ_TB // 64, ln_body, 0, unroll=2)


def _conv(z, w, wb, ln_g, ln_b):
    nb = SEQ // SEQ_TB
    hb = SEQ_TB // CONV_HALO
    ca_col = (2 * GLA_DK + 2 * GLA_DV) // CONV_CH
    row = lambda b, i: b * nb + i
    halo = lambda b, i: jnp.maximum(row(b, i) * hb - 1, 0)
    return pl.pallas_call(
        _conv_kernel,
        grid=(BATCH, nb),
        in_specs=[
            pl.BlockSpec((SEQ_TB, CONV_CH), lambda b, i: (row(b, i), ca_col)),
            pl.BlockSpec((SEQ_TB, CONV_CH), lambda b, i: (row(b, i), ca_col + 1)),
            pl.BlockSpec((CONV_HALO, CONV_CH), lambda b, i: (halo(b, i), ca_col)),
            pl.BlockSpec((CONV_HALO, CONV_CH), lambda b, i: (halo(b, i), ca_col + 1)),
            pl.BlockSpec((CONV_WIDTH, SUBLANES, CONV_CH), lambda b, i: (0, 0, 0)),
            pl.BlockSpec((1, CONV_CH), lambda b, i: (0, 0)),
            pl.BlockSpec((1, CONV_CH), lambda b, i: (0, 0)),
            pl.BlockSpec((1, CONV_CH), lambda b, i: (0, 0)),
        ],
        out_specs=pl.BlockSpec((SEQ_TB, CONV_CH), lambda b, i: (row(b, i), 0)),
        out_shape=jax.ShapeDtypeStruct((TOKENS, CONV_CH), BF16),
        scratch_shapes=[pltpu.VMEM((SUBLANES, SEQ_TB + CONV_HALO, CONV_CH), F32),
                        pltpu.VMEM((SEQ_TB, CONV_CH), F32)],
        compiler_params=_cparams(("parallel", "parallel")),
        name="conv",
    )(z, z, z, z, w, wb, ln_g, ln_b)


ATT_HALF = ATT_TQ // 2


def _attn_kernel(q_ref, k0_ref, k1_ref, k2_ref, v0_ref, v1_ref, v2_ref,
                 m0_ref, ta_ref, tb_ref, o_ref):
    i = pl.program_id(1)
    pen = [jnp.where(i >= 2, 0.0, NEG_BIG), jnp.where(i >= 1, 0.0, NEG_BIG), None]
    k_refs = (k0_ref, k1_ref, k2_ref)
    v_refs = (v0_ref, v1_ref, v2_ref)
    nt = (((1,), (1,)), ((), ()))
    tn = (((0,), (0,)), ((), ()))
    H = ATT_HALF

    def scores(h):
        hs = slice(h * ATT_HEAD_DIM, (h + 1) * ATT_HEAD_DIM)
        q = q_ref[:, hs]
        tiles = {}
        for j in range(3):
            st = lax.dot_general(k_refs[j][:, hs], q, nt, preferred_element_type=F32)
            for kh in range(2):
                for a in range(2):
                    n = 2 * j + kh - a
                    if n < 0 or n > 4:
                        continue
                    t = st[kh * H:(kh + 1) * H, a * H:(a + 1) * H]
                    if n == 0:
                        t = t + m0_ref[...]
                    elif n == 3:
                        t = t + ta_ref[h]
                    elif n == 4:
                        t = t + tb_ref[h]
                    tiles[(j, kh, a)] = t
        maxima = []
        for a in range(2):
            mx = None
            for key in tiles:
                if key[2] == a:
                    cur = jnp.max(tiles[key], axis=0, keepdims=True)
                    if pen[key[0]] is not None:
                        cur = cur + pen[key[0]]
                    mx = cur if mx is None else jnp.maximum(mx, cur)
            maxima.append(mx)
        return tiles, maxima

    def weights(tiles, maxima):
        probs = {}
        inv_l = []
        for a in range(2):
            mine = [key for key in tiles if key[2] == a]
            shift = [maxima[a] if pj is None else maxima[a] - pj for pj in pen]
            tot = None
            for key in mine:
                p = jnp.exp2(tiles[key] - shift[key[0]])
                probs[key] = p.astype(BF16)
                cur = jnp.sum(p, axis=0, keepdims=True)
                tot = cur if tot is None else tot + cur
            inv_l.append(1.0 / tot)
        return probs, inv_l

    def values(h, probs, inv_l):
        hs = slice(h * ATT_HEAD_DIM, (h + 1) * ATT_HEAD_DIM)
        zero = jnp.zeros((H, H), BF16)
        ot = None
        for j in range(3):
            pt = jnp.concatenate(
                [jnp.concatenate([probs.get((j, kh, a), zero) for a in range(2)], axis=1)
                 for kh in range(2)], axis=0)
            cur = lax.dot_general(v_refs[j][:, hs], pt, tn, preferred_element_type=F32)
            ot = cur if ot is None else ot + cur
        ot = ot * jnp.concatenate(inv_l, axis=1)
        o_ref[:, hs] = ot.T.astype(o_ref.dtype)

    scored, weighted = {}, {}
    for t in range(ATT_HEADS + 2):
        if t < ATT_HEADS:
            scored[t] = scores(t)
        if 0 <= t - 1 < ATT_HEADS:
            weighted[t - 1] = weights(*scored.pop(t - 1))
        if 0 <= t - 2 < ATT_HEADS:
            values(t - 2, *weighted.pop(t - 2))


def _attention(qkv, m0, ta, tb):
    nb = SEQ // ATT_TQ
    row = lambda b, i: b * nb + i
    back = lambda d: (lambda b, i: (b * nb + jnp.maximum(i - d, 0)))
    spec = lambda rowfn, col: pl.BlockSpec((ATT_TQ, D_MODEL), lambda b, i: (rowfn(b, i), col))
    table = pl.BlockSpec((ATT_HEADS, ATT_HALF, ATT_HALF), lambda b, i: (0, 0, 0))
    return pl.pallas_call(
        _attn_kernel,
        grid=(BATCH, nb),
        in_specs=[
            spec(row, 0),
            spec(back(2), 1), spec(back(1), 1), spec(row, 1),
            spec(back(2), 2), spec(back(1), 2), spec(row, 2),
            pl.BlockSpec((ATT_HALF, ATT_HALF), lambda b, i: (0, 0)),
            table, table,
        ],
        out_specs=pl.BlockSpec((ATT_TQ, D_MODEL), lambda b, i: (row(b, i), 0)),
        out_shape=jax.ShapeDtypeStruct((TOKENS, D_MODEL), BF16),
        compiler_params=_cparams(("parallel", "parallel")),
        name="attention",
    )(qkv, qkv, qkv, qkv, qkv, qkv, qkv, m0, ta, tb)


def _attention_bias_tables(rel_bias):
    H = ATT_HALF
    assert H == REL_CLIP
    rb = rel_bias.astype(F32)
    rel = (rb - rb[:, 2 * REL_CLIP:]) * LOG2E

    def toeplitz(g):
        flat = jnp.tile(g, (1, H))[:, :H * (2 * H - 1)]
        return flat.reshape(-1, H, 2 * H - 1)[:, :, :H]

    ta = toeplitz(jnp.concatenate([jnp.zeros((ATT_HEADS, H), F32),
                                   rel[:, REL_CLIP:2 * REL_CLIP]], axis=1))
    tb = toeplitz(jnp.concatenate([rel[:, REL_CLIP:2 * REL_CLIP], rel[:, :REL_CLIP]], axis=1))
    kc = jnp.arange(H)[:, None] // ATT_CHUNK
    qc = jnp.arange(H)[None, :] // ATT_CHUNK
    tb = jnp.where((kc <= qc)[None], tb, NEG_BIG)
    m0 = jnp.where(kc >= qc, 0.0, NEG_BIG).astype(F32)
    return m0, ta, tb


def kernel(x, p, ffn_norm, ffn_w_gate, ffn_w_up, ffn_w_down, mix_norm, ab_w_in, gla_gate_w, gla_gate_b, gla_norm_g, conv_dw, conv_dw_b, conv_ln_g, conv_ln_b, ab_w_out, att_w_qkv, att_rel_bias, att_w_o, pl_norm, pl_w_gate, pl_w_proj, final_norm):
    xs = x.reshape(TOKENS, D_MODEL)
    ps = p.reshape(DEPTH, TOKENS, D_PL)
    row = lambda a: a.reshape(1, -1).astype(F32)

    def ffn(xs, i, s):
        return _ffn(xs, row(ffn_norm[i, s]), ffn_w_gate, ffn_w_up, ffn_w_down, i, s)

    for i in range(DEPTH):
        e = i // 2
        xs = ffn(xs, i, 0)
        if i % 2 == 0:
            w_in = ab_w_in[e]
            gz_lo = 2 * GLA_DK + 2 * GLA_DV
            gz_hi = gz_lo + GLA_GATE_RANK
            w_bf = w_in.astype(BF16)
            w_conv = w_bf[:, gz_hi:]
            w_gz = jnp.pad(w_bf[:, gz_lo:gz_hi], ((0, 0), (0, LANES - GLA_GATE_RANK)))
            z_all, gz = _rms_proj(
                xs, row(mix_norm[i]),
                [(w_bf, gz_lo // MM_TN), (w_conv, 2 * CONV_CH // MM_TN)],
                jnp.ones((1, AB_MAIN), F32), BF16, w_extra=w_gz)
            gate_w = jnp.pad(gla_gate_w[e], ((0, LANES - GLA_GATE_RANK), (0, 0))).astype(BF16)
            a_out = _gla(z_all, gz, gate_w, row(gla_gate_b[e]), row(gla_norm_g[e]))
            taps = jnp.broadcast_to(conv_dw[e].astype(F32)[:, None, :],
                                    (CONV_WIDTH, SUBLANES, CONV_CH))
            b_out = _conv(z_all, taps, row(conv_dw_b[e]), row(conv_ln_g[e]), row(conv_ln_b[e]))
            xs = _out_proj([a_out, b_out], ab_w_out[e], xs)
        else:
            colscale = jnp.concatenate([jnp.full((1, D_MODEL), ATT_HEAD_DIM ** -0.5 * LOG2E, F32),
                                        jnp.ones((1, 2 * D_MODEL), F32)], axis=1)
            qkv = _rms_proj(xs, row(mix_norm[i]), [(att_w_qkv[e], 3 * D_MODEL // MM_TN)],
                            colscale, BF16)
            m0, ta, tb = _attention_bias_tables(att_rel_bias[e])
            o = _attention(qkv, m0, ta, tb)
            xs = _out_proj([o], att_w_o[e], xs)
        xs = ffn(xs, i, 1)
        xs = _pl_embed(xs, ps, row(pl_norm[i]), pl_w_gate, pl_w_proj, row(final_norm),
                       i, i == DEPTH - 1)
    return xs.reshape(BATCH, SEQ, D_MODEL)
```

```python
import functools

import jax
import jax.numpy as jnp
from jax import lax
from jax.experimental import pallas as pl
from jax.experimental.pallas import tpu as pltpu

F32 = jnp.float32
BF16 = jnp.bfloat16

D_MODEL = 2048
BATCH = 4
SEQ = 2048
DEPTH = 2
TOKENS = BATCH * SEQ
D_PL = 256
D_FF = 5632
EPS = 1e-6

GLA_HEADS = 4
GLA_DK = 512
GLA_DV = 1024
GLA_HEAD_K = 128
GLA_HEAD_V = 256
GLA_GATE_RANK = 16
GLA_GATE_TAU = 16.0
GLA_CHUNK = 64
GLA_SUB = 16
GLA_SCALED_KEY_MAX = 2.0 ** 40
GLA_CHUNKS_PER_STEP = 2
CONV_CH = 1024
CONV_WIDTH = 31
AB_MAIN = 2 * GLA_DK + 2 * GLA_DV + 2 * CONV_CH
ATT_HEADS = 16
ATT_HEAD_DIM = 128
ATT_CHUNK = 64
LEFT_CHUNKS = 8
REL_CLIP = 128
NEG_BIG = -1e30
LOG2E = 1.4426950408889634

V7X_VMEM_BYTES = 64 * 1024 * 1024
LANES = 128
SUBLANES = 8
VMEM_LIMIT = V7X_VMEM_BYTES - 4 * 1024 * 1024

FFN_TM = 1024
FFN_TF = 512
FFN_TF_FIRST = 512
FFN_TN = 512
MM_TM = 1024
MM_TN = 1024
PL_TM = 256
OP_TM = 512
SEQ_TB = 512
CONV_HALO = 32
CONV_RC = 128
ATT_TQ = 256
RMS_ROWS = 256


def _cparams(sem):
    return pltpu.CompilerParams(dimension_semantics=sem, vmem_limit_bytes=VMEM_LIMIT)


def _rms_rows_to(dst_ref, x_ref, g_ref, rows):
    def body(c, carry):
        sl = pl.ds(pl.multiple_of(c * RMS_ROWS, RMS_ROWS), RMS_ROWS)
        x = x_ref[sl, :]
        ms = jnp.mean(x * x, axis=-1, keepdims=True)
        dst_ref[sl, :] = (x * lax.rsqrt(ms + EPS) * g_ref[...]).astype(dst_ref.dtype)
        return carry
    lax.fori_loop(0, rows // RMS_ROWS, body, 0)


def _ffn_step(f, load_residual, g_ref, wg_ref, wu_ref, wd_ref, o_ref, h_ref):
    @pl.when(f == 0)
    def _():
        load_residual()
        _rms_rows_to(h_ref, o_ref, g_ref, FFN_TM)

    h = h_ref[...]
    gate = jnp.dot(h, wg_ref[...], preferred_element_type=F32)
    up = jnp.dot(h, wu_ref[...], preferred_element_type=F32)
    a = (0.5 * gate * jax.nn.sigmoid(gate) * up).astype(BF16)

    for n in range(D_MODEL // FFN_TN):
        cs = slice(n * FFN_TN, (n + 1) * FFN_TN)
        o_ref[:, cs] += jnp.dot(a, wd_ref[:, cs], preferred_element_type=F32)


def _ffn_first_kernel(x_hbm, g_ref, wg_ref, wu_ref, wd_ref, o_ref, wgb_ref, wub_ref, wdb_ref,
                      h_ref, sem):
    def load_residual():
        copy = pltpu.make_async_copy(x_hbm.at[pl.ds(0, FFN_TM)], o_ref, sem)
        copy.start()
        copy.wait()

    wgb_ref[...] = wg_ref[...].astype(BF16)
    wub_ref[...] = wu_ref[...].astype(BF16)
    wdb_ref[...] = wd_ref[...].astype(BF16)
    _ffn_step(pl.program_id(1), load_residual, g_ref, wgb_ref, wub_ref, wdb_ref, o_ref, h_ref)


def _ffn_rest_kernel(n_f, x_ref, g_ref, wg_ref, wu_ref, wd_ref, first_ref, o_ref, h_ref, sem):
    s = pl.program_id(0)

    @pl.when(s == 0)
    def _():
        copy = pltpu.make_async_copy(first_ref, o_ref, sem)
        copy.start()
        copy.wait()

    def load_residual():
        o_ref[...] = x_ref[...]

    @pl.when(s > 0)
    def _():
        _ffn_step((s - 1) % n_f, load_residual, g_ref, wg_ref, wu_ref, wd_ref, o_ref, h_ref)


def _ffn(x, g, wg, wu, wd, layer, half):
    m = x.shape[0]
    scratch = [pltpu.VMEM((FFN_TM, D_MODEL), BF16)]
    tf = FFN_TF_FIRST
    first, wgb, wub, wdb = pl.pallas_call(
        _ffn_first_kernel,
        grid=(1, D_FF // tf),
        in_specs=[
            pl.BlockSpec(memory_space=pl.ANY),
            pl.BlockSpec((1, D_MODEL), lambda i, f: (0, 0)),
            pl.BlockSpec((None, None, D_MODEL, tf), lambda i, f: (layer, half, 0, f)),
            pl.BlockSpec((None, None, D_MODEL, tf), lambda i, f: (layer, half, 0, f)),
            pl.BlockSpec((None, None, tf, D_MODEL), lambda i, f: (layer, half, f, 0)),
        ],
        out_specs=[
            pl.BlockSpec((FFN_TM, D_MODEL), lambda i, f: (0, 0), pipeline_mode=pl.Buffered(1)),
            pl.BlockSpec((D_MODEL, tf), lambda i, f: (0, f)),
            pl.BlockSpec((D_MODEL, tf), lambda i, f: (0, f)),
            pl.BlockSpec((tf, D_MODEL), lambda i, f: (f, 0)),
        ],
        out_shape=[
            jax.ShapeDtypeStruct((FFN_TM, D_MODEL), F32),
            jax.ShapeDtypeStruct((D_MODEL, D_FF), BF16),
            jax.ShapeDtypeStruct((D_MODEL, D_FF), BF16),
            jax.ShapeDtypeStruct((D_FF, D_MODEL), BF16),
        ],
        scratch_shapes=scratch + [pltpu.SemaphoreType.DMA(())],
        compiler_params=_cparams(("parallel", "arbitrary")),
        name="ffn_first",
    )(x, g, wg, wu, wd)
    tf = FFN_TF
    n_f = D_FF // tf
    tile = lambda s: jnp.where(s == 0, 0, 1 + (s - 1) // n_f)
    f_of = lambda s: jnp.where(s == 0, 0, (s - 1) % n_f)
    return pl.pallas_call(
        functools.partial(_ffn_rest_kernel, n_f),
        grid=(1 + (m // FFN_TM - 1) * n_f,),
        in_specs=[
            pl.BlockSpec((FFN_TM, D_MODEL), lambda s: (jnp.maximum(tile(s), 1), 0)),
            pl.BlockSpec((1, D_MODEL), lambda s: (0, 0)),
            pl.BlockSpec((D_MODEL, tf), lambda s: (0, f_of(s))),
            pl.BlockSpec((D_MODEL, tf), lambda s: (0, f_of(s))),
            pl.BlockSpec((tf, D_MODEL), lambda s: (f_of(s), 0)),
            pl.BlockSpec(memory_space=pl.ANY),
        ],
        out_specs=pl.BlockSpec((FFN_TM, D_MODEL), lambda s: (tile(s), 0)),
        out_shape=jax.ShapeDtypeStruct((m, D_MODEL), F32),
        scratch_shapes=scratch + [pltpu.SemaphoreType.DMA(())],
        compiler_params=_cparams(("arbitrary",)),
        name="ffn_rest",
    )(x, g, wgb, wub, wdb, first)


def _rms_proj_kernel(seg_tiles, has_extra, *refs):
    n_seg = len(seg_tiles)
    x_ref, g_ref = refs[:2]
    w_refs = refs[2:2 + n_seg]
    cs_ref = refs[2 + n_seg]
    if has_extra:
        we_ref, o_ref, oe_ref, h_ref = refs[3 + n_seg:]
    else:
        o_ref, h_ref = refs[3 + n_seg:]
    j = pl.program_id(1)

    @pl.when(j == 0)
    def _():
        _rms_rows_to(h_ref, x_ref, g_ref, MM_TM)
        if has_extra:
            oe_ref[...] = jnp.dot(h_ref[...], we_ref[...].astype(BF16),
                                  preferred_element_type=F32)

    start = 0
    for w_ref, tiles in zip(w_refs, seg_tiles):
        @pl.when((j >= start) & (j < start + tiles))
        def _(w_ref=w_ref):
            acc = jnp.dot(h_ref[...], w_ref[...].astype(BF16), preferred_element_type=F32)
            o_ref[...] = (acc * cs_ref[...]).astype(o_ref.dtype)
        start += tiles


def _rms_proj(x, g, w_list, colscale, out_dtype, w_extra=None):
    m = x.shape[0]
    seg_tiles = tuple(t for _, t in w_list)
    n_tiles = sum(seg_tiles)
    has_extra = w_extra is not None
    in_specs = [
        pl.BlockSpec((MM_TM, D_MODEL), lambda i, j: (i, 0)),
        pl.BlockSpec((1, D_MODEL), lambda i, j: (0, 0)),
    ]
    start = 0
    for _, tiles in w_list:
        in_specs.append(pl.BlockSpec(
            (D_MODEL, MM_TN),
            lambda i, j, s=start, t=tiles: (0, jnp.clip(j - s, 0, t - 1))))
        start += tiles
    in_specs.append(pl.BlockSpec((1, MM_TN), lambda i, j: (0, j)))
    out_specs = pl.BlockSpec((MM_TM, MM_TN), lambda i, j: (i, j))
    out_shape = jax.ShapeDtypeStruct((m, n_tiles * MM_TN), out_dtype)
    args = [x, g] + [w for w, _ in w_list] + [colscale]
    if has_extra:
        in_specs.append(pl.BlockSpec((D_MODEL, LANES), lambda i, j: (0, 0)))
        out_specs = [out_specs, pl.BlockSpec((MM_TM, LANES), lambda i, j: (i, 0))]
        out_shape = [out_shape, jax.ShapeDtypeStruct((m, LANES), F32)]
        args.append(w_extra)
    return pl.pallas_call(
        functools.partial(_rms_proj_kernel, seg_tiles, has_extra),
        grid=(m // MM_TM, n_tiles),
        in_specs=in_specs,
        out_specs=out_specs,
        out_shape=out_shape,
        scratch_shapes=[pltpu.VMEM((MM_TM, D_MODEL), BF16)],
        compiler_params=_cparams(("parallel", "arbitrary")),
        name="rms_proj",
    )(*args)


def _out_proj_kernel(n_lhs, *refs):
    lhs_refs = refs[:n_lhs]
    w_refs = refs[n_lhs:2 * n_lhs]
    x_ref, o_ref = refs[2 * n_lhs], refs[2 * n_lhs + 1]
    wb_refs = refs[2 * n_lhs + 2:]

    @pl.when(pl.program_id(0) == 0)
    def _():
        for w_ref, wb_ref in zip(w_refs, wb_refs):
            wb_ref[...] = w_ref[...].astype(BF16)

    acc = x_ref[...]
    for a_ref, wb_ref in zip(lhs_refs, wb_refs):
        acc = acc + jnp.dot(a_ref[...], wb_ref[...], preferred_element_type=F32)
    o_ref[...] = acc


def _out_proj(lhs_list, w, x):
    m = x.shape[0]
    n_lhs = len(lhs_list)
    kw = lhs_list[0].shape[1]
    in_specs = [pl.BlockSpec((OP_TM, kw), lambda i: (i, 0)) for _ in lhs_list]
    in_specs += [pl.BlockSpec((kw, D_MODEL), lambda i, t=t: (t, 0), pipeline_mode=pl.Buffered(1))
                 for t in range(n_lhs)]
    in_specs += [pl.BlockSpec((OP_TM, D_MODEL), lambda i: (i, 0))]
    return pl.pallas_call(
        functools.partial(_out_proj_kernel, n_lhs),
        grid=(m // OP_TM,),
        in_specs=in_specs,
        out_specs=pl.BlockSpec((OP_TM, D_MODEL), lambda i: (i, 0)),
        out_shape=jax.ShapeDtypeStruct((m, D_MODEL), F32),
        scratch_shapes=[pltpu.VMEM((kw, D_MODEL), BF16) for _ in lhs_list],
        compiler_params=_cparams(("arbitrary",)),
        name="out_proj",
    )(*lhs_list, *([w] * n_lhs), x)


def _pl_embed_kernel(final, x_ref, p_ref, g_ref, wg_ref, wp_ref, fg_ref, o_ref,
                     h_ref, wgb_ref, wpb_ref):
    @pl.when(pl.program_id(0) == 0)
    def _():
        wgb_ref[...] = wg_ref[...].astype(BF16)
        wpb_ref[...] = wp_ref[...].astype(BF16)

    _rms_rows_to(h_ref, x_ref, g_ref, PL_TM)
    gate = jax.nn.sigmoid(jnp.dot(h_ref[...], wgb_ref[...], preferred_element_type=F32))
    proj = jnp.dot(p_ref[...].astype(BF16), wpb_ref[...], preferred_element_type=F32)
    y = x_ref[...] + gate * proj
    if final:
        ms = jnp.mean(y * y, axis=-1, keepdims=True)
        y = y * lax.rsqrt(ms + EPS) * fg_ref[...]
    o_ref[...] = y


def _pl_embed(x, p, g, wg, wp, final_g, layer, final):
    m = x.shape[0]
    return pl.pallas_call(
        functools.partial(_pl_embed_kernel, final),
        grid=(m // PL_TM,),
        in_specs=[
            pl.BlockSpec((PL_TM, D_MODEL), lambda i: (i, 0)),
            pl.BlockSpec((None, PL_TM, D_PL), lambda i: (layer, i, 0)),
            pl.BlockSpec((1, D_MODEL), lambda i: (0, 0)),
            pl.BlockSpec((None, D_MODEL, D_MODEL), lambda i: (layer, 0, 0),
                         pipeline_mode=pl.Buffered(1)),
            pl.BlockSpec((None, D_PL, D_MODEL), lambda i: (layer, 0, 0),
                         pipeline_mode=pl.Buffered(1)),
            pl.BlockSpec((1, D_MODEL), lambda i: (0, 0)),
        ],
        out_specs=pl.BlockSpec((PL_TM, D_MODEL), lambda i: (i, 0)),
        out_shape=jax.ShapeDtypeStruct((m, D_MODEL), F32),
        scratch_shapes=[pltpu.VMEM((PL_TM, D_MODEL), BF16),
                        pltpu.VMEM((D_MODEL, D_MODEL), BF16),
                        pltpu.VMEM((D_PL, D_MODEL), BF16)],
        compiler_params=_cparams(("arbitrary",)),
        name="pl_embed",
    )(x, p, g, wg, wp, final_g)


def _gla_kernel(q_ref, k_ref, v_ref, r_ref, gz_ref, gw_ref, gb_ref, ng_ref, o_ref,
                b_ref, kh_ref, s_ref):
    C, SB = GLA_CHUNK, GLA_SUB
    n_chunks = SEQ_TB // C

    @pl.when(pl.program_id(1) == 0)
    def _():
        s_ref[...] = jnp.zeros_like(s_ref)

    lin = jnp.dot(gz_ref[...].astype(BF16), gw_ref[...], preferred_element_type=F32) + gb_ref[...]
    log_a = -(jnp.maximum(-lin, 0.0) + jnp.log1p(jnp.exp(-jnp.abs(lin)))) * (LOG2E / GLA_GATE_TAU)
    tri = (lax.broadcasted_iota(jnp.int32, (C, C), 0)
           >= lax.broadcasted_iota(jnp.int32, (C, C), 1)).astype(F32)
    for c in range(n_chunks):
        b_ref[c * C:(c + 1) * C, :] = jnp.dot(
            tri, log_a[c * C:(c + 1) * C, :], preferred_element_type=F32,
            precision=lax.Precision.HIGHEST)
    kh_ref[...] = k_ref[...].astype(F32) * jnp.exp2(-b_ref[...])
    scaled_ok = jnp.max(jnp.abs(kh_ref[...])) <= GLA_SCALED_KEY_MAX

    lane = lax.broadcasted_iota(jnp.int32, (SB, C), 1)
    trow = lax.broadcasted_iota(jnp.int32, (SB, C), 0)
    causal = (lax.broadcasted_iota(jnp.int32, (C, C), 0)
              >= lax.broadcasted_iota(jnp.int32, (C, C), 1))
    nt = (((1,), (1,)), ((), ()))
    tn = (((0,), (0,)), ((), ()))

    def guarded_products(q, k, b):
        blocks = []
        for i in range(C // SB):
            s0 = i * SB
            q_i = q[s0:s0 + SB]
            b_i = b[s0:s0 + SB]
            acc = jnp.zeros((SB, C), F32)
            if i > 0:
                p_i = b[s0 - 1:s0, :]
                qt = (q_i * jnp.exp2(b_i - p_i)).astype(BF16)
                kt = (k * jnp.exp2(p_i - b)).astype(BF16)
                off = lax.dot_general(qt, kt, nt, preferred_element_type=F32)
                acc = jnp.where(lane < s0, off, 0.0)
            for s in range(SB):
                b_s = b[s0 + s:s0 + s + 1, :]
                k_s = k[s0 + s:s0 + s + 1, :]
                e = jnp.exp2(b_i - b_s)
                col = jnp.sum(q_i * (k_s * e), axis=1, keepdims=True)
                acc = jnp.where((lane == s0 + s) & (trow >= s), col, acc)
            blocks.append(acc)
        return jnp.concatenate(blocks, axis=0)

    def chunk_body(scaled, per_step, it, carry):
        units = []
        for u in range(per_step):
            rows = pl.ds(pl.multiple_of((it * per_step + u) * C, C), C)
            for h in range(GLA_HEADS):
                units.append((u, h, rows))
        hk = lambda h: slice(h * GLA_HEAD_K, (h + 1) * GLA_HEAD_K)
        hv = lambda h: slice(h * GLA_HEAD_V, (h + 1) * GLA_HEAD_V)

        part = {}
        for u, h, rows in units:
            q = q_ref[rows, hk(h)].astype(F32) * (GLA_HEAD_K ** -0.5)
            v = v_ref[rows, hv(h)].astype(BF16)
            b = b_ref[rows, hk(h)]
            b_last = b[C - 1:C, :]
            qb = (q * jnp.exp2(b)).astype(BF16)
            if scaled:
                kh = kh_ref[rows, hk(h)]
                a_mat = jnp.where(causal, lax.dot_general(qb, kh.astype(BF16), nt,
                                                          preferred_element_type=F32), 0.0)
                k_dec = (kh * jnp.exp2(b_last)).astype(BF16)
            else:
                k = k_ref[rows, hk(h)].astype(F32)
                a_mat = guarded_products(q, k, b)
                k_dec = (k * jnp.exp2(b_last - b)).astype(BF16)
            upd = lax.dot_general(v, k_dec, tn, preferred_element_type=F32)
            local = jnp.dot(a_mat.astype(BF16), v, preferred_element_type=F32)
            part[(u, h)] = (qb, jnp.exp2(b_last), upd, local)

        state = {}
        for h in range(GLA_HEADS):
            st = s_ref[h]
            for u in range(per_step):
                state[(u, h)] = st
                _, decay, upd, _ = part[(u, h)]
                st = st * decay + upd
            s_ref[h] = st

        for u, h, rows in units:
            qb, _, _, local = part[(u, h)]
            o = local + lax.dot_general(qb, state[(u, h)].astype(BF16), nt,
                                        preferred_element_type=F32)
            on = o * lax.rsqrt(jnp.mean(o * o, axis=-1, keepdims=True) + EPS) * ng_ref[...]
            r = r_ref[rows, hv(h)].astype(F32)
            o_ref[rows, hv(h)] = (on * (r * jax.nn.sigmoid(r))).astype(o_ref.dtype)
        return carry

    def run(scaled):
        per_step = GLA_CHUNKS_PER_STEP if scaled else 1
        lax.fori_loop(0, n_chunks // per_step,
                      functools.partial(chunk_body, scaled, per_step), 0)

    lax.cond(scaled_ok, lambda: run(True), lambda: run(False))


def _gla(z, gz, gate_w, gate_b, norm_g):
    nb = SEQ // SEQ_TB
    row = lambda b, i: b * nb + i
    return pl.pallas_call(
        _gla_kernel,
        grid=(BATCH, nb),
        in_specs=[
            pl.BlockSpec((SEQ_TB, GLA_DK), lambda b, i: (row(b, i), 0)),
            pl.BlockSpec((SEQ_TB, GLA_DK), lambda b, i: (row(b, i), 1)),
            pl.BlockSpec((SEQ_TB, GLA_DV), lambda b, i: (row(b, i), 1)),
            pl.BlockSpec((SEQ_TB, GLA_DV), lambda b, i: (row(b, i), 2)),
            pl.BlockSpec((SEQ_TB, LANES), lambda b, i: (row(b, i), 0)),
            pl.BlockSpec((LANES, GLA_DK), lambda b, i: (0, 0)),
            pl.BlockSpec((1, GLA_DK), lambda b, i: (0, 0)),
            pl.BlockSpec((1, GLA_HEAD_V), lambda b, i: (0, 0)),
        ],
        out_specs=pl.BlockSpec((SEQ_TB, GLA_DV), lambda b, i: (row(b, i), 0)),
        out_shape=jax.ShapeDtypeStruct((TOKENS, GLA_DV), BF16),
        scratch_shapes=[pltpu.VMEM((SEQ_TB, GLA_DK), F32),
                        pltpu.VMEM((SEQ_TB, GLA_DK), F32),
                        pltpu.VMEM((GLA_HEADS, GLA_HEAD_V, GLA_HEAD_K), F32)],
        compiler_params=_cparams(("parallel", "arbitrary")),
        name="gla",
    )(z, z, z, z, gz, gate_w, gate_b, norm_g)


def _conv_kernel(ca_ref, cb_ref, ha_ref, hb_ref, w_ref, wb_ref, lg_ref, lb_ref, o_ref,
                 sh_ref, y_ref):
    H = CONV_HALO
    n_sh = SUBLANES
    keep = (pl.program_id(1) > 0).astype(F32)

    u_main = ca_ref[...].astype(F32) * jax.nn.sigmoid(cb_ref[...].astype(F32))
    u_halo = ha_ref[...].astype(F32) * jax.nn.sigmoid(hb_ref[...].astype(F32)) * keep
    for r in range(n_sh):
        sh_ref[r, 0:H - r, :] = u_halo[r:H, :]
        sh_ref[r, H - r:H - r + SEQ_TB, :] = u_main

    first = H - (CONV_WIDTH - 1)
    ct_w = 128

    groups = CONV_RC // SUBLANES

    def row_body(cs, bias, rc, carry):
        t0 = pl.multiple_of(rc * CONV_RC, CONV_RC)
        accs = [bias, None]
        for r in range(n_sh):
            offs = [o for o in range(first, first + CONV_WIDTH) if o % n_sh == r]
            lo, hi = offs[0] - r, offs[-1] - r
            slab = sh_ref[r, pl.ds(t0 + lo, CONV_RC + hi - lo), cs]
            for off in offs:
                a = off - r - lo
                win = slab[a:a + CONV_RC].reshape(groups, SUBLANES, ct_w)
                term = w_ref[off - first, :, cs][None] * win
                accs[r % 2] = term if accs[r % 2] is None else accs[r % 2] + term
        y_ref[pl.ds(t0, CONV_RC), cs] = (accs[0] + accs[1]).reshape(CONV_RC, ct_w)
        return carry

    for ct in range(CONV_CH // ct_w):
        cs = slice(ct * ct_w, (ct + 1) * ct_w)
        bias = jnp.broadcast_to(wb_ref[:, cs][None], (groups, SUBLANES, ct_w))
        lax.fori_loop(0, SEQ_TB // CONV_RC, functools.partial(row_body, cs, bias), 0)

    def ln_body(rc, carry):
        sl = pl.ds(pl.multiple_of(rc * 64, 64), 64)
        y = y_ref[sl, :]
        mu = jnp.mean(y, axis=-1, keepdims=True)
        var = jnp.mean(jnp.square(y - mu), axis=-1, keepdims=True)
        t = (y - mu) * lax.rsqrt(var + EPS) * lg_ref[...] + lb_ref[...]
        o_ref[sl, :] = (t * jax.nn.sigmoid(t)).astype(o_ref.dtype)
        return carry

    lax.fori_loop(0, SEQ_TB // 64, ln_body, 0, unroll=2)


def _conv(z, w, wb, ln_g, ln_b):
    nb = SEQ // SEQ_TB
    hb = SEQ_TB // CONV_HALO
    ca_col = (2 * GLA_DK + 2 * GLA_DV) // CONV_CH
    row = lambda b, i: b * nb + i
    halo = lambda b, i: jnp.maximum(row(b, i) * hb - 1, 0)
    return pl.pallas_call(
        _conv_kernel,
        grid=(BATCH, nb),
        in_specs=[
            pl.BlockSpec((SEQ_TB, CONV_CH), lambda b, i: (row(b, i), ca_col)),
            pl.BlockSpec((SEQ_TB, CONV_CH), lambda b, i: (row(b, i), ca_col + 1)),
            pl.BlockSpec((CONV_HALO, CONV_CH), lambda b, i: (halo(b, i), ca_col)),
            pl.BlockSpec((CONV_HALO, CONV_CH), lambda b, i: (halo(b, i), ca_col + 1)),
            pl.BlockSpec((CONV_WIDTH, SUBLANES, CONV_CH), lambda b, i: (0, 0, 0)),
            pl.BlockSpec((1, CONV_CH), lambda b, i: (0, 0)),
            pl.BlockSpec((1, CONV_CH), lambda b, i: (0, 0)),
            pl.BlockSpec((1, CONV_CH), lambda b, i: (0, 0)),
        ],
        out_specs=pl.BlockSpec((SEQ_TB, CONV_CH), lambda b, i: (row(b, i), 0)),
        out_shape=jax.ShapeDtypeStruct((TOKENS, CONV_CH), BF16),
        scratch_shapes=[pltpu.VMEM((SUBLANES, SEQ_TB + CONV_HALO, CONV_CH), F32),
                        pltpu.VMEM((SEQ_TB, CONV_CH), F32)],
        compiler_params=_cparams(("parallel", "parallel")),
        name="conv",
    )(z, z, z, z, w, wb, ln_g, ln_b)


ATT_HALF = ATT_TQ // 2


def _attn_kernel(q_ref, k0_ref, k1_ref, k2_ref, v0_ref, v1_ref, v2_ref,
                 m0_ref, ta_ref, tb_ref, o_ref):
    i = pl.program_id(1)
    pen = [jnp.where(i >= 2, 0.0, NEG_BIG), jnp.where(i >= 1, 0.0, NEG_BIG), None]
    k_refs = (k0_ref, k1_ref, k2_ref)
    v_refs = (v0_ref, v1_ref, v2_ref)
    nt = (((1,), (1,)), ((), ()))
    tn = (((0,), (0,)), ((), ()))
    H = ATT_HALF

    def scores(h):
        hs = slice(h * ATT_HEAD_DIM, (h + 1) * ATT_HEAD_DIM)
        q = q_ref[:, hs]
        tiles = {}
        for j in range(3):
            st = lax.dot_general(k_refs[j][:, hs], q, nt, preferred_element_type=F32)
            for kh in range(2):
                for a in range(2):
                    n = 2 * j + kh - a
                    if n < 0 or n > 4:
                        continue
                    t = st[kh * H:(kh + 1) * H, a * H:(a + 1) * H]
                    if n == 0:
                        t = t + m0_ref[...]
                    elif n == 3:
                        t = t + ta_ref[h]
                    elif n == 4:
                        t = t + tb_ref[h]
                    tiles[(j, kh, a)] = t
        maxima = []
        for a in range(2):
            mx = None
            for key in tiles:
                if key[2] == a:
                    cur = jnp.max(tiles[key], axis=0, keepdims=True)
                    if pen[key[0]] is not None:
                        cur = cur + pen[key[0]]
                    mx = cur if mx is None else jnp.maximum(mx, cur)
            maxima.append(mx)
        return tiles, maxima

    def weights(tiles, maxima):
        probs = {}
        inv_l = []
        for a in range(2):
            mine = [key for key in tiles if key[2] == a]
            shift = [maxima[a] if pj is None else maxima[a] - pj for pj in pen]
            tot = None
            for key in mine:
                p = jnp.exp2(tiles[key] - shift[key[0]])
                probs[key] = p.astype(BF16)
                cur = jnp.sum(p, axis=0, keepdims=True)
                tot = cur if tot is None else tot + cur
            inv_l.append(1.0 / tot)
        return probs, inv_l

    def values(h, probs, inv_l):
        hs = slice(h * ATT_HEAD_DIM, (h + 1) * ATT_HEAD_DIM)
        zero = jnp.zeros((H, H), BF16)
        ot = None
        for j in range(3):
            pt = jnp.concatenate(
                [jnp.concatenate([probs.get((j, kh, a), zero) for a in range(2)], axis=1)
                 for kh in range(2)], axis=0)
            cur = lax.dot_general(v_refs[j][:, hs], pt, tn, preferred_element_type=F32)
            ot = cur if ot is None else ot + cur
        ot = ot * jnp.concatenate(inv_l, axis=1)
        o_ref[:, hs] = ot.T.astype(o_ref.dtype)

    scored, weighted = {}, {}
    for t in range(ATT_HEADS + 2):
        if t < ATT_HEADS:
            scored[t] = scores(t)
        if 0 <= t - 1 < ATT_HEADS:
            weighted[t - 1] = weights(*scored.pop(t - 1))
        if 0 <= t - 2 < ATT_HEADS:
            values(t - 2, *weighted.pop(t - 2))


def _attention(qkv, m0, ta, tb):
    nb = SEQ // ATT_TQ
    row = lambda b, i: b * nb + i
    back = lambda d: (lambda b, i: (b * nb + jnp.maximum(i - d, 0)))
    spec = lambda rowfn, col: pl.BlockSpec((ATT_TQ, D_MODEL), lambda b, i: (rowfn(b, i), col))
    table = pl.BlockSpec((ATT_HEADS, ATT_HALF, ATT_HALF), lambda b, i: (0, 0, 0))
    return pl.pallas_call(
        _attn_kernel,
        grid=(BATCH, nb),
        in_specs=[
            spec(row, 0),
            spec(back(2), 1), spec(back(1), 1), spec(row, 1),
            spec(back(2), 2), spec(back(1), 2), spec(row, 2),
            pl.BlockSpec((ATT_HALF, ATT_HALF), lambda b, i: (0, 0)),
            table, table,
        ],
        out_specs=pl.BlockSpec((ATT_TQ, D_MODEL), lambda b, i: (row(b, i), 0)),
        out_shape=jax.ShapeDtypeStruct((TOKENS, D_MODEL), BF16),
        compiler_params=_cparams(("parallel", "parallel")),
        name="attention",
    )(qkv, qkv, qkv, qkv, qkv, qkv, qkv, m0, ta, tb)


def _attention_bias_tables(rel_bias):
    H = ATT_HALF
    assert H == REL_CLIP
    rb = rel_bias.astype(F32)
    rel = (rb - rb[:, 2 * REL_CLIP:]) * LOG2E

    def toeplitz(g):
        flat = jnp.tile(g, (1, H))[:, :H * (2 * H - 1)]
        return flat.reshape(-1, H, 2 * H - 1)[:, :, :H]

    ta = toeplitz(jnp.concatenate([jnp.zeros((ATT_HEADS, H), F32),
                                   rel[:, REL_CLIP:2 * REL_CLIP]], axis=1))
    tb = toeplitz(jnp.concatenate([rel[:, REL_CLIP:2 * REL_CLIP], rel[:, :REL_CLIP]], axis=1))
    kc = jnp.arange(H)[:, None] // ATT_CHUNK
    qc = jnp.arange(H)[None, :] // ATT_CHUNK
    tb = jnp.where((kc <= qc)[None], tb, NEG_BIG)
    m0 = jnp.where(kc >= qc, 0.0, NEG_BIG).astype(F32)
    return m0, ta, tb


def kernel(x, p, ffn_norm, ffn_w_gate, ffn_w_up, ffn_w_down, mix_norm, ab_w_in, gla_gate_w, gla_gate_b, gla_norm_g, conv_dw, conv_dw_b, conv_ln_g, conv_ln_b, ab_w_out, att_w_qkv, att_rel_bias, att_w_o, pl_norm, pl_w_gate, pl_w_proj, final_norm):
    xs = x.reshape(TOKENS, D_MODEL)
    ps = p.reshape(DEPTH, TOKENS, D_PL)
    row = lambda a: a.reshape(1, -1).astype(F32)

    def ffn(xs, i, s):
        return _ffn(xs, row(ffn_norm[i, s]), ffn_w_gate, ffn_w_up, ffn_w_down, i, s)

    for i in range(DEPTH):
        e = i // 2
        xs = ffn(xs, i, 0)
        if i % 2 == 0:
            w_in = ab_w_in[e]
            gz_lo = 2 * GLA_DK + 2 * GLA_DV
            gz_hi = gz_lo + GLA_GATE_RANK
            w_bf = w_in.astype(BF16)
            w_conv = w_bf[:, gz_hi:]
            w_gz = jnp.pad(w_bf[:, gz_lo:gz_hi], ((0, 0), (0, LANES - GLA_GATE_RANK)))
            z_all, gz = _rms_proj(
                xs, row(mix_norm[i]),
                [(w_bf, gz_lo // MM_TN), (w_conv, 2 * CONV_CH // MM_TN)],
                jnp.ones((1, AB_MAIN), F32), BF16, w_extra=w_gz)
            gate_w = jnp.pad(gla_gate_w[e], ((0, LANES - GLA_GATE_RANK), (0, 0))).astype(BF16)
            a_out = _gla(z_all, gz, gate_w, row(gla_gate_b[e]), row(gla_norm_g[e]))
            taps = jnp.broadcast_to(conv_dw[e].astype(F32)[:, None, :],
                                    (CONV_WIDTH, SUBLANES, CONV_CH))
            b_out = _conv(z_all, taps, row(conv_dw_b[e]), row(conv_ln_g[e]), row(conv_ln_b[e]))
            xs = _out_proj([a_out, b_out], ab_w_out[e], xs)
        else:
            colscale = jnp.concatenate([jnp.full((1, D_MODEL), ATT_HEAD_DIM ** -0.5 * LOG2E, F32),
                                        jnp.ones((1, 2 * D_MODEL), F32)], axis=1)
            qkv = _rms_proj(xs, row(mix_norm[i]), [(att_w_qkv[e], 3 * D_MODEL // MM_TN)],
                            colscale, BF16)
            m0, ta, tb = _attention_bias_tables(att_rel_bias[e])
            o = _attention(qkv, m0, ta, tb)
            xs = _out_proj([o], att_w_o[e], xs)
        xs = ffn(xs, i, 1)
        xs = _pl_embed(xs, ps, row(pl_norm[i]), pl_w_gate, pl_w_proj, row(final_norm),
                       i, i == DEPTH - 1)
    return xs.reshape(BATCH, SEQ, D_MODEL)
```

```python
import functools

import jax
import jax.numpy as jnp
from jax import lax
from jax.experimental import pallas as pl
from jax.experimental.pallas import tpu as pltpu

F32 = jnp.float32
BF16 = jnp.bfloat16

D_MODEL = 2048
BATCH = 4
SEQ = 2048
DEPTH = 2
TOKENS = BATCH * SEQ
D_PL = 256
D_FF = 5632
EPS = 1e-6

GLA_HEADS = 4
GLA_DK = 512
GLA_DV = 1024
GLA_HEAD_K = 128
GLA_HEAD_V = 256
GLA_GATE_RANK = 16
GLA_GATE_TAU = 16.0
GLA_CHUNK = 64
GLA_SUB = 16
GLA_SCALED_KEY_MAX = 2.0 ** 40
GLA_CHUNKS_PER_STEP = 2
CONV_CH = 1024
CONV_WIDTH = 31
AB_MAIN = 2 * GLA_DK + 2 * GLA_DV + 2 * CONV_CH
ATT_HEADS = 16
ATT_HEAD_DIM = 128
ATT_CHUNK = 64
LEFT_CHUNKS = 8
REL_CLIP = 128
NEG_BIG = -1e30
LOG2E = 1.4426950408889634

V7X_VMEM_BYTES = 64 * 1024 * 1024
LANES = 128
SUBLANES = 8
VMEM_LIMIT = V7X_VMEM_BYTES - 4 * 1024 * 1024

FFN_TM = 1024
FFN_TF = 512
FFN_TF_FIRST = 256
FFN_FIRST_ROWS = 2 * FFN_TM
FFN_TN = 512
MM_TM = 1024
MM_TN = 1024
PL_TM = 256
OP_TM = 512
SEQ_TB = 512
CONV_HALO = 32
CONV_RC = 128
ATT_TQ = 256
RMS_ROWS = 256


def _cparams(sem):
    return pltpu.CompilerParams(dimension_semantics=sem, vmem_limit_bytes=VMEM_LIMIT)


def _rms_rows_to(dst_ref, x_ref, g_ref, rows):
    def body(c, carry):
        sl = pl.ds(pl.multiple_of(c * RMS_ROWS, RMS_ROWS), RMS_ROWS)
        x = x_ref[sl, :]
        ms = jnp.mean(x * x, axis=-1, keepdims=True)
        dst_ref[sl, :] = (x * lax.rsqrt(ms + EPS) * g_ref[...]).astype(dst_ref.dtype)
        return carry
    lax.fori_loop(0, rows // RMS_ROWS, body, 0)


def _ffn_step(f, load_residual, g_ref, wg_ref, wu_ref, wd_ref, o_ref, h_ref):
    @pl.when(f == 0)
    def _():
        load_residual()
        _rms_rows_to(h_ref, o_ref, g_ref, o_ref.shape[0])

    h = h_ref[...]
    gate = jnp.dot(h, wg_ref[...], preferred_element_type=F32)
    up = jnp.dot(h, wu_ref[...], preferred_element_type=F32)
    a = (0.5 * gate * jax.nn.sigmoid(gate) * up).astype(BF16)

    for n in range(D_MODEL // FFN_TN):
        cs = slice(n * FFN_TN, (n + 1) * FFN_TN)
        o_ref[:, cs] += jnp.dot(a, wd_ref[:, cs], preferred_element_type=F32)


def _ffn_first_kernel(x_hbm, g_ref, wg_ref, wu_ref, wd_ref, o_ref, wgb_ref, wub_ref, wdb_ref,
                      h_ref, sem):
    def load_residual():
        copy = pltpu.make_async_copy(x_hbm.at[pl.ds(0, FFN_FIRST_ROWS)], o_ref, sem)
        copy.start()
        copy.wait()

    wgb_ref[...] = wg_ref[...].astype(BF16)
    wub_ref[...] = wu_ref[...].astype(BF16)
    wdb_ref[...] = wd_ref[...].astype(BF16)
    _ffn_step(pl.program_id(1), load_residual, g_ref, wgb_ref, wub_ref, wdb_ref, o_ref, h_ref)


def _ffn_rest_kernel(n_f, n_copy, x_ref, g_ref, wg_ref, wu_ref, wd_ref, first_ref, o_ref,
                     h_ref, sem):
    s = pl.program_id(0)

    @pl.when(s < n_copy)
    def _():
        rows = pl.ds(pl.multiple_of(s * FFN_TM, FFN_TM), FFN_TM)
        copy = pltpu.make_async_copy(first_ref.at[rows], o_ref, sem)
        copy.start()
        copy.wait()

    def load_residual():
        o_ref[...] = x_ref[...]

    @pl.when(s >= n_copy)
    def _():
        _ffn_step((s - n_copy) % n_f, load_residual, g_ref, wg_ref, wu_ref, wd_ref, o_ref,
                  h_ref)


def _ffn(x, g, wg, wu, wd, layer, half):
    m = x.shape[0]
    tf = FFN_TF_FIRST
    first, wgb, wub, wdb = pl.pallas_call(
        _ffn_first_kernel,
        grid=(1, D_FF // tf),
        in_specs=[
            pl.BlockSpec(memory_space=pl.ANY),
            pl.BlockSpec((1, D_MODEL), lambda i, f: (0, 0)),
            pl.BlockSpec((None, None, D_MODEL, tf), lambda i, f: (layer, half, 0, f)),
            pl.BlockSpec((None, None, D_MODEL, tf), lambda i, f: (layer, half, 0, f)),
            pl.BlockSpec((None, None, tf, D_MODEL), lambda i, f: (layer, half, f, 0)),
        ],
        out_specs=[
            pl.BlockSpec((FFN_FIRST_ROWS, D_MODEL), lambda i, f: (0, 0),
                         pipeline_mode=pl.Buffered(1)),
            pl.BlockSpec((D_MODEL, tf), lambda i, f: (0, f)),
            pl.BlockSpec((D_MODEL, tf), lambda i, f: (0, f)),
            pl.BlockSpec((tf, D_MODEL), lambda i, f: (f, 0)),
        ],
        out_shape=[
            jax.ShapeDtypeStruct((FFN_FIRST_ROWS, D_MODEL), F32),
            jax.ShapeDtypeStruct((D_MODEL, D_FF), BF16),
            jax.ShapeDtypeStruct((D_MODEL, D_FF), BF16),
            jax.ShapeDtypeStruct((D_FF, D_MODEL), BF16),
        ],
        scratch_shapes=[pltpu.VMEM((FFN_FIRST_ROWS, D_MODEL), BF16), pltpu.SemaphoreType.DMA(())],
        compiler_params=_cparams(("parallel", "arbitrary")),
        name="ffn_first",
    )(x, g, wg, wu, wd)
    tf = FFN_TF
    n_f = D_FF // tf
    n_copy = FFN_FIRST_ROWS // FFN_TM
    tile = lambda s: jnp.where(s < n_copy, s, n_copy + (s - n_copy) // n_f)
    f_of = lambda s: jnp.where(s < n_copy, 0, (s - n_copy) % n_f)
    return pl.pallas_call(
        functools.partial(_ffn_rest_kernel, n_f, n_copy),
        grid=(n_copy + (m // FFN_TM - n_copy) * n_f,),
        in_specs=[
            pl.BlockSpec((FFN_TM, D_MODEL), lambda s: (jnp.maximum(tile(s), n_copy), 0)),
            pl.BlockSpec((1, D_MODEL), lambda s: (0, 0)),
            pl.BlockSpec((D_MODEL, tf), lambda s: (0, f_of(s))),
            pl.BlockSpec((D_MODEL, tf), lambda s: (0, f_of(s))),
            pl.BlockSpec((tf, D_MODEL), lambda s: (f_of(s), 0)),
            pl.BlockSpec(memory_space=pl.ANY),
        ],
        out_specs=pl.BlockSpec((FFN_TM, D_MODEL), lambda s: (tile(s), 0)),
        out_shape=jax.ShapeDtypeStruct((m, D_MODEL), F32),
        scratch_shapes=[pltpu.VMEM((FFN_TM, D_MODEL), BF16), pltpu.SemaphoreType.DMA(())],
        compiler_params=_cparams(("arbitrary",)),
        name="ffn_rest",
    )(x, g, wgb, wub, wdb, first)


def _rms_proj_kernel(seg_tiles, has_extra, *refs):
    n_seg = len(seg_tiles)
    x_ref, g_ref = refs[:2]
    w_refs = refs[2:2 + n_seg]
    cs_ref = refs[2 + n_seg]
    if has_extra:
        we_ref, o_ref, oe_ref, h_ref = refs[3 + n_seg:]
    else:
        o_ref, h_ref = refs[3 + n_seg:]
    j = pl.program_id(1)

    @pl.when(j == 0)
    def _():
        _rms_rows_to(h_ref, x_ref, g_ref, MM_TM)
        if has_extra:
            oe_ref[...] = jnp.dot(h_ref[...], we_ref[...].astype(BF16),
                                  preferred_element_type=F32)

    start = 0
    for w_ref, tiles in zip(w_refs, seg_tiles):
        @pl.when((j >= start) & (j < start + tiles))
        def _(w_ref=w_ref):
            acc = jnp.dot(h_ref[...], w_ref[...].astype(BF16), preferred_element_type=F32)
            o_ref[...] = (acc * cs_ref[...]).astype(o_ref.dtype)
        start += tiles


def _rms_proj(x, g, w_list, colscale, out_dtype, w_extra=None):
    m = x.shape[0]
    seg_tiles = tuple(t for _, t in w_list)
    n_tiles = sum(seg_tiles)
    has_extra = w_extra is not None
    in_specs = [
        pl.BlockSpec((MM_TM, D_MODEL), lambda i, j: (i, 0)),
        pl.BlockSpec((1, D_MODEL), lambda i, j: (0, 0)),
    ]
    start = 0
    for _, tiles in w_list:
        in_specs.append(pl.BlockSpec(
            (D_MODEL, MM_TN),
            lambda i, j, s=start, t=tiles: (0, jnp.clip(j - s, 0, t - 1))))
        start += tiles
    in_specs.append(pl.BlockSpec((1, MM_TN), lambda i, j: (0, j)))
    out_specs = pl.BlockSpec((MM_TM, MM_TN), lambda i, j: (i, j))
    out_shape = jax.ShapeDtypeStruct((m, n_tiles * MM_TN), out_dtype)
    args = [x, g] + [w for w, _ in w_list] + [colscale]
    if has_extra:
        in_specs.append(pl.BlockSpec((D_MODEL, LANES), lambda i, j: (0, 0)))
        out_specs = [out_specs, pl.BlockSpec((MM_TM, LANES), lambda i, j: (i, 0))]
        out_shape = [out_shape, jax.ShapeDtypeStruct((m, LANES), F32)]
        args.append(w_extra)
    return pl.pallas_call(
        functools.partial(_rms_proj_kernel, seg_tiles, has_extra),
        grid=(m // MM_TM, n_tiles),
        in_specs=in_specs,
        out_specs=out_specs,
        out_shape=out_shape,
        scratch_shapes=[pltpu.VMEM((MM_TM, D_MODEL), BF16)],
        compiler_params=_cparams(("parallel", "arbitrary")),
        name="rms_proj",
    )(*args)


def _out_proj_kernel(n_lhs, *refs):
    lhs_refs = refs[:n_lhs]
    w_refs = refs[n_lhs:2 * n_lhs]
    x_ref, o_ref = refs[2 * n_lhs], refs[2 * n_lhs + 1]
    wb_refs = refs[2 * n_lhs + 2:]

    @pl.when(pl.program_id(0) == 0)
    def _():
        for w_ref, wb_ref in zip(w_refs, wb_refs):
            wb_ref[...] = w_ref[...].astype(BF16)

    acc = x_ref[...]
    for a_ref, wb_ref in zip(lhs_refs, wb_refs):
        acc = acc + jnp.dot(a_ref[...], wb_ref[...], preferred_element_type=F32)
    o_ref[...] = acc


def _out_proj(lhs_list, w, x):
    m = x.shape[0]
    n_lhs = len(lhs_list)
    kw = lhs_list[0].shape[1]
    in_specs = [pl.BlockSpec((OP_TM, kw), lambda i: (i, 0)) for _ in lhs_list]
    in_specs += [pl.BlockSpec((kw, D_MODEL), lambda i, t=t: (t, 0), pipeline_mode=pl.Buffered(1))
                 for t in range(n_lhs)]
    in_specs += [pl.BlockSpec((OP_TM, D_MODEL), lambda i: (i, 0))]
    return pl.pallas_call(
        functools.partial(_out_proj_kernel, n_lhs),
        grid=(m // OP_TM,),
        in_specs=in_specs,
        out_specs=pl.BlockSpec((OP_TM, D_MODEL), lambda i: (i, 0)),
        out_shape=jax.ShapeDtypeStruct((m, D_MODEL), F32),
        scratch_shapes=[pltpu.VMEM((kw, D_MODEL), BF16) for _ in lhs_list],
        compiler_params=_cparams(("arbitrary",)),
        name="out_proj",
    )(*lhs_list, *([w] * n_lhs), x)


def _pl_embed_kernel(final, x_ref, p_ref, g_ref, wg_ref, wp_ref, fg_ref, o_ref,
                     h_ref, wgb_ref, wpb_ref):
    @pl.when(pl.program_id(0) == 0)
    def _():
        wgb_ref[...] = wg_ref[...].astype(BF16)
        wpb_ref[...] = wp_ref[...].astype(BF16)

    _rms_rows_to(h_ref, x_ref, g_ref, PL_TM)
    gate = jax.nn.sigmoid(jnp.dot(h_ref[...], wgb_ref[...], preferred_element_type=F32))
    proj = jnp.dot(p_ref[...].astype(BF16), wpb_ref[...], preferred_element_type=F32)
    y = x_ref[...] + gate * proj
    if final:
        ms = jnp.mean(y * y, axis=-1, keepdims=True)
        y = y * lax.rsqrt(ms + EPS) * fg_ref[...]
    o_ref[...] = y


def _pl_embed(x, p, g, wg, wp, final_g, layer, final):
    m = x.shape[0]
    return pl.pallas_call(
        functools.partial(_pl_embed_kernel, final),
        grid=(m // PL_TM,),
        in_specs=[
            pl.BlockSpec((PL_TM, D_MODEL), lambda i: (i, 0)),
            pl.BlockSpec((None, PL_TM, D_PL), lambda i: (layer, i, 0)),
            pl.BlockSpec((1, D_MODEL), lambda i: (0, 0)),
            pl.BlockSpec((None, D_MODEL, D_MODEL), lambda i: (layer, 0, 0),
                         pipeline_mode=pl.Buffered(1)),
            pl.BlockSpec((None, D_PL, D_MODEL), lambda i: (layer, 0, 0),
                         pipeline_mode=pl.Buffered(1)),
            pl.BlockSpec((1, D_MODEL), lambda i: (0, 0)),
        ],
        out_specs=pl.BlockSpec((PL_TM, D_MODEL), lambda i: (i, 0)),
        out_shape=jax.ShapeDtypeStruct((m, D_MODEL), F32),
        scratch_shapes=[pltpu.VMEM((PL_TM, D_MODEL), BF16),
                        pltpu.VMEM((D_MODEL, D_MODEL), BF16),
                        pltpu.VMEM((D_PL, D_MODEL), BF16)],
        compiler_params=_cparams(("arbitrary",)),
        name="pl_embed",
    )(x, p, g, wg, wp, final_g)


def _gla_kernel(q_ref, k_ref, v_ref, r_ref, gz_ref, gw_ref, gb_ref, ng_ref, o_ref,
                b_ref, kh_ref, s_ref):
    C, SB = GLA_CHUNK, GLA_SUB
    n_chunks = SEQ_TB // C

    @pl.when(pl.program_id(1) == 0)
    def _():
        s_ref[...] = jnp.zeros_like(s_ref)

    lin = jnp.dot(gz_ref[...].astype(BF16), gw_ref[...], preferred_element_type=F32) + gb_ref[...]
    log_a = -(jnp.maximum(-lin, 0.0) + jnp.log1p(jnp.exp(-jnp.abs(lin)))) * (LOG2E / GLA_GATE_TAU)
    tri = (lax.broadcasted_iota(jnp.int32, (C, C), 0)
           >= lax.broadcasted_iota(jnp.int32, (C, C), 1)).astype(F32)
    for c in range(n_chunks):
        b_ref[c * C:(c + 1) * C, :] = jnp.dot(
            tri, log_a[c * C:(c + 1) * C, :], preferred_element_type=F32,
            precision=lax.Precision.HIGHEST)
    kh_ref[...] = k_ref[...].astype(F32) * jnp.exp2(-b_ref[...])
    scaled_ok = jnp.max(jnp.abs(kh_ref[...])) <= GLA_SCALED_KEY_MAX

    lane = lax.broadcasted_iota(jnp.int32, (SB, C), 1)
    trow = lax.broadcasted_iota(jnp.int32, (SB, C), 0)
    causal = (lax.broadcasted_iota(jnp.int32, (C, C), 0)
              >= lax.broadcasted_iota(jnp.int32, (C, C), 1))
    nt = (((1,), (1,)), ((), ()))
    tn = (((0,), (0,)), ((), ()))

    def guarded_products(q, k, b):
        blocks = []
        for i in range(C // SB):
            s0 = i * SB
            q_i = q[s0:s0 + SB]
            b_i = b[s0:s0 + SB]
            acc = jnp.zeros((SB, C), F32)
            if i > 0:
                p_i = b[s0 - 1:s0, :]
                qt = (q_i * jnp.exp2(b_i - p_i)).astype(BF16)
                kt = (k * jnp.exp2(p_i - b)).astype(BF16)
                off = lax.dot_general(qt, kt, nt, preferred_element_type=F32)
                acc = jnp.where(lane < s0, off, 0.0)
            for s in range(SB):
                b_s = b[s0 + s:s0 + s + 1, :]
                k_s = k[s0 + s:s0 + s + 1, :]
                e = jnp.exp2(b_i - b_s)
                col = jnp.sum(q_i * (k_s * e), axis=1, keepdims=True)
                acc = jnp.where((lane == s0 + s) & (trow >= s), col, acc)
            blocks.append(acc)
        return jnp.concatenate(blocks, axis=0)

    def chunk_body(scaled, per_step, it, carry):
        units = []
        for u in range(per_step):
            rows = pl.ds(pl.multiple_of((it * per_step + u) * C, C), C)
            for h in range(GLA_HEADS):
                units.append((u, h, rows))
        hk = lambda h: slice(h * GLA_HEAD_K, (h + 1) * GLA_HEAD_K)
        hv = lambda h: slice(h * GLA_HEAD_V, (h + 1) * GLA_HEAD_V)

        part = {}
        for u, h, rows in units:
            q = q_ref[rows, hk(h)].astype(F32) * (GLA_HEAD_K ** -0.5)
            v = v_ref[rows, hv(h)].astype(BF16)
            b = b_ref[rows, hk(h)]
            b_last = b[C - 1:C, :]
            qb = (q * jnp.exp2(b)).astype(BF16)
            if scaled:
                kh = kh_ref[rows, hk(h)]
                a_mat = jnp.where(causal, lax.dot_general(qb, kh.astype(BF16), nt,
                                                          preferred_element_type=F32), 0.0)
                k_dec = (kh * jnp.exp2(b_last)).astype(BF16)
            else:
                k = k_ref[rows, hk(h)].astype(F32)
                a_mat = guarded_products(q, k, b)
                k_dec = (k * jnp.exp2(b_last - b)).astype(BF16)
            upd = lax.dot_general(v, k_dec, tn, preferred_element_type=F32)
            local = jnp.dot(a_mat.astype(BF16), v, preferred_element_type=F32)
            part[(u, h)] = (qb, jnp.exp2(b_last), upd, local)

        state = {}
        for h in range(GLA_HEADS):
            st = s_ref[h]
            for u in range(per_step):
                state[(u, h)] = st
                _, decay, upd, _ = part[(u, h)]
                st = st * decay + upd
            s_ref[h] = st

        for u, h, rows in units:
            qb, _, _, local = part[(u, h)]
            o = local + lax.dot_general(qb, state[(u, h)].astype(BF16), nt,
                                        preferred_element_type=F32)
            on = o * lax.rsqrt(jnp.mean(o * o, axis=-1, keepdims=True) + EPS) * ng_ref[...]
            r = r_ref[rows, hv(h)].astype(F32)
            o_ref[rows, hv(h)] = (on * (r * jax.nn.sigmoid(r))).astype(o_ref.dtype)
        return carry

    def run(scaled):
        per_step = GLA_CHUNKS_PER_STEP if scaled else 1
        lax.fori_loop(0, n_chunks // per_step,
                      functools.partial(chunk_body, scaled, per_step), 0)

    lax.cond(scaled_ok, lambda: run(True), lambda: run(False))


def _gla(z, gz, gate_w, gate_b, norm_g):
    nb = SEQ // SEQ_TB
    row = lambda b, i: b * nb + i
    return pl.pallas_call(
        _gla_kernel,
        grid=(BATCH, nb),
        in_specs=[
            pl.BlockSpec((SEQ_TB, GLA_DK), lambda b, i: (row(b, i), 0)),
            pl.BlockSpec((SEQ_TB, GLA_DK), lambda b, i: (row(b, i), 1)),
            pl.BlockSpec((SEQ_TB, GLA_DV), lambda b, i: (row(b, i), 1)),
            pl.BlockSpec((SEQ_TB, GLA_DV), lambda b, i: (row(b, i), 2)),
            pl.BlockSpec((SEQ_TB, LANES), lambda b, i: (row(b, i), 0)),
            pl.BlockSpec((LANES, GLA_DK), lambda b, i: (0, 0)),
            pl.BlockSpec((1, GLA_DK), lambda b, i: (0, 0)),
            pl.BlockSpec((1, GLA_HEAD_V), lambda b, i: (0, 0)),
        ],
        out_specs=pl.BlockSpec((SEQ_TB, GLA_DV), lambda b, i: (row(b, i), 0)),
        out_shape=jax.ShapeDtypeStruct((TOKENS, GLA_DV), BF16),
        scratch_shapes=[pltpu.VMEM((SEQ_TB, GLA_DK), F32),
                        pltpu.VMEM((SEQ_TB, GLA_DK), F32),
                        pltpu.VMEM((GLA_HEADS, GLA_HEAD_V, GLA_HEAD_K), F32)],
        compiler_params=_cparams(("parallel", "arbitrary")),
        name="gla",
    )(z, z, z, z, gz, gate_w, gate_b, norm_g)


def _conv_kernel(ca_ref, cb_ref, ha_ref, hb_ref, w_ref, wb_ref, lg_ref, lb_ref, o_ref,
                 sh_ref, y_ref):
    H = CONV_HALO
    n_sh = SUBLANES
    keep = (pl.program_id(1) > 0).astype(F32)

    u_main = ca_ref[...].astype(F32) * jax.nn.sigmoid(cb_ref[...].astype(F32))
    u_halo = ha_ref[...].astype(F32) * jax.nn.sigmoid(hb_ref[...].astype(F32)) * keep
    for r in range(n_sh):
        sh_ref[r, 0:H - r, :] = u_halo[r:H, :]
        sh_ref[r, H - r:H - r + SEQ_TB, :] = u_main

    first = H - (CONV_WIDTH - 1)
    ct_w = 128

    groups = CONV_RC // SUBLANES

    def row_body(cs, bias, rc, carry):
        t0 = pl.multiple_of(rc * CONV_RC, CONV_RC)
        accs = [bias, None]
        for r in range(n_sh):
            offs = [o for o in range(first, first + CONV_WIDTH) if o % n_sh == r]
            lo, hi = offs[0] - r, offs[-1] - r
            slab = sh_ref[r, pl.ds(t0 + lo, CONV_RC + hi - lo), cs]
            for off in offs:
                a = off - r - lo
                win = slab[a:a + CONV_RC].reshape(groups, SUBLANES, ct_w)
                term = w_ref[off - first, :, cs][None] * win
                accs[r % 2] = term if accs[r % 2] is None else accs[r % 2] + term
        y_ref[pl.ds(t0, CONV_RC), cs] = (accs[0] + accs[1]).reshape(CONV_RC, ct_w)
        return carry

    for ct in range(CONV_CH // ct_w):
        cs = slice(ct * ct_w, (ct + 1) * ct_w)
        bias = jnp.broadcast_to(wb_ref[:, cs][None], (groups, SUBLANES, ct_w))
        lax.fori_loop(0, SEQ_TB // CONV_RC, functools.partial(row_body, cs, bias), 0)

    def ln_body(rc, carry):
        sl = pl.ds(pl.multiple_of(rc * 64, 64), 64)
        y = y_ref[sl, :]
        mu = jnp.mean(y, axis=-1, keepdims=True)
        var = jnp.mean(jnp.square(y - mu), axis=-1, keepdims=True)
        t = (y - mu) * lax.rsqrt(var + EPS) * lg_ref[...] + lb_ref[...]
        o_ref[sl, :] = (t * jax.nn.sigmoid(t)).astype(o_ref.dtype)
        return carry

    lax.fori_loop(0, SEQ_TB // 64, ln_body, 0, unroll=2)


def _conv(z, w, wb, ln_g, ln_b):
    nb = SEQ // SEQ_TB
    hb = SEQ_TB // CONV_HALO
    ca_col = (2 * GLA_DK + 2 * GLA_DV) // CONV_CH
    row = lambda b, i: b * nb + i
    halo = lambda b, i: jnp.maximum(row(b, i) * hb - 1, 0)
    return pl.pallas_call(
        _conv_kernel,
        grid=(BATCH, nb),
        in_specs=[
            pl.BlockSpec((SEQ_TB, CONV_CH), lambda b, i: (row(b, i), ca_col)),
            pl.BlockSpec((SEQ_TB, CONV_CH), lambda b, i: (row(b, i), ca_col + 1)),
            pl.BlockSpec((CONV_HALO, CONV_CH), lambda b, i: (halo(b, i), ca_col)),
            pl.BlockSpec((CONV_HALO, CONV_CH), lambda b, i: (halo(b, i), ca_col + 1)),
            pl.BlockSpec((CONV_WIDTH, SUBLANES, CONV_CH), lambda b, i: (0, 0, 0)),
            pl.BlockSpec((1, CONV_CH), lambda b, i: (0, 0)),
            pl.BlockSpec((1, CONV_CH), lambda b, i: (0, 0)),
            pl.BlockSpec((1, CONV_CH), lambda b, i: (0, 0)),
        ],
        out_specs=pl.BlockSpec((SEQ_TB, CONV_CH), lambda b, i: (row(b, i), 0)),
        out_shape=jax.ShapeDtypeStruct((TOKENS, CONV_CH), BF16),
        scratch_shapes=[pltpu.VMEM((SUBLANES, SEQ_TB + CONV_HALO, CONV_CH), F32),
                        pltpu.VMEM((SEQ_TB, CONV_CH), F32)],
        compiler_params=_cparams(("parallel", "parallel")),
        name="conv",
    )(z, z, z, z, w, wb, ln_g, ln_b)


ATT_HALF = ATT_TQ // 2


def _attn_kernel(q_ref, k0_ref, k1_ref, k2_ref, v0_ref, v1_ref, v2_ref,
                 m0_ref, ta_ref, tb_ref, o_ref):
    i = pl.program_id(1)
    pen = [jnp.where(i >= 2, 0.0, NEG_BIG), jnp.where(i >= 1, 0.0, NEG_BIG), None]
    k_refs = (k0_ref, k1_ref, k2_ref)
    v_refs = (v0_ref, v1_ref, v2_ref)
    nt = (((1,), (1,)), ((), ()))
    tn = (((0,), (0,)), ((), ()))
    H = ATT_HALF

    def scores(h):
        hs = slice(h * ATT_HEAD_DIM, (h + 1) * ATT_HEAD_DIM)
        q = q_ref[:, hs]
        tiles = {}
        for j in range(3):
            st = lax.dot_general(k_refs[j][:, hs], q, nt, preferred_element_type=F32)
            for kh in range(2):
                for a in range(2):
                    n = 2 * j + kh - a
                    if n < 0 or n > 4:
                        continue
                    t = st[kh * H:(kh + 1) * H, a * H:(a + 1) * H]
                    if n == 0:
                        t = t + m0_ref[...]
                    elif n == 3:
                        t = t + ta_ref[h]
                    elif n == 4:
                        t = t + tb_ref[h]
                    tiles[(j, kh, a)] = t
        maxima = []
        for a in range(2):
            mx = None
            for key in tiles:
                if key[2] == a:
                    cur = jnp.max(tiles[key], axis=0, keepdims=True)
                    if pen[key[0]] is not None:
                        cur = cur + pen[key[0]]
                    mx = cur if mx is None else jnp.maximum(mx, cur)
            maxima.append(mx)
        return tiles, maxima

    def weights(tiles, maxima):
        probs = {}
        inv_l = []
        for a in range(2):
            mine = [key for key in tiles if key[2] == a]
            shift = [maxima[a] if pj is None else maxima[a] - pj for pj in pen]
            tot = None
            for key in mine:
                p = jnp.exp2(tiles[key] - shift[key[0]])
                probs[key] = p.astype(BF16)
                cur = jnp.sum(p, axis=0, keepdims=True)
                tot = cur if tot is None else tot + cur
            inv_l.append(1.0 / tot)
        return probs, inv_l

    def values(h, probs, inv_l):
        hs = slice(h * ATT_HEAD_DIM, (h + 1) * ATT_HEAD_DIM)
        zero = jnp.zeros((H, H), BF16)
        ot = None
        for j in range(3):
            pt = jnp.concatenate(
                [jnp.concatenate([probs.get((j, kh, a), zero) for a in range(2)], axis=1)
                 for kh in range(2)], axis=0)
            cur = lax.dot_general(v_refs[j][:, hs], pt, tn, preferred_element_type=F32)
            ot = cur if ot is None else ot + cur
        ot = ot * jnp.concatenate(inv_l, axis=1)
        o_ref[:, hs] = ot.T.astype(o_ref.dtype)

    scored, weighted = {}, {}
    for t in range(ATT_HEADS + 2):
        if t < ATT_HEADS:
            scored[t] = scores(t)
        if 0 <= t - 1 < ATT_HEADS:
            weighted[t - 1] = weights(*scored.pop(t - 1))
        if 0 <= t - 2 < ATT_HEADS:
            values(t - 2, *weighted.pop(t - 2))


def _attention(qkv, m0, ta, tb):
    nb = SEQ // ATT_TQ
    row = lambda b, i: b * nb + i
    back = lambda d: (lambda b, i: (b * nb + jnp.maximum(i - d, 0)))
    spec = lambda rowfn, col: pl.BlockSpec((ATT_TQ, D_MODEL), lambda b, i: (rowfn(b, i), col))
    table = pl.BlockSpec((ATT_HEADS, ATT_HALF, ATT_HALF), lambda b, i: (0, 0, 0))
    return pl.pallas_call(
        _attn_kernel,
        grid=(BATCH, nb),
        in_specs=[
            spec(row, 0),
            spec(back(2), 1), spec(back(1), 1), spec(row, 1),
            spec(back(2), 2), spec(back(1), 2), spec(row, 2),
            pl.BlockSpec((ATT_HALF, ATT_HALF), lambda b, i: (0, 0)),
            table, table,
        ],
        out_specs=pl.BlockSpec((ATT_TQ, D_MODEL), lambda b, i: (row(b, i), 0)),
        out_shape=jax.ShapeDtypeStruct((TOKENS, D_MODEL), BF16),
        compiler_params=_cparams(("parallel", "parallel")),
        name="attention",
    )(qkv, qkv, qkv, qkv, qkv, qkv, qkv, m0, ta, tb)


def _attention_bias_tables(rel_bias):
    H = ATT_HALF
    assert H == REL_CLIP
    rb = rel_bias.astype(F32)
    rel = (rb - rb[:, 2 * REL_CLIP:]) * LOG2E

    def toeplitz(g):
        flat = jnp.tile(g, (1, H))[:, :H * (2 * H - 1)]
        return flat.reshape(-1, H, 2 * H - 1)[:, :, :H]

    ta = toeplitz(jnp.concatenate([jnp.zeros((ATT_HEADS, H), F32),
                                   rel[:, REL_CLIP:2 * REL_CLIP]], axis=1))
    tb = toeplitz(jnp.concatenate([rel[:, REL_CLIP:2 * REL_CLIP], rel[:, :REL_CLIP]], axis=1))
    kc = jnp.arange(H)[:, None] // ATT_CHUNK
    qc = jnp.arange(H)[None, :] // ATT_CHUNK
    tb = jnp.where((kc <= qc)[None], tb, NEG_BIG)
    m0 = jnp.where(kc >= qc, 0.0, NEG_BIG).astype(F32)
    return m0, ta, tb


def kernel(x, p, ffn_norm, ffn_w_gate, ffn_w_up, ffn_w_down, mix_norm, ab_w_in, gla_gate_w, gla_gate_b, gla_norm_g, conv_dw, conv_dw_b, conv_ln_g, conv_ln_b, ab_w_out, att_w_qkv, att_rel_bias, att_w_o, pl_norm, pl_w_gate, pl_w_proj, final_norm):
    xs = x.reshape(TOKENS, D_MODEL)
    ps = p.reshape(DEPTH, TOKENS, D_PL)
    row = lambda a: a.reshape(1, -1).astype(F32)

    def ffn(xs, i, s):
        return _ffn(xs, row(ffn_norm[i, s]), ffn_w_gate, ffn_w_up, ffn_w_down, i, s)

    for i in range(DEPTH):
        e = i // 2
        xs = ffn(xs, i, 0)
        if i % 2 == 0:
            w_in = ab_w_in[e]
            gz_lo = 2 * GLA_DK + 2 * GLA_DV
            gz_hi = gz_lo + GLA_GATE_RANK
            w_bf = w_in.astype(BF16)
            w_conv = w_bf[:, gz_hi:]
            w_gz = jnp.pad(w_bf[:, gz_lo:gz_hi], ((0, 0), (0, LANES - GLA_GATE_RANK)))
            z_all, gz = _rms_proj(
                xs, row(mix_norm[i]),
                [(w_bf, gz_lo // MM_TN), (w_conv, 2 * CONV_CH // MM_TN)],
                jnp.ones((1, AB_MAIN), F32), BF16, w_extra=w_gz)
            gate_w = jnp.pad(gla_gate_w[e], ((0, LANES - GLA_GATE_RANK), (0, 0))).astype(BF16)
            a_out = _gla(z_all, gz, gate_w, row(gla_gate_b[e]), row(gla_norm_g[e]))
            taps = jnp.broadcast_to(conv_dw[e].astype(F32)[:, None, :],
                                    (CONV_WIDTH, SUBLANES, CONV_CH))
            b_out = _conv(z_all, taps, row(conv_dw_b[e]), row(conv_ln_g[e]), row(conv_ln_b[e]))
            xs = _out_proj([a_out, b_out], ab_w_out[e], xs)
        else:
            colscale = jnp.concatenate([jnp.full((1, D_MODEL), ATT_HEAD_DIM ** -0.5 * LOG2E, F32),
                                        jnp.ones((1, 2 * D_MODEL), F32)], axis=1)
            qkv = _rms_proj(xs, row(mix_norm[i]), [(att_w_qkv[e], 3 * D_MODEL // MM_TN)],
                            colscale, BF16)
            m0, ta, tb = _attention_bias_tables(att_rel_bias[e])
            o = _attention(qkv, m0, ta, tb)
            xs = _out_proj([o], att_w_o[e], xs)
        xs = ffn(xs, i, 1)
        xs = _pl_embed(xs, ps, row(pl_norm[i]), pl_w_gate, pl_w_proj, row(final_norm),
                       i, i == DEPTH - 1)
    return xs.reshape(BATCH, SEQ, D_MODEL)
```

```python
import functools

import jax
import jax.numpy as jnp
from jax import lax
from jax.experimental import pallas as pl
from jax.experimental.pallas import tpu as pltpu

F32 = jnp.float32
BF16 = jnp.bfloat16

D_MODEL = 2048
BATCH = 4
SEQ = 2048
DEPTH = 2
TOKENS = BATCH * SEQ
D_PL = 256
D_FF = 5632
EPS = 1e-6

GLA_HEADS = 4
GLA_DK = 512
GLA_DV = 1024
GLA_HEAD_K = 128
GLA_HEAD_V = 256
GLA_GATE_RANK = 16
GLA_GATE_TAU = 16.0
GLA_CHUNK = 64
GLA_SUB = 16
GLA_SCALED_KEY_MAX = 2.0 ** 40
GLA_CHUNKS_PER_STEP = 4
CONV_CH = 1024
CONV_WIDTH = 31
AB_MAIN = 2 * GLA_DK + 2 * GLA_DV + 2 * CONV_CH
ATT_HEADS = 16
ATT_HEAD_DIM = 128
ATT_CHUNK = 64
LEFT_CHUNKS = 8
REL_CLIP = 128
NEG_BIG = -1e30
LOG2E = 1.4426950408889634

V7X_VMEM_BYTES = 64 * 1024 * 1024
LANES = 128
SUBLANES = 8
VMEM_LIMIT = V7X_VMEM_BYTES - 4 * 1024 * 1024

FFN_TM = 1024
FFN_TF = 512
FFN_FIRST_ROWS = FFN_TM
FFN_TN = 512
MM_TM = 1024
MM_TN = 1024
PL_TM = 256
OP_TM = 512
SEQ_TB = 512
CONV_HALO = 32
CONV_RC = 128
ATT_TQ = 256
RMS_ROWS = 256


def _cparams(sem):
    return pltpu.CompilerParams(dimension_semantics=sem, vmem_limit_bytes=VMEM_LIMIT)


def _rms_rows_to(dst_ref, x_ref, g_ref, rows):
    def body(c, carry):
        sl = pl.ds(pl.multiple_of(c * RMS_ROWS, RMS_ROWS), RMS_ROWS)
        x = x_ref[sl, :]
        ms = jnp.mean(x * x, axis=-1, keepdims=True)
        dst_ref[sl, :] = (x * lax.rsqrt(ms + EPS) * g_ref[...]).astype(dst_ref.dtype)
        return carry
    lax.fori_loop(0, rows // RMS_ROWS, body, 0)


def _ffn_step(f, load_residual, g_ref, wgu_ref, wd_ref, o_ref, h_ref):
    @pl.when(f == 0)
    def _():
        load_residual()
        _rms_rows_to(h_ref, o_ref, g_ref, o_ref.shape[0])

    tf = wd_ref.shape[0]
    gate_up = jnp.dot(h_ref[...], wgu_ref[...], preferred_element_type=F32)
    gate, up = gate_up[:, :tf], gate_up[:, tf:]
    a = (0.5 * gate * jax.nn.sigmoid(gate) * up).astype(BF16)

    for n in range(D_MODEL // FFN_TN):
        cs = slice(n * FFN_TN, (n + 1) * FFN_TN)
        o_ref[:, cs] += jnp.dot(a, wd_ref[:, cs], preferred_element_type=F32)


def _ffn_first_kernel(x_hbm, g_ref, wg_ref, wu_ref, wd_ref, o_ref, wgub_ref, wdb_ref,
                      h_ref, sem):
    def load_residual():
        copy = pltpu.make_async_copy(x_hbm.at[pl.ds(0, FFN_FIRST_ROWS)], o_ref, sem)
        copy.start()
        copy.wait()

    tf = wd_ref.shape[0]
    wgub_ref[:, :tf] = wg_ref[...].astype(BF16)
    wgub_ref[:, tf:] = wu_ref[...].astype(BF16)
    wdb_ref[...] = wd_ref[...].astype(BF16)
    _ffn_step(pl.program_id(1), load_residual, g_ref, wgub_ref, wdb_ref, o_ref, h_ref)


def _ffn_rest_kernel(n_f, n_copy, x_ref, g_ref, wgu_ref, wd_ref, first_ref, o_ref,
                     h_ref, sem):
    s = pl.program_id(0)

    @pl.when(s < n_copy)
    def _():
        rows = pl.ds(pl.multiple_of(s * FFN_TM, FFN_TM), FFN_TM)
        copy = pltpu.make_async_copy(first_ref.at[rows], o_ref, sem)
        copy.start()
        copy.wait()

    def load_residual():
        o_ref[...] = x_ref[...]

    @pl.when(s >= n_copy)
    def _():
        _ffn_step((s - n_copy) % n_f, load_residual, g_ref, wgu_ref, wd_ref, o_ref, h_ref)


def _ffn(x, g, wg, wu, wd, layer, half):
    m = x.shape[0]
    tf = FFN_TF
    first, wgub, wdb = pl.pallas_call(
        _ffn_first_kernel,
        grid=(1, D_FF // tf),
        in_specs=[
            pl.BlockSpec(memory_space=pl.ANY),
            pl.BlockSpec((1, D_MODEL), lambda i, f: (0, 0)),
            pl.BlockSpec((None, None, D_MODEL, tf), lambda i, f: (layer, half, 0, f)),
            pl.BlockSpec((None, None, D_MODEL, tf), lambda i, f: (layer, half, 0, f)),
            pl.BlockSpec((None, None, tf, D_MODEL), lambda i, f: (layer, half, f, 0)),
        ],
        out_specs=[
            pl.BlockSpec((FFN_FIRST_ROWS, D_MODEL), lambda i, f: (0, 0),
                         pipeline_mode=pl.Buffered(1)),
            pl.BlockSpec((D_MODEL, 2 * tf), lambda i, f: (0, f)),
            pl.BlockSpec((tf, D_MODEL), lambda i, f: (f, 0)),
        ],
        out_shape=[
            jax.ShapeDtypeStruct((FFN_FIRST_ROWS, D_MODEL), F32),
            jax.ShapeDtypeStruct((D_MODEL, 2 * D_FF), BF16),
            jax.ShapeDtypeStruct((D_FF, D_MODEL), BF16),
        ],
        scratch_shapes=[pltpu.VMEM((FFN_FIRST_ROWS, D_MODEL), BF16), pltpu.SemaphoreType.DMA(())],
        compiler_params=_cparams(("parallel", "arbitrary")),
        name="ffn_first",
    )(x, g, wg, wu, wd)
    n_f = D_FF // tf
    n_copy = FFN_FIRST_ROWS // FFN_TM
    tile = lambda s: jnp.where(s < n_copy, s, n_copy + (s - n_copy) // n_f)
    f_of = lambda s: jnp.where(s < n_copy, 0, (s - n_copy) % n_f)
    return pl.pallas_call(
        functools.partial(_ffn_rest_kernel, n_f, n_copy),
        grid=(n_copy + (m // FFN_TM - n_copy) * n_f,),
        in_specs=[
            pl.BlockSpec((FFN_TM, D_MODEL), lambda s: (jnp.maximum(tile(s), n_copy), 0)),
            pl.BlockSpec((1, D_MODEL), lambda s: (0, 0)),
            pl.BlockSpec((D_MODEL, 2 * tf), lambda s: (0, f_of(s))),
            pl.BlockSpec((tf, D_MODEL), lambda s: (f_of(s), 0)),
            pl.BlockSpec(memory_space=pl.ANY),
        ],
        out_specs=pl.BlockSpec((FFN_TM, D_MODEL), lambda s: (tile(s), 0)),
        out_shape=jax.ShapeDtypeStruct((m, D_MODEL), F32),
        scratch_shapes=[pltpu.VMEM((FFN_TM, D_MODEL), BF16), pltpu.SemaphoreType.DMA(())],
        compiler_params=_cparams(("arbitrary",)),
        name="ffn_rest",
    )(x, g, wgub, wdb, first)


def _rms_proj_kernel(seg_tiles, has_extra, *refs):
    n_seg = len(seg_tiles)
    x_ref, g_ref = refs[:2]
    w_refs = refs[2:2 + n_seg]
    cs_ref = refs[2 + n_seg]
    if has_extra:
        we_ref, o_ref, oe_ref, h_ref = refs[3 + n_seg:]
    else:
        o_ref, h_ref = refs[3 + n_seg:]
    j = pl.program_id(1)

    @pl.when(j == 0)
    def _():
        _rms_rows_to(h_ref, x_ref, g_ref, MM_TM)
        if has_extra:
            oe_ref[...] = jnp.dot(h_ref[...], we_ref[...].astype(BF16),
                                  preferred_element_type=F32)

    start = 0
    for w_ref, tiles in zip(w_refs, seg_tiles):
        @pl.when((j >= start) & (j < start + tiles))
        def _(w_ref=w_ref):
            acc = jnp.dot(h_ref[...], w_ref[...].astype(BF16), preferred_element_type=F32)
            o_ref[...] = (acc * cs_ref[...]).astype(o_ref.dtype)
        start += tiles


def _rms_proj(x, g, w_list, colscale, out_dtype, w_extra=None):
    m = x.shape[0]
    seg_tiles = tuple(t for _, t in w_list)
    n_tiles = sum(seg_tiles)
    has_extra = w_extra is not None
    in_specs = [
        pl.BlockSpec((MM_TM, D_MODEL), lambda i, j: (i, 0)),
        pl.BlockSpec((1, D_MODEL), lambda i, j: (0, 0)),
    ]
    start = 0
    for _, tiles in w_list:
        in_specs.append(pl.BlockSpec(
            (D_MODEL, MM_TN),
            lambda i, j, s=start, t=tiles: (0, jnp.clip(j - s, 0, t - 1))))
        start += tiles
    in_specs.append(pl.BlockSpec((1, MM_TN), lambda i, j: (0, j)))
    out_specs = pl.BlockSpec((MM_TM, MM_TN), lambda i, j: (i, j))
    out_shape = jax.ShapeDtypeStruct((m, n_tiles * MM_TN), out_dtype)
    args = [x, g] + [w for w, _ in w_list] + [colscale]
    if has_extra:
        in_specs.append(pl.BlockSpec((D_MODEL, LANES), lambda i, j: (0, 0)))
        out_specs = [out_specs, pl.BlockSpec((MM_TM, LANES), lambda i, j: (i, 0))]
        out_shape = [out_shape, jax.ShapeDtypeStruct((m, LANES), F32)]
        args.append(w_extra)
    return pl.pallas_call(
        functools.partial(_rms_proj_kernel, seg_tiles, has_extra),
        grid=(m // MM_TM, n_tiles),
        in_specs=in_specs,
        out_specs=out_specs,
        out_shape=out_shape,
        scratch_shapes=[pltpu.VMEM((MM_TM, D_MODEL), BF16)],
        compiler_params=_cparams(("parallel", "arbitrary")),
        name="rms_proj",
    )(*args)


def _out_proj_kernel(n_lhs, *refs):
    lhs_refs = refs[:n_lhs]
    w_refs = refs[n_lhs:2 * n_lhs]
    x_ref, o_ref = refs[2 * n_lhs], refs[2 * n_lhs + 1]
    wb_refs = refs[2 * n_lhs + 2:]

    @pl.when(pl.program_id(0) == 0)
    def _():
        for w_ref, wb_ref in zip(w_refs, wb_refs):
            wb_ref[...] = w_ref[...].astype(BF16)

    acc = x_ref[...]
    for a_ref, wb_ref in zip(lhs_refs, wb_refs):
        acc = acc + jnp.dot(a_ref[...], wb_ref[...], preferred_element_type=F32)
    o_ref[...] = acc


def _out_proj(lhs_list, w, x):
    m = x.shape[0]
    n_lhs = len(lhs_list)
    kw = lhs_list[0].shape[1]
    in_specs = [pl.BlockSpec((OP_TM, kw), lambda i: (i, 0)) for _ in lhs_list]
    in_specs += [pl.BlockSpec((kw, D_MODEL), lambda i, t=t: (t, 0), pipeline_mode=pl.Buffered(1))
                 for t in range(n_lhs)]
    in_specs += [pl.BlockSpec((OP_TM, D_MODEL), lambda i: (i, 0))]
    return pl.pallas_call(
        functools.partial(_out_proj_kernel, n_lhs),
        grid=(m // OP_TM,),
        in_specs=in_specs,
        out_specs=pl.BlockSpec((OP_TM, D_MODEL), lambda i: (i, 0)),
        out_shape=jax.ShapeDtypeStruct((m, D_MODEL), F32),
        scratch_shapes=[pltpu.VMEM((kw, D_MODEL), BF16) for _ in lhs_list],
        compiler_params=_cparams(("arbitrary",)),
        name="out_proj",
    )(*lhs_list, *([w] * n_lhs), x)


def _pl_embed_kernel(final, x_ref, p_ref, g_ref, wg_ref, wp_ref, fg_ref, o_ref,
                     h_ref, wgb_ref, wpb_ref):
    @pl.when(pl.program_id(0) == 0)
    def _():
        wgb_ref[...] = wg_ref[...].astype(BF16)
        wpb_ref[...] = wp_ref[...].astype(BF16)

    _rms_rows_to(h_ref, x_ref, g_ref, PL_TM)
    gate = jax.nn.sigmoid(jnp.dot(h_ref[...], wgb_ref[...], preferred_element_type=F32))
    proj = jnp.dot(p_ref[...].astype(BF16), wpb_ref[...], preferred_element_type=F32)
    y = x_ref[...] + gate * proj
    if final:
        ms = jnp.mean(y * y, axis=-1, keepdims=True)
        y = y * lax.rsqrt(ms + EPS) * fg_ref[...]
    o_ref[...] = y


def _pl_embed(x, p, g, wg, wp, final_g, layer, final):
    m = x.shape[0]
    return pl.pallas_call(
        functools.partial(_pl_embed_kernel, final),
        grid=(m // PL_TM,),
        in_specs=[
            pl.BlockSpec((PL_TM, D_MODEL), lambda i: (i, 0)),
            pl.BlockSpec((None, PL_TM, D_PL), lambda i: (layer, i, 0)),
            pl.BlockSpec((1, D_MODEL), lambda i: (0, 0)),
            pl.BlockSpec((None, D_MODEL, D_MODEL), lambda i: (layer, 0, 0),
                         pipeline_mode=pl.Buffered(1)),
            pl.BlockSpec((None, D_PL, D_MODEL), lambda i: (layer, 0, 0),
                         pipeline_mode=pl.Buffered(1)),
            pl.BlockSpec((1, D_MODEL), lambda i: (0, 0)),
        ],
        out_specs=pl.BlockSpec((PL_TM, D_MODEL), lambda i: (i, 0)),
        out_shape=jax.ShapeDtypeStruct((m, D_MODEL), F32),
        scratch_shapes=[pltpu.VMEM((PL_TM, D_MODEL), BF16),
                        pltpu.VMEM((D_MODEL, D_MODEL), BF16),
                        pltpu.VMEM((D_PL, D_MODEL), BF16)],
        compiler_params=_cparams(("arbitrary",)),
        name="pl_embed",
    )(x, p, g, wg, wp, final_g)


def _gla_kernel(q_ref, k_ref, v_ref, r_ref, gz_ref, gw_ref, gb_ref, ng_ref, o_ref,
                b_ref, kh_ref, s_ref):
    C, SB = GLA_CHUNK, GLA_SUB
    n_chunks = SEQ_TB // C

    @pl.when(pl.program_id(1) == 0)
    def _():
        s_ref[...] = jnp.zeros_like(s_ref)

    lin = jnp.dot(gz_ref[...].astype(BF16), gw_ref[...], preferred_element_type=F32) + gb_ref[...]
    log_a = -(jnp.maximum(-lin, 0.0) + jnp.log1p(jnp.exp(-jnp.abs(lin)))) * (LOG2E / GLA_GATE_TAU)
    tri = (lax.broadcasted_iota(jnp.int32, (C, C), 0)
           >= lax.broadcasted_iota(jnp.int32, (C, C), 1)).astype(F32)
    for c in range(n_chunks):
        b_ref[c * C:(c + 1) * C, :] = jnp.dot(
            tri, log_a[c * C:(c + 1) * C, :], preferred_element_type=F32,
            precision=lax.Precision.HIGHEST)
    kh_ref[...] = k_ref[...].astype(F32) * jnp.exp2(-b_ref[...])
    scaled_ok = jnp.max(jnp.abs(kh_ref[...])) <= GLA_SCALED_KEY_MAX

    lane = lax.broadcasted_iota(jnp.int32, (SB, C), 1)
    trow = lax.broadcasted_iota(jnp.int32, (SB, C), 0)
    causal = (lax.broadcasted_iota(jnp.int32, (C, C), 0)
              >= lax.broadcasted_iota(jnp.int32, (C, C), 1))
    nt = (((1,), (1,)), ((), ()))
    tn = (((0,), (0,)), ((), ()))

    def guarded_products(q, k, b):
        blocks = []
        for i in range(C // SB):
            s0 = i * SB
            q_i = q[s0:s0 + SB]
            b_i = b[s0:s0 + SB]
            acc = jnp.zeros((SB, C), F32)
            if i > 0:
                p_i = b[s0 - 1:s0, :]
                qt = (q_i * jnp.exp2(b_i - p_i)).astype(BF16)
                kt = (k * jnp.exp2(p_i - b)).astype(BF16)
                off = lax.dot_general(qt, kt, nt, preferred_element_type=F32)
                acc = jnp.where(lane < s0, off, 0.0)
            for s in range(SB):
                b_s = b[s0 + s:s0 + s + 1, :]
                k_s = k[s0 + s:s0 + s + 1, :]
                e = jnp.exp2(b_i - b_s)
                col = jnp.sum(q_i * (k_s * e), axis=1, keepdims=True)
                acc = jnp.where((lane == s0 + s) & (trow >= s), col, acc)
            blocks.append(acc)
        return jnp.concatenate(blocks, axis=0)

    def chunk_body(scaled, per_step, it, carry):
        units = []
        for u in range(per_step):
            rows = pl.ds(pl.multiple_of((it * per_step + u) * C, C), C)
            for h in range(GLA_HEADS):
                units.append((u, h, rows))
        hk = lambda h: slice(h * GLA_HEAD_K, (h + 1) * GLA_HEAD_K)
        hv = lambda h: slice(h * GLA_HEAD_V, (h + 1) * GLA_HEAD_V)

        part = {}
        for u, h, rows in units:
            q = q_ref[rows, hk(h)].astype(F32) * (GLA_HEAD_K ** -0.5)
            v = v_ref[rows, hv(h)].astype(BF16)
            b = b_ref[rows, hk(h)]
            b_last = b[C - 1:C, :]
            qb = (q * jnp.exp2(b)).astype(BF16)
            if scaled:
                kh = kh_ref[rows, hk(h)]
                a_mat = jnp.where(causal, lax.dot_general(qb, kh.astype(BF16), nt,
                                                          preferred_element_type=F32), 0.0)
                k_dec = (kh * jnp.exp2(b_last)).astype(BF16)
            else:
                k = k_ref[rows, hk(h)].astype(F32)
                a_mat = guarded_products(q, k, b)
                k_dec = (k * jnp.exp2(b_last - b)).astype(BF16)
            upd = lax.dot_general(v, k_dec, tn, preferred_element_type=F32)
            local = jnp.dot(a_mat.astype(BF16), v, preferred_element_type=F32)
            part[(u, h)] = (qb, jnp.exp2(b_last), upd, local)

        state = {}
        for h in range(GLA_HEADS):
            st = s_ref[h]
            for u in range(per_step):
                state[(u, h)] = st
                _, decay, upd, _ = part[(u, h)]
                st = st * decay + upd
            s_ref[h] = st

        for u, h, rows in units:
            qb, _, _, local = part[(u, h)]
            o = local + lax.dot_general(qb, state[(u, h)].astype(BF16), nt,
                                        preferred_element_type=F32)
            on = o * lax.rsqrt(jnp.mean(o * o, axis=-1, keepdims=True) + EPS) * ng_ref[...]
            r = r_ref[rows, hv(h)].astype(F32)
            o_ref[rows, hv(h)] = (on * (r * jax.nn.sigmoid(r))).astype(o_ref.dtype)
        return carry

    def run(scaled):
        per_step = GLA_CHUNKS_PER_STEP if scaled else 1
        lax.fori_loop(0, n_chunks // per_step,
                      functools.partial(chunk_body, scaled, per_step), 0)

    lax.cond(scaled_ok, lambda: run(True), lambda: run(False))


def _gla(z, gz, gate_w, gate_b, norm_g):
    nb = SEQ // SEQ_TB
    row = lambda b, i: b * nb + i
    return pl.pallas_call(
        _gla_kernel,
        grid=(BATCH, nb),
        in_specs=[
            pl.BlockSpec((SEQ_TB, GLA_DK), lambda b, i: (row(b, i), 0)),
            pl.BlockSpec((SEQ_TB, GLA_DK), lambda b, i: (row(b, i), 1)),
            pl.BlockSpec((SEQ_TB, GLA_DV), lambda b, i: (row(b, i), 1)),
            pl.BlockSpec((SEQ_TB, GLA_DV), lambda b, i: (row(b, i), 2)),
            pl.BlockSpec((SEQ_TB, LANES), lambda b, i: (row(b, i), 0)),
            pl.BlockSpec((LANES, GLA_DK), lambda b, i: (0, 0)),
            pl.BlockSpec((1, GLA_DK), lambda b, i: (0, 0)),
            pl.BlockSpec((1, GLA_HEAD_V), lambda b, i: (0, 0)),
        ],
        out_specs=pl.BlockSpec((SEQ_TB, GLA_DV), lambda b, i: (row(b, i), 0)),
        out_shape=jax.ShapeDtypeStruct((TOKENS, GLA_DV), BF16),
        scratch_shapes=[pltpu.VMEM((SEQ_TB, GLA_DK), F32),
                        pltpu.VMEM((SEQ_TB, GLA_DK), F32),
                        pltpu.VMEM((GLA_HEADS, GLA_HEAD_V, GLA_HEAD_K), F32)],
        compiler_params=_cparams(("parallel", "arbitrary")),
        name="gla",
    )(z, z, z, z, gz, gate_w, gate_b, norm_g)


def _conv_kernel(ca_ref, cb_ref, ha_ref, hb_ref, w_ref, wb_ref, lg_ref, lb_ref, o_ref,
                 sh_ref, y_ref):
    H = CONV_HALO
    n_sh = SUBLANES
    keep = (pl.program_id(1) > 0).astype(F32)

    u_main = ca_ref[...].astype(F32) * jax.nn.sigmoid(cb_ref[...].astype(F32))
    u_halo = ha_ref[...].astype(F32) * jax.nn.sigmoid(hb_ref[...].astype(F32)) * keep
    for r in range(n_sh):
        sh_ref[r, 0:H - r, :] = u_halo[r:H, :]
        sh_ref[r, H - r:H - r + SEQ_TB, :] = u_main

    first = H - (CONV_WIDTH - 1)
    ct_w = 128

    groups = CONV_RC // SUBLANES

    def row_body(cs, bias, rc, carry):
        t0 = pl.multiple_of(rc * CONV_RC, CONV_RC)
        accs = [bias, None]
        for r in range(n_sh):
            offs = [o for o in range(first, first + CONV_WIDTH) if o % n_sh == r]
            lo, hi = offs[0] - r, offs[-1] - r
            slab = sh_ref[r, pl.ds(t0 + lo, CONV_RC + hi - lo), cs]
            for off in offs:
                a = off - r - lo
                win = slab[a:a + CONV_RC].reshape(groups, SUBLANES, ct_w)
                term = w_ref[off - first, :, cs][None] * win
                accs[r % 2] = term if accs[r % 2] is None else accs[r % 2] + term
        y_ref[pl.ds(t0, CONV_RC), cs] = (accs[0] + accs[1]).reshape(CONV_RC, ct_w)
        return carry

    for ct in range(CONV_CH // ct_w):
        cs = slice(ct * ct_w, (ct + 1) * ct_w)
        bias = jnp.broadcast_to(wb_ref[:, cs][None], (groups, SUBLANES, ct_w))
        lax.fori_loop(0, SEQ_TB // CONV_RC, functools.partial(row_body, cs, bias), 0)

    def ln_body(rc, carry):
        sl = pl.ds(pl.multiple_of(rc * 64, 64), 64)
        y = y_ref[sl, :]
        mu = jnp.mean(y, axis=-1, keepdims=True)
        var = jnp.mean(jnp.square(y - mu), axis=-1, keepdims=True)
        t = (y - mu) * lax.rsqrt(var + EPS) * lg_ref[...] + lb_ref[...]
        o_ref[sl, :] = (t * jax.nn.sigmoid(t)).astype(o_ref.dtype)
        return carry

    lax.fori_loop(0, SEQ_TB // 64, ln_body, 0, unroll=2)


def _conv(z, w, wb, ln_g, ln_b):
    nb = SEQ // SEQ_TB
    hb = SEQ_TB // CONV_HALO
    ca_col = (2 * GLA_DK + 2 * GLA_DV) // CONV_CH
    row = lambda b, i: b * nb + i
    halo = lambda b, i: jnp.maximum(row(b, i) * hb - 1, 0)
    return pl.pallas_call(
        _conv_kernel,
        grid=(BATCH, nb),
        in_specs=[
            pl.BlockSpec((SEQ_TB, CONV_CH), lambda b, i: (row(b, i), ca_col)),
            pl.BlockSpec((SEQ_TB, CONV_CH), lambda b, i: (row(b, i), ca_col + 1)),
            pl.BlockSpec((CONV_HALO, CONV_CH), lambda b, i: (halo(b, i), ca_col)),
            pl.BlockSpec((CONV_HALO, CONV_CH), lambda b, i: (halo(b, i), ca_col + 1)),
            pl.BlockSpec((CONV_WIDTH, SUBLANES, CONV_CH), lambda b, i: (0, 0, 0)),
            pl.BlockSpec((1, CONV_CH), lambda b, i: (0, 0)),
            pl.BlockSpec((1, CONV_CH), lambda b, i: (0, 0)),
            pl.BlockSpec((1, CONV_CH), lambda b, i: (0, 0)),
        ],
        out_specs=pl.BlockSpec((SEQ_TB, CONV_CH), lambda b, i: (row(b, i), 0)),
        out_shape=jax.ShapeDtypeStruct((TOKENS, CONV_CH), BF16),
        scratch_shapes=[pltpu.VMEM((SUBLANES, SEQ_TB + CONV_HALO, CONV_CH), F32),
                        pltpu.VMEM((SEQ_TB, CONV_CH), F32)],
        compiler_params=_cparams(("parallel", "parallel")),
        name="conv",
    )(z, z, z, z, w, wb, ln_g, ln_b)


ATT_HALF = ATT_TQ // 2


def _attn_kernel(q_ref, k0_ref, k1_ref, k2_ref, v0_ref, v1_ref, v2_ref,
                 m0_ref, ta_ref, tb_ref, o_ref):
    i = pl.program_id(1)
    pen = [jnp.where(i >= 2, 0.0, NEG_BIG), jnp.where(i >= 1, 0.0, NEG_BIG), None]
    k_refs = (k0_ref, k1_ref, k2_ref)
    v_refs = (v0_ref, v1_ref, v2_ref)
    nt = (((1,), (1,)), ((), ()))
    tn = (((0,), (0,)), ((), ()))
    H = ATT_HALF

    def scores(h):
        hs = slice(h * ATT_HEAD_DIM, (h + 1) * ATT_HEAD_DIM)
        q = q_ref[:, hs]
        tiles = {}
        for j in range(3):
            st = lax.dot_general(k_refs[j][:, hs], q, nt, preferred_element_type=F32)
            for kh in range(2):
                for a in range(2):
                    n = 2 * j + kh - a
                    if n < 0 or n > 4:
                        continue
                    t = st[kh * H:(kh + 1) * H, a * H:(a + 1) * H]
                    if n == 0:
                        t = t + m0_ref[...]
                    elif n == 3:
                        t = t + ta_ref[h]
                    elif n == 4:
                        t = t + tb_ref[h]
                    tiles[(j, kh, a)] = t
        maxima = []
        for a in range(2):
            mx = None
            for key in tiles:
                if key[2] == a:
                    cur = jnp.max(tiles[key], axis=0, keepdims=True)
                    if pen[key[0]] is not None:
                        cur = cur + pen[key[0]]
                    mx = cur if mx is None else jnp.maximum(mx, cur)
            maxima.append(mx)
        return tiles, maxima

    def weights(tiles, maxima):
        probs = {}
        inv_l = []
        for a in range(2):
            mine = [key for key in tiles if key[2] == a]
            shift = [maxima[a] if pj is None else maxima[a] - pj for pj in pen]
            tot = None
            for key in mine:
                p = jnp.exp2(tiles[key] - shift[key[0]])
                probs[key] = p.astype(BF16)
                cur = jnp.sum(p, axis=0, keepdims=True)
                tot = cur if tot is None else tot + cur
            inv_l.append(1.0 / tot)
        return probs, inv_l

    def values(h, probs, inv_l):
        hs = slice(h * ATT_HEAD_DIM, (h + 1) * ATT_HEAD_DIM)
        zero = jnp.zeros((H, H), BF16)
        ot = None
        for j in range(3):
            pt = jnp.concatenate(
                [jnp.concatenate([probs.get((j, kh, a), zero) for a in range(2)], axis=1)
                 for kh in range(2)], axis=0)
            cur = lax.dot_general(v_refs[j][:, hs], pt, tn, preferred_element_type=F32)
            ot = cur if ot is None else ot + cur
        ot = ot * jnp.concatenate(inv_l, axis=1)
        o_ref[:, hs] = ot.T.astype(o_ref.dtype)

    scored, weighted = {}, {}
    for t in range(ATT_HEADS + 2):
        if t < ATT_HEADS:
            scored[t] = scores(t)
        if 0 <= t - 1 < ATT_HEADS:
            weighted[t - 1] = weights(*scored.pop(t - 1))
        if 0 <= t - 2 < ATT_HEADS:
            values(t - 2, *weighted.pop(t - 2))


def _attention(qkv, m0, ta, tb):
    nb = SEQ // ATT_TQ
    row = lambda b, i: b * nb + i
    back = lambda d: (lambda b, i: (b * nb + jnp.maximum(i - d, 0)))
    spec = lambda rowfn, col: pl.BlockSpec((ATT_TQ, D_MODEL), lambda b, i: (rowfn(b, i), col))
    table = pl.BlockSpec((ATT_HEADS, ATT_HALF, ATT_HALF), lambda b, i: (0, 0, 0))
    return pl.pallas_call(
        _attn_kernel,
        grid=(BATCH, nb),
        in_specs=[
            spec(row, 0),
            spec(back(2), 1), spec(back(1), 1), spec(row, 1),
            spec(back(2), 2), spec(back(1), 2), spec(row, 2),
            pl.BlockSpec((ATT_HALF, ATT_HALF), lambda b, i: (0, 0)),
            table, table,
        ],
        out_specs=pl.BlockSpec((ATT_TQ, D_MODEL), lambda b, i: (row(b, i), 0)),
        out_shape=jax.ShapeDtypeStruct((TOKENS, D_MODEL), BF16),
        compiler_params=_cparams(("parallel", "parallel")),
        name="attention",
    )(qkv, qkv, qkv, qkv, qkv, qkv, qkv, m0, ta, tb)


def _attention_bias_tables(rel_bias):
    H = ATT_HALF
    assert H == REL_CLIP
    rb = rel_bias.astype(F32)
    rel = (rb - rb[:, 2 * REL_CLIP:]) * LOG2E

    def toeplitz(g):
        flat = jnp.tile(g, (1, H))[:, :H * (2 * H - 1)]
        return flat.reshape(-1, H, 2 * H - 1)[:, :, :H]

    ta = toeplitz(jnp.concatenate([jnp.zeros((ATT_HEADS, H), F32),
                                   rel[:, REL_CLIP:2 * REL_CLIP]], axis=1))
    tb = toeplitz(jnp.concatenate([rel[:, REL_CLIP:2 * REL_CLIP], rel[:, :REL_CLIP]], axis=1))
    kc = jnp.arange(H)[:, None] // ATT_CHUNK
    qc = jnp.arange(H)[None, :] // ATT_CHUNK
    tb = jnp.where((kc <= qc)[None], tb, NEG_BIG)
    m0 = jnp.where(kc >= qc, 0.0, NEG_BIG).astype(F32)
    return m0, ta, tb


def kernel(x, p, ffn_norm, ffn_w_gate, ffn_w_up, ffn_w_down, mix_norm, ab_w_in, gla_gate_w, gla_gate_b, gla_norm_g, conv_dw, conv_dw_b, conv_ln_g, conv_ln_b, ab_w_out, att_w_qkv, att_rel_bias, att_w_o, pl_norm, pl_w_gate, pl_w_proj, final_norm):
    xs = x.reshape(TOKENS, D_MODEL)
    ps = p.reshape(DEPTH, TOKENS, D_PL)
    row = lambda a: a.reshape(1, -1).astype(F32)

    def ffn(xs, i, s):
        return _ffn(xs, row(ffn_norm[i, s]), ffn_w_gate, ffn_w_up, ffn_w_down, i, s)

    for i in range(DEPTH):
        e = i // 2
        xs = ffn(xs, i, 0)
        if i % 2 == 0:
            w_in = ab_w_in[e]
            gz_lo = 2 * GLA_DK + 2 * GLA_DV
            gz_hi = gz_lo + GLA_GATE_RANK
            w_bf = w_in.astype(BF16)
            w_conv = w_bf[:, gz_hi:]
            w_gz = jnp.pad(w_bf[:, gz_lo:gz_hi], ((0, 0), (0, LANES - GLA_GATE_RANK)))
            z_all, gz = _rms_proj(
                xs, row(mix_norm[i]),
                [(w_bf, gz_lo // MM_TN), (w_conv, 2 * CONV_CH // MM_TN)],
                jnp.ones((1, AB_MAIN), F32), BF16, w_extra=w_gz)
            gate_w = jnp.pad(gla_gate_w[e], ((0, LANES - GLA_GATE_RANK), (0, 0))).astype(BF16)
            a_out = _gla(z_all, gz, gate_w, row(gla_gate_b[e]), row(gla_norm_g[e]))
            taps = jnp.broadcast_to(conv_dw[e].astype(F32)[:, None, :],
                                    (CONV_WIDTH, SUBLANES, CONV_CH))
            b_out = _conv(z_all, taps, row(conv_dw_b[e]), row(conv_ln_g[e]), row(conv_ln_b[e]))
            xs = _out_proj([a_out, b_out], ab_w_out[e], xs)
        else:
            colscale = jnp.concatenate([jnp.full((1, D_MODEL), ATT_HEAD_DIM ** -0.5 * LOG2E, F32),
                                        jnp.ones((1, 2 * D_MODEL), F32)], axis=1)
            qkv = _rms_proj(xs, row(mix_norm[i]), [(att_w_qkv[e], 3 * D_MODEL // MM_TN)],
                            colscale, BF16)
            m0, ta, tb = _attention_bias_tables(att_rel_bias[e])
            o = _attention(qkv, m0, ta, tb)
            xs = _out_proj([o], att_w_o[e], xs)
        xs = ffn(xs, i, 1)
        xs = _pl_embed(xs, ps, row(pl_norm[i]), pl_w_gate, pl_w_proj, row(final_norm),
                       i, i == DEPTH - 1)
    return xs.reshape(BATCH, SEQ, D_MODEL)
```

```python
import functools

import jax
import jax.numpy as jnp
from jax import lax
from jax.experimental import pallas as pl
from jax.experimental.pallas import tpu as pltpu

F32 = jnp.float32
BF16 = jnp.bfloat16

D_MODEL = 2048
BATCH = 4
SEQ = 2048
DEPTH = 2
TOKENS = BATCH * SEQ
D_PL = 256
D_FF = 5632
EPS = 1e-6

GLA_HEADS = 4
GLA_DK = 512
GLA_DV = 1024
GLA_HEAD_K = 128
GLA_HEAD_V = 256
GLA_GATE_RANK = 16
GLA_GATE_TAU = 16.0
GLA_CHUNK = 64
GLA_SUB = 16
GLA_SCALED_KEY_MAX = 2.0 ** 40
GLA_CHUNKS_PER_STEP = 4
CONV_CH = 1024
CONV_WIDTH = 31
AB_MAIN = 2 * GLA_DK + 2 * GLA_DV + 2 * CONV_CH
ATT_HEADS = 16
ATT_HEAD_DIM = 128
ATT_CHUNK = 64
LEFT_CHUNKS = 8
REL_CLIP = 128
NEG_BIG = -1e30
LOG2E = 1.4426950408889634

V7X_VMEM_BYTES = 64 * 1024 * 1024
LANES = 128
SUBLANES = 8
VMEM_LIMIT = V7X_VMEM_BYTES - 4 * 1024 * 1024

FFN_TM = 1024
FFN_TF = 512
FFN_FIRST_ROWS = FFN_TM
FFN_TN = 512
MM_TM = 1024
MM_TN = 1024
PL_TM = 256
OP_TM = 512
SEQ_TB = 512
CONV_HALO = 32
CONV_RC = 128
ATT_TQ = 256
RMS_ROWS = 256


def _cparams(sem):
    return pltpu.CompilerParams(dimension_semantics=sem, vmem_limit_bytes=VMEM_LIMIT)


def _rms_rows_to(dst_ref, x_ref, g_ref, rows):
    def body(c, carry):
        sl = pl.ds(pl.multiple_of(c * RMS_ROWS, RMS_ROWS), RMS_ROWS)
        x = x_ref[sl, :]
        ms = jnp.mean(x * x, axis=-1, keepdims=True)
        dst_ref[sl, :] = (x * lax.rsqrt(ms + EPS) * g_ref[...]).astype(dst_ref.dtype)
        return carry
    lax.fori_loop(0, rows // RMS_ROWS, body, 0)


def _ffn_step(f, load_residual, g_ref, wg_ref, wu_ref, wd_ref, o_ref, h_ref):
    @pl.when(f == 0)
    def _():
        load_residual()
        _rms_rows_to(h_ref, o_ref, g_ref, o_ref.shape[0])

    h = h_ref[...]
    gate = jnp.dot(h, wg_ref[...], preferred_element_type=F32)
    up = jnp.dot(h, wu_ref[...], preferred_element_type=F32)
    a = (0.5 * gate * jax.nn.sigmoid(gate) * up).astype(BF16)

    for n in range(D_MODEL // FFN_TN):
        cs = slice(n * FFN_TN, (n + 1) * FFN_TN)
        o_ref[:, cs] += jnp.dot(a, wd_ref[:, cs], preferred_element_type=F32)


def _ffn_first_kernel(x_hbm, g_ref, wg_ref, wu_ref, wd_ref, o_ref, wgb_ref, wub_ref, wdb_ref,
                      h_ref, sem):
    def load_residual():
        copy = pltpu.make_async_copy(x_hbm.at[pl.ds(0, FFN_FIRST_ROWS)], o_ref, sem)
        copy.start()
        copy.wait()

    wgb_ref[...] = wg_ref[...].astype(BF16)
    wub_ref[...] = wu_ref[...].astype(BF16)
    wdb_ref[...] = wd_ref[...].astype(BF16)
    _ffn_step(pl.program_id(1), load_residual, g_ref, wgb_ref, wub_ref, wdb_ref, o_ref, h_ref)


def _ffn_rest_kernel(n_f, n_copy, x_ref, g_ref, wg_ref, wu_ref, wd_ref, first_ref, o_ref,
                     h_ref, sem):
    s = pl.program_id(0)

    @pl.when(s < n_copy)
    def _():
        rows = pl.ds(pl.multiple_of(s * FFN_TM, FFN_TM), FFN_TM)
        copy = pltpu.make_async_copy(first_ref.at[rows], o_ref, sem)
        copy.start()
        copy.wait()

    def load_residual():
        o_ref[...] = x_ref[...]

    @pl.when(s >= n_copy)
    def _():
        _ffn_step((s - n_copy) % n_f, load_residual, g_ref, wg_ref, wu_ref, wd_ref, o_ref,
                  h_ref)


def _ffn(x, g, wg, wu, wd, layer, half):
    m = x.shape[0]
    tf = FFN_TF
    first, wgb, wub, wdb = pl.pallas_call(
        _ffn_first_kernel,
        grid=(1, D_FF // tf),
        in_specs=[
            pl.BlockSpec(memory_space=pl.ANY),
            pl.BlockSpec((1, D_MODEL), lambda i, f: (0, 0)),
            pl.BlockSpec((None, None, D_MODEL, tf), lambda i, f: (layer, half, 0, f)),
            pl.BlockSpec((None, None, D_MODEL, tf), lambda i, f: (layer, half, 0, f)),
            pl.BlockSpec((None, None, tf, D_MODEL), lambda i, f: (layer, half, f, 0)),
        ],
        out_specs=[
            pl.BlockSpec((FFN_FIRST_ROWS, D_MODEL), lambda i, f: (0, 0),
                         pipeline_mode=pl.Buffered(1)),
            pl.BlockSpec((D_MODEL, tf), lambda i, f: (0, f)),
            pl.BlockSpec((D_MODEL, tf), lambda i, f: (0, f)),
            pl.BlockSpec((tf, D_MODEL), lambda i, f: (f, 0)),
        ],
        out_shape=[
            jax.ShapeDtypeStruct((FFN_FIRST_ROWS, D_MODEL), F32),
            jax.ShapeDtypeStruct((D_MODEL, D_FF), BF16),
            jax.ShapeDtypeStruct((D_MODEL, D_FF), BF16),
            jax.ShapeDtypeStruct((D_FF, D_MODEL), BF16),
        ],
        scratch_shapes=[pltpu.VMEM((FFN_FIRST_ROWS, D_MODEL), BF16), pltpu.SemaphoreType.DMA(())],
        compiler_params=_cparams(("parallel", "arbitrary")),
        name="ffn_first",
    )(x, g, wg, wu, wd)
    n_f = D_FF // tf
    n_copy = FFN_FIRST_ROWS // FFN_TM
    tile = lambda s: jnp.where(s < n_copy, s, n_copy + (s - n_copy) // n_f)
    f_of = lambda s: jnp.where(s < n_copy, 0, (s - n_copy) % n_f)
    return pl.pallas_call(
        functools.partial(_ffn_rest_kernel, n_f, n_copy),
        grid=(n_copy + (m // FFN_TM - n_copy) * n_f,),
        in_specs=[
            pl.BlockSpec((FFN_TM, D_MODEL), lambda s: (jnp.maximum(tile(s), n_copy), 0)),
            pl.BlockSpec((1, D_MODEL), lambda s: (0, 0)),
            pl.BlockSpec((D_MODEL, tf), lambda s: (0, f_of(s))),
            pl.BlockSpec((D_MODEL, tf), lambda s: (0, f_of(s))),
            pl.BlockSpec((tf, D_MODEL), lambda s: (f_of(s), 0)),
            pl.BlockSpec(memory_space=pl.ANY),
        ],
        out_specs=pl.BlockSpec((FFN_TM, D_MODEL), lambda s: (tile(s), 0)),
        out_shape=jax.ShapeDtypeStruct((m, D_MODEL), F32),
        scratch_shapes=[pltpu.VMEM((FFN_TM, D_MODEL), BF16), pltpu.SemaphoreType.DMA(())],
        compiler_params=_cparams(("arbitrary",)),
        name="ffn_rest",
    )(x, g, wgb, wub, wdb, first)


def _rms_proj_kernel(seg_tiles, has_extra, *refs):
    n_seg = len(seg_tiles)
    x_ref, g_ref = refs[:2]
    w_refs = refs[2:2 + n_seg]
    cs_ref = refs[2 + n_seg]
    if has_extra:
        we_ref, o_ref, oe_ref, h_ref = refs[3 + n_seg:]
    else:
        o_ref, h_ref = refs[3 + n_seg:]
    j = pl.program_id(1)

    @pl.when(j == 0)
    def _():
        _rms_rows_to(h_ref, x_ref, g_ref, MM_TM)
        if has_extra:
            oe_ref[...] = jnp.dot(h_ref[...], we_ref[...].astype(BF16),
                                  preferred_element_type=F32)

    start = 0
    for w_ref, tiles in zip(w_refs, seg_tiles):
        @pl.when((j >= start) & (j < start + tiles))
        def _(w_ref=w_ref):
            acc = jnp.dot(h_ref[...], w_ref[...].astype(BF16), preferred_element_type=F32)
            o_ref[...] = (acc * cs_ref[...]).astype(o_ref.dtype)
        start += tiles


def _rms_proj(x, g, w_list, colscale, out_dtype, w_extra=None):
    m = x.shape[0]
    seg_tiles = tuple(t for _, t in w_list)
    n_tiles = sum(seg_tiles)
    has_extra = w_extra is not None
    in_specs = [
        pl.BlockSpec((MM_TM, D_MODEL), lambda i, j: (i, 0)),
        pl.BlockSpec((1, D_MODEL), lambda i, j: (0, 0)),
    ]
    start = 0
    for _, tiles in w_list:
        in_specs.append(pl.BlockSpec(
            (D_MODEL, MM_TN),
            lambda i, j, s=start, t=tiles: (0, jnp.clip(j - s, 0, t - 1))))
        start += tiles
    in_specs.append(pl.BlockSpec((1, MM_TN), lambda i, j: (0, j)))
    out_specs = pl.BlockSpec((MM_TM, MM_TN), lambda i, j: (i, j))
    out_shape = jax.ShapeDtypeStruct((m, n_tiles * MM_TN), out_dtype)
    args = [x, g] + [w for w, _ in w_list] + [colscale]
    if has_extra:
        in_specs.append(pl.BlockSpec((D_MODEL, LANES), lambda i, j: (0, 0)))
        out_specs = [out_specs, pl.BlockSpec((MM_TM, LANES), lambda i, j: (i, 0))]
        out_shape = [out_shape, jax.ShapeDtypeStruct((m, LANES), F32)]
        args.append(w_extra)
    return pl.pallas_call(
        functools.partial(_rms_proj_kernel, seg_tiles, has_extra),
        grid=(m // MM_TM, n_tiles),
        in_specs=in_specs,
        out_specs=out_specs,
        out_shape=out_shape,
        scratch_shapes=[pltpu.VMEM((MM_TM, D_MODEL), BF16)],
        compiler_params=_cparams(("parallel", "arbitrary")),
        name="rms_proj",
    )(*args)


def _out_proj_kernel(n_lhs, *refs):
    lhs_refs = refs[:n_lhs]
    w_refs = refs[n_lhs:2 * n_lhs]
    x_ref, o_ref = refs[2 * n_lhs], refs[2 * n_lhs + 1]
    wb_refs = refs[2 * n_lhs + 2:]

    @pl.when(pl.program_id(0) == 0)
    def _():
        for w_ref, wb_ref in zip(w_refs, wb_refs):
            wb_ref[...] = w_ref[...].astype(BF16)

    acc = x_ref[...]
    for a_ref, wb_ref in zip(lhs_refs, wb_refs):
        acc = acc + jnp.dot(a_ref[...], wb_ref[...], preferred_element_type=F32)
    o_ref[...] = acc


def _out_proj(lhs_list, w, x):
    m = x.shape[0]
    n_lhs = len(lhs_list)
    kw = lhs_list[0].shape[1]
    in_specs = [pl.BlockSpec((OP_TM, kw), lambda i: (i, 0)) for _ in lhs_list]
    in_specs += [pl.BlockSpec((kw, D_MODEL), lambda i, t=t: (t, 0), pipeline_mode=pl.Buffered(1))
                 for t in range(n_lhs)]
    in_specs += [pl.BlockSpec((OP_TM, D_MODEL), lambda i: (i, 0))]
    return pl.pallas_call(
        functools.partial(_out_proj_kernel, n_lhs),
        grid=(m // OP_TM,),
        in_specs=in_specs,
        out_specs=pl.BlockSpec((OP_TM, D_MODEL), lambda i: (i, 0)),
        out_shape=jax.ShapeDtypeStruct((m, D_MODEL), F32),
        scratch_shapes=[pltpu.VMEM((kw, D_MODEL), BF16) for _ in lhs_list],
        compiler_params=_cparams(("arbitrary",)),
        name="out_proj",
    )(*lhs_list, *([w] * n_lhs), x)


def _pl_embed_kernel(final, x_ref, p_ref, g_ref, wg_ref, wp_ref, fg_ref, o_ref,
                     h_ref, wgb_ref, wpb_ref):
    @pl.when(pl.program_id(0) == 0)
    def _():
        wgb_ref[...] = wg_ref[...].astype(BF16)
        wpb_ref[...] = wp_ref[...].astype(BF16)

    _rms_rows_to(h_ref, x_ref, g_ref, PL_TM)
    gate = jax.nn.sigmoid(jnp.dot(h_ref[...], wgb_ref[...], preferred_element_type=F32))
    proj = jnp.dot(p_ref[...].astype(BF16), wpb_ref[...], preferred_element_type=F32)
    y = x_ref[...] + gate * proj
    if final:
        ms = jnp.mean(y * y, axis=-1, keepdims=True)
        y = y * lax.rsqrt(ms + EPS) * fg_ref[...]
    o_ref[...] = y


def _pl_embed(x, p, g, wg, wp, final_g, layer, final):
    m = x.shape[0]
    return pl.pallas_call(
        functools.partial(_pl_embed_kernel, final),
        grid=(m // PL_TM,),
        in_specs=[
            pl.BlockSpec((PL_TM, D_MODEL), lambda i: (i, 0)),
            pl.BlockSpec((None, PL_TM, D_PL), lambda i: (layer, i, 0)),
            pl.BlockSpec((1, D_MODEL), lambda i: (0, 0)),
            pl.BlockSpec((None, D_MODEL, D_MODEL), lambda i: (layer, 0, 0),
                         pipeline_mode=pl.Buffered(1)),
            pl.BlockSpec((None, D_PL, D_MODEL), lambda i: (layer, 0, 0),
                         pipeline_mode=pl.Buffered(1)),
            pl.BlockSpec((1, D_MODEL), lambda i: (0, 0)),
        ],
        out_specs=pl.BlockSpec((PL_TM, D_MODEL), lambda i: (i, 0)),
        out_shape=jax.ShapeDtypeStruct((m, D_MODEL), F32),
        scratch_shapes=[pltpu.VMEM((PL_TM, D_MODEL), BF16),
                        pltpu.VMEM((D_MODEL, D_MODEL), BF16),
                        pltpu.VMEM((D_PL, D_MODEL), BF16)],
        compiler_params=_cparams(("arbitrary",)),
        name="pl_embed",
    )(x, p, g, wg, wp, final_g)


def _gla_kernel(q_ref, k_ref, v_ref, r_ref, gz_ref, gw_ref, gb_ref, ng_ref, o_ref,
                b_ref, kh_ref, s_ref):
    C, SB = GLA_CHUNK, GLA_SUB
    n_chunks = SEQ_TB // C

    @pl.when(pl.program_id(1) == 0)
    def _():
        s_ref[...] = jnp.zeros_like(s_ref)

    lin = jnp.dot(gz_ref[...].astype(BF16), gw_ref[...], preferred_element_type=F32) + gb_ref[...]
    log_a = -(jnp.maximum(-lin, 0.0) + jnp.log1p(jnp.exp(-jnp.abs(lin)))) * (LOG2E / GLA_GATE_TAU)
    tri = (lax.broadcasted_iota(jnp.int32, (C, C), 0)
           >= lax.broadcasted_iota(jnp.int32, (C, C), 1)).astype(F32)
    for c in range(n_chunks):
        b_ref[c * C:(c + 1) * C, :] = jnp.dot(
            tri, log_a[c * C:(c + 1) * C, :], preferred_element_type=F32,
            precision=lax.Precision.HIGHEST)
    kh_ref[...] = k_ref[...].astype(F32) * jnp.exp2(-b_ref[...])
    scaled_ok = jnp.max(jnp.abs(kh_ref[...])) <= GLA_SCALED_KEY_MAX

    lane = lax.broadcasted_iota(jnp.int32, (SB, C), 1)
    trow = lax.broadcasted_iota(jnp.int32, (SB, C), 0)
    causal = (lax.broadcasted_iota(jnp.int32, (C, C), 0)
              >= lax.broadcasted_iota(jnp.int32, (C, C), 1))
    nt = (((1,), (1,)), ((), ()))
    tn = (((0,), (0,)), ((), ()))

    def guarded_products(q, k, b):
        blocks = []
        for i in range(C // SB):
            s0 = i * SB
            q_i = q[s0:s0 + SB]
            b_i = b[s0:s0 + SB]
            acc = jnp.zeros((SB, C), F32)
            if i > 0:
                p_i = b[s0 - 1:s0, :]
                qt = (q_i * jnp.exp2(b_i - p_i)).astype(BF16)
                kt = (k * jnp.exp2(p_i - b)).astype(BF16)
                off = lax.dot_general(qt, kt, nt, preferred_element_type=F32)
                acc = jnp.where(lane < s0, off, 0.0)
            for s in range(SB):
                b_s = b[s0 + s:s0 + s + 1, :]
                k_s = k[s0 + s:s0 + s + 1, :]
                e = jnp.exp2(b_i - b_s)
                col = jnp.sum(q_i * (k_s * e), axis=1, keepdims=True)
                acc = jnp.where((lane == s0 + s) & (trow >= s), col, acc)
            blocks.append(acc)
        return jnp.concatenate(blocks, axis=0)

    def chunk_body(scaled, per_step, it, carry):
        units = []
        for u in range(per_step):
            rows = pl.ds(pl.multiple_of((it * per_step + u) * C, C), C)
            for h in range(GLA_HEADS):
                units.append((u, h, rows))
        hk = lambda h: slice(h * GLA_HEAD_K, (h + 1) * GLA_HEAD_K)
        hv = lambda h: slice(h * GLA_HEAD_V, (h + 1) * GLA_HEAD_V)

        part = {}
        for u, h, rows in units:
            q = q_ref[rows, hk(h)].astype(F32) * (GLA_HEAD_K ** -0.5)
            v = v_ref[rows, hv(h)].astype(BF16)
            b = b_ref[rows, hk(h)]
            b_last = b[C - 1:C, :]
            qb = (q * jnp.exp2(b)).astype(BF16)
            if scaled:
                kh = kh_ref[rows, hk(h)]
                a_mat = jnp.where(causal, lax.dot_general(qb, kh.astype(BF16), nt,
                                                          preferred_element_type=F32), 0.0)
                k_dec = (kh * jnp.exp2(b_last)).astype(BF16)
            else:
                k = k_ref[rows, hk(h)].astype(F32)
                a_mat = guarded_products(q, k, b)
                k_dec = (k * jnp.exp2(b_last - b)).astype(BF16)
            upd = lax.dot_general(v, k_dec, tn, preferred_element_type=F32)
            local = jnp.dot(a_mat.astype(BF16), v, preferred_element_type=F32)
            part[(u, h)] = (qb, jnp.exp2(b_last), upd, local)

        state = {}
        for h in range(GLA_HEADS):
            st = s_ref[h]
            for u in range(per_step):
                state[(u, h)] = st
                _, decay, upd, _ = part[(u, h)]
                st = st * decay + upd
            s_ref[h] = st

        for u, h, rows in units:
            qb, _, _, local = part[(u, h)]
            o = local + lax.dot_general(qb, state[(u, h)].astype(BF16), nt,
                                        preferred_element_type=F32)
            on = o * lax.rsqrt(jnp.mean(o * o, axis=-1, keepdims=True) + EPS) * ng_ref[...]
            r = r_ref[rows, hv(h)].astype(F32)
            o_ref[rows, hv(h)] = (on * (r * jax.nn.sigmoid(r))).astype(o_ref.dtype)
        return carry

    def run(scaled):
        per_step = GLA_CHUNKS_PER_STEP if scaled else 1
        lax.fori_loop(0, n_chunks // per_step,
                      functools.partial(chunk_body, scaled, per_step), 0)

    lax.cond(scaled_ok, lambda: run(True), lambda: run(False))


def _gla(z, gz, gate_w, gate_b, norm_g):
    nb = SEQ // SEQ_TB
    row = lambda b, i: b * nb + i
    return pl.pallas_call(
        _gla_kernel,
        grid=(BATCH, nb),
        in_specs=[
            pl.BlockSpec((SEQ_TB, GLA_DK), lambda b, i: (row(b, i), 0)),
            pl.BlockSpec((SEQ_TB, GLA_DK), lambda b, i: (row(b, i), 1)),
            pl.BlockSpec((SEQ_TB, GLA_DV), lambda b, i: (row(b, i), 1)),
            pl.BlockSpec((SEQ_TB, GLA_DV), lambda b, i: (row(b, i), 2)),
            pl.BlockSpec((SEQ_TB, LANES), lambda b, i: (row(b, i), 0)),
            pl.BlockSpec((LANES, GLA_DK), lambda b, i: (0, 0)),
            pl.BlockSpec((1, GLA_DK), lambda b, i: (0, 0)),
            pl.BlockSpec((1, GLA_HEAD_V), lambda b, i: (0, 0)),
        ],
        out_specs=pl.BlockSpec((SEQ_TB, GLA_DV), lambda b, i: (row(b, i), 0)),
        out_shape=jax.ShapeDtypeStruct((TOKENS, GLA_DV), BF16),
        scratch_shapes=[pltpu.VMEM((SEQ_TB, GLA_DK), F32),
                        pltpu.VMEM((SEQ_TB, GLA_DK), F32),
                        pltpu.VMEM((GLA_HEADS, GLA_HEAD_V, GLA_HEAD_K), F32)],
        compiler_params=_cparams(("parallel", "arbitrary")),
        name="gla",
    )(z, z, z, z, gz, gate_w, gate_b, norm_g)


def _conv_kernel(ca_ref, cb_ref, ha_ref, hb_ref, w_ref, wb_ref, lg_ref, lb_ref, o_ref,
                 sh_ref, y_ref):
    H = CONV_HALO
    n_sh = SUBLANES
    keep = (pl.program_id(1) > 0).astype(F32)

    u_main = ca_ref[...].astype(F32) * jax.nn.sigmoid(cb_ref[...].astype(F32))
    u_halo = ha_ref[...].astype(F32) * jax.nn.sigmoid(hb_ref[...].astype(F32)) * keep
    for r in range(n_sh):
        sh_ref[r, 0:H - r, :] = u_halo[r:H, :]
        sh_ref[r, H - r:H - r + SEQ_TB, :] = u_main

    first = H - (CONV_WIDTH - 1)
    ct_w = 128

    groups = CONV_RC // SUBLANES

    def row_body(cs, bias, rc, carry):
        t0 = pl.multiple_of(rc * CONV_RC, CONV_RC)
        accs = [bias, None]
        for r in range(n_sh):
            offs = [o for o in range(first, first + CONV_WIDTH) if o % n_sh == r]
            lo, hi = offs[0] - r, offs[-1] - r
            slab = sh_ref[r, pl.ds(t0 + lo, CONV_RC + hi - lo), cs]
            for off in offs:
                a = off - r - lo
                win = slab[a:a + CONV_RC].reshape(groups, SUBLANES, ct_w)
                term = w_ref[off - first, :, cs][None] * win
                accs[r % 2] = term if accs[r % 2] is None else accs[r % 2] + term
        y_ref[pl.ds(t0, CONV_RC), cs] = (accs[0] + accs[1]).reshape(CONV_RC, ct_w)
        return carry

    for ct in range(CONV_CH // ct_w):
        cs = slice(ct * ct_w, (ct + 1) * ct_w)
        bias = jnp.broadcast_to(wb_ref[:, cs][None], (groups, SUBLANES, ct_w))
        lax.fori_loop(0, SEQ_TB // CONV_RC, functools.partial(row_body, cs, bias), 0)

    def ln_body(rc, carry):
        sl = pl.ds(pl.multiple_of(rc * 64, 64), 64)
        y = y_ref[sl, :]
        mu = jnp.mean(y, axis=-1, keepdims=True)
        var = jnp.mean(jnp.square(y - mu), axis=-1, keepdims=True)
        t = (y - mu) * lax.rsqrt(var + EPS) * lg_ref[...] + lb_ref[...]
        o_ref[sl, :] = (t * jax.nn.sigmoid(t)).astype(o_ref.dtype)
        return carry

    lax.fori_loop(0, SEQ_TB // 64, ln_body, 0, unroll=2)


def _conv(z, w, wb, ln_g, ln_b):
    nb = SEQ // SEQ_TB
    hb = SEQ_TB // CONV_HALO
    ca_col = (2 * GLA_DK + 2 * GLA_DV) // CONV_CH
    row = lambda b, i: b * nb + i
    halo = lambda b, i: jnp.maximum(row(b, i) * hb - 1, 0)
    return pl.pallas_call(
        _conv_kernel,
        grid=(BATCH, nb),
        in_specs=[
            pl.BlockSpec((SEQ_TB, CONV_CH), lambda b, i: (row(b, i), ca_col)),
            pl.BlockSpec((SEQ_TB, CONV_CH), lambda b, i: (row(b, i), ca_col + 1)),
            pl.BlockSpec((CONV_HALO, CONV_CH), lambda b, i: (halo(b, i), ca_col)),
            pl.BlockSpec((CONV_HALO, CONV_CH), lambda b, i: (halo(b, i), ca_col + 1)),
            pl.BlockSpec((CONV_WIDTH, SUBLANES, CONV_CH), lambda b, i: (0, 0, 0)),
            pl.BlockSpec((1, CONV_CH), lambda b, i: (0, 0)),
            pl.BlockSpec((1, CONV_CH), lambda b, i: (0, 0)),
            pl.BlockSpec((1, CONV_CH), lambda b, i: (0, 0)),
        ],
        out_specs=pl.BlockSpec((SEQ_TB, CONV_CH), lambda b, i: (row(b, i), 0)),
        out_shape=jax.ShapeDtypeStruct((TOKENS, CONV_CH), BF16),
        scratch_shapes=[pltpu.VMEM((SUBLANES, SEQ_TB + CONV_HALO, CONV_CH), F32),
                        pltpu.VMEM((SEQ_TB, CONV_CH), F32)],
        compiler_params=_cparams(("parallel", "parallel")),
        name="conv",
    )(z, z, z, z, w, wb, ln_g, ln_b)


ATT_HALF = ATT_TQ // 2


def _attn_kernel(q_ref, k0_ref, k1_ref, k2_ref, v0_ref, v1_ref, v2_ref,
                 m0_ref, ta_ref, tb_ref, o_ref):
    i = pl.program_id(1)
    pen = [jnp.where(i >= 2, 0.0, NEG_BIG), jnp.where(i >= 1, 0.0, NEG_BIG), None]
    k_refs = (k0_ref, k1_ref, k2_ref)
    v_refs = (v0_ref, v1_ref, v2_ref)
    nt = (((1,), (1,)), ((), ()))
    tn = (((0,), (0,)), ((), ()))
    H = ATT_HALF

    def scores(h):
        hs = slice(h * ATT_HEAD_DIM, (h + 1) * ATT_HEAD_DIM)
        q = q_ref[:, hs]
        tiles = {}
        for j in range(3):
            st = lax.dot_general(k_refs[j][:, hs], q, nt, preferred_element_type=F32)
            for kh in range(2):
                for a in range(2):
                    n = 2 * j + kh - a
                    if n < 0 or n > 4:
                        continue
                    t = st[kh * H:(kh + 1) * H, a * H:(a + 1) * H]
                    if n == 0:
                        t = t + m0_ref[...]
                    elif n == 3:
                        t = t + ta_ref[h]
                    elif n == 4:
                        t = t + tb_ref[h]
                    tiles[(j, kh, a)] = t
        maxima = []
        for a in range(2):
            mx = None
            for key in tiles:
                if key[2] == a:
                    cur = jnp.max(tiles[key], axis=0, keepdims=True)
                    if pen[key[0]] is not None:
                        cur = cur + pen[key[0]]
                    mx = cur if mx is None else jnp.maximum(mx, cur)
            maxima.append(mx)
        return tiles, maxima

    def weights(tiles, maxima):
        probs = {}
        inv_l = []
        for a in range(2):
            mine = [key for key in tiles if key[2] == a]
            shift = [maxima[a] if pj is None else maxima[a] - pj for pj in pen]
            tot = None
            for key in mine:
                p = jnp.exp2(tiles[key] - shift[key[0]])
                probs[key] = p.astype(BF16)
                cur = jnp.sum(p, axis=0, keepdims=True)
                tot = cur if tot is None else tot + cur
            inv_l.append(1.0 / tot)
        return probs, inv_l

    def values(h, probs, inv_l):
        hs = slice(h * ATT_HEAD_DIM, (h + 1) * ATT_HEAD_DIM)
        zero = jnp.zeros((H, H), BF16)
        ot = None
        for j in range(3):
            pt = jnp.concatenate(
                [jnp.concatenate([probs.get((j, kh, a), zero) for a in range(2)], axis=1)
                 for kh in range(2)], axis=0)
            cur = lax.dot_general(v_refs[j][:, hs], pt, tn, preferred_element_type=F32)
            ot = cur if ot is None else ot + cur
        ot = ot * jnp.concatenate(inv_l, axis=1)
        o_ref[:, hs] = ot.T.astype(o_ref.dtype)

    scored, weighted = {}, {}
    for t in range(ATT_HEADS + 2):
        if t < ATT_HEADS:
            scored[t] = scores(t)
        if 0 <= t - 1 < ATT_HEADS:
            weighted[t - 1] = weights(*scored.pop(t - 1))
        if 0 <= t - 2 < ATT_HEADS:
            values(t - 2, *weighted.pop(t - 2))


def _attention(qkv, m0, ta, tb):
    nb = SEQ // ATT_TQ
    row = lambda b, i: b * nb + i
    back = lambda d: (lambda b, i: (b * nb + jnp.maximum(i - d, 0)))
    spec = lambda rowfn, col: pl.BlockSpec((ATT_TQ, D_MODEL), lambda b, i: (rowfn(b, i), col))
    table = pl.BlockSpec((ATT_HEADS, ATT_HALF, ATT_HALF), lambda b, i: (0, 0, 0))
    return pl.pallas_call(
        _attn_kernel,
        grid=(BATCH, nb),
        in_specs=[
            spec(row, 0),
            spec(back(2), 1), spec(back(1), 1), spec(row, 1),
            spec(back(2), 2), spec(back(1), 2), spec(row, 2),
            pl.BlockSpec((ATT_HALF, ATT_HALF), lambda b, i: (0, 0)),
            table, table,
        ],
        out_specs=pl.BlockSpec((ATT_TQ, D_MODEL), lambda b, i: (row(b, i), 0)),
        out_shape=jax.ShapeDtypeStruct((TOKENS, D_MODEL), BF16),
        compiler_params=_cparams(("parallel", "parallel")),
        name="attention",
    )(qkv, qkv, qkv, qkv, qkv, qkv, qkv, m0, ta, tb)


def _attention_bias_tables(rel_bias):
    H = ATT_HALF
    assert H == REL_CLIP
    rb = rel_bias.astype(F32)
    rel = (rb - rb[:, 2 * REL_CLIP:]) * LOG2E

    def toeplitz(g):
        flat = jnp.tile(g, (1, H))[:, :H * (2 * H - 1)]
        return flat.reshape(-1, H, 2 * H - 1)[:, :, :H]

    ta = toeplitz(jnp.concatenate([jnp.zeros((ATT_HEADS, H), F32),
                                   rel[:, REL_CLIP:2 * REL_CLIP]], axis=1))
    tb = toeplitz(jnp.concatenate([rel[:, REL_CLIP:2 * REL_CLIP], rel[:, :REL_CLIP]], axis=1))
    kc = jnp.arange(H)[:, None] // ATT_CHUNK
    qc = jnp.arange(H)[None, :] // ATT_CHUNK
    tb = jnp.where((kc <= qc)[None], tb, NEG_BIG)
    m0 = jnp.where(kc >= qc, 0.0, NEG_BIG).astype(F32)
    return m0, ta, tb


def kernel(x, p, ffn_norm, ffn_w_gate, ffn_w_up, ffn_w_down, mix_norm, ab_w_in, gla_gate_w, gla_gate_b, gla_norm_g, conv_dw, conv_dw_b, conv_ln_g, conv_ln_b, ab_w_out, att_w_qkv, att_rel_bias, att_w_o, pl_norm, pl_w_gate, pl_w_proj, final_norm):
    xs = x.reshape(TOKENS, D_MODEL)
    ps = p.reshape(DEPTH, TOKENS, D_PL)
    row = lambda a: a.reshape(1, -1).astype(F32)

    def ffn(xs, i, s):
        return _ffn(xs, row(ffn_norm[i, s]), ffn_w_gate, ffn_w_up, ffn_w_down, i, s)

    for i in range(DEPTH):
        e = i // 2
        xs = ffn(xs, i, 0)
        if i % 2 == 0:
            w_in = ab_w_in[e]
            gz_lo = 2 * GLA_DK + 2 * GLA_DV
            gz_hi = gz_lo + GLA_GATE_RANK
            w_bf = w_in.astype(BF16)
            w_conv = w_bf[:, gz_hi:]
            w_gz = jnp.pad(w_bf[:, gz_lo:gz_hi], ((0, 0), (0, LANES - GLA_GATE_RANK)))
            z_all, gz = _rms_proj(
                xs, row(mix_norm[i]),
                [(w_bf, gz_lo // MM_TN), (w_conv, 2 * CONV_CH // MM_TN)],
                jnp.ones((1, AB_MAIN), F32), BF16, w_extra=w_gz)
            gate_w = jnp.pad(gla_gate_w[e], ((0, LANES - GLA_GATE_RANK), (0, 0))).astype(BF16)
            a_out = _gla(z_all, gz, gate_w, row(gla_gate_b[e]), row(gla_norm_g[e]))
            taps = jnp.broadcast_to(conv_dw[e].astype(F32)[:, None, :],
                                    (CONV_WIDTH, SUBLANES, CONV_CH))
            b_out = _conv(z_all, taps, row(conv_dw_b[e]), row(conv_ln_g[e]), row(conv_ln_b[e]))
            xs = _out_proj([a_out, b_out], ab_w_out[e], xs)
        else:
            colscale = jnp.concatenate([jnp.full((1, D_MODEL), ATT_HEAD_DIM ** -0.5 * LOG2E, F32),
                                        jnp.ones((1, 2 * D_MODEL), F32)], axis=1)
            qkv = _rms_proj(xs, row(mix_norm[i]), [(att_w_qkv[e], 3 * D_MODEL // MM_TN)],
                            colscale, BF16)
            m0, ta, tb = _attention_bias_tables(att_rel_bias[e])
            o = _attention(qkv, m0, ta, tb)
            xs = _out_proj([o], att_w_o[e], xs)
        xs = ffn(xs, i, 1)
        xs = _pl_embed(xs, ps, row(pl_norm[i]), pl_w_gate, pl_w_proj, row(final_norm),
                       i, i == DEPTH - 1)
    return xs.reshape(BATCH, SEQ, D_MODEL)
```

```python
import functools

import jax
import jax.numpy as jnp
from jax import lax
from jax.experimental import pallas as pl
from jax.experimental.pallas import tpu as pltpu

F32 = jnp.float32
BF16 = jnp.bfloat16

D_MODEL = 2048
BATCH = 4
SEQ = 2048
DEPTH = 2
TOKENS = BATCH * SEQ
D_PL = 256
D_FF = 5632
EPS = 1e-6

GLA_HEADS = 4
GLA_DK = 512
GLA_DV = 1024
GLA_HEAD_K = 128
GLA_HEAD_V = 256
GLA_GATE_RANK = 16
GLA_GATE_TAU = 16.0
GLA_CHUNK = 64
GLA_SUB = 16
GLA_SCALED_KEY_MAX = 2.0 ** 40
GLA_CHUNKS_PER_STEP = 4
CONV_CH = 1024
CONV_WIDTH = 31
AB_MAIN = 2 * GLA_DK + 2 * GLA_DV + 2 * CONV_CH
ATT_HEADS = 16
ATT_HEAD_DIM = 128
ATT_CHUNK = 64
LEFT_CHUNKS = 8
REL_CLIP = 128
NEG_BIG = -1e30
LOG2E = 1.4426950408889634

V7X_VMEM_BYTES = 64 * 1024 * 1024
LANES = 128
SUBLANES = 8
VMEM_LIMIT = V7X_VMEM_BYTES - 4 * 1024 * 1024

FFN_TM = 1024
FFN_TF = 512
FFN_FIRST_ROWS = FFN_TM
FFN_TN = 512
MM_TM = 1024
MM_TN = 1024
PL_TM = 256
OP_TM = 512
SEQ_TB = 512
CONV_HALO = 32
CONV_RC = 128
ATT_TQ = 256
RMS_ROWS = 256


def _cparams(sem):
    return pltpu.CompilerParams(dimension_semantics=sem, vmem_limit_bytes=VMEM_LIMIT)


def _rms_rows_to(dst_ref, x_ref, g_ref, rows):
    def body(c, carry):
        sl = pl.ds(pl.multiple_of(c * RMS_ROWS, RMS_ROWS), RMS_ROWS)
        x = x_ref[sl, :]
        ms = jnp.mean(x * x, axis=-1, keepdims=True)
        dst_ref[sl, :] = (x * lax.rsqrt(ms + EPS) * g_ref[...]).astype(dst_ref.dtype)
        return carry
    lax.fori_loop(0, rows // RMS_ROWS, body, 0)


def _ffn_step(f, load_residual, g_ref, wg_ref, wu_ref, wd_ref, o_ref, h_ref):
    @pl.when(f == 0)
    def _():
        load_residual()
        _rms_rows_to(h_ref, o_ref, g_ref, o_ref.shape[0])

    h = h_ref[...]
    gate = jnp.dot(h, wg_ref[...], preferred_element_type=F32)
    up = jnp.dot(h, wu_ref[...], preferred_element_type=F32)
    a = (0.5 * gate * jax.nn.sigmoid(gate) * up).astype(BF16)

    for n in range(D_MODEL // FFN_TN):
        cs = slice(n * FFN_TN, (n + 1) * FFN_TN)
        o_ref[:, cs] += jnp.dot(a, wd_ref[:, cs], preferred_element_type=F32)


def _ffn_first_kernel(x_hbm, g_ref, wg_ref, wu_ref, wd_ref, o_ref, wgb_ref, wub_ref, wdb_ref,
                      h_ref, sem):
    def load_residual():
        copy = pltpu.make_async_copy(x_hbm.at[pl.ds(0, FFN_FIRST_ROWS)], o_ref, sem)
        copy.start()
        copy.wait()

    wgb_ref[...] = wg_ref[...].astype(BF16)
    wub_ref[...] = wu_ref[...].astype(BF16)
    wdb_ref[...] = wd_ref[...].astype(BF16)
    _ffn_step(pl.program_id(1), load_residual, g_ref, wgb_ref, wub_ref, wdb_ref, o_ref, h_ref)


def _ffn_rest_kernel(n_f, n_copy, x_ref, g_ref, wg_ref, wu_ref, wd_ref, first_ref, o_ref,
                     h_ref, sem):
    s = pl.program_id(0)

    @pl.when(s < n_copy)
    def _():
        rows = pl.ds(pl.multiple_of(s * FFN_TM, FFN_TM), FFN_TM)
        copy = pltpu.make_async_copy(first_ref.at[rows], o_ref, sem)
        copy.start()
        copy.wait()

    def load_residual():
        o_ref[...] = x_ref[...]

    @pl.when(s >= n_copy)
    def _():
        _ffn_step((s - n_copy) % n_f, load_residual, g_ref, wg_ref, wu_ref, wd_ref, o_ref,
                  h_ref)


def _ffn(x, g, wg, wu, wd, layer, half):
    m = x.shape[0]
    tf = FFN_TF
    first, wgb, wub, wdb = pl.pallas_call(
        _ffn_first_kernel,
        grid=(1, D_FF // tf),
        in_specs=[
            pl.BlockSpec(memory_space=pl.ANY),
            pl.BlockSpec((1, D_MODEL), lambda i, f: (0, 0)),
            pl.BlockSpec((None, None, D_MODEL, tf), lambda i, f: (layer, half, 0, f)),
            pl.BlockSpec((None, None, D_MODEL, tf), lambda i, f: (layer, half, 0, f)),
            pl.BlockSpec((None, None, tf, D_MODEL), lambda i, f: (layer, half, f, 0)),
        ],
        out_specs=[
            pl.BlockSpec((FFN_FIRST_ROWS, D_MODEL), lambda i, f: (0, 0),
                         pipeline_mode=pl.Buffered(1)),
            pl.BlockSpec((D_MODEL, tf), lambda i, f: (0, f)),
            pl.BlockSpec((D_MODEL, tf), lambda i, f: (0, f)),
            pl.BlockSpec((tf, D_MODEL), lambda i, f: (f, 0)),
        ],
        out_shape=[
            jax.ShapeDtypeStruct((FFN_FIRST_ROWS, D_MODEL), F32),
            jax.ShapeDtypeStruct((D_MODEL, D_FF), BF16),
            jax.ShapeDtypeStruct((D_MODEL, D_FF), BF16),
            jax.ShapeDtypeStruct((D_FF, D_MODEL), BF16),
        ],
        scratch_shapes=[pltpu.VMEM((FFN_FIRST_ROWS, D_MODEL), BF16), pltpu.SemaphoreType.DMA(())],
        compiler_params=_cparams(("parallel", "arbitrary")),
        name="ffn_first",
    )(x, g, wg, wu, wd)
    n_f = D_FF // tf
    n_copy = FFN_FIRST_ROWS // FFN_TM
    tile = lambda s: jnp.where(s < n_copy, s, n_copy + (s - n_copy) // n_f)
    f_of = lambda s: jnp.where(s < n_copy, 0, (s - n_copy) % n_f)
    return pl.pallas_call(
        functools.partial(_ffn_rest_kernel, n_f, n_copy),
        grid=(n_copy + (m // FFN_TM - n_copy) * n_f,),
        in_specs=[
            pl.BlockSpec((FFN_TM, D_MODEL), lambda s: (jnp.maximum(tile(s), n_copy), 0)),
            pl.BlockSpec((1, D_MODEL), lambda s: (0, 0)),
            pl.BlockSpec((D_MODEL, tf), lambda s: (0, f_of(s))),
            pl.BlockSpec((D_MODEL, tf), lambda s: (0, f_of(s))),
            pl.BlockSpec((tf, D_MODEL), lambda s: (f_of(s), 0)),
            pl.BlockSpec(memory_space=pl.ANY),
        ],
        out_specs=pl.BlockSpec((FFN_TM, D_MODEL), lambda s: (tile(s), 0)),
        out_shape=jax.ShapeDtypeStruct((m, D_MODEL), F32),
        scratch_shapes=[pltpu.VMEM((FFN_TM, D_MODEL), BF16), pltpu.SemaphoreType.DMA(())],
        compiler_params=_cparams(("arbitrary",)),
        name="ffn_rest",
    )(x, g, wgb, wub, wdb, first)


def _rms_proj_kernel(seg_tiles, has_extra, *refs):
    n_seg = len(seg_tiles)
    x_ref, g_ref = refs[:2]
    w_refs = refs[2:2 + n_seg]
    cs_ref = refs[2 + n_seg]
    if has_extra:
        we_ref, o_ref, oe_ref, h_ref = refs[3 + n_seg:]
    else:
        o_ref, h_ref = refs[3 + n_seg:]
    j = pl.program_id(1)

    @pl.when(j == 0)
    def _():
        _rms_rows_to(h_ref, x_ref, g_ref, MM_TM)
        if has_extra:
            oe_ref[...] = jnp.dot(h_ref[...], we_ref[...].astype(BF16),
                                  preferred_element_type=F32)

    start = 0
    for w_ref, tiles in zip(w_refs, seg_tiles):
        @pl.when((j >= start) & (j < start + tiles))
        def _(w_ref=w_ref):
            acc = jnp.dot(h_ref[...], w_ref[...].astype(BF16), preferred_element_type=F32)
            o_ref[...] = (acc * cs_ref[...]).astype(o_ref.dtype)
        start += tiles


def _rms_proj(x, g, w_list, colscale, out_dtype, w_extra=None):
    m = x.shape[0]
    seg_tiles = tuple(t for _, t in w_list)
    n_tiles = sum(seg_tiles)
    has_extra = w_extra is not None
    in_specs = [
        pl.BlockSpec((MM_TM, D_MODEL), lambda i, j: (i, 0)),
        pl.BlockSpec((1, D_MODEL), lambda i, j: (0, 0)),
    ]
    start = 0
    for _, tiles in w_list:
        in_specs.append(pl.BlockSpec(
            (D_MODEL, MM_TN),
            lambda i, j, s=start, t=tiles: (0, jnp.clip(j - s, 0, t - 1))))
        start += tiles
    in_specs.append(pl.BlockSpec((1, MM_TN), lambda i, j: (0, j)))
    out_specs = pl.BlockSpec((MM_TM, MM_TN), lambda i, j: (i, j))
    out_shape = jax.ShapeDtypeStruct((m, n_tiles * MM_TN), out_dtype)
    args = [x, g] + [w for w, _ in w_list] + [colscale]
    if has_extra:
        in_specs.append(pl.BlockSpec((D_MODEL, LANES), lambda i, j: (0, 0)))
        out_specs = [out_specs, pl.BlockSpec((MM_TM, LANES), lambda i, j: (i, 0))]
        out_shape = [out_shape, jax.ShapeDtypeStruct((m, LANES), F32)]
        args.append(w_extra)
    return pl.pallas_call(
        functools.partial(_rms_proj_kernel, seg_tiles, has_extra),
        grid=(m // MM_TM, n_tiles),
        in_specs=in_specs,
        out_specs=out_specs,
        out_shape=out_shape,
        scratch_shapes=[pltpu.VMEM((MM_TM, D_MODEL), BF16)],
        compiler_params=_cparams(("parallel", "arbitrary")),
        name="rms_proj",
    )(*args)


def _out_proj_kernel(n_lhs, *refs):
    lhs_refs = refs[:n_lhs]
    w_refs = refs[n_lhs:2 * n_lhs]
    x_ref, o_ref = refs[2 * n_lhs], refs[2 * n_lhs + 1]
    wb_refs = refs[2 * n_lhs + 2:]

    @pl.when(pl.program_id(0) == 0)
    def _():
        for w_ref, wb_ref in zip(w_refs, wb_refs):
            wb_ref[...] = w_ref[...].astype(BF16)

    acc = x_ref[...]
    for a_ref, wb_ref in zip(lhs_refs, wb_refs):
        acc = acc + jnp.dot(a_ref[...], wb_ref[...], preferred_element_type=F32)
    o_ref[...] = acc


def _out_proj(lhs_list, w, x):
    m = x.shape[0]
    n_lhs = len(lhs_list)
    kw = lhs_list[0].shape[1]
    in_specs = [pl.BlockSpec((OP_TM, kw), lambda i: (i, 0)) for _ in lhs_list]
    in_specs += [pl.BlockSpec((kw, D_MODEL), lambda i, t=t: (t, 0), pipeline_mode=pl.Buffered(1))
                 for t in range(n_lhs)]
    in_specs += [pl.BlockSpec((OP_TM, D_MODEL), lambda i: (i, 0))]
    return pl.pallas_call(
        functools.partial(_out_proj_kernel, n_lhs),
        grid=(m // OP_TM,),
        in_specs=in_specs,
        out_specs=pl.BlockSpec((OP_TM, D_MODEL), lambda i: (i, 0)),
        out_shape=jax.ShapeDtypeStruct((m, D_MODEL), F32),
        scratch_shapes=[pltpu.VMEM((kw, D_MODEL), BF16) for _ in lhs_list],
        compiler_params=_cparams(("arbitrary",)),
        name="out_proj",
    )(*lhs_list, *([w] * n_lhs), x)


def _pl_embed_kernel(final, x_ref, p_ref, g_ref, wg_ref, wp_ref, fg_ref, o_ref,
                     h_ref, wgb_ref, wpb_ref):
    @pl.when(pl.program_id(0) == 0)
    def _():
        wgb_ref[...] = wg_ref[...].astype(BF16)
        wpb_ref[...] = wp_ref[...].astype(BF16)

    _rms_rows_to(h_ref, x_ref, g_ref, PL_TM)
    gate = jax.nn.sigmoid(jnp.dot(h_ref[...], wgb_ref[...], preferred_element_type=F32))
    proj = jnp.dot(p_ref[...].astype(BF16), wpb_ref[...], preferred_element_type=F32)
    y = x_ref[...] + gate * proj
    if final:
        ms = jnp.mean(y * y, axis=-1, keepdims=True)
        y = y * lax.rsqrt(ms + EPS) * fg_ref[...]
    o_ref[...] = y


def _pl_embed(x, p, g, wg, wp, final_g, layer, final):
    m = x.shape[0]
    return pl.pallas_call(
        functools.partial(_pl_embed_kernel, final),
        grid=(m // PL_TM,),
        in_specs=[
            pl.BlockSpec((PL_TM, D_MODEL), lambda i: (i, 0)),
            pl.BlockSpec((None, PL_TM, D_PL), lambda i: (layer, i, 0)),
            pl.BlockSpec((1, D_MODEL), lambda i: (0, 0)),
            pl.BlockSpec((None, D_MODEL, D_MODEL), lambda i: (layer, 0, 0),
                         pipeline_mode=pl.Buffered(1)),
            pl.BlockSpec((None, D_PL, D_MODEL), lambda i: (layer, 0, 0),
                         pipeline_mode=pl.Buffered(1)),
            pl.BlockSpec((1, D_MODEL), lambda i: (0, 0)),
        ],
        out_specs=pl.BlockSpec((PL_TM, D_MODEL), lambda i: (i, 0)),
        out_shape=jax.ShapeDtypeStruct((m, D_MODEL), F32),
        scratch_shapes=[pltpu.VMEM((PL_TM, D_MODEL), BF16),
                        pltpu.VMEM((D_MODEL, D_MODEL), BF16),
                        pltpu.VMEM((D_PL, D_MODEL), BF16)],
        compiler_params=_cparams(("arbitrary",)),
        name="pl_embed",
    )(x, p, g, wg, wp, final_g)


def _gla_kernel(q_ref, k_ref, v_ref, r_ref, gz_ref, gw_ref, gb_ref, ng_ref, o_ref,
                b_ref, kh_ref, s_ref):
    C, SB = GLA_CHUNK, GLA_SUB
    n_chunks = SEQ_TB // C

    @pl.when(pl.program_id(1) == 0)
    def _():
        s_ref[...] = jnp.zeros_like(s_ref)

    lin = jnp.dot(gz_ref[...].astype(BF16), gw_ref[...], preferred_element_type=F32) + gb_ref[...]
    log_a = -(jnp.maximum(-lin, 0.0) + jnp.log1p(jnp.exp(-jnp.abs(lin)))) * (LOG2E / GLA_GATE_TAU)
    tri = (lax.broadcasted_iota(jnp.int32, (C, C), 0)
           >= lax.broadcasted_iota(jnp.int32, (C, C), 1)).astype(F32)
    for c in range(n_chunks):
        b_ref[c * C:(c + 1) * C, :] = jnp.dot(
            tri, log_a[c * C:(c + 1) * C, :], preferred_element_type=F32,
            precision=lax.Precision.HIGHEST)
    kh_ref[...] = k_ref[...].astype(F32) * jnp.exp2(-b_ref[...])
    scaled_ok = jnp.max(jnp.abs(kh_ref[...])) <= GLA_SCALED_KEY_MAX

    lane = lax.broadcasted_iota(jnp.int32, (SB, C), 1)
    trow = lax.broadcasted_iota(jnp.int32, (SB, C), 0)
    causal = (lax.broadcasted_iota(jnp.int32, (C, C), 0)
              >= lax.broadcasted_iota(jnp.int32, (C, C), 1))
    nt = (((1,), (1,)), ((), ()))
    tn = (((0,), (0,)), ((), ()))

    def guarded_products(q, k, b):
        blocks = []
        for i in range(C // SB):
            s0 = i * SB
            q_i = q[s0:s0 + SB]
            b_i = b[s0:s0 + SB]
            acc = jnp.zeros((SB, C), F32)
            if i > 0:
                p_i = b[s0 - 1:s0, :]
                qt = (q_i * jnp.exp2(b_i - p_i)).astype(BF16)
                kt = (k * jnp.exp2(p_i - b)).astype(BF16)
                off = lax.dot_general(qt, kt, nt, preferred_element_type=F32)
                acc = jnp.where(lane < s0, off, 0.0)
            for s in range(SB):
                b_s = b[s0 + s:s0 + s + 1, :]
                k_s = k[s0 + s:s0 + s + 1, :]
                e = jnp.exp2(b_i - b_s)
                col = jnp.sum(q_i * (k_s * e), axis=1, keepdims=True)
                acc = jnp.where((lane == s0 + s) & (trow >= s), col, acc)
            blocks.append(acc)
        return jnp.concatenate(blocks, axis=0)

    def chunk_body(scaled, per_step, it, carry):
        units = []
        for u in range(per_step):
            rows = pl.ds(pl.multiple_of((it * per_step + u) * C, C), C)
            for h in range(GLA_HEADS):
                units.append((u, h, rows))
        hk = lambda h: slice(h * GLA_HEAD_K, (h + 1) * GLA_HEAD_K)
        hv = lambda h: slice(h * GLA_HEAD_V, (h + 1) * GLA_HEAD_V)

        part = {}
        for u, h, rows in units:
            q = q_ref[rows, hk(h)].astype(F32) * (GLA_HEAD_K ** -0.5)
            v = v_ref[rows, hv(h)].astype(BF16)
            b = b_ref[rows, hk(h)]
            b_last = b[C - 1:C, :]
            qb = (q * jnp.exp2(b)).astype(BF16)
            if scaled:
                kh = kh_ref[rows, hk(h)]
                a_mat = jnp.where(causal, lax.dot_general(qb, kh.astype(BF16), nt,
                                                          preferred_element_type=F32), 0.0)
                k_dec = (kh * jnp.exp2(b_last)).astype(BF16)
            else:
                k = k_ref[rows, hk(h)].astype(F32)
                a_mat = guarded_products(q, k, b)
                k_dec = (k * jnp.exp2(b_last - b)).astype(BF16)
            upd = lax.dot_general(v, k_dec, tn, preferred_element_type=F32)
            local = jnp.dot(a_mat.astype(BF16), v, preferred_element_type=F32)
            part[(u, h)] = (qb, jnp.exp2(b_last), upd, local)

        state = {}
        for h in range(GLA_HEADS):
            st = s_ref[h]
            for u in range(per_step):
                state[(u, h)] = st
                _, decay, upd, _ = part[(u, h)]
                st = st * decay + upd
            s_ref[h] = st

        for u, h, rows in units:
            qb, _, _, local = part[(u, h)]
            o = local + lax.dot_general(qb, state[(u, h)].astype(BF16), nt,
                                        preferred_element_type=F32)
            on = o * lax.rsqrt(jnp.mean(o * o, axis=-1, keepdims=True) + EPS) * ng_ref[...]
            r = r_ref[rows, hv(h)].astype(F32)
            o_ref[rows, hv(h)] = (on * (r * jax.nn.sigmoid(r))).astype(o_ref.dtype)
        return carry

    def run(scaled):
        per_step = GLA_CHUNKS_PER_STEP if scaled else 1
        lax.fori_loop(0, n_chunks // per_step,
                      functools.partial(chunk_body, scaled, per_step), 0)

    lax.cond(scaled_ok, lambda: run(True), lambda: run(False))


def _gla(z, gz, gate_w, gate_b, norm_g):
    nb = SEQ // SEQ_TB
    row = lambda b, i: b * nb + i
    return pl.pallas_call(
        _gla_kernel,
        grid=(BATCH, nb),
        in_specs=[
            pl.BlockSpec((SEQ_TB, GLA_DK), lambda b, i: (row(b, i), 0)),
            pl.BlockSpec((SEQ_TB, GLA_DK), lambda b, i: (row(b, i), 1)),
            pl.BlockSpec((SEQ_TB, GLA_DV), lambda b, i: (row(b, i), 1)),
            pl.BlockSpec((SEQ_TB, GLA_DV), lambda b, i: (row(b, i), 2)),
            pl.BlockSpec((SEQ_TB, LANES), lambda b, i: (row(b, i), 0)),
            pl.BlockSpec((LANES, GLA_DK), lambda b, i: (0, 0)),
            pl.BlockSpec((1, GLA_DK), lambda b, i: (0, 0)),
            pl.BlockSpec((1, GLA_HEAD_V), lambda b, i: (0, 0)),
        ],
        out_specs=pl.BlockSpec((SEQ_TB, GLA_DV), lambda b, i: (row(b, i), 0)),
        out_shape=jax.ShapeDtypeStruct((TOKENS, GLA_DV), BF16),
        scratch_shapes=[pltpu.VMEM((SEQ_TB, GLA_DK), F32),
                        pltpu.VMEM((SEQ_TB, GLA_DK), F32),
                        pltpu.VMEM((GLA_HEADS, GLA_HEAD_V, GLA_HEAD_K), F32)],
        compiler_params=_cparams(("parallel", "arbitrary")),
        name="gla",
    )(z, z, z, z, gz, gate_w, gate_b, norm_g)


def _conv_kernel(ca_ref, cb_ref, ha_ref, hb_ref, w_ref, wb_ref, lg_ref, lb_ref, o_ref,
                 sh_ref, y_ref):
    H = CONV_HALO
    n_sh = SUBLANES
    keep = (pl.program_id(1) > 0).astype(F32)

    u_main = ca_ref[...].astype(F32) * jax.nn.sigmoid(cb_ref[...].astype(F32))
    u_halo = ha_ref[...].astype(F32) * jax.nn.sigmoid(hb_ref[...].astype(F32)) * keep
    for r in range(n_sh):
        sh_ref[r, 0:H - r, :] = u_halo[r:H, :]
        sh_ref[r, H - r:H - r + SEQ_TB, :] = u_main

    first = H - (CONV_WIDTH - 1)
    ct_w = 128

    groups = CONV_RC // SUBLANES

    def row_body(cs, bias, rc, carry):
        t0 = pl.multiple_of(rc * CONV_RC, CONV_RC)
        accs = [bias, None]
        for r in range(n_sh):
            offs = [o for o in range(first, first + CONV_WIDTH) if o % n_sh == r]
            lo, hi = offs[0] - r, offs[-1] - r
            slab = sh_ref[r, pl.ds(t0 + lo, CONV_RC + hi - lo), cs]
            for off in offs:
                a = off - r - lo
                win = slab[a:a + CONV_RC].reshape(groups, SUBLANES, ct_w)
                term = w_ref[off - first, :, cs][None] * win
                accs[r % 2] = term if accs[r % 2] is None else accs[r % 2] + term
        y_ref[pl.ds(t0, CONV_RC), cs] = (accs[0] + accs[1]).reshape(CONV_RC, ct_w)
        return carry

    for ct in range(CONV_CH // ct_w):
        cs = slice(ct * ct_w, (ct + 1) * ct_w)
        bias = jnp.broadcast_to(wb_ref[:, cs][None], (groups, SUBLANES, ct_w))
        lax.fori_loop(0, SEQ_TB // CONV_RC, functools.partial(row_body, cs, bias), 0)

    def ln_body(rc, carry):
        sl = pl.ds(pl.multiple_of(rc * 64, 64), 64)
        y = y_ref[sl, :]
        mu = jnp.mean(y, axis=-1, keepdims=True)
        var = jnp.mean(jnp.square(y - mu), axis=-1, keepdims=True)
        t = (y - mu) * lax.rsqrt(var + EPS) * lg_ref[...] + lb_ref[...]
        o_ref[sl, :] = (t * jax.nn.sigmoid(t)).astype(o_ref.dtype)
        return carry

    lax.fori_loop(0, SEQ_TB // 64, ln_body, 0, unroll=2)


def _conv(z, w, wb, ln_g, ln_b):
    nb = SEQ // SEQ_TB
    hb = SEQ_TB // CONV_HALO
    ca_col = (2 * GLA_DK + 2 * GLA_DV) // CONV_CH
    row = lambda b, i: b * nb + i
    halo = lambda b, i: jnp.maximum(row(b, i) * hb - 1, 0)
    return pl.pallas_call(
        _conv_kernel,
        grid=(BATCH, nb),
        in_specs=[
            pl.BlockSpec((SEQ_TB, CONV_CH), lambda b, i: (row(b, i), ca_col)),
            pl.BlockSpec((SEQ_TB, CONV_CH), lambda b, i: (row(b, i), ca_col + 1)),
            pl.BlockSpec((CONV_HALO, CONV_CH), lambda b, i: (halo(b, i), ca_col)),
            pl.BlockSpec((CONV_HALO, CONV_CH), lambda b, i: (halo(b, i), ca_col + 1)),
            pl.BlockSpec((CONV_WIDTH, SUBLANES, CONV_CH), lambda b, i: (0, 0, 0)),
            pl.BlockSpec((1, CONV_CH), lambda b, i: (0, 0)),
            pl.BlockSpec((1, CONV_CH), lambda b, i: (0, 0)),
            pl.BlockSpec((1, CONV_CH), lambda b, i: (0, 0)),
        ],
        out_specs=pl.BlockSpec((SEQ_TB, CONV_CH), lambda b, i: (row(b, i), 0)),
        out_shape=jax.ShapeDtypeStruct((TOKENS, CONV_CH), BF16),
        scratch_shapes=[pltpu.VMEM((SUBLANES, SEQ_TB + CONV_HALO, CONV_CH), F32),
                        pltpu.VMEM((SEQ_TB, CONV_CH), F32)],
        compiler_params=_cparams(("parallel", "parallel")),
        name="conv",
    )(z, z, z, z, w, wb, ln_g, ln_b)


ATT_HALF = ATT_TQ // 2


def _attn_kernel(q_ref, k0_ref, k1_ref, k2_ref, v0_ref, v1_ref, v2_ref,
                 m0_ref, ta_ref, tb_ref, o_ref):
    k_refs = (k0_ref, k1_ref, k2_ref)
    v_refs = (v0_ref, v1_ref, v2_ref)
    nt = (((1,), (1,)), ((), ()))
    tn = (((0,), (0,)), ((), ()))
    H = ATT_HALF

    def scores(j_min, h):
        hs = slice(h * ATT_HEAD_DIM, (h + 1) * ATT_HEAD_DIM)
        q = q_ref[:, hs]
        tiles = {}
        for j in range(j_min, 3):
            st = lax.dot_general(k_refs[j][:, hs], q, nt, preferred_element_type=F32)
            for kh in range(2):
                for a in range(2):
                    n = 2 * j + kh - a
                    if n < 0 or n > 4:
                        continue
                    t = st[kh * H:(kh + 1) * H, a * H:(a + 1) * H]
                    if n == 0:
                        t = t + m0_ref[...]
                    elif n == 3:
                        t = t + ta_ref[h]
                    elif n == 4:
                        t = t + tb_ref[h]
                    tiles[(j, kh, a)] = t
        maxima = []
        for a in range(2):
            mx = None
            for key in tiles:
                if key[2] == a:
                    cur = jnp.max(tiles[key], axis=0, keepdims=True)
                    mx = cur if mx is None else jnp.maximum(mx, cur)
            maxima.append(mx)
        return tiles, maxima

    def weights(tiles, maxima):
        probs = {}
        inv_l = []
        for a in range(2):
            mine = [key for key in tiles if key[2] == a]
            tot = None
            for key in mine:
                p = jnp.exp2(tiles[key] - maxima[a])
                probs[key] = p.astype(BF16)
                cur = jnp.sum(p, axis=0, keepdims=True)
                tot = cur if tot is None else tot + cur
            inv_l.append(1.0 / tot)
        return probs, inv_l

    def values(j_min, h, probs, inv_l):
        hs = slice(h * ATT_HEAD_DIM, (h + 1) * ATT_HEAD_DIM)
        zero = jnp.zeros((H, H), BF16)
        ot = None
        for j in range(j_min, 3):
            pt = jnp.concatenate(
                [jnp.concatenate([probs.get((j, kh, a), zero) for a in range(2)], axis=1)
                 for kh in range(2)], axis=0)
            cur = lax.dot_general(v_refs[j][:, hs], pt, tn, preferred_element_type=F32)
            ot = cur if ot is None else ot + cur
        ot = ot * jnp.concatenate(inv_l, axis=1)
        o_ref[:, hs] = ot.T.astype(o_ref.dtype)

    def all_heads(j_min):
        scored, weighted = {}, {}
        for t in range(ATT_HEADS + 2):
            if t < ATT_HEADS:
                scored[t] = scores(j_min, t)
            if 0 <= t - 1 < ATT_HEADS:
                weighted[t - 1] = weights(*scored.pop(t - 1))
            if 0 <= t - 2 < ATT_HEADS:
                values(j_min, t - 2, *weighted.pop(t - 2))

    i = pl.program_id(1)
    lax.cond(i >= 2, lambda: all_heads(0),
             lambda: lax.cond(i == 1, lambda: all_heads(1), lambda: all_heads(2)))


def _attention(qkv, m0, ta, tb):
    nb = SEQ // ATT_TQ
    row = lambda b, i: b * nb + i
    back = lambda d: (lambda b, i: (b * nb + jnp.maximum(i - d, 0)))
    spec = lambda rowfn, col: pl.BlockSpec((ATT_TQ, D_MODEL), lambda b, i: (rowfn(b, i), col))
    table = pl.BlockSpec((ATT_HEADS, ATT_HALF, ATT_HALF), lambda b, i: (0, 0, 0))
    return pl.pallas_call(
        _attn_kernel,
        grid=(BATCH, nb),
        in_specs=[
            spec(row, 0),
            spec(back(2), 1), spec(back(1), 1), spec(row, 1),
            spec(back(2), 2), spec(back(1), 2), spec(row, 2),
            pl.BlockSpec((ATT_HALF, ATT_HALF), lambda b, i: (0, 0)),
            table, table,
        ],
        out_specs=pl.BlockSpec((ATT_TQ, D_MODEL), lambda b, i: (row(b, i), 0)),
        out_shape=jax.ShapeDtypeStruct((TOKENS, D_MODEL), BF16),
        compiler_params=_cparams(("parallel", "parallel")),
        name="attention",
    )(qkv, qkv, qkv, qkv, qkv, qkv, qkv, m0, ta, tb)


def _attention_bias_tables(rel_bias):
    H = ATT_HALF
    assert H == REL_CLIP
    rb = rel_bias.astype(F32)
    rel = (rb - rb[:, 2 * REL_CLIP:]) * LOG2E

    def toeplitz(g):
        flat = jnp.tile(g, (1, H))[:, :H * (2 * H - 1)]
        return flat.reshape(-1, H, 2 * H - 1)[:, :, :H]

    ta = toeplitz(jnp.concatenate([jnp.zeros((ATT_HEADS, H), F32),
                                   rel[:, REL_CLIP:2 * REL_CLIP]], axis=1))
    tb = toeplitz(jnp.concatenate([rel[:, REL_CLIP:2 * REL_CLIP], rel[:, :REL_CLIP]], axis=1))
    kc = jnp.arange(H)[:, None] // ATT_CHUNK
    qc = jnp.arange(H)[None, :] // ATT_CHUNK
    tb = jnp.where((kc <= qc)[None], tb, NEG_BIG)
    m0 = jnp.where(kc >= qc, 0.0, NEG_BIG).astype(F32)
    return m0, ta, tb


def kernel(x, p, ffn_norm, ffn_w_gate, ffn_w_up, ffn_w_down, mix_norm, ab_w_in, gla_gate_w, gla_gate_b, gla_norm_g, conv_dw, conv_dw_b, conv_ln_g, conv_ln_b, ab_w_out, att_w_qkv, att_rel_bias, att_w_o, pl_norm, pl_w_gate, pl_w_proj, final_norm):
    xs = x.reshape(TOKENS, D_MODEL)
    ps = p.reshape(DEPTH, TOKENS, D_PL)
    row = lambda a: a.reshape(1, -1).astype(F32)

    def ffn(xs, i, s):
        return _ffn(xs, row(ffn_norm[i, s]), ffn_w_gate, ffn_w_up, ffn_w_down, i, s)

    for i in range(DEPTH):
        e = i // 2
        xs = ffn(xs, i, 0)
        if i % 2 == 0:
            w_in = ab_w_in[e]
            gz_lo = 2 * GLA_DK + 2 * GLA_DV
            gz_hi = gz_lo + GLA_GATE_RANK
            w_bf = w_in.astype(BF16)
            w_conv = w_bf[:, gz_hi:]
            w_gz = jnp.pad(w_bf[:, gz_lo:gz_hi], ((0, 0), (0, LANES - GLA_GATE_RANK)))
            z_all, gz = _rms_proj(
                xs, row(mix_norm[i]),
                [(w_bf, gz_lo // MM_TN), (w_conv, 2 * CONV_CH // MM_TN)],
                jnp.ones((1, AB_MAIN), F32), BF16, w_extra=w_gz)
            gate_w = jnp.pad(gla_gate_w[e], ((0, LANES - GLA_GATE_RANK), (0, 0))).astype(BF16)
            a_out = _gla(z_all, gz, gate_w, row(gla_gate_b[e]), row(gla_norm_g[e]))
            taps = jnp.broadcast_to(conv_dw[e].astype(F32)[:, None, :],
                                    (CONV_WIDTH, SUBLANES, CONV_CH))
            b_out = _conv(z_all, taps, row(conv_dw_b[e]), row(conv_ln_g[e]), row(conv_ln_b[e]))
            xs = _out_proj([a_out, b_out], ab_w_out[e], xs)
        else:
            colscale = jnp.concatenate([jnp.full((1, D_MODEL), ATT_HEAD_DIM ** -0.5 * LOG2E, F32),
                                        jnp.ones((1, 2 * D_MODEL), F32)], axis=1)
            qkv = _rms_proj(xs, row(mix_norm[i]), [(att_w_qkv[e], 3 * D_MODEL // MM_TN)],
                            colscale, BF16)
            m0, ta, tb = _attention_bias_tables(att_rel_bias[e])
            o = _attention(qkv, m0, ta, tb)
            xs = _out_proj([o], att_w_o[e], xs)
        xs = ffn(xs, i, 1)
        xs = _pl_embed(xs, ps, row(pl_norm[i]), pl_w_gate, pl_w_proj, row(final_norm),
                       i, i == DEPTH - 1)
    return xs.reshape(BATCH, SEQ, D_MODEL)
```

```python
import functools

import jax
import jax.numpy as jnp
from jax import lax
from jax.experimental import pallas as pl
from jax.experimental.pallas import tpu as pltpu

F32 = jnp.float32
BF16 = jnp.bfloat16

D_MODEL = 2048
BATCH = 4
SEQ = 2048
DEPTH = 2
TOKENS = BATCH * SEQ
D_PL = 256
D_FF = 5632
EPS = 1e-6

GLA_HEADS = 4
GLA_DK = 512
GLA_DV = 1024
GLA_HEAD_K = 128
GLA_HEAD_V = 256
GLA_GATE_RANK = 16
GLA_GATE_TAU = 16.0
GLA_CHUNK = 64
GLA_SUB = 16
GLA_SCALED_KEY_MAX = 2.0 ** 40
GLA_CHUNKS_PER_STEP = 4
CONV_CH = 1024
CONV_WIDTH = 31
AB_MAIN = 2 * GLA_DK + 2 * GLA_DV + 2 * CONV_CH
ATT_HEADS = 16
ATT_HEAD_DIM = 128
ATT_CHUNK = 64
LEFT_CHUNKS = 8
REL_CLIP = 128
NEG_BIG = -1e30
LOG2E = 1.4426950408889634

V7X_VMEM_BYTES = 64 * 1024 * 1024
LANES = 128
SUBLANES = 8
VMEM_LIMIT = V7X_VMEM_BYTES - 4 * 1024 * 1024

FFN_TM = 1024
FFN_TF = 512
FFN_FIRST_ROWS = FFN_TM
FFN_TN = 512
MM_TM = 1024
MM_TN = 1024
PL_TM = 256
OP_TM = 512
SEQ_TB = 512
GLA_TB = 1024
CONV_HALO = 32
CONV_RC = 128
ATT_TQ = 256
RMS_ROWS = 256


def _cparams(sem):
    return pltpu.CompilerParams(dimension_semantics=sem, vmem_limit_bytes=VMEM_LIMIT)


def _rms_rows_to(dst_ref, x_ref, g_ref, rows):
    def body(c, carry):
        sl = pl.ds(pl.multiple_of(c * RMS_ROWS, RMS_ROWS), RMS_ROWS)
        x = x_ref[sl, :]
        ms = jnp.mean(x * x, axis=-1, keepdims=True)
        dst_ref[sl, :] = (x * lax.rsqrt(ms + EPS) * g_ref[...]).astype(dst_ref.dtype)
        return carry
    lax.fori_loop(0, rows // RMS_ROWS, body, 0)


def _ffn_step(f, load_residual, g_ref, wg_ref, wu_ref, wd_ref, o_ref, h_ref):
    @pl.when(f == 0)
    def _():
        load_residual()
        _rms_rows_to(h_ref, o_ref, g_ref, o_ref.shape[0])

    h = h_ref[...]
    gate = jnp.dot(h, wg_ref[...], preferred_element_type=F32)
    up = jnp.dot(h, wu_ref[...], preferred_element_type=F32)
    a = (0.5 * gate * jax.nn.sigmoid(gate) * up).astype(BF16)

    for n in range(D_MODEL // FFN_TN):
        cs = slice(n * FFN_TN, (n + 1) * FFN_TN)
        o_ref[:, cs] += jnp.dot(a, wd_ref[:, cs], preferred_element_type=F32)


def _ffn_first_kernel(x_hbm, g_ref, wg_ref, wu_ref, wd_ref, o_ref, wgb_ref, wub_ref, wdb_ref,
                      h_ref, sem):
    def load_residual():
        copy = pltpu.make_async_copy(x_hbm.at[pl.ds(0, FFN_FIRST_ROWS)], o_ref, sem)
        copy.start()
        copy.wait()

    wgb_ref[...] = wg_ref[...].astype(BF16)
    wub_ref[...] = wu_ref[...].astype(BF16)
    wdb_ref[...] = wd_ref[...].astype(BF16)
    _ffn_step(pl.program_id(1), load_residual, g_ref, wgb_ref, wub_ref, wdb_ref, o_ref, h_ref)


def _ffn_rest_kernel(n_f, n_copy, x_ref, g_ref, wg_ref, wu_ref, wd_ref, first_ref, o_ref,
                     h_ref, sem):
    s = pl.program_id(0)

    @pl.when(s < n_copy)
    def _():
        rows = pl.ds(pl.multiple_of(s * FFN_TM, FFN_TM), FFN_TM)
        copy = pltpu.make_async_copy(first_ref.at[rows], o_ref, sem)
        copy.start()
        copy.wait()

    def load_residual():
        o_ref[...] = x_ref[...]

    @pl.when(s >= n_copy)
    def _():
        _ffn_step((s - n_copy) % n_f, load_residual, g_ref, wg_ref, wu_ref, wd_ref, o_ref,
                  h_ref)


def _ffn(x, g, wg, wu, wd, layer, half):
    m = x.shape[0]
    tf = FFN_TF
    first, wgb, wub, wdb = pl.pallas_call(
        _ffn_first_kernel,
        grid=(1, D_FF // tf),
        in_specs=[
            pl.BlockSpec(memory_space=pl.ANY),
            pl.BlockSpec((1, D_MODEL), lambda i, f: (0, 0)),
            pl.BlockSpec((None, None, D_MODEL, tf), lambda i, f: (layer, half, 0, f)),
            pl.BlockSpec((None, None, D_MODEL, tf), lambda i, f: (layer, half, 0, f)),
            pl.BlockSpec((None, None, tf, D_MODEL), lambda i, f: (layer, half, f, 0)),
        ],
        out_specs=[
            pl.BlockSpec((FFN_FIRST_ROWS, D_MODEL), lambda i, f: (0, 0),
                         pipeline_mode=pl.Buffered(1)),
            pl.BlockSpec((D_MODEL, tf), lambda i, f: (0, f)),
            pl.BlockSpec((D_MODEL, tf), lambda i, f: (0, f)),
            pl.BlockSpec((tf, D_MODEL), lambda i, f: (f, 0)),
        ],
        out_shape=[
            jax.ShapeDtypeStruct((FFN_FIRST_ROWS, D_MODEL), F32),
            jax.ShapeDtypeStruct((D_MODEL, D_FF), BF16),
            jax.ShapeDtypeStruct((D_MODEL, D_FF), BF16),
            jax.ShapeDtypeStruct((D_FF, D_MODEL), BF16),
        ],
        scratch_shapes=[pltpu.VMEM((FFN_FIRST_ROWS, D_MODEL), BF16), pltpu.SemaphoreType.DMA(())],
        compiler_params=_cparams(("parallel", "arbitrary")),
        name="ffn_first",
    )(x, g, wg, wu, wd)
    n_f = D_FF // tf
    n_copy = FFN_FIRST_ROWS // FFN_TM
    tile = lambda s: jnp.where(s < n_copy, s, n_copy + (s - n_copy) // n_f)
    f_of = lambda s: jnp.where(s < n_copy, 0, (s - n_copy) % n_f)
    return pl.pallas_call(
        functools.partial(_ffn_rest_kernel, n_f, n_copy),
        grid=(n_copy + (m // FFN_TM - n_copy) * n_f,),
        in_specs=[
            pl.BlockSpec((FFN_TM, D_MODEL), lambda s: (jnp.maximum(tile(s), n_copy), 0)),
            pl.BlockSpec((1, D_MODEL), lambda s: (0, 0)),
            pl.BlockSpec((D_MODEL, tf), lambda s: (0, f_of(s))),
            pl.BlockSpec((D_MODEL, tf), lambda s: (0, f_of(s))),
            pl.BlockSpec((tf, D_MODEL), lambda s: (f_of(s), 0)),
            pl.BlockSpec(memory_space=pl.ANY),
        ],
        out_specs=pl.BlockSpec((FFN_TM, D_MODEL), lambda s: (tile(s), 0)),
        out_shape=jax.ShapeDtypeStruct((m, D_MODEL), F32),
        scratch_shapes=[pltpu.VMEM((FFN_TM, D_MODEL), BF16), pltpu.SemaphoreType.DMA(())],
        compiler_params=_cparams(("arbitrary",)),
        name="ffn_rest",
    )(x, g, wgb, wub, wdb, first)


def _rms_proj_kernel(seg_tiles, has_extra, *refs):
    n_seg = len(seg_tiles)
    x_ref, g_ref = refs[:2]
    w_refs = refs[2:2 + n_seg]
    cs_ref = refs[2 + n_seg]
    if has_extra:
        we_ref, o_ref, oe_ref, h_ref = refs[3 + n_seg:]
    else:
        o_ref, h_ref = refs[3 + n_seg:]
    j = pl.program_id(1)

    @pl.when(j == 0)
    def _():
        _rms_rows_to(h_ref, x_ref, g_ref, MM_TM)
        if has_extra:
            oe_ref[...] = jnp.dot(h_ref[...], we_ref[...].astype(BF16),
                                  preferred_element_type=F32)

    start = 0
    for w_ref, tiles in zip(w_refs, seg_tiles):
        @pl.when((j >= start) & (j < start + tiles))
        def _(w_ref=w_ref):
            acc = jnp.dot(h_ref[...], w_ref[...].astype(BF16), preferred_element_type=F32)
            o_ref[...] = (acc * cs_ref[...]).astype(o_ref.dtype)
        start += tiles


def _rms_proj(x, g, w_list, colscale, out_dtype, w_extra=None):
    m = x.shape[0]
    seg_tiles = tuple(t for _, t in w_list)
    n_tiles = sum(seg_tiles)
    has_extra = w_extra is not None
    in_specs = [
        pl.BlockSpec((MM_TM, D_MODEL), lambda i, j: (i, 0)),
        pl.BlockSpec((1, D_MODEL), lambda i, j: (0, 0)),
    ]
    start = 0
    for _, tiles in w_list:
        in_specs.append(pl.BlockSpec(
            (D_MODEL, MM_TN),
            lambda i, j, s=start, t=tiles: (0, jnp.clip(j - s, 0, t - 1))))
        start += tiles
    in_specs.append(pl.BlockSpec((1, MM_TN), lambda i, j: (0, j)))
    out_specs = pl.BlockSpec((MM_TM, MM_TN), lambda i, j: (i, j))
    out_shape = jax.ShapeDtypeStruct((m, n_tiles * MM_TN), out_dtype)
    args = [x, g] + [w for w, _ in w_list] + [colscale]
    if has_extra:
        in_specs.append(pl.BlockSpec((D_MODEL, LANES), lambda i, j: (0, 0)))
        out_specs = [out_specs, pl.BlockSpec((MM_TM, LANES), lambda i, j: (i, 0))]
        out_shape = [out_shape, jax.ShapeDtypeStruct((m, LANES), F32)]
        args.append(w_extra)
    return pl.pallas_call(
        functools.partial(_rms_proj_kernel, seg_tiles, has_extra),
        grid=(m // MM_TM, n_tiles),
        in_specs=in_specs,
        out_specs=out_specs,
        out_shape=out_shape,
        scratch_shapes=[pltpu.VMEM((MM_TM, D_MODEL), BF16)],
        compiler_params=_cparams(("parallel", "arbitrary")),
        name="rms_proj",
    )(*args)


def _out_proj_kernel(n_lhs, *refs):
    lhs_refs = refs[:n_lhs]
    w_refs = refs[n_lhs:2 * n_lhs]
    x_ref, o_ref = refs[2 * n_lhs], refs[2 * n_lhs + 1]
    wb_refs = refs[2 * n_lhs + 2:]

    @pl.when(pl.program_id(0) == 0)
    def _():
        for w_ref, wb_ref in zip(w_refs, wb_refs):
            wb_ref[...] = w_ref[...].astype(BF16)

    acc = x_ref[...]
    for a_ref, wb_ref in zip(lhs_refs, wb_refs):
        acc = acc + jnp.dot(a_ref[...], wb_ref[...], preferred_element_type=F32)
    o_ref[...] = acc


def _out_proj(lhs_list, w, x):
    m = x.shape[0]
    n_lhs = len(lhs_list)
    kw = lhs_list[0].shape[1]
    in_specs = [pl.BlockSpec((OP_TM, kw), lambda i: (i, 0)) for _ in lhs_list]
    in_specs += [pl.BlockSpec((kw, D_MODEL), lambda i, t=t: (t, 0), pipeline_mode=pl.Buffered(1))
                 for t in range(n_lhs)]
    in_specs += [pl.BlockSpec((OP_TM, D_MODEL), lambda i: (i, 0))]
    return pl.pallas_call(
        functools.partial(_out_proj_kernel, n_lhs),
        grid=(m // OP_TM,),
        in_specs=in_specs,
        out_specs=pl.BlockSpec((OP_TM, D_MODEL), lambda i: (i, 0)),
        out_shape=jax.ShapeDtypeStruct((m, D_MODEL), F32),
        scratch_shapes=[pltpu.VMEM((kw, D_MODEL), BF16) for _ in lhs_list],
        compiler_params=_cparams(("arbitrary",)),
        name="out_proj",
    )(*lhs_list, *([w] * n_lhs), x)


def _pl_embed_kernel(final, x_ref, p_ref, g_ref, wg_ref, wp_ref, fg_ref, o_ref,
                     h_ref, wgb_ref, wpb_ref):
    @pl.when(pl.program_id(0) == 0)
    def _():
        wgb_ref[...] = wg_ref[...].astype(BF16)
        wpb_ref[...] = wp_ref[...].astype(BF16)

    _rms_rows_to(h_ref, x_ref, g_ref, PL_TM)
    gate = jax.nn.sigmoid(jnp.dot(h_ref[...], wgb_ref[...], preferred_element_type=F32))
    proj = jnp.dot(p_ref[...].astype(BF16), wpb_ref[...], preferred_element_type=F32)
    y = x_ref[...] + gate * proj
    if final:
        ms = jnp.mean(y * y, axis=-1, keepdims=True)
        y = y * lax.rsqrt(ms + EPS) * fg_ref[...]
    o_ref[...] = y


def _pl_embed(x, p, g, wg, wp, final_g, layer, final):
    m = x.shape[0]
    return pl.pallas_call(
        functools.partial(_pl_embed_kernel, final),
        grid=(m // PL_TM,),
        in_specs=[
            pl.BlockSpec((PL_TM, D_MODEL), lambda i: (i, 0)),
            pl.BlockSpec((None, PL_TM, D_PL), lambda i: (layer, i, 0)),
            pl.BlockSpec((1, D_MODEL), lambda i: (0, 0)),
            pl.BlockSpec((None, D_MODEL, D_MODEL), lambda i: (layer, 0, 0),
                         pipeline_mode=pl.Buffered(1)),
            pl.BlockSpec((None, D_PL, D_MODEL), lambda i: (layer, 0, 0),
                         pipeline_mode=pl.Buffered(1)),
            pl.BlockSpec((1, D_MODEL), lambda i: (0, 0)),
        ],
        out_specs=pl.BlockSpec((PL_TM, D_MODEL), lambda i: (i, 0)),
        out_shape=jax.ShapeDtypeStruct((m, D_MODEL), F32),
        scratch_shapes=[pltpu.VMEM((PL_TM, D_MODEL), BF16),
                        pltpu.VMEM((D_MODEL, D_MODEL), BF16),
                        pltpu.VMEM((D_PL, D_MODEL), BF16)],
        compiler_params=_cparams(("arbitrary",)),
        name="pl_embed",
    )(x, p, g, wg, wp, final_g)


def _gla_kernel(q_ref, k_ref, v_ref, r_ref, gz_ref, gw_ref, gb_ref, ng_ref, o_ref,
                b_ref, kh_ref, s_ref):
    C, SB = GLA_CHUNK, GLA_SUB
    n_chunks = GLA_TB // C

    @pl.when(pl.program_id(1) == 0)
    def _():
        s_ref[...] = jnp.zeros_like(s_ref)

    lin = jnp.dot(gz_ref[...].astype(BF16), gw_ref[...], preferred_element_type=F32) + gb_ref[...]
    log_a = -(jnp.maximum(-lin, 0.0) + jnp.log1p(jnp.exp(-jnp.abs(lin)))) * (LOG2E / GLA_GATE_TAU)
    tri = (lax.broadcasted_iota(jnp.int32, (C, C), 0)
           >= lax.broadcasted_iota(jnp.int32, (C, C), 1)).astype(F32)
    for c in range(n_chunks):
        b_ref[c * C:(c + 1) * C, :] = jnp.dot(
            tri, log_a[c * C:(c + 1) * C, :], preferred_element_type=F32,
            precision=lax.Precision.HIGHEST)
    kh_ref[...] = k_ref[...].astype(F32) * jnp.exp2(-b_ref[...])
    scaled_ok = jnp.max(jnp.abs(kh_ref[...])) <= GLA_SCALED_KEY_MAX

    lane = lax.broadcasted_iota(jnp.int32, (SB, C), 1)
    trow = lax.broadcasted_iota(jnp.int32, (SB, C), 0)
    causal = (lax.broadcasted_iota(jnp.int32, (C, C), 0)
              >= lax.broadcasted_iota(jnp.int32, (C, C), 1))
    nt = (((1,), (1,)), ((), ()))
    tn = (((0,), (0,)), ((), ()))

    def guarded_products(q, k, b):
        blocks = []
        for i in range(C // SB):
            s0 = i * SB
            q_i = q[s0:s0 + SB]
            b_i = b[s0:s0 + SB]
            acc = jnp.zeros((SB, C), F32)
            if i > 0:
                p_i = b[s0 - 1:s0, :]
                qt = (q_i * jnp.exp2(b_i - p_i)).astype(BF16)
                kt = (k * jnp.exp2(p_i - b)).astype(BF16)
                off = lax.dot_general(qt, kt, nt, preferred_element_type=F32)
                acc = jnp.where(lane < s0, off, 0.0)
            for s in range(SB):
                b_s = b[s0 + s:s0 + s + 1, :]
                k_s = k[s0 + s:s0 + s + 1, :]
                e = jnp.exp2(b_i - b_s)
                col = jnp.sum(q_i * (k_s * e), axis=1, keepdims=True)
                acc = jnp.where((lane == s0 + s) & (trow >= s), col, acc)
            blocks.append(acc)
        return jnp.concatenate(blocks, axis=0)

    def chunk_body(scaled, per_step, it, carry):
        units = []
        for u in range(per_step):
            rows = pl.ds(pl.multiple_of((it * per_step + u) * C, C), C)
            for h in range(GLA_HEADS):
                units.append((u, h, rows))
        hk = lambda h: slice(h * GLA_HEAD_K, (h + 1) * GLA_HEAD_K)
        hv = lambda h: slice(h * GLA_HEAD_V, (h + 1) * GLA_HEAD_V)

        part = {}
        for u, h, rows in units:
            q = q_ref[rows, hk(h)].astype(F32) * (GLA_HEAD_K ** -0.5)
            v = v_ref[rows, hv(h)].astype(BF16)
            b = b_ref[rows, hk(h)]
            b_last = b[C - 1:C, :]
            qb = (q * jnp.exp2(b)).astype(BF16)
            if scaled:
                kh = kh_ref[rows, hk(h)]
                a_mat = jnp.where(causal, lax.dot_general(qb, kh.astype(BF16), nt,
                                                          preferred_element_type=F32), 0.0)
                k_dec = (kh * jnp.exp2(b_last)).astype(BF16)
            else:
                k = k_ref[rows, hk(h)].astype(F32)
                a_mat = guarded_products(q, k, b)
                k_dec = (k * jnp.exp2(b_last - b)).astype(BF16)
            upd = lax.dot_general(v, k_dec, tn, preferred_element_type=F32)
            local = jnp.dot(a_mat.astype(BF16), v, preferred_element_type=F32)
            part[(u, h)] = (qb, jnp.exp2(b_last), upd, local)

        state = {}
        for h in range(GLA_HEADS):
            st = s_ref[h]
            for u in range(per_step):
                state[(u, h)] = st
                _, decay, upd, _ = part[(u, h)]
                st = st * decay + upd
            s_ref[h] = st

        for u, h, rows in units:
            qb, _, _, local = part[(u, h)]
            o = local + lax.dot_general(qb, state[(u, h)].astype(BF16), nt,
                                        preferred_element_type=F32)
            on = o * lax.rsqrt(jnp.mean(o * o, axis=-1, keepdims=True) + EPS) * ng_ref[...]
            r = r_ref[rows, hv(h)].astype(F32)
            o_ref[rows, hv(h)] = (on * (r * jax.nn.sigmoid(r))).astype(o_ref.dtype)
        return carry

    def run(scaled):
        per_step = GLA_CHUNKS_PER_STEP if scaled else 1
        lax.fori_loop(0, n_chunks // per_step,
                      functools.partial(chunk_body, scaled, per_step), 0)

    lax.cond(scaled_ok, lambda: run(True), lambda: run(False))


def _gla(z, gz, gate_w, gate_b, norm_g):
    nb = SEQ // GLA_TB
    row = lambda b, i: b * nb + i
    return pl.pallas_call(
        _gla_kernel,
        grid=(BATCH, nb),
        in_specs=[
            pl.BlockSpec((GLA_TB, GLA_DK), lambda b, i: (row(b, i), 0)),
            pl.BlockSpec((GLA_TB, GLA_DK), lambda b, i: (row(b, i), 1)),
            pl.BlockSpec((GLA_TB, GLA_DV), lambda b, i: (row(b, i), 1)),
            pl.BlockSpec((GLA_TB, GLA_DV), lambda b, i: (row(b, i), 2)),
            pl.BlockSpec((GLA_TB, LANES), lambda b, i: (row(b, i), 0)),
            pl.BlockSpec((LANES, GLA_DK), lambda b, i: (0, 0)),
            pl.BlockSpec((1, GLA_DK), lambda b, i: (0, 0)),
            pl.BlockSpec((1, GLA_HEAD_V), lambda b, i: (0, 0)),
        ],
        out_specs=pl.BlockSpec((GLA_TB, GLA_DV), lambda b, i: (row(b, i), 0)),
        out_shape=jax.ShapeDtypeStruct((TOKENS, GLA_DV), BF16),
        scratch_shapes=[pltpu.VMEM((GLA_TB, GLA_DK), F32),
                        pltpu.VMEM((GLA_TB, GLA_DK), F32),
                        pltpu.VMEM((GLA_HEADS, GLA_HEAD_V, GLA_HEAD_K), F32)],
        compiler_params=_cparams(("parallel", "arbitrary")),
        name="gla",
    )(z, z, z, z, gz, gate_w, gate_b, norm_g)


def _conv_kernel(ca_ref, cb_ref, ha_ref, hb_ref, w_ref, wb_ref, lg_ref, lb_ref, o_ref,
                 sh_ref, y_ref):
    H = CONV_HALO
    n_sh = SUBLANES
    keep = (pl.program_id(1) > 0).astype(F32)

    u_main = ca_ref[...].astype(F32) * jax.nn.sigmoid(cb_ref[...].astype(F32))
    u_halo = ha_ref[...].astype(F32) * jax.nn.sigmoid(hb_ref[...].astype(F32)) * keep
    for r in range(n_sh):
        sh_ref[r, 0:H - r, :] = u_halo[r:H, :]
        sh_ref[r, H - r:H - r + SEQ_TB, :] = u_main

    first = H - (CONV_WIDTH - 1)
    ct_w = 128

    groups = CONV_RC // SUBLANES

    def row_body(cs, bias, rc, carry):
        t0 = pl.multiple_of(rc * CONV_RC, CONV_RC)
        accs = [bias, None]
        for r in range(n_sh):
            offs = [o for o in range(first, first + CONV_WIDTH) if o % n_sh == r]
            lo, hi = offs[0] - r, offs[-1] - r
            slab = sh_ref[r, pl.ds(t0 + lo, CONV_RC + hi - lo), cs]
            for off in offs:
                a = off - r - lo
                win = slab[a:a + CONV_RC].reshape(groups, SUBLANES, ct_w)
                term = w_ref[off - first, :, cs][None] * win
                accs[r % 2] = term if accs[r % 2] is None else accs[r % 2] + term
        y_ref[pl.ds(t0, CONV_RC), cs] = (accs[0] + accs[1]).reshape(CONV_RC, ct_w)
        return carry

    for ct in range(CONV_CH // ct_w):
        cs = slice(ct * ct_w, (ct + 1) * ct_w)
        bias = jnp.broadcast_to(wb_ref[:, cs][None], (groups, SUBLANES, ct_w))
        lax.fori_loop(0, SEQ_TB // CONV_RC, functools.partial(row_body, cs, bias), 0)

    def ln_body(rc, carry):
        sl = pl.ds(pl.multiple_of(rc * 64, 64), 64)
        y = y_ref[sl, :]
        mu = jnp.mean(y, axis=-1, keepdims=True)
        var = jnp.mean(jnp.square(y - mu), axis=-1, keepdims=True)
        t = (y - mu) * lax.rsqrt(var + EPS) * lg_ref[...] + lb_ref[...]
        o_ref[sl, :] = (t * jax.nn.sigmoid(t)).astype(o_ref.dtype)
        return carry

    lax.fori_loop(0, SEQ_TB // 64, ln_body, 0, unroll=2)


def _conv(z, w, wb, ln_g, ln_b):
    nb = SEQ // SEQ_TB
    hb = SEQ_TB // CONV_HALO
    ca_col = (2 * GLA_DK + 2 * GLA_DV) // CONV_CH
    row = lambda b, i: b * nb + i
    halo = lambda b, i: jnp.maximum(row(b, i) * hb - 1, 0)
    return pl.pallas_call(
        _conv_kernel,
        grid=(BATCH, nb),
        in_specs=[
            pl.BlockSpec((SEQ_TB, CONV_CH), lambda b, i: (row(b, i), ca_col)),
            pl.BlockSpec((SEQ_TB, CONV_CH), lambda b, i: (row(b, i), ca_col + 1)),
            pl.BlockSpec((CONV_HALO, CONV_CH), lambda b, i: (halo(b, i), ca_col)),
            pl.BlockSpec((CONV_HALO, CONV_CH), lambda b, i: (halo(b, i), ca_col + 1)),
            pl.BlockSpec((CONV_WIDTH, SUBLANES, CONV_CH), lambda b, i: (0, 0, 0)),
            pl.BlockSpec((1, CONV_CH), lambda b, i: (0, 0)),
            pl.BlockSpec((1, CONV_CH), lambda b, i: (0, 0)),
            pl.BlockSpec((1, CONV_CH), lambda b, i: (0, 0)),
        ],
        out_specs=pl.BlockSpec((SEQ_TB, CONV_CH), lambda b, i: (row(b, i), 0)),
        out_shape=jax.ShapeDtypeStruct((TOKENS, CONV_CH), BF16),
        scratch_shapes=[pltpu.VMEM((SUBLANES, SEQ_TB + CONV_HALO, CONV_CH), F32),
                        pltpu.VMEM((SEQ_TB, CONV_CH), F32)],
        compiler_params=_cparams(("parallel", "parallel")),
        name="conv",
    )(z, z, z, z, w, wb, ln_g, ln_b)


ATT_HALF = ATT_TQ // 2


def _attn_kernel(q_ref, k0_ref, k1_ref, k2_ref, v0_ref, v1_ref, v2_ref,
                 m0_ref, ta_ref, tb_ref, o_ref):
    k_refs = (k0_ref, k1_ref, k2_ref)
    v_refs = (v0_ref, v1_ref, v2_ref)
    nt = (((1,), (1,)), ((), ()))
    tn = (((0,), (0,)), ((), ()))
    H = ATT_HALF

    def scores(j_min, h):
        hs = slice(h * ATT_HEAD_DIM, (h + 1) * ATT_HEAD_DIM)
        q = q_ref[:, hs]
        tiles = {}
        for j in range(j_min, 3):
            st = lax.dot_general(k_refs[j][:, hs], q, nt, preferred_element_type=F32)
            for kh in range(2):
                for a in range(2):
                    n = 2 * j + kh - a
                    if n < 0 or n > 4:
                        continue
                    t = st[kh * H:(kh + 1) * H, a * H:(a + 1) * H]
                    if n == 0:
                        t = t + m0_ref[...]
                    elif n == 3:
                        t = t + ta_ref[h]
                    elif n == 4:
                        t = t + tb_ref[h]
                    tiles[(j, kh, a)] = t
        maxima = []
        for a in range(2):
            mx = None
            for key in tiles:
                if key[2] == a:
                    cur = jnp.max(tiles[key], axis=0, keepdims=True)
                    mx = cur if mx is None else jnp.maximum(mx, cur)
            maxima.append(mx)
        return tiles, maxima

    def weights(tiles, maxima):
        probs = {}
        inv_l = []
        for a in range(2):
            mine = [key for key in tiles if key[2] == a]
            tot = None
            for key in mine:
                p = jnp.exp2(tiles[key] - maxima[a])
                probs[key] = p.astype(BF16)
                cur = jnp.sum(p, axis=0, keepdims=True)
                tot = cur if tot is None else tot + cur
            inv_l.append(1.0 / tot)
        return probs, inv_l

    def values(j_min, h, probs, inv_l):
        hs = slice(h * ATT_HEAD_DIM, (h + 1) * ATT_HEAD_DIM)
        zero = jnp.zeros((H, H), BF16)
        ot = None
        for j in range(j_min, 3):
            pt = jnp.concatenate(
                [jnp.concatenate([probs.get((j, kh, a), zero) for a in range(2)], axis=1)
                 for kh in range(2)], axis=0)
            cur = lax.dot_general(v_refs[j][:, hs], pt, tn, preferred_element_type=F32)
            ot = cur if ot is None else ot + cur
        ot = ot * jnp.concatenate(inv_l, axis=1)
        o_ref[:, hs] = ot.T.astype(o_ref.dtype)

    def all_heads(j_min):
        scored, weighted = {}, {}
        for t in range(ATT_HEADS + 2):
            if t < ATT_HEADS:
                scored[t] = scores(j_min, t)
            if 0 <= t - 1 < ATT_HEADS:
                weighted[t - 1] = weights(*scored.pop(t - 1))
            if 0 <= t - 2 < ATT_HEADS:
                values(j_min, t - 2, *weighted.pop(t - 2))

    i = pl.program_id(1)
    lax.cond(i >= 2, lambda: all_heads(0),
             lambda: lax.cond(i == 1, lambda: all_heads(1), lambda: all_heads(2)))


def _attention(qkv, m0, ta, tb):
    nb = SEQ // ATT_TQ
    row = lambda b, i: b * nb + i
    back = lambda d: (lambda b, i: (b * nb + jnp.maximum(i - d, 0)))
    spec = lambda rowfn, col: pl.BlockSpec((ATT_TQ, D_MODEL), lambda b, i: (rowfn(b, i), col))
    table = pl.BlockSpec((ATT_HEADS, ATT_HALF, ATT_HALF), lambda b, i: (0, 0, 0))
    return pl.pallas_call(
        _attn_kernel,
        grid=(BATCH, nb),
        in_specs=[
            spec(row, 0),
            spec(back(2), 1), spec(back(1), 1), spec(row, 1),
            spec(back(2), 2), spec(back(1), 2), spec(row, 2),
            pl.BlockSpec((ATT_HALF, ATT_HALF), lambda b, i: (0, 0)),
            table, table,
        ],
        out_specs=pl.BlockSpec((ATT_TQ, D_MODEL), lambda b, i: (row(b, i), 0)),
        out_shape=jax.ShapeDtypeStruct((TOKENS, D_MODEL), BF16),
        compiler_params=_cparams(("parallel", "parallel")),
        name="attention",
    )(qkv, qkv, qkv, qkv, qkv, qkv, qkv, m0, ta, tb)


def _attention_bias_tables(rel_bias):
    H = ATT_HALF
    assert H == REL_CLIP
    rb = rel_bias.astype(F32)
    rel = (rb - rb[:, 2 * REL_CLIP:]) * LOG2E

    def toeplitz(g):
        flat = jnp.tile(g, (1, H))[:, :H * (2 * H - 1)]
        return flat.reshape(-1, H, 2 * H - 1)[:, :, :H]

    ta = toeplitz(jnp.concatenate([jnp.zeros((ATT_HEADS, H), F32),
                                   rel[:, REL_CLIP:2 * REL_CLIP]], axis=1))
    tb = toeplitz(jnp.concatenate([rel[:, REL_CLIP:2 * REL_CLIP], rel[:, :REL_CLIP]], axis=1))
    kc = jnp.arange(H)[:, None] // ATT_CHUNK
    qc = jnp.arange(H)[None, :] // ATT_CHUNK
    tb = jnp.where((kc <= qc)[None], tb, NEG_BIG)
    m0 = jnp.where(kc >= qc, 0.0, NEG_BIG).astype(F32)
    return m0, ta, tb


def kernel(x, p, ffn_norm, ffn_w_gate, ffn_w_up, ffn_w_down, mix_norm, ab_w_in, gla_gate_w, gla_gate_b, gla_norm_g, conv_dw, conv_dw_b, conv_ln_g, conv_ln_b, ab_w_out, att_w_qkv, att_rel_bias, att_w_o, pl_norm, pl_w_gate, pl_w_proj, final_norm):
    xs = x.reshape(TOKENS, D_MODEL)
    ps = p.reshape(DEPTH, TOKENS, D_PL)
    row = lambda a: a.reshape(1, -1).astype(F32)

    def ffn(xs, i, s):
        return _ffn(xs, row(ffn_norm[i, s]), ffn_w_gate, ffn_w_up, ffn_w_down, i, s)

    for i in range(DEPTH):
        e = i // 2
        xs = ffn(xs, i, 0)
        if i % 2 == 0:
            w_in = ab_w_in[e]
            gz_lo = 2 * GLA_DK + 2 * GLA_DV
            gz_hi = gz_lo + GLA_GATE_RANK
            w_bf = w_in.astype(BF16)
            w_conv = w_bf[:, gz_hi:]
            w_gz = jnp.pad(w_bf[:, gz_lo:gz_hi], ((0, 0), (0, LANES - GLA_GATE_RANK)))
            z_all, gz = _rms_proj(
                xs, row(mix_norm[i]),
                [(w_bf, gz_lo // MM_TN), (w_conv, 2 * CONV_CH // MM_TN)],
                jnp.ones((1, AB_MAIN), F32), BF16, w_extra=w_gz)
            gate_w = jnp.pad(gla_gate_w[e], ((0, LANES - GLA_GATE_RANK), (0, 0))).astype(BF16)
            a_out = _gla(z_all, gz, gate_w, row(gla_gate_b[e]), row(gla_norm_g[e]))
            taps = jnp.broadcast_to(conv_dw[e].astype(F32)[:, None, :],
                                    (CONV_WIDTH, SUBLANES, CONV_CH))
            b_out = _conv(z_all, taps, row(conv_dw_b[e]), row(conv_ln_g[e]), row(conv_ln_b[e]))
            xs = _out_proj([a_out, b_out], ab_w_out[e], xs)
        else:
            colscale = jnp.concatenate([jnp.full((1, D_MODEL), ATT_HEAD_DIM ** -0.5 * LOG2E, F32),
                                        jnp.ones((1, 2 * D_MODEL), F32)], axis=1)
            qkv = _rms_proj(xs, row(mix_norm[i]), [(att_w_qkv[e], 3 * D_MODEL // MM_TN)],
                            colscale, BF16)
            m0, ta, tb = _attention_bias_tables(att_rel_bias[e])
            o = _attention(qkv, m0, ta, tb)
            xs = _out_proj([o], att_w_o[e], xs)
        xs = ffn(xs, i, 1)
        xs = _pl_embed(xs, ps, row(pl_norm[i]), pl_w_gate, pl_w_proj, row(final_norm),
                       i, i == DEPTH - 1)
    return xs.reshape(BATCH, SEQ, D_MODEL)
```

```python
import functools

import jax
import jax.numpy as jnp
from jax import lax
from jax.experimental import pallas as pl
from jax.experimental.pallas import tpu as pltpu

F32 = jnp.float32
BF16 = jnp.bfloat16

D_MODEL = 2048
BATCH = 4
SEQ = 2048
DEPTH = 2
TOKENS = BATCH * SEQ
D_PL = 256
D_FF = 5632
EPS = 1e-6

GLA_HEADS = 4
GLA_DK = 512
GLA_DV = 1024
GLA_HEAD_K = 128
GLA_HEAD_V = 256
GLA_GATE_RANK = 16
GLA_GATE_TAU = 16.0
GLA_CHUNK = 64
GLA_SUB = 16
GLA_SCALED_KEY_MAX = 2.0 ** 40
GLA_CHUNKS_PER_STEP = 4
CONV_CH = 1024
CONV_WIDTH = 31
AB_MAIN = 2 * GLA_DK + 2 * GLA_DV + 2 * CONV_CH
ATT_HEADS = 16
ATT_HEAD_DIM = 128
ATT_CHUNK = 64
LEFT_CHUNKS = 8
REL_CLIP = 128
NEG_BIG = -1e30
LOG2E = 1.4426950408889634

V7X_VMEM_BYTES = 64 * 1024 * 1024
LANES = 128
SUBLANES = 8
VMEM_LIMIT = V7X_VMEM_BYTES - 4 * 1024 * 1024

FFN_TM = 1024
FFN_TF = 512
FFN_FIRST_ROWS = FFN_TM
FFN_TN = 512
MM_TM = 1024
MM_TN = 1024
PL_TM = 256
OP_TM = 512
SEQ_TB = 1024
GLA_TB = 1024
CONV_HALO = 32
CONV_RC = 128
ATT_TQ = 256
RMS_ROWS = 256


def _cparams(sem):
    return pltpu.CompilerParams(dimension_semantics=sem, vmem_limit_bytes=VMEM_LIMIT)


def _rms_rows_to(dst_ref, x_ref, g_ref, rows):
    def body(c, carry):
        sl = pl.ds(pl.multiple_of(c * RMS_ROWS, RMS_ROWS), RMS_ROWS)
        x = x_ref[sl, :]
        ms = jnp.mean(x * x, axis=-1, keepdims=True)
        dst_ref[sl, :] = (x * lax.rsqrt(ms + EPS) * g_ref[...]).astype(dst_ref.dtype)
        return carry
    lax.fori_loop(0, rows // RMS_ROWS, body, 0)


def _ffn_step(f, load_residual, g_ref, wg_ref, wu_ref, wd_ref, o_ref, h_ref):
    @pl.when(f == 0)
    def _():
        load_residual()
        _rms_rows_to(h_ref, o_ref, g_ref, o_ref.shape[0])

    h = h_ref[...]
    gate = jnp.dot(h, wg_ref[...], preferred_element_type=F32)
    up = jnp.dot(h, wu_ref[...], preferred_element_type=F32)
    a = (0.5 * gate * jax.nn.sigmoid(gate) * up).astype(BF16)

    for n in range(D_MODEL // FFN_TN):
        cs = slice(n * FFN_TN, (n + 1) * FFN_TN)
        o_ref[:, cs] += jnp.dot(a, wd_ref[:, cs], preferred_element_type=F32)


def _ffn_first_kernel(x_hbm, g_ref, wg_ref, wu_ref, wd_ref, o_ref, wgb_ref, wub_ref, wdb_ref,
                      h_ref, sem):
    def load_residual():
        copy = pltpu.make_async_copy(x_hbm.at[pl.ds(0, FFN_FIRST_ROWS)], o_ref, sem)
        copy.start()
        copy.wait()

    wgb_ref[...] = wg_ref[...].astype(BF16)
    wub_ref[...] = wu_ref[...].astype(BF16)
    wdb_ref[...] = wd_ref[...].astype(BF16)
    _ffn_step(pl.program_id(1), load_residual, g_ref, wgb_ref, wub_ref, wdb_ref, o_ref, h_ref)


def _ffn_rest_kernel(n_f, n_copy, x_ref, g_ref, wg_ref, wu_ref, wd_ref, first_ref, o_ref,
                     h_ref, sem):
    s = pl.program_id(0)

    @pl.when(s < n_copy)
    def _():
        rows = pl.ds(pl.multiple_of(s * FFN_TM, FFN_TM), FFN_TM)
        copy = pltpu.make_async_copy(first_ref.at[rows], o_ref, sem)
        copy.start()
        copy.wait()

    def load_residual():
        o_ref[...] = x_ref[...]

    @pl.when(s >= n_copy)
    def _():
        _ffn_step((s - n_copy) % n_f, load_residual, g_ref, wg_ref, wu_ref, wd_ref, o_ref,
                  h_ref)


def _ffn(x, g, wg, wu, wd, layer, half):
    m = x.shape[0]
    tf = FFN_TF
    first, wgb, wub, wdb = pl.pallas_call(
        _ffn_first_kernel,
        grid=(1, D_FF // tf),
        in_specs=[
            pl.BlockSpec(memory_space=pl.ANY),
            pl.BlockSpec((1, D_MODEL), lambda i, f: (0, 0)),
            pl.BlockSpec((None, None, D_MODEL, tf), lambda i, f: (layer, half, 0, f)),
            pl.BlockSpec((None, None, D_MODEL, tf), lambda i, f: (layer, half, 0, f)),
            pl.BlockSpec((None, None, tf, D_MODEL), lambda i, f: (layer, half, f, 0)),
        ],
        out_specs=[
            pl.BlockSpec((FFN_FIRST_ROWS, D_MODEL), lambda i, f: (0, 0),
                         pipeline_mode=pl.Buffered(1)),
            pl.BlockSpec((D_MODEL, tf), lambda i, f: (0, f)),
            pl.BlockSpec((D_MODEL, tf), lambda i, f: (0, f)),
            pl.BlockSpec((tf, D_MODEL), lambda i, f: (f, 0)),
        ],
        out_shape=[
            jax.ShapeDtypeStruct((FFN_FIRST_ROWS, D_MODEL), F32),
            jax.ShapeDtypeStruct((D_MODEL, D_FF), BF16),
            jax.ShapeDtypeStruct((D_MODEL, D_FF), BF16),
            jax.ShapeDtypeStruct((D_FF, D_MODEL), BF16),
        ],
        scratch_shapes=[pltpu.VMEM((FFN_FIRST_ROWS, D_MODEL), BF16), pltpu.SemaphoreType.DMA(())],
        compiler_params=_cparams(("parallel", "arbitrary")),
        name="ffn_first",
    )(x, g, wg, wu, wd)
    n_f = D_FF // tf
    n_copy = FFN_FIRST_ROWS // FFN_TM
    tile = lambda s: jnp.where(s < n_copy, s, n_copy + (s - n_copy) // n_f)
    f_of = lambda s: jnp.where(s < n_copy, 0, (s - n_copy) % n_f)
    return pl.pallas_call(
        functools.partial(_ffn_rest_kernel, n_f, n_copy),
        grid=(n_copy + (m // FFN_TM - n_copy) * n_f,),
        in_specs=[
            pl.BlockSpec((FFN_TM, D_MODEL), lambda s: (jnp.maximum(tile(s), n_copy), 0)),
            pl.BlockSpec((1, D_MODEL), lambda s: (0, 0)),
            pl.BlockSpec((D_MODEL, tf), lambda s: (0, f_of(s))),
            pl.BlockSpec((D_MODEL, tf), lambda s: (0, f_of(s))),
            pl.BlockSpec((tf, D_MODEL), lambda s: (f_of(s), 0)),
            pl.BlockSpec(memory_space=pl.ANY),
        ],
        out_specs=pl.BlockSpec((FFN_TM, D_MODEL), lambda s: (tile(s), 0)),
        out_shape=jax.ShapeDtypeStruct((m, D_MODEL), F32),
        scratch_shapes=[pltpu.VMEM((FFN_TM, D_MODEL), BF16), pltpu.SemaphoreType.DMA(())],
        compiler_params=_cparams(("arbitrary",)),
        name="ffn_rest",
    )(x, g, wgb, wub, wdb, first)


def _rms_proj_kernel(seg_tiles, has_extra, *refs):
    n_seg = len(seg_tiles)
    x_ref, g_ref = refs[:2]
    w_refs = refs[2:2 + n_seg]
    cs_ref = refs[2 + n_seg]
    if has_extra:
        we_ref, o_ref, oe_ref, h_ref = refs[3 + n_seg:]
    else:
        o_ref, h_ref = refs[3 + n_seg:]
    j = pl.program_id(1)

    @pl.when(j == 0)
    def _():
        _rms_rows_to(h_ref, x_ref, g_ref, MM_TM)
        if has_extra:
            oe_ref[...] = jnp.dot(h_ref[...], we_ref[...].astype(BF16),
                                  preferred_element_type=F32)

    start = 0
    for w_ref, tiles in zip(w_refs, seg_tiles):
        @pl.when((j >= start) & (j < start + tiles))
        def _(w_ref=w_ref):
            acc = jnp.dot(h_ref[...], w_ref[...].astype(BF16), preferred_element_type=F32)
            o_ref[...] = (acc * cs_ref[...]).astype(o_ref.dtype)
        start += tiles


def _rms_proj(x, g, w_list, colscale, out_dtype, w_extra=None):
    m = x.shape[0]
    seg_tiles = tuple(t for _, t in w_list)
    n_tiles = sum(seg_tiles)
    has_extra = w_extra is not None
    in_specs = [
        pl.BlockSpec((MM_TM, D_MODEL), lambda i, j: (i, 0)),
        pl.BlockSpec((1, D_MODEL), lambda i, j: (0, 0)),
    ]
    start = 0
    for _, tiles in w_list:
        in_specs.append(pl.BlockSpec(
            (D_MODEL, MM_TN),
            lambda i, j, s=start, t=tiles: (0, jnp.clip(j - s, 0, t - 1))))
        start += tiles
    in_specs.append(pl.BlockSpec((1, MM_TN), lambda i, j: (0, j)))
    out_specs = pl.BlockSpec((MM_TM, MM_TN), lambda i, j: (i, j))
    out_shape = jax.ShapeDtypeStruct((m, n_tiles * MM_TN), out_dtype)
    args = [x, g] + [w for w, _ in w_list] + [colscale]
    if has_extra:
        in_specs.append(pl.BlockSpec((D_MODEL, LANES), lambda i, j: (0, 0)))
        out_specs = [out_specs, pl.BlockSpec((MM_TM, LANES), lambda i, j: (i, 0))]
        out_shape = [out_shape, jax.ShapeDtypeStruct((m, LANES), F32)]
        args.append(w_extra)
    return pl.pallas_call(
        functools.partial(_rms_proj_kernel, seg_tiles, has_extra),
        grid=(m // MM_TM, n_tiles),
        in_specs=in_specs,
        out_specs=out_specs,
        out_shape=out_shape,
        scratch_shapes=[pltpu.VMEM((MM_TM, D_MODEL), BF16)],
        compiler_params=_cparams(("parallel", "arbitrary")),
        name="rms_proj",
    )(*args)


def _out_proj_kernel(n_lhs, *refs):
    lhs_refs = refs[:n_lhs]
    w_refs = refs[n_lhs:2 * n_lhs]
    x_ref, o_ref = refs[2 * n_lhs], refs[2 * n_lhs + 1]
    wb_refs = refs[2 * n_lhs + 2:]

    @pl.when(pl.program_id(0) == 0)
    def _():
        for w_ref, wb_ref in zip(w_refs, wb_refs):
            wb_ref[...] = w_ref[...].astype(BF16)

    acc = x_ref[...]
    for a_ref, wb_ref in zip(lhs_refs, wb_refs):
        acc = acc + jnp.dot(a_ref[...], wb_ref[...], preferred_element_type=F32)
    o_ref[...] = acc


def _out_proj(lhs_list, w, x):
    m = x.shape[0]
    n_lhs = len(lhs_list)
    kw = lhs_list[0].shape[1]
    in_specs = [pl.BlockSpec((OP_TM, kw), lambda i: (i, 0)) for _ in lhs_list]
    in_specs += [pl.BlockSpec((kw, D_MODEL), lambda i, t=t: (t, 0), pipeline_mode=pl.Buffered(1))
                 for t in range(n_lhs)]
    in_specs += [pl.BlockSpec((OP_TM, D_MODEL), lambda i: (i, 0))]
    return pl.pallas_call(
        functools.partial(_out_proj_kernel, n_lhs),
        grid=(m // OP_TM,),
        in_specs=in_specs,
        out_specs=pl.BlockSpec((OP_TM, D_MODEL), lambda i: (i, 0)),
        out_shape=jax.ShapeDtypeStruct((m, D_MODEL), F32),
        scratch_shapes=[pltpu.VMEM((kw, D_MODEL), BF16) for _ in lhs_list],
        compiler_params=_cparams(("arbitrary",)),
        name="out_proj",
    )(*lhs_list, *([w] * n_lhs), x)


def _pl_embed_kernel(final, x_ref, p_ref, g_ref, wg_ref, wp_ref, fg_ref, o_ref,
                     h_ref, wgb_ref, wpb_ref):
    @pl.when(pl.program_id(0) == 0)
    def _():
        wgb_ref[...] = wg_ref[...].astype(BF16)
        wpb_ref[...] = wp_ref[...].astype(BF16)

    _rms_rows_to(h_ref, x_ref, g_ref, PL_TM)
    gate = jax.nn.sigmoid(jnp.dot(h_ref[...], wgb_ref[...], preferred_element_type=F32))
    proj = jnp.dot(p_ref[...].astype(BF16), wpb_ref[...], preferred_element_type=F32)
    y = x_ref[...] + gate * proj
    if final:
        ms = jnp.mean(y * y, axis=-1, keepdims=True)
        y = y * lax.rsqrt(ms + EPS) * fg_ref[...]
    o_ref[...] = y


def _pl_embed(x, p, g, wg, wp, final_g, layer, final):
    m = x.shape[0]
    return pl.pallas_call(
        functools.partial(_pl_embed_kernel, final),
        grid=(m // PL_TM,),
        in_specs=[
            pl.BlockSpec((PL_TM, D_MODEL), lambda i: (i, 0)),
            pl.BlockSpec((None, PL_TM, D_PL), lambda i: (layer, i, 0)),
            pl.BlockSpec((1, D_MODEL), lambda i: (0, 0)),
            pl.BlockSpec((None, D_MODEL, D_MODEL), lambda i: (layer, 0, 0),
                         pipeline_mode=pl.Buffered(1)),
            pl.BlockSpec((None, D_PL, D_MODEL), lambda i: (layer, 0, 0),
                         pipeline_mode=pl.Buffered(1)),
            pl.BlockSpec((1, D_MODEL), lambda i: (0, 0)),
        ],
        out_specs=pl.BlockSpec((PL_TM, D_MODEL), lambda i: (i, 0)),
        out_shape=jax.ShapeDtypeStruct((m, D_MODEL), F32),
        scratch_shapes=[pltpu.VMEM((PL_TM, D_MODEL), BF16),
                        pltpu.VMEM((D_MODEL, D_MODEL), BF16),
                        pltpu.VMEM((D_PL, D_MODEL), BF16)],
        compiler_params=_cparams(("arbitrary",)),
        name="pl_embed",
    )(x, p, g, wg, wp, final_g)


def _gla_kernel(q_ref, k_ref, v_ref, r_ref, gz_ref, gw_ref, gb_ref, ng_ref, o_ref,
                b_ref, kh_ref, s_ref):
    C, SB = GLA_CHUNK, GLA_SUB
    n_chunks = GLA_TB // C

    @pl.when(pl.program_id(1) == 0)
    def _():
        s_ref[...] = jnp.zeros_like(s_ref)

    lin = jnp.dot(gz_ref[...].astype(BF16), gw_ref[...], preferred_element_type=F32) + gb_ref[...]
    log_a = -(jnp.maximum(-lin, 0.0) + jnp.log1p(jnp.exp(-jnp.abs(lin)))) * (LOG2E / GLA_GATE_TAU)
    tri = (lax.broadcasted_iota(jnp.int32, (C, C), 0)
           >= lax.broadcasted_iota(jnp.int32, (C, C), 1)).astype(F32)
    for c in range(n_chunks):
        b_ref[c * C:(c + 1) * C, :] = jnp.dot(
            tri, log_a[c * C:(c + 1) * C, :], preferred_element_type=F32,
            precision=lax.Precision.HIGHEST)
    kh_ref[...] = k_ref[...].astype(F32) * jnp.exp2(-b_ref[...])
    scaled_ok = jnp.max(jnp.abs(kh_ref[...])) <= GLA_SCALED_KEY_MAX

    lane = lax.broadcasted_iota(jnp.int32, (SB, C), 1)
    trow = lax.broadcasted_iota(jnp.int32, (SB, C), 0)
    causal = (lax.broadcasted_iota(jnp.int32, (C, C), 0)
              >= lax.broadcasted_iota(jnp.int32, (C, C), 1))
    nt = (((1,), (1,)), ((), ()))
    tn = (((0,), (0,)), ((), ()))

    def guarded_products(q, k, b):
        blocks = []
        for i in range(C // SB):
            s0 = i * SB
            q_i = q[s0:s0 + SB]
            b_i = b[s0:s0 + SB]
            acc = jnp.zeros((SB, C), F32)
            if i > 0:
                p_i = b[s0 - 1:s0, :]
                qt = (q_i * jnp.exp2(b_i - p_i)).astype(BF16)
                kt = (k * jnp.exp2(p_i - b)).astype(BF16)
                off = lax.dot_general(qt, kt, nt, preferred_element_type=F32)
                acc = jnp.where(lane < s0, off, 0.0)
            for s in range(SB):
                b_s = b[s0 + s:s0 + s + 1, :]
                k_s = k[s0 + s:s0 + s + 1, :]
                e = jnp.exp2(b_i - b_s)
                col = jnp.sum(q_i * (k_s * e), axis=1, keepdims=True)
                acc = jnp.where((lane == s0 + s) & (trow >= s), col, acc)
            blocks.append(acc)
        return jnp.concatenate(blocks, axis=0)

    def chunk_body(scaled, per_step, it, carry):
        units = []
        for u in range(per_step):
            rows = pl.ds(pl.multiple_of((it * per_step + u) * C, C), C)
            for h in range(GLA_HEADS):
                units.append((u, h, rows))
        hk = lambda h: slice(h * GLA_HEAD_K, (h + 1) * GLA_HEAD_K)
        hv = lambda h: slice(h * GLA_HEAD_V, (h + 1) * GLA_HEAD_V)

        part = {}
        for u, h, rows in units:
            q = q_ref[rows, hk(h)].astype(F32) * (GLA_HEAD_K ** -0.5)
            v = v_ref[rows, hv(h)].astype(BF16)
            b = b_ref[rows, hk(h)]
            b_last = b[C - 1:C, :]
            qb = (q * jnp.exp2(b)).astype(BF16)
            if scaled:
                kh = kh_ref[rows, hk(h)]
                a_mat = jnp.where(causal, lax.dot_general(qb, kh.astype(BF16), nt,
                                                          preferred_element_type=F32), 0.0)
                k_dec = (kh * jnp.exp2(b_last)).astype(BF16)
            else:
                k = k_ref[rows, hk(h)].astype(F32)
                a_mat = guarded_products(q, k, b)
                k_dec = (k * jnp.exp2(b_last - b)).astype(BF16)
            upd = lax.dot_general(v, k_dec, tn, preferred_element_type=F32)
            local = jnp.dot(a_mat.astype(BF16), v, preferred_element_type=F32)
            part[(u, h)] = (qb, jnp.exp2(b_last), upd, local)

        state = {}
        for h in range(GLA_HEADS):
            st = s_ref[h]
            for u in range(per_step):
                state[(u, h)] = st
                _, decay, upd, _ = part[(u, h)]
                st = st * decay + upd
            s_ref[h] = st

        for u, h, rows in units:
            qb, _, _, local = part[(u, h)]
            o = local + lax.dot_general(qb, state[(u, h)].astype(BF16), nt,
                                        preferred_element_type=F32)
            on = o * lax.rsqrt(jnp.mean(o * o, axis=-1, keepdims=True) + EPS) * ng_ref[...]
            r = r_ref[rows, hv(h)].astype(F32)
            o_ref[rows, hv(h)] = (on * (r * jax.nn.sigmoid(r))).astype(o_ref.dtype)
        return carry

    def run(scaled):
        per_step = GLA_CHUNKS_PER_STEP if scaled else 1
        lax.fori_loop(0, n_chunks // per_step,
                      functools.partial(chunk_body, scaled, per_step), 0)

    lax.cond(scaled_ok, lambda: run(True), lambda: run(False))


def _gla(z, gz, gate_w, gate_b, norm_g):
    nb = SEQ // GLA_TB
    row = lambda b, i: b * nb + i
    return pl.pallas_call(
        _gla_kernel,
        grid=(BATCH, nb),
        in_specs=[
            pl.BlockSpec((GLA_TB, GLA_DK), lambda b, i: (row(b, i), 0)),
            pl.BlockSpec((GLA_TB, GLA_DK), lambda b, i: (row(b, i), 1)),
            pl.BlockSpec((GLA_TB, GLA_DV), lambda b, i: (row(b, i), 1)),
            pl.BlockSpec((GLA_TB, GLA_DV), lambda b, i: (row(b, i), 2)),
            pl.BlockSpec((GLA_TB, LANES), lambda b, i: (row(b, i), 0)),
            pl.BlockSpec((LANES, GLA_DK), lambda b, i: (0, 0)),
            pl.BlockSpec((1, GLA_DK), lambda b, i: (0, 0)),
            pl.BlockSpec((1, GLA_HEAD_V), lambda b, i: (0, 0)),
        ],
        out_specs=pl.BlockSpec((GLA_TB, GLA_DV), lambda b, i: (row(b, i), 0)),
        out_shape=jax.ShapeDtypeStruct((TOKENS, GLA_DV), BF16),
        scratch_shapes=[pltpu.VMEM((GLA_TB, GLA_DK), F32),
                        pltpu.VMEM((GLA_TB, GLA_DK), F32),
                        pltpu.VMEM((GLA_HEADS, GLA_HEAD_V, GLA_HEAD_K), F32)],
        compiler_params=_cparams(("parallel", "arbitrary")),
        name="gla",
    )(z, z, z, z, gz, gate_w, gate_b, norm_g)


def _conv_kernel(ca_ref, cb_ref, ha_ref, hb_ref, w_ref, wb_ref, lg_ref, lb_ref, o_ref,
                 sh_ref, y_ref):
    H = CONV_HALO
    n_sh = SUBLANES
    keep = (pl.program_id(1) > 0).astype(F32)

    u_main = ca_ref[...].astype(F32) * jax.nn.sigmoid(cb_ref[...].astype(F32))
    u_halo = ha_ref[...].astype(F32) * jax.nn.sigmoid(hb_ref[...].astype(F32)) * keep
    for r in range(n_sh):
        sh_ref[r, 0:H - r, :] = u_halo[r:H, :]
        sh_ref[r, H - r:H - r + SEQ_TB, :] = u_main

    first = H - (CONV_WIDTH - 1)
    ct_w = 128

    groups = CONV_RC // SUBLANES

    def row_body(cs, bias, rc, carry):
        t0 = pl.multiple_of(rc * CONV_RC, CONV_RC)
        accs = [bias, None]
        for r in range(n_sh):
            offs = [o for o in range(first, first + CONV_WIDTH) if o % n_sh == r]
            lo, hi = offs[0] - r, offs[-1] - r
            slab = sh_ref[r, pl.ds(t0 + lo, CONV_RC + hi - lo), cs]
            for off in offs:
                a = off - r - lo
                win = slab[a:a + CONV_RC].reshape(groups, SUBLANES, ct_w)
                term = w_ref[off - first, :, cs][None] * win
                accs[r % 2] = term if accs[r % 2] is None else accs[r % 2] + term
        y_ref[pl.ds(t0, CONV_RC), cs] = (accs[0] + accs[1]).reshape(CONV_RC, ct_w)
        return carry

    for ct in range(CONV_CH // ct_w):
        cs = slice(ct * ct_w, (ct + 1) * ct_w)
        bias = jnp.broadcast_to(wb_ref[:, cs][None], (groups, SUBLANES, ct_w))
        lax.fori_loop(0, SEQ_TB // CONV_RC, functools.partial(row_body, cs, bias), 0)

    def ln_body(rc, carry):
        sl = pl.ds(pl.multiple_of(rc * 64, 64), 64)
        y = y_ref[sl, :]
        mu = jnp.mean(y, axis=-1, keepdims=True)
        var = jnp.mean(jnp.square(y - mu), axis=-1, keepdims=True)
        t = (y - mu) * lax.rsqrt(var + EPS) * lg_ref[...] + lb_ref[...]
        o_ref[sl, :] = (t * jax.nn.sigmoid(t)).astype(o_ref.dtype)
        return carry

    lax.fori_loop(0, SEQ_TB // 64, ln_body, 0, unroll=2)


def _conv(z, w, wb, ln_g, ln_b):
    nb = SEQ // SEQ_TB
    hb = SEQ_TB // CONV_HALO
    ca_col = (2 * GLA_DK + 2 * GLA_DV) // CONV_CH
    row = lambda b, i: b * nb + i
    halo = lambda b, i: jnp.maximum(row(b, i) * hb - 1, 0)
    return pl.pallas_call(
        _conv_kernel,
        grid=(BATCH, nb),
        in_specs=[
            pl.BlockSpec((SEQ_TB, CONV_CH), lambda b, i: (row(b, i), ca_col)),
            pl.BlockSpec((SEQ_TB, CONV_CH), lambda b, i: (row(b, i), ca_col + 1)),
            pl.BlockSpec((CONV_HALO, CONV_CH), lambda b, i: (halo(b, i), ca_col)),
            pl.BlockSpec((CONV_HALO, CONV_CH), lambda b, i: (halo(b, i), ca_col + 1)),
            pl.BlockSpec((CONV_WIDTH, SUBLANES, CONV_CH), lambda b, i: (0, 0, 0)),
            pl.BlockSpec((1, CONV_CH), lambda b, i: (0, 0)),
            pl.BlockSpec((1, CONV_CH), lambda b, i: (0, 0)),
            pl.BlockSpec((1, CONV_CH), lambda b, i: (0, 0)),
        ],
        out_specs=pl.BlockSpec((SEQ_TB, CONV_CH), lambda b, i: (row(b, i), 0)),
        out_shape=jax.ShapeDtypeStruct((TOKENS, CONV_CH), BF16),
        scratch_shapes=[pltpu.VMEM((SUBLANES, SEQ_TB + CONV_HALO, CONV_CH), F32),
                        pltpu.VMEM((SEQ_TB, CONV_CH), F32)],
        compiler_params=_cparams(("parallel", "parallel")),
        name="conv",
    )(z, z, z, z, w, wb, ln_g, ln_b)


ATT_HALF = ATT_TQ // 2


def _attn_kernel(q_ref, k0_ref, k1_ref, k2_ref, v0_ref, v1_ref, v2_ref,
                 m0_ref, ta_ref, tb_ref, o_ref):
    k_refs = (k0_ref, k1_ref, k2_ref)
    v_refs = (v0_ref, v1_ref, v2_ref)
    nt = (((1,), (1,)), ((), ()))
    tn = (((0,), (0,)), ((), ()))
    H = ATT_HALF

    def scores(j_min, h):
        hs = slice(h * ATT_HEAD_DIM, (h + 1) * ATT_HEAD_DIM)
        q = q_ref[:, hs]
        tiles = {}
        for j in range(j_min, 3):
            st = lax.dot_general(k_refs[j][:, hs], q, nt, preferred_element_type=F32)
            for kh in range(2):
                for a in range(2):
                    n = 2 * j + kh - a
                    if n < 0 or n > 4:
                        continue
                    t = st[kh * H:(kh + 1) * H, a * H:(a + 1) * H]
                    if n == 0:
                        t = t + m0_ref[...]
                    elif n == 3:
                        t = t + ta_ref[h]
                    elif n == 4:
                        t = t + tb_ref[h]
                    tiles[(j, kh, a)] = t
        maxima = []
        for a in range(2):
            mx = None
            for key in tiles:
                if key[2] == a:
                    cur = jnp.max(tiles[key], axis=0, keepdims=True)
                    mx = cur if mx is None else jnp.maximum(mx, cur)
            maxima.append(mx)
        return tiles, maxima

    def weights(tiles, maxima):
        probs = {}
        inv_l = []
        for a in range(2):
            mine = [key for key in tiles if key[2] == a]
            tot = None
            for key in mine:
                p = jnp.exp2(tiles[key] - maxima[a])
                probs[key] = p.astype(BF16)
                cur = jnp.sum(p, axis=0, keepdims=True)
                tot = cur if tot is None else tot + cur
            inv_l.append(1.0 / tot)
        return probs, inv_l

    def values(j_min, h, probs, inv_l):
        hs = slice(h * ATT_HEAD_DIM, (h + 1) * ATT_HEAD_DIM)
        zero = jnp.zeros((H, H), BF16)
        ot = None
        for j in range(j_min, 3):
            pt = jnp.concatenate(
                [jnp.concatenate([probs.get((j, kh, a), zero) for a in range(2)], axis=1)
                 for kh in range(2)], axis=0)
            cur = lax.dot_general(v_refs[j][:, hs], pt, tn, preferred_element_type=F32)
            ot = cur if ot is None else ot + cur
        ot = ot * jnp.concatenate(inv_l, axis=1)
        o_ref[:, hs] = ot.T.astype(o_ref.dtype)

    def all_heads(j_min):
        scored, weighted = {}, {}
        for t in range(ATT_HEADS + 2):
            if t < ATT_HEADS:
                scored[t] = scores(j_min, t)
            if 0 <= t - 1 < ATT_HEADS:
                weighted[t - 1] = weights(*scored.pop(t - 1))
            if 0 <= t - 2 < ATT_HEADS:
                values(j_min, t - 2, *weighted.pop(t - 2))

    i = pl.program_id(1)
    lax.cond(i >= 2, lambda: all_heads(0),
             lambda: lax.cond(i == 1, lambda: all_heads(1), lambda: all_heads(2)))


def _attention(qkv, m0, ta, tb):
    nb = SEQ // ATT_TQ
    row = lambda b, i: b * nb + i
    back = lambda d: (lambda b, i: (b * nb + jnp.maximum(i - d, 0)))
    spec = lambda rowfn, col: pl.BlockSpec((ATT_TQ, D_MODEL), lambda b, i: (rowfn(b, i), col))
    table = pl.BlockSpec((ATT_HEADS, ATT_HALF, ATT_HALF), lambda b, i: (0, 0, 0))
    return pl.pallas_call(
        _attn_kernel,
        grid=(BATCH, nb),
        in_specs=[
            spec(row, 0),
            spec(back(2), 1), spec(back(1), 1), spec(row, 1),
            spec(back(2), 2), spec(back(1), 2), spec(row, 2),
            pl.BlockSpec((ATT_HALF, ATT_HALF), lambda b, i: (0, 0)),
            table, table,
        ],
        out_specs=pl.BlockSpec((ATT_TQ, D_MODEL), lambda b, i: (row(b, i), 0)),
        out_shape=jax.ShapeDtypeStruct((TOKENS, D_MODEL), BF16),
        compiler_params=_cparams(("parallel", "parallel")),
        name="attention",
    )(qkv, qkv, qkv, qkv, qkv, qkv, qkv, m0, ta, tb)


def _attention_bias_tables(rel_bias):
    H = ATT_HALF
    assert H == REL_CLIP
    rb = rel_bias.astype(F32)
    rel = (rb - rb[:, 2 * REL_CLIP:]) * LOG2E

    def toeplitz(g):
        flat = jnp.tile(g, (1, H))[:, :H * (2 * H - 1)]
        return flat.reshape(-1, H, 2 * H - 1)[:, :, :H]

    ta = toeplitz(jnp.concatenate([jnp.zeros((ATT_HEADS, H), F32),
                                   rel[:, REL_CLIP:2 * REL_CLIP]], axis=1))
    tb = toeplitz(jnp.concatenate([rel[:, REL_CLIP:2 * REL_CLIP], rel[:, :REL_CLIP]], axis=1))
    kc = jnp.arange(H)[:, None] // ATT_CHUNK
    qc = jnp.arange(H)[None, :] // ATT_CHUNK
    tb = jnp.where((kc <= qc)[None], tb, NEG_BIG)
    m0 = jnp.where(kc >= qc, 0.0, NEG_BIG).astype(F32)
    return m0, ta, tb


def kernel(x, p, ffn_norm, ffn_w_gate, ffn_w_up, ffn_w_down, mix_norm, ab_w_in, gla_gate_w, gla_gate_b, gla_norm_g, conv_dw, conv_dw_b, conv_ln_g, conv_ln_b, ab_w_out, att_w_qkv, att_rel_bias, att_w_o, pl_norm, pl_w_gate, pl_w_proj, final_norm):
    xs = x.reshape(TOKENS, D_MODEL)
    ps = p.reshape(DEPTH, TOKENS, D_PL)
    row = lambda a: a.reshape(1, -1).astype(F32)

    def ffn(xs, i, s):
        return _ffn(xs, row(ffn_norm[i, s]), ffn_w_gate, ffn_w_up, ffn_w_down, i, s)

    for i in range(DEPTH):
        e = i // 2
        xs = ffn(xs, i, 0)
        if i % 2 == 0:
            w_in = ab_w_in[e]
            gz_lo = 2 * GLA_DK + 2 * GLA_DV
            gz_hi = gz_lo + GLA_GATE_RANK
            w_bf = w_in.astype(BF16)
            w_conv = w_bf[:, gz_hi:]
            w_gz = jnp.pad(w_bf[:, gz_lo:gz_hi], ((0, 0), (0, LANES - GLA_GATE_RANK)))
            z_all, gz = _rms_proj(
                xs, row(mix_norm[i]),
                [(w_bf, gz_lo // MM_TN), (w_conv, 2 * CONV_CH // MM_TN)],
                jnp.ones((1, AB_MAIN), F32), BF16, w_extra=w_gz)
            gate_w = jnp.pad(gla_gate_w[e], ((0, LANES - GLA_GATE_RANK), (0, 0))).astype(BF16)
            a_out = _gla(z_all, gz, gate_w, row(gla_gate_b[e]), row(gla_norm_g[e]))
            taps = jnp.broadcast_to(conv_dw[e].astype(F32)[:, None, :],
                                    (CONV_WIDTH, SUBLANES, CONV_CH))
            b_out = _conv(z_all, taps, row(conv_dw_b[e]), row(conv_ln_g[e]), row(conv_ln_b[e]))
            xs = _out_proj([a_out, b_out], ab_w_out[e], xs)
        else:
            colscale = jnp.concatenate([jnp.full((1, D_MODEL), ATT_HEAD_DIM ** -0.5 * LOG2E, F32),
                                        jnp.ones((1, 2 * D_MODEL), F32)], axis=1)
            qkv = _rms_proj(xs, row(mix_norm[i]), [(att_w_qkv[e], 3 * D_MODEL // MM_TN)],
                            colscale, BF16)
            m0, ta, tb = _attention_bias_tables(att_rel_bias[e])
            o = _attention(qkv, m0, ta, tb)
            xs = _out_proj([o], att_w_o[e], xs)
        xs = ffn(xs, i, 1)
        xs = _pl_embed(xs, ps, row(pl_norm[i]), pl_w_gate, pl_w_proj, row(final_norm),
                       i, i == DEPTH - 1)
    return xs.reshape(BATCH, SEQ, D_MODEL)
```

```python
import functools

import jax
import jax.numpy as jnp
from jax import lax
from jax.experimental import pallas as pl
from jax.experimental.pallas import tpu as pltpu

F32 = jnp.float32
BF16 = jnp.bfloat16

D_MODEL = 2048
BATCH = 4
SEQ = 2048
DEPTH = 2
TOKENS = BATCH * SEQ
D_PL = 256
D_FF = 5632
EPS = 1e-6

GLA_HEADS = 4
GLA_DK = 512
GLA_DV = 1024
GLA_HEAD_K = 128
GLA_HEAD_V = 256
GLA_GATE_RANK = 16
GLA_GATE_TAU = 16.0
GLA_CHUNK = 64
GLA_SUB = 16
GLA_SCALED_KEY_MAX = 2.0 ** 40
GLA_CHUNKS_PER_STEP = 4
CONV_CH = 1024
CONV_WIDTH = 31
AB_MAIN = 2 * GLA_DK + 2 * GLA_DV + 2 * CONV_CH
ATT_HEADS = 16
ATT_HEAD_DIM = 128
ATT_CHUNK = 64
LEFT_CHUNKS = 8
REL_CLIP = 128
NEG_BIG = -1e30
LOG2E = 1.4426950408889634

V7X_VMEM_BYTES = 64 * 1024 * 1024
LANES = 128
SUBLANES = 8
VMEM_LIMIT = V7X_VMEM_BYTES - 4 * 1024 * 1024

FFN_TM = 1024
FFN_TF = 512
FFN_FIRST_ROWS = FFN_TM
FFN_TN = 512
MM_TM = 1024
MM_TN = 1024
PL_TM = 256
OP_TM = 512
SEQ_TB = 512
CONV_HALO = 32
CONV_RC = 128
ATT_TQ = 256
RMS_ROWS = 256


def _cparams(sem):
    return pltpu.CompilerParams(dimension_semantics=sem, vmem_limit_bytes=VMEM_LIMIT)


def _rms_rows_to(dst_ref, x_ref, g_ref, rows):
    def body(c, carry):
        sl = pl.ds(pl.multiple_of(c * RMS_ROWS, RMS_ROWS), RMS_ROWS)
        x = x_ref[sl, :]
        ms = jnp.mean(x * x, axis=-1, keepdims=True)
        dst_ref[sl, :] = (x * lax.rsqrt(ms + EPS) * g_ref[...]).astype(dst_ref.dtype)
        return carry
    lax.fori_loop(0, rows // RMS_ROWS, body, 0)


def _ffn_step(f, load_residual, g_ref, wg_ref, wu_ref, wd_ref, o_ref, h_ref):
    @pl.when(f == 0)
    def _():
        load_residual()
        _rms_rows_to(h_ref, o_ref, g_ref, o_ref.shape[0])

    h = h_ref[...]
    gate = jnp.dot(h, wg_ref[...], preferred_element_type=F32)
    up = jnp.dot(h, wu_ref[...], preferred_element_type=F32)
    a = (0.5 * gate * jax.nn.sigmoid(gate) * up).astype(BF16)

    for n in range(D_MODEL // FFN_TN):
        cs = slice(n * FFN_TN, (n + 1) * FFN_TN)
        o_ref[:, cs] += jnp.dot(a, wd_ref[:, cs], preferred_element_type=F32)


def _ffn_first_kernel(x_hbm, g_ref, wg_ref, wu_ref, wd_ref, o_ref, wgb_ref, wub_ref, wdb_ref,
                      h_ref, sem):
    def load_residual():
        copy = pltpu.make_async_copy(x_hbm.at[pl.ds(0, FFN_FIRST_ROWS)], o_ref, sem)
        copy.start()
        copy.wait()

    wgb_ref[...] = wg_ref[...].astype(BF16)
    wub_ref[...] = wu_ref[...].astype(BF16)
    wdb_ref[...] = wd_ref[...].astype(BF16)
    _ffn_step(pl.program_id(1), load_residual, g_ref, wgb_ref, wub_ref, wdb_ref, o_ref, h_ref)


def _ffn_rest_kernel(n_f, n_copy, x_ref, g_ref, wg_ref, wu_ref, wd_ref, first_ref, o_ref,
                     h_ref, sem):
    s = pl.program_id(0)

    @pl.when(s < n_copy)
    def _():
        rows = pl.ds(pl.multiple_of(s * FFN_TM, FFN_TM), FFN_TM)
        copy = pltpu.make_async_copy(first_ref.at[rows], o_ref, sem)
        copy.start()
        copy.wait()

    def load_residual():
        o_ref[...] = x_ref[...]

    @pl.when(s >= n_copy)
    def _():
        _ffn_step((s - n_copy) % n_f, load_residual, g_ref, wg_ref, wu_ref, wd_ref, o_ref,
                  h_ref)


def _ffn(x, g, wg, wu, wd, layer, half):
    m = x.shape[0]
    tf = FFN_TF
    first, wgb, wub, wdb = pl.pallas_call(
        _ffn_first_kernel,
        grid=(1, D_FF // tf),
        in_specs=[
            pl.BlockSpec(memory_space=pl.ANY),
            pl.BlockSpec((1, D_MODEL), lambda i, f: (0, 0)),
            pl.BlockSpec((None, None, D_MODEL, tf), lambda i, f: (layer, half, 0, f)),
            pl.BlockSpec((None, None, D_MODEL, tf), lambda i, f: (layer, half, 0, f)),
            pl.BlockSpec((None, None, tf, D_MODEL), lambda i, f: (layer, half, f, 0)),
        ],
        out_specs=[
            pl.BlockSpec((FFN_FIRST_ROWS, D_MODEL), lambda i, f: (0, 0),
                         pipeline_mode=pl.Buffered(1)),
            pl.BlockSpec((D_MODEL, tf), lambda i, f: (0, f)),
            pl.BlockSpec((D_MODEL, tf), lambda i, f: (0, f)),
            pl.BlockSpec((tf, D_MODEL), lambda i, f: (f, 0)),
        ],
        out_shape=[
            jax.ShapeDtypeStruct((FFN_FIRST_ROWS, D_MODEL), F32),
            jax.ShapeDtypeStruct((D_MODEL, D_FF), BF16),
            jax.ShapeDtypeStruct((D_MODEL, D_FF), BF16),
            jax.ShapeDtypeStruct((D_FF, D_MODEL), BF16),
        ],
        scratch_shapes=[pltpu.VMEM((FFN_FIRST_ROWS, D_MODEL), BF16), pltpu.SemaphoreType.DMA(())],
        compiler_params=_cparams(("parallel", "arbitrary")),
        name="ffn_first",
    )(x, g, wg, wu, wd)
    n_f = D_FF // tf
    n_copy = FFN_FIRST_ROWS // FFN_TM
    tile = lambda s: jnp.where(s < n_copy, s, n_copy + (s - n_copy) // n_f)
    f_of = lambda s: jnp.where(s < n_copy, 0, (s - n_copy) % n_f)
    return pl.pallas_call(
        functools.partial(_ffn_rest_kernel, n_f, n_copy),
        grid=(n_copy + (m // FFN_TM - n_copy) * n_f,),
        in_specs=[
            pl.BlockSpec((FFN_TM, D_MODEL), lambda s: (jnp.maximum(tile(s), n_copy), 0)),
            pl.BlockSpec((1, D_MODEL), lambda s: (0, 0)),
            pl.BlockSpec((D_MODEL, tf), lambda s: (0, f_of(s))),
            pl.BlockSpec((D_MODEL, tf), lambda s: (0, f_of(s))),
            pl.BlockSpec((tf, D_MODEL), lambda s: (f_of(s), 0)),
            pl.BlockSpec(memory_space=pl.ANY),
        ],
        out_specs=pl.BlockSpec((FFN_TM, D_MODEL), lambda s: (tile(s), 0)),
        out_shape=jax.ShapeDtypeStruct((m, D_MODEL), F32),
        scratch_shapes=[pltpu.VMEM((FFN_TM, D_MODEL), BF16), pltpu.SemaphoreType.DMA(())],
        compiler_params=_cparams(("arbitrary",)),
        name="ffn_rest",
    )(x, g, wgb, wub, wdb, first)


def _rms_proj_kernel(seg_tiles, lead_scale, has_extra, *refs):
    n_seg = len(seg_tiles)
    x_ref, g_ref = refs[:2]
    w_refs = refs[2:2 + n_seg]
    if has_extra:
        we_ref, o_ref, oe_ref, h_ref = refs[2 + n_seg:]
    else:
        o_ref, h_ref = refs[2 + n_seg:]
    j = pl.program_id(1)
    scaled_tiles, scale_value = lead_scale

    @pl.when(j == 0)
    def _():
        _rms_rows_to(h_ref, x_ref, g_ref, MM_TM)
        if has_extra:
            oe_ref[...] = jnp.dot(h_ref[...], we_ref[...].astype(BF16),
                                  preferred_element_type=F32)

    start = 0
    for w_ref, tiles in zip(w_refs, seg_tiles):
        @pl.when((j >= start) & (j < start + tiles))
        def _(w_ref=w_ref):
            acc = jnp.dot(h_ref[...], w_ref[...].astype(BF16), preferred_element_type=F32)
            if scaled_tiles:
                acc = acc * jnp.where(j < scaled_tiles, scale_value, 1.0)
            o_ref[...] = acc.astype(o_ref.dtype)
        start += tiles


def _rms_proj(x, g, w_list, lead_scale, out_dtype, w_extra=None):
    m = x.shape[0]
    seg_tiles = tuple(t for _, t in w_list)
    n_tiles = sum(seg_tiles)
    has_extra = w_extra is not None
    in_specs = [
        pl.BlockSpec((MM_TM, D_MODEL), lambda i, j: (i, 0)),
        pl.BlockSpec((1, D_MODEL), lambda i, j: (0, 0)),
    ]
    start = 0
    for _, tiles in w_list:
        in_specs.append(pl.BlockSpec(
            (D_MODEL, MM_TN),
            lambda i, j, s=start, t=tiles: (0, jnp.clip(j - s, 0, t - 1))))
        start += tiles
    out_specs = pl.BlockSpec((MM_TM, MM_TN), lambda i, j: (i, j))
    out_shape = jax.ShapeDtypeStruct((m, n_tiles * MM_TN), out_dtype)
    args = [x, g] + [w for w, _ in w_list]
    if has_extra:
        in_specs.append(pl.BlockSpec((D_MODEL, LANES), lambda i, j: (0, 0)))
        out_specs = [out_specs, pl.BlockSpec((MM_TM, LANES), lambda i, j: (i, 0))]
        out_shape = [out_shape, jax.ShapeDtypeStruct((m, LANES), F32)]
        args.append(w_extra)
    return pl.pallas_call(
        functools.partial(_rms_proj_kernel, seg_tiles, lead_scale, has_extra),
        grid=(m // MM_TM, n_tiles),
        in_specs=in_specs,
        out_specs=out_specs,
        out_shape=out_shape,
        scratch_shapes=[pltpu.VMEM((MM_TM, D_MODEL), BF16)],
        compiler_params=_cparams(("parallel", "arbitrary")),
        name="rms_proj",
    )(*args)


def _out_proj_kernel(n_lhs, *refs):
    lhs_refs = refs[:n_lhs]
    w_refs = refs[n_lhs:2 * n_lhs]
    x_ref, o_ref = refs[2 * n_lhs], refs[2 * n_lhs + 1]
    wb_refs = refs[2 * n_lhs + 2:]

    @pl.when(pl.program_id(0) == 0)
    def _():
        for w_ref, wb_ref in zip(w_refs, wb_refs):
            wb_ref[...] = w_ref[...].astype(BF16)

    acc = x_ref[...]
    for a_ref, wb_ref in zip(lhs_refs, wb_refs):
        acc = acc + jnp.dot(a_ref[...], wb_ref[...], preferred_element_type=F32)
    o_ref[...] = acc


def _out_proj(lhs_list, w, x):
    m = x.shape[0]
    n_lhs = len(lhs_list)
    kw = lhs_list[0].shape[1]
    in_specs = [pl.BlockSpec((OP_TM, kw), lambda i: (i, 0)) for _ in lhs_list]
    in_specs += [pl.BlockSpec((kw, D_MODEL), lambda i, t=t: (t, 0), pipeline_mode=pl.Buffered(1))
                 for t in range(n_lhs)]
    in_specs += [pl.BlockSpec((OP_TM, D_MODEL), lambda i: (i, 0))]
    return pl.pallas_call(
        functools.partial(_out_proj_kernel, n_lhs),
        grid=(m // OP_TM,),
        in_specs=in_specs,
        out_specs=pl.BlockSpec((OP_TM, D_MODEL), lambda i: (i, 0)),
        out_shape=jax.ShapeDtypeStruct((m, D_MODEL), F32),
        scratch_shapes=[pltpu.VMEM((kw, D_MODEL), BF16) for _ in lhs_list],
        compiler_params=_cparams(("arbitrary",)),
        name="out_proj",
    )(*lhs_list, *([w] * n_lhs), x)


def _pl_embed_kernel(final, x_ref, p_ref, g_ref, wg_ref, wp_ref, fg_ref, o_ref,
                     h_ref, wgb_ref, wpb_ref):
    @pl.when(pl.program_id(0) == 0)
    def _():
        wgb_ref[...] = wg_ref[...].astype(BF16)
        wpb_ref[...] = wp_ref[...].astype(BF16)

    _rms_rows_to(h_ref, x_ref, g_ref, PL_TM)
    gate = jax.nn.sigmoid(jnp.dot(h_ref[...], wgb_ref[...], preferred_element_type=F32))
    proj = jnp.dot(p_ref[...].astype(BF16), wpb_ref[...], preferred_element_type=F32)
    y = x_ref[...] + gate * proj
    if final:
        ms = jnp.mean(y * y, axis=-1, keepdims=True)
        y = y * lax.rsqrt(ms + EPS) * fg_ref[...]
    o_ref[...] = y


def _pl_embed(x, p, g, wg, wp, final_g, layer, final):
    m = x.shape[0]
    return pl.pallas_call(
        functools.partial(_pl_embed_kernel, final),
        grid=(m // PL_TM,),
        in_specs=[
            pl.BlockSpec((PL_TM, D_MODEL), lambda i: (i, 0)),
            pl.BlockSpec((None, PL_TM, D_PL), lambda i: (layer, i, 0)),
            pl.BlockSpec((1, D_MODEL), lambda i: (0, 0)),
            pl.BlockSpec((None, D_MODEL, D_MODEL), lambda i: (layer, 0, 0),
                         pipeline_mode=pl.Buffered(1)),
            pl.BlockSpec((None, D_PL, D_MODEL), lambda i: (layer, 0, 0),
                         pipeline_mode=pl.Buffered(1)),
            pl.BlockSpec((1, D_MODEL), lambda i: (0, 0)),
        ],
        out_specs=pl.BlockSpec((PL_TM, D_MODEL), lambda i: (i, 0)),
        out_shape=jax.ShapeDtypeStruct((m, D_MODEL), F32),
        scratch_shapes=[pltpu.VMEM((PL_TM, D_MODEL), BF16),
                        pltpu.VMEM((D_MODEL, D_MODEL), BF16),
                        pltpu.VMEM((D_PL, D_MODEL), BF16)],
        compiler_params=_cparams(("arbitrary",)),
        name="pl_embed",
    )(x, p, g, wg, wp, final_g)


def _gla_kernel(q_ref, k_ref, v_ref, r_ref, gz_ref, gw_ref, gb_ref, ng_ref, o_ref,
                b_ref, kh_ref, s_ref):
    C, SB = GLA_CHUNK, GLA_SUB
    n_chunks = SEQ_TB // C

    @pl.when(pl.program_id(1) == 0)
    def _():
        s_ref[...] = jnp.zeros_like(s_ref)

    lin = jnp.dot(gz_ref[...].astype(BF16), gw_ref[...], preferred_element_type=F32) + gb_ref[...]
    log_a = -(jnp.maximum(-lin, 0.0) + jnp.log1p(jnp.exp(-jnp.abs(lin)))) * (LOG2E / GLA_GATE_TAU)
    tri = (lax.broadcasted_iota(jnp.int32, (C, C), 0)
           >= lax.broadcasted_iota(jnp.int32, (C, C), 1)).astype(F32)
    for c in range(n_chunks):
        b_ref[c * C:(c + 1) * C, :] = jnp.dot(
            tri, log_a[c * C:(c + 1) * C, :], preferred_element_type=F32,
            precision=lax.Precision.HIGHEST)
    kh_ref[...] = k_ref[...].astype(F32) * jnp.exp2(-b_ref[...])
    scaled_ok = jnp.max(jnp.abs(kh_ref[...])) <= GLA_SCALED_KEY_MAX

    lane = lax.broadcasted_iota(jnp.int32, (SB, C), 1)
    trow = lax.broadcasted_iota(jnp.int32, (SB, C), 0)
    causal = (lax.broadcasted_iota(jnp.int32, (C, C), 0)
              >= lax.broadcasted_iota(jnp.int32, (C, C), 1))
    nt = (((1,), (1,)), ((), ()))
    tn = (((0,), (0,)), ((), ()))

    def guarded_products(q, k, b):
        blocks = []
        for i in range(C // SB):
            s0 = i * SB
            q_i = q[s0:s0 + SB]
            b_i = b[s0:s0 + SB]
            acc = jnp.zeros((SB, C), F32)
            if i > 0:
                p_i = b[s0 - 1:s0, :]
                qt = (q_i * jnp.exp2(b_i - p_i)).astype(BF16)
                kt = (k * jnp.exp2(p_i - b)).astype(BF16)
                off = lax.dot_general(qt, kt, nt, preferred_element_type=F32)
                acc = jnp.where(lane < s0, off, 0.0)
            for s in range(SB):
                b_s = b[s0 + s:s0 + s + 1, :]
                k_s = k[s0 + s:s0 + s + 1, :]
                e = jnp.exp2(b_i - b_s)
                col = jnp.sum(q_i * (k_s * e), axis=1, keepdims=True)
                acc = jnp.where((lane == s0 + s) & (trow >= s), col, acc)
            blocks.append(acc)
        return jnp.concatenate(blocks, axis=0)

    def chunk_body(scaled, per_step, it, carry):
        units = []
        for u in range(per_step):
            rows = pl.ds(pl.multiple_of((it * per_step + u) * C, C), C)
            for h in range(GLA_HEADS):
                units.append((u, h, rows))
        hk = lambda h: slice(h * GLA_HEAD_K, (h + 1) * GLA_HEAD_K)
        hv = lambda h: slice(h * GLA_HEAD_V, (h + 1) * GLA_HEAD_V)

        part = {}
        for u, h, rows in units:
            q = q_ref[rows, hk(h)].astype(F32) * (GLA_HEAD_K ** -0.5)
            v = v_ref[rows, hv(h)].astype(BF16)
            b = b_ref[rows, hk(h)]
            b_last = b[C - 1:C, :]
            qb = (q * jnp.exp2(b)).astype(BF16)
            if scaled:
                kh = kh_ref[rows, hk(h)]
                a_mat = jnp.where(causal, lax.dot_general(qb, kh.astype(BF16), nt,
                                                          preferred_element_type=F32), 0.0)
                k_dec = (kh * jnp.exp2(b_last)).astype(BF16)
            else:
                k = k_ref[rows, hk(h)].astype(F32)
                a_mat = guarded_products(q, k, b)
                k_dec = (k * jnp.exp2(b_last - b)).astype(BF16)
            upd = lax.dot_general(v, k_dec, tn, preferred_element_type=F32)
            local = jnp.dot(a_mat.astype(BF16), v, preferred_element_type=F32)
            part[(u, h)] = (qb, jnp.exp2(b_last), upd, local)

        state = {}
        for h in range(GLA_HEADS):
            st = s_ref[h]
            for u in range(per_step):
                state[(u, h)] = st
                _, decay, upd, _ = part[(u, h)]
                st = st * decay + upd
            s_ref[h] = st

        for u, h, rows in units:
            qb, _, _, local = part[(u, h)]
            o = local + lax.dot_general(qb, state[(u, h)].astype(BF16), nt,
                                        preferred_element_type=F32)
            on = o * lax.rsqrt(jnp.mean(o * o, axis=-1, keepdims=True) + EPS) * ng_ref[...]
            r = r_ref[rows, hv(h)].astype(F32)
            o_ref[rows, hv(h)] = (on * (r * jax.nn.sigmoid(r))).astype(o_ref.dtype)
        return carry

    def run(scaled):
        per_step = GLA_CHUNKS_PER_STEP if scaled else 1
        lax.fori_loop(0, n_chunks // per_step,
                      functools.partial(chunk_body, scaled, per_step), 0)

    lax.cond(scaled_ok, lambda: run(True), lambda: run(False))


def _gla(z, gz, gate_w, gate_b, norm_g):
    nb = SEQ // SEQ_TB
    row = lambda b, i: b * nb + i
    return pl.pallas_call(
        _gla_kernel,
        grid=(BATCH, nb),
        in_specs=[
            pl.BlockSpec((SEQ_TB, GLA_DK), lambda b, i: (row(b, i), 0)),
            pl.BlockSpec((SEQ_TB, GLA_DK), lambda b, i: (row(b, i), 1)),
            pl.BlockSpec((SEQ_TB, GLA_DV), lambda b, i: (row(b, i), 1)),
            pl.BlockSpec((SEQ_TB, GLA_DV), lambda b, i: (row(b, i), 2)),
            pl.BlockSpec((SEQ_TB, LANES), lambda b, i: (row(b, i), 0)),
            pl.BlockSpec((LANES, GLA_DK), lambda b, i: (0, 0)),
            pl.BlockSpec((1, GLA_DK), lambda b, i: (0, 0)),
            pl.BlockSpec((1, GLA_HEAD_V), lambda b, i: (0, 0)),
        ],
        out_specs=pl.BlockSpec((SEQ_TB, GLA_DV), lambda b, i: (row(b, i), 0)),
        out_shape=jax.ShapeDtypeStruct((TOKENS, GLA_DV), BF16),
        scratch_shapes=[pltpu.VMEM((SEQ_TB, GLA_DK), F32),
                        pltpu.VMEM((SEQ_TB, GLA_DK), F32),
                        pltpu.VMEM((GLA_HEADS, GLA_HEAD_V, GLA_HEAD_K), F32)],
        compiler_params=_cparams(("parallel", "arbitrary")),
        name="gla",
    )(z, z, z, z, gz, gate_w, gate_b, norm_g)


def _conv_kernel(ca_ref, cb_ref, ha_ref, hb_ref, w_ref, wb_ref, lg_ref, lb_ref, o_ref,
                 sh_ref, y_ref):
    H = CONV_HALO
    n_sh = SUBLANES
    keep = (pl.program_id(1) > 0).astype(F32)

    u_main = ca_ref[...].astype(F32) * jax.nn.sigmoid(cb_ref[...].astype(F32))
    u_halo = ha_ref[...].astype(F32) * jax.nn.sigmoid(hb_ref[...].astype(F32)) * keep
    for r in range(n_sh):
        sh_ref[r, 0:H - r, :] = u_halo[r:H, :]
        sh_ref[r, H - r:H - r + SEQ_TB, :] = u_main

    first = H - (CONV_WIDTH - 1)
    ct_w = 128

    groups = CONV_RC // SUBLANES

    def row_body(cs, bias, rc, carry):
        t0 = pl.multiple_of(rc * CONV_RC, CONV_RC)
        accs = [bias, None]
        for r in range(n_sh):
            offs = [o for o in range(first, first + CONV_WIDTH) if o % n_sh == r]
            lo, hi = offs[0] - r, offs[-1] - r
            slab = sh_ref[r, pl.ds(t0 + lo, CONV_RC + hi - lo), cs]
            for off in offs:
                a = off - r - lo
                win = slab[a:a + CONV_RC].reshape(groups, SUBLANES, ct_w)
                term = w_ref[off - first, :, cs][None] * win
                accs[r % 2] = term if accs[r % 2] is None else accs[r % 2] + term
        y_ref[pl.ds(t0, CONV_RC), cs] = (accs[0] + accs[1]).reshape(CONV_RC, ct_w)
        return carry

    for ct in range(CONV_CH // ct_w):
        cs = slice(ct * ct_w, (ct + 1) * ct_w)
        bias = jnp.broadcast_to(wb_ref[:, cs][None], (groups, SUBLANES, ct_w))
        lax.fori_loop(0, SEQ_TB // CONV_RC, functools.partial(row_body, cs, bias), 0)

    def ln_body(rc, carry):
        sl = pl.ds(pl.multiple_of(rc * 64, 64), 64)
        y = y_ref[sl, :]
        mu = jnp.mean(y, axis=-1, keepdims=True)
        var = jnp.mean(jnp.square(y - mu), axis=-1, keepdims=True)
        t = (y - mu) * lax.rsqrt(var + EPS) * lg_ref[...] + lb_ref[...]
        o_ref[sl, :] = (t * jax.nn.sigmoid(t)).astype(o_ref.dtype)
        return carry

    lax.fori_loop(0, SEQ_TB // 64, ln_body, 0, unroll=2)


def _conv(z, w, wb, ln_g, ln_b):
    nb = SEQ // SEQ_TB
    hb = SEQ_TB // CONV_HALO
    ca_col = (2 * GLA_DK + 2 * GLA_DV) // CONV_CH
    row = lambda b, i: b * nb + i
    halo = lambda b, i: jnp.maximum(row(b, i) * hb - 1, 0)
    return pl.pallas_call(
        _conv_kernel,
        grid=(BATCH, nb),
        in_specs=[
            pl.BlockSpec((SEQ_TB, CONV_CH), lambda b, i: (row(b, i), ca_col)),
            pl.BlockSpec((SEQ_TB, CONV_CH), lambda b, i: (row(b, i), ca_col + 1)),
            pl.BlockSpec((CONV_HALO, CONV_CH), lambda b, i: (halo(b, i), ca_col)),
            pl.BlockSpec((CONV_HALO, CONV_CH), lambda b, i: (halo(b, i), ca_col + 1)),
            pl.BlockSpec((CONV_WIDTH, SUBLANES, CONV_CH), lambda b, i: (0, 0, 0)),
            pl.BlockSpec((1, CONV_CH), lambda b, i: (0, 0)),
            pl.BlockSpec((1, CONV_CH), lambda b, i: (0, 0)),
            pl.BlockSpec((1, CONV_CH), lambda b, i: (0, 0)),
        ],
        out_specs=pl.BlockSpec((SEQ_TB, CONV_CH), lambda b, i: (row(b, i), 0)),
        out_shape=jax.ShapeDtypeStruct((TOKENS, CONV_CH), BF16),
        scratch_shapes=[pltpu.VMEM((SUBLANES, SEQ_TB + CONV_HALO, CONV_CH), F32),
                        pltpu.VMEM((SEQ_TB, CONV_CH), F32)],
        compiler_params=_cparams(("parallel", "parallel")),
        name="conv",
    )(z, z, z, z, w, wb, ln_g, ln_b)


ATT_HALF = ATT_TQ // 2


def _attn_kernel(q_ref, k0_ref, k1_ref, k2_ref, v0_ref, v1_ref, v2_ref,
                 m0_ref, ta_ref, tb_ref, o_ref):
    k_refs = (k0_ref, k1_ref, k2_ref)
    v_refs = (v0_ref, v1_ref, v2_ref)
    nt = (((1,), (1,)), ((), ()))
    tn = (((0,), (0,)), ((), ()))
    H = ATT_HALF

    def scores(j_min, h):
        hs = slice(h * ATT_HEAD_DIM, (h + 1) * ATT_HEAD_DIM)
        q = q_ref[:, hs]
        tiles = {}
        for j in range(j_min, 3):
            st = lax.dot_general(k_refs[j][:, hs], q, nt, preferred_element_type=F32)
            for kh in range(2):
                for a in range(2):
                    n = 2 * j + kh - a
                    if n < 0 or n > 4:
                        continue
                    t = st[kh * H:(kh + 1) * H, a * H:(a + 1) * H]
                    if n == 0:
                        t = t + m0_ref[...]
                    elif n == 3:
                        t = t + ta_ref[h]
                    elif n == 4:
                        t = t + tb_ref[h]
                    tiles[(j, kh, a)] = t
        maxima = []
        for a in range(2):
            mx = None
            for key in tiles:
                if key[2] == a:
                    cur = jnp.max(tiles[key], axis=0, keepdims=True)
                    mx = cur if mx is None else jnp.maximum(mx, cur)
            maxima.append(mx)
        return tiles, maxima

    def weights(tiles, maxima):
        probs = {}
        inv_l = []
        for a in range(2):
            mine = [key for key in tiles if key[2] == a]
            tot = None
            for key in mine:
                p = jnp.exp2(tiles[key] - maxima[a])
                probs[key] = p.astype(BF16)
                cur = jnp.sum(p, axis=0, keepdims=True)
                tot = cur if tot is None else tot + cur
            inv_l.append(1.0 / tot)
        return probs, inv_l

    def values(j_min, h, probs, inv_l):
        hs = slice(h * ATT_HEAD_DIM, (h + 1) * ATT_HEAD_DIM)
        zero = jnp.zeros((H, H), BF16)
        ot = None
        for j in range(j_min, 3):
            pt = jnp.concatenate(
                [jnp.concatenate([probs.get((j, kh, a), zero) for a in range(2)], axis=1)
                 for kh in range(2)], axis=0)
            cur = lax.dot_general(v_refs[j][:, hs], pt, tn, preferred_element_type=F32)
            ot = cur if ot is None else ot + cur
        ot = ot * jnp.concatenate(inv_l, axis=1)
        o_ref[:, hs] = ot.T.astype(o_ref.dtype)

    def all_heads(j_min):
        scored, weighted = {}, {}
        for t in range(ATT_HEADS + 2):
            if t < ATT_HEADS:
                scored[t] = scores(j_min, t)
            if 0 <= t - 1 < ATT_HEADS:
                weighted[t - 1] = weights(*scored.pop(t - 1))
            if 0 <= t - 2 < ATT_HEADS:
                values(j_min, t - 2, *weighted.pop(t - 2))

    i = pl.program_id(1)
    lax.cond(i >= 2, lambda: all_heads(0),
             lambda: lax.cond(i == 1, lambda: all_heads(1), lambda: all_heads(2)))


def _attention(qkv, m0, ta, tb):
    nb = SEQ // ATT_TQ
    row = lambda b, i: b * nb + i
    back = lambda d: (lambda b, i: (b * nb + jnp.maximum(i - d, 0)))
    spec = lambda rowfn, col: pl.BlockSpec((ATT_TQ, D_MODEL), lambda b, i: (rowfn(b, i), col))
    table = pl.BlockSpec((ATT_HEADS, ATT_HALF, ATT_HALF), lambda b, i: (0, 0, 0))
    return pl.pallas_call(
        _attn_kernel,
        grid=(BATCH, nb),
        in_specs=[
            spec(row, 0),
            spec(back(2), 1), spec(back(1), 1), spec(row, 1),
            spec(back(2), 2), spec(back(1), 2), spec(row, 2),
            pl.BlockSpec((ATT_HALF, ATT_HALF), lambda b, i: (0, 0)),
            table, table,
        ],
        out_specs=pl.BlockSpec((ATT_TQ, D_MODEL), lambda b, i: (row(b, i), 0)),
        out_shape=jax.ShapeDtypeStruct((TOKENS, D_MODEL), BF16),
        compiler_params=_cparams(("parallel", "parallel")),
        name="attention",
    )(qkv, qkv, qkv, qkv, qkv, qkv, qkv, m0, ta, tb)


def _attention_bias_tables(rel_bias):
    H = ATT_HALF
    assert H == REL_CLIP
    rb = rel_bias.astype(F32)
    rel = (rb - rb[:, 2 * REL_CLIP:]) * LOG2E

    def toeplitz(g):
        flat = jnp.tile(g, (1, H))[:, :H * (2 * H - 1)]
        return flat.reshape(-1, H, 2 * H - 1)[:, :, :H]

    ta = toeplitz(jnp.concatenate([jnp.zeros((ATT_HEADS, H), F32),
                                   rel[:, REL_CLIP:2 * REL_CLIP]], axis=1))
    tb = toeplitz(jnp.concatenate([rel[:, REL_CLIP:2 * REL_CLIP], rel[:, :REL_CLIP]], axis=1))
    kc = jnp.arange(H)[:, None] // ATT_CHUNK
    qc = jnp.arange(H)[None, :] // ATT_CHUNK
    tb = jnp.where((kc <= qc)[None], tb, NEG_BIG)
    m0 = jnp.where(kc >= qc, 0.0, NEG_BIG).astype(F32)
    return m0, ta, tb


def kernel(x, p, ffn_norm, ffn_w_gate, ffn_w_up, ffn_w_down, mix_norm, ab_w_in, gla_gate_w, gla_gate_b, gla_norm_g, conv_dw, conv_dw_b, conv_ln_g, conv_ln_b, ab_w_out, att_w_qkv, att_rel_bias, att_w_o, pl_norm, pl_w_gate, pl_w_proj, final_norm):
    xs = x.reshape(TOKENS, D_MODEL)
    ps = p.reshape(DEPTH, TOKENS, D_PL)
    row = lambda a: a.reshape(1, -1).astype(F32)

    def ffn(xs, i, s):
        return _ffn(xs, row(ffn_norm[i, s]), ffn_w_gate, ffn_w_up, ffn_w_down, i, s)

    for i in range(DEPTH):
        e = i // 2
        xs = ffn(xs, i, 0)
        if i % 2 == 0:
            w_in = ab_w_in[e]
            gz_lo = 2 * GLA_DK + 2 * GLA_DV
            gz_hi = gz_lo + GLA_GATE_RANK
            w_bf = w_in.astype(BF16)
            w_conv = w_bf[:, gz_hi:]
            w_gz = jnp.pad(w_bf[:, gz_lo:gz_hi], ((0, 0), (0, LANES - GLA_GATE_RANK)))
            z_all, gz = _rms_proj(
                xs, row(mix_norm[i]),
                [(w_bf, gz_lo // MM_TN), (w_conv, 2 * CONV_CH // MM_TN)],
                (0, 1.0), BF16, w_extra=w_gz)
            gate_w = jnp.pad(gla_gate_w[e], ((0, LANES - GLA_GATE_RANK), (0, 0))).astype(BF16)
            a_out = _gla(z_all, gz, gate_w, row(gla_gate_b[e]), row(gla_norm_g[e]))
            taps = jnp.broadcast_to(conv_dw[e].astype(F32)[:, None, :],
                                    (CONV_WIDTH, SUBLANES, CONV_CH))
            b_out = _conv(z_all, taps, row(conv_dw_b[e]), row(conv_ln_g[e]), row(conv_ln_b[e]))
            xs = _out_proj([a_out, b_out], ab_w_out[e], xs)
        else:
            qkv = _rms_proj(xs, row(mix_norm[i]), [(att_w_qkv[e], 3 * D_MODEL // MM_TN)],
                            (D_MODEL // MM_TN, ATT_HEAD_DIM ** -0.5 * LOG2E), BF16)
            m0, ta, tb = _attention_bias_tables(att_rel_bias[e])
            o = _attention(qkv, m0, ta, tb)
            xs = _out_proj([o], att_w_o[e], xs)
        xs = ffn(xs, i, 1)
        xs = _pl_embed(xs, ps, row(pl_norm[i]), pl_w_gate, pl_w_proj, row(final_norm),
                       i, i == DEPTH - 1)
    return xs.reshape(BATCH, SEQ, D_MODEL)
```

```python
import functools

import jax
import jax.numpy as jnp
from jax import lax
from jax.experimental import pallas as pl
from jax.experimental.pallas import tpu as pltpu

F32 = jnp.float32
BF16 = jnp.bfloat16

D_MODEL = 2048
BATCH = 4
SEQ = 2048
DEPTH = 2
TOKENS = BATCH * SEQ
D_PL = 256
D_FF = 5632
EPS = 1e-6

GLA_HEADS = 4
GLA_DK = 512
GLA_DV = 1024
GLA_HEAD_K = 128
GLA_HEAD_V = 256
GLA_GATE_RANK = 16
GLA_GATE_TAU = 16.0
GLA_CHUNK = 64
GLA_SUB = 16
GLA_SCALED_KEY_MAX = 2.0 ** 40
GLA_CHUNKS_PER_STEP = 4
CONV_CH = 1024
CONV_WIDTH = 31
AB_MAIN = 2 * GLA_DK + 2 * GLA_DV + 2 * CONV_CH
ATT_HEADS = 16
ATT_HEAD_DIM = 128
ATT_CHUNK = 64
LEFT_CHUNKS = 8
REL_CLIP = 128
NEG_BIG = -1e30
LOG2E = 1.4426950408889634

V7X_VMEM_BYTES = 64 * 1024 * 1024
LANES = 128
SUBLANES = 8
VMEM_LIMIT = V7X_VMEM_BYTES - 4 * 1024 * 1024

FFN_TM = 1024
FFN_TF = 512
FFN_FIRST_ROWS = FFN_TM
FFN_TF_FIRST = 256
FFN_RING = 3
FFN_TN = 512
MM_TM = 1024
MM_TN = 1024
PL_TM = 256
OP_TM = 512
SEQ_TB = 512
CONV_HALO = 32
CONV_RC = 128
ATT_TQ = 256
RMS_ROWS = 256


def _cparams(sem):
    return pltpu.CompilerParams(dimension_semantics=sem, vmem_limit_bytes=VMEM_LIMIT)


def _rms_rows_to(dst_ref, x_ref, g_ref, rows):
    def body(c, carry):
        sl = pl.ds(pl.multiple_of(c * RMS_ROWS, RMS_ROWS), RMS_ROWS)
        x = x_ref[sl, :]
        ms = jnp.mean(x * x, axis=-1, keepdims=True)
        dst_ref[sl, :] = (x * lax.rsqrt(ms + EPS) * g_ref[...]).astype(dst_ref.dtype)
        return carry
    lax.fori_loop(0, rows // RMS_ROWS, body, 0)


def _ffn_step(f, load_residual, g_ref, wg_ref, wu_ref, wd_ref, o_ref, h_ref):
    @pl.when(f == 0)
    def _():
        load_residual()
        _rms_rows_to(h_ref, o_ref, g_ref, o_ref.shape[0])

    h = h_ref[...]
    gate = jnp.dot(h, wg_ref[...], preferred_element_type=F32)
    up = jnp.dot(h, wu_ref[...], preferred_element_type=F32)
    a = (0.5 * gate * jax.nn.sigmoid(gate) * up).astype(BF16)

    for n in range(D_MODEL // FFN_TN):
        cs = slice(n * FFN_TN, (n + 1) * FFN_TN)
        o_ref[:, cs] += jnp.dot(a, wd_ref[:, cs], preferred_element_type=F32)


def _ffn_first_kernel(layer, half, tf, n_f, x_hbm, g_ref, wg_hbm, wu_hbm, wd_hbm, o_ref,
                      wgb_ref, wub_ref, wdb_ref, h_ref, wgf, wuf, wdf, sem, wsem):
    f = pl.program_id(1)

    def copies(t, slot):
        cols = pl.ds(pl.multiple_of(t * tf, tf), tf)
        return (pltpu.make_async_copy(wg_hbm.at[layer, half, :, cols], wgf.at[slot],
                                      wsem.at[0, slot]),
                pltpu.make_async_copy(wu_hbm.at[layer, half, :, cols], wuf.at[slot],
                                      wsem.at[1, slot]),
                pltpu.make_async_copy(wd_hbm.at[layer, half, cols, :], wdf.at[slot],
                                      wsem.at[2, slot]))

    @pl.when(f == 0)
    def _():
        for t in range(FFN_RING - 1):
            for c in copies(t, t):
                c.start()

    ahead = f + FFN_RING - 1

    @pl.when(ahead < n_f)
    def _():
        for c in copies(ahead, ahead % FFN_RING):
            c.start()

    slot = f % FFN_RING
    for c in copies(f, slot):
        c.wait()

    def load_residual():
        copy = pltpu.make_async_copy(x_hbm.at[pl.ds(0, FFN_FIRST_ROWS)], o_ref, sem)
        copy.start()
        copy.wait()

    wgb_ref[...] = wgf[slot].astype(BF16)
    wub_ref[...] = wuf[slot].astype(BF16)
    wdb_ref[...] = wdf[slot].astype(BF16)
    _ffn_step(f, load_residual, g_ref, wgb_ref, wub_ref, wdb_ref, o_ref, h_ref)


def _ffn_rest_kernel(n_f, n_copy, x_ref, g_ref, wg_ref, wu_ref, wd_ref, first_ref, o_ref,
                     h_ref, sem):
    s = pl.program_id(0)

    @pl.when(s < n_copy)
    def _():
        rows = pl.ds(pl.multiple_of(s * FFN_TM, FFN_TM), FFN_TM)
        copy = pltpu.make_async_copy(first_ref.at[rows], o_ref, sem)
        copy.start()
        copy.wait()

    def load_residual():
        o_ref[...] = x_ref[...]

    @pl.when(s >= n_copy)
    def _():
        _ffn_step((s - n_copy) % n_f, load_residual, g_ref, wg_ref, wu_ref, wd_ref, o_ref,
                  h_ref)


def _ffn(x, g, wg, wu, wd, layer, half):
    m = x.shape[0]
    tf = FFN_TF_FIRST
    n_first = D_FF // tf
    first, wgb, wub, wdb = pl.pallas_call(
        functools.partial(_ffn_first_kernel, layer, half, tf, n_first),
        grid=(1, n_first),
        in_specs=[
            pl.BlockSpec(memory_space=pl.ANY),
            pl.BlockSpec((1, D_MODEL), lambda i, f: (0, 0)),
            pl.BlockSpec(memory_space=pl.ANY),
            pl.BlockSpec(memory_space=pl.ANY),
            pl.BlockSpec(memory_space=pl.ANY),
        ],
        out_specs=[
            pl.BlockSpec((FFN_FIRST_ROWS, D_MODEL), lambda i, f: (0, 0),
                         pipeline_mode=pl.Buffered(1)),
            pl.BlockSpec((D_MODEL, tf), lambda i, f: (0, f)),
            pl.BlockSpec((D_MODEL, tf), lambda i, f: (0, f)),
            pl.BlockSpec((tf, D_MODEL), lambda i, f: (f, 0)),
        ],
        out_shape=[
            jax.ShapeDtypeStruct((FFN_FIRST_ROWS, D_MODEL), F32),
            jax.ShapeDtypeStruct((D_MODEL, D_FF), BF16),
            jax.ShapeDtypeStruct((D_MODEL, D_FF), BF16),
            jax.ShapeDtypeStruct((D_FF, D_MODEL), BF16),
        ],
        scratch_shapes=[pltpu.VMEM((FFN_FIRST_ROWS, D_MODEL), BF16),
                        pltpu.VMEM((FFN_RING, D_MODEL, tf), F32),
                        pltpu.VMEM((FFN_RING, D_MODEL, tf), F32),
                        pltpu.VMEM((FFN_RING, tf, D_MODEL), F32),
                        pltpu.SemaphoreType.DMA(()),
                        pltpu.SemaphoreType.DMA((3, FFN_RING))],
        compiler_params=_cparams(("arbitrary", "arbitrary")),
        name="ffn_first",
    )(x, g, wg, wu, wd)
    tf = FFN_TF
    n_f = D_FF // tf
    n_copy = FFN_FIRST_ROWS // FFN_TM
    tile = lambda s: jnp.where(s < n_copy, s, n_copy + (s - n_copy) // n_f)
    f_of = lambda s: jnp.where(s < n_copy, 0, (s - n_copy) % n_f)
    return pl.pallas_call(
        functools.partial(_ffn_rest_kernel, n_f, n_copy),
        grid=(n_copy + (m // FFN_TM - n_copy) * n_f,),
        in_specs=[
            pl.BlockSpec((FFN_TM, D_MODEL), lambda s: (jnp.maximum(tile(s), n_copy), 0)),
            pl.BlockSpec((1, D_MODEL), lambda s: (0, 0)),
            pl.BlockSpec((D_MODEL, tf), lambda s: (0, f_of(s))),
            pl.BlockSpec((D_MODEL, tf), lambda s: (0, f_of(s))),
            pl.BlockSpec((tf, D_MODEL), lambda s: (f_of(s), 0)),
            pl.BlockSpec(memory_space=pl.ANY),
        ],
        out_specs=pl.BlockSpec((FFN_TM, D_MODEL), lambda s: (tile(s), 0)),
        out_shape=jax.ShapeDtypeStruct((m, D_MODEL), F32),
        scratch_shapes=[pltpu.VMEM((FFN_TM, D_MODEL), BF16), pltpu.SemaphoreType.DMA(())],
        compiler_params=_cparams(("arbitrary",)),
        name="ffn_rest",
    )(x, g, wgb, wub, wdb, first)


def _rms_proj_kernel(seg_tiles, has_extra, *refs):
    n_seg = len(seg_tiles)
    x_ref, g_ref = refs[:2]
    w_refs = refs[2:2 + n_seg]
    cs_ref = refs[2 + n_seg]
    if has_extra:
        we_ref, o_ref, oe_ref, h_ref = refs[3 + n_seg:]
    else:
        o_ref, h_ref = refs[3 + n_seg:]
    j = pl.program_id(1)

    @pl.when(j == 0)
    def _():
        _rms_rows_to(h_ref, x_ref, g_ref, MM_TM)
        if has_extra:
            oe_ref[...] = jnp.dot(h_ref[...], we_ref[...].astype(BF16),
                                  preferred_element_type=F32)

    start = 0
    for w_ref, tiles in zip(w_refs, seg_tiles):
        @pl.when((j >= start) & (j < start + tiles))
        def _(w_ref=w_ref):
            acc = jnp.dot(h_ref[...], w_ref[...].astype(BF16), preferred_element_type=F32)
            o_ref[...] = (acc * cs_ref[...]).astype(o_ref.dtype)
        start += tiles


def _rms_proj(x, g, w_list, colscale, out_dtype, w_extra=None):
    m = x.shape[0]
    seg_tiles = tuple(t for _, t in w_list)
    n_tiles = sum(seg_tiles)
    has_extra = w_extra is not None
    in_specs = [
        pl.BlockSpec((MM_TM, D_MODEL), lambda i, j: (i, 0)),
        pl.BlockSpec((1, D_MODEL), lambda i, j: (0, 0)),
    ]
    start = 0
    for _, tiles in w_list:
        in_specs.append(pl.BlockSpec(
            (D_MODEL, MM_TN),
            lambda i, j, s=start, t=tiles: (0, jnp.clip(j - s, 0, t - 1))))
        start += tiles
    in_specs.append(pl.BlockSpec((1, MM_TN), lambda i, j: (0, j)))
    out_specs = pl.BlockSpec((MM_TM, MM_TN), lambda i, j: (i, j))
    out_shape = jax.ShapeDtypeStruct((m, n_tiles * MM_TN), out_dtype)
    args = [x, g] + [w for w, _ in w_list] + [colscale]
    if has_extra:
        in_specs.append(pl.BlockSpec((D_MODEL, LANES), lambda i, j: (0, 0)))
        out_specs = [out_specs, pl.BlockSpec((MM_TM, LANES), lambda i, j: (i, 0))]
        out_shape = [out_shape, jax.ShapeDtypeStruct((m, LANES), F32)]
        args.append(w_extra)
    return pl.pallas_call(
        functools.partial(_rms_proj_kernel, seg_tiles, has_extra),
        grid=(m // MM_TM, n_tiles),
        in_specs=in_specs,
        out_specs=out_specs,
        out_shape=out_shape,
        scratch_shapes=[pltpu.VMEM((MM_TM, D_MODEL), BF16)],
        compiler_params=_cparams(("parallel", "arbitrary")),
        name="rms_proj",
    )(*args)


def _out_proj_kernel(n_lhs, *refs):
    lhs_refs = refs[:n_lhs]
    w_refs = refs[n_lhs:2 * n_lhs]
    x_ref, o_ref = refs[2 * n_lhs], refs[2 * n_lhs + 1]
    wb_refs = refs[2 * n_lhs + 2:]

    @pl.when(pl.program_id(0) == 0)
    def _():
        for w_ref, wb_ref in zip(w_refs, wb_refs):
            wb_ref[...] = w_ref[...].astype(BF16)

    acc = x_ref[...]
    for a_ref, wb_ref in zip(lhs_refs, wb_refs):
        acc = acc + jnp.dot(a_ref[...], wb_ref[...], preferred_element_type=F32)
    o_ref[...] = acc


def _out_proj(lhs_list, w, x):
    m = x.shape[0]
    n_lhs = len(lhs_list)
    kw = lhs_list[0].shape[1]
    in_specs = [pl.BlockSpec((OP_TM, kw), lambda i: (i, 0)) for _ in lhs_list]
    in_specs += [pl.BlockSpec((kw, D_MODEL), lambda i, t=t: (t, 0), pipeline_mode=pl.Buffered(1))
                 for t in range(n_lhs)]
    in_specs += [pl.BlockSpec((OP_TM, D_MODEL), lambda i: (i, 0))]
    return pl.pallas_call(
        functools.partial(_out_proj_kernel, n_lhs),
        grid=(m // OP_TM,),
        in_specs=in_specs,
        out_specs=pl.BlockSpec((OP_TM, D_MODEL), lambda i: (i, 0)),
        out_shape=jax.ShapeDtypeStruct((m, D_MODEL), F32),
        scratch_shapes=[pltpu.VMEM((kw, D_MODEL), BF16) for _ in lhs_list],
        compiler_params=_cparams(("arbitrary",)),
        name="out_proj",
    )(*lhs_list, *([w] * n_lhs), x)


def _pl_embed_kernel(final, x_ref, p_ref, g_ref, wg_ref, wp_ref, fg_ref, o_ref,
                     h_ref, wgb_ref, wpb_ref):
    @pl.when(pl.program_id(0) == 0)
    def _():
        wgb_ref[...] = wg_ref[...].astype(BF16)
        wpb_ref[...] = wp_ref[...].astype(BF16)

    _rms_rows_to(h_ref, x_ref, g_ref, PL_TM)
    gate = jax.nn.sigmoid(jnp.dot(h_ref[...], wgb_ref[...], preferred_element_type=F32))
    proj = jnp.dot(p_ref[...].astype(BF16), wpb_ref[...], preferred_element_type=F32)
    y = x_ref[...] + gate * proj
    if final:
        ms = jnp.mean(y * y, axis=-1, keepdims=True)
        y = y * lax.rsqrt(ms + EPS) * fg_ref[...]
    o_ref[...] = y


def _pl_embed(x, p, g, wg, wp, final_g, layer, final):
    m = x.shape[0]
    return pl.pallas_call(
        functools.partial(_pl_embed_kernel, final),
        grid=(m // PL_TM,),
        in_specs=[
            pl.BlockSpec((PL_TM, D_MODEL), lambda i: (i, 0)),
            pl.BlockSpec((None, PL_TM, D_PL), lambda i: (layer, i, 0)),
            pl.BlockSpec((1, D_MODEL), lambda i: (0, 0)),
            pl.BlockSpec((None, D_MODEL, D_MODEL), lambda i: (layer, 0, 0),
                         pipeline_mode=pl.Buffered(1)),
            pl.BlockSpec((None, D_PL, D_MODEL), lambda i: (layer, 0, 0),
                         pipeline_mode=pl.Buffered(1)),
            pl.BlockSpec((1, D_MODEL), lambda i: (0, 0)),
        ],
        out_specs=pl.BlockSpec((PL_TM, D_MODEL), lambda i: (i, 0)),
        out_shape=jax.ShapeDtypeStruct((m, D_MODEL), F32),
        scratch_shapes=[pltpu.VMEM((PL_TM, D_MODEL), BF16),
                        pltpu.VMEM((D_MODEL, D_MODEL), BF16),
                        pltpu.VMEM((D_PL, D_MODEL), BF16)],
        compiler_params=_cparams(("arbitrary",)),
        name="pl_embed",
    )(x, p, g, wg, wp, final_g)


def _gla_kernel(q_ref, k_ref, v_ref, r_ref, gz_ref, gw_ref, gb_ref, ng_ref, o_ref,
                b_ref, kh_ref, s_ref):
    C, SB = GLA_CHUNK, GLA_SUB
    n_chunks = SEQ_TB // C

    @pl.when(pl.program_id(1) == 0)
    def _():
        s_ref[...] = jnp.zeros_like(s_ref)

    lin = jnp.dot(gz_ref[...].astype(BF16), gw_ref[...], preferred_element_type=F32) + gb_ref[...]
    log_a = -(jnp.maximum(-lin, 0.0) + jnp.log1p(jnp.exp(-jnp.abs(lin)))) * (LOG2E / GLA_GATE_TAU)
    tri = (lax.broadcasted_iota(jnp.int32, (C, C), 0)
           >= lax.broadcasted_iota(jnp.int32, (C, C), 1)).astype(F32)
    for c in range(n_chunks):
        b_ref[c * C:(c + 1) * C, :] = jnp.dot(
            tri, log_a[c * C:(c + 1) * C, :], preferred_element_type=F32,
            precision=lax.Precision.HIGHEST)
    kh_ref[...] = k_ref[...].astype(F32) * jnp.exp2(-b_ref[...])
    scaled_ok = jnp.max(jnp.abs(kh_ref[...])) <= GLA_SCALED_KEY_MAX

    lane = lax.broadcasted_iota(jnp.int32, (SB, C), 1)
    trow = lax.broadcasted_iota(jnp.int32, (SB, C), 0)
    causal = (lax.broadcasted_iota(jnp.int32, (C, C), 0)
              >= lax.broadcasted_iota(jnp.int32, (C, C), 1))
    nt = (((1,), (1,)), ((), ()))
    tn = (((0,), (0,)), ((), ()))

    def guarded_products(q, k, b):
        blocks = []
        for i in range(C // SB):
            s0 = i * SB
            q_i = q[s0:s0 + SB]
            b_i = b[s0:s0 + SB]
            acc = jnp.zeros((SB, C), F32)
            if i > 0:
                p_i = b[s0 - 1:s0, :]
                qt = (q_i * jnp.exp2(b_i - p_i)).astype(BF16)
                kt = (k * jnp.exp2(p_i - b)).astype(BF16)
                off = lax.dot_general(qt, kt, nt, preferred_element_type=F32)
                acc = jnp.where(lane < s0, off, 0.0)
            for s in range(SB):
                b_s = b[s0 + s:s0 + s + 1, :]
                k_s = k[s0 + s:s0 + s + 1, :]
                e = jnp.exp2(b_i - b_s)
                col = jnp.sum(q_i * (k_s * e), axis=1, keepdims=True)
                acc = jnp.where((lane == s0 + s) & (trow >= s), col, acc)
            blocks.append(acc)
        return jnp.concatenate(blocks, axis=0)

    def chunk_body(scaled, per_step, it, carry):
        units = []
        for u in range(per_step):
            rows = pl.ds(pl.multiple_of((it * per_step + u) * C, C), C)
            for h in range(GLA_HEADS):
                units.append((u, h, rows))
        hk = lambda h: slice(h * GLA_HEAD_K, (h + 1) * GLA_HEAD_K)
        hv = lambda h: slice(h * GLA_HEAD_V, (h + 1) * GLA_HEAD_V)

        part = {}
        for u, h, rows in units:
            q = q_ref[rows, hk(h)].astype(F32) * (GLA_HEAD_K ** -0.5)
            v = v_ref[rows, hv(h)].astype(BF16)
            b = b_ref[rows, hk(h)]
            b_last = b[C - 1:C, :]
            qb = (q * jnp.exp2(b)).astype(BF16)
            if scaled:
                kh = kh_ref[rows, hk(h)]
                a_mat = jnp.where(causal, lax.dot_general(qb, kh.astype(BF16), nt,
                                                          preferred_element_type=F32), 0.0)
                k_dec = (kh * jnp.exp2(b_last)).astype(BF16)
            else:
                k = k_ref[rows, hk(h)].astype(F32)
                a_mat = guarded_products(q, k, b)
                k_dec = (k * jnp.exp2(b_last - b)).astype(BF16)
            upd = lax.dot_general(v, k_dec, tn, preferred_element_type=F32)
            local = jnp.dot(a_mat.astype(BF16), v, preferred_element_type=F32)
            part[(u, h)] = (qb, jnp.exp2(b_last), upd, local)

        state = {}
        for h in range(GLA_HEADS):
            st = s_ref[h]
            for u in range(per_step):
                state[(u, h)] = st
                _, decay, upd, _ = part[(u, h)]
                st = st * decay + upd
            s_ref[h] = st

        for u, h, rows in units:
            qb, _, _, local = part[(u, h)]
            o = local + lax.dot_general(qb, state[(u, h)].astype(BF16), nt,
                                        preferred_element_type=F32)
            on = o * lax.rsqrt(jnp.mean(o * o, axis=-1, keepdims=True) + EPS) * ng_ref[...]
            r = r_ref[rows, hv(h)].astype(F32)
            o_ref[rows, hv(h)] = (on * (r * jax.nn.sigmoid(r))).astype(o_ref.dtype)
        return carry

    def run(scaled):
        per_step = GLA_CHUNKS_PER_STEP if scaled else 1
        lax.fori_loop(0, n_chunks // per_step,
                      functools.partial(chunk_body, scaled, per_step), 0)

    lax.cond(scaled_ok, lambda: run(True), lambda: run(False))


def _gla(z, gz, gate_w, gate_b, norm_g):
    nb = SEQ // SEQ_TB
    row = lambda b, i: b * nb + i
    return pl.pallas_call(
        _gla_kernel,
        grid=(BATCH, nb),
        in_specs=[
            pl.BlockSpec((SEQ_TB, GLA_DK), lambda b, i: (row(b, i), 0)),
            pl.BlockSpec((SEQ_TB, GLA_DK), lambda b, i: (row(b, i), 1)),
            pl.BlockSpec((SEQ_TB, GLA_DV), lambda b, i: (row(b, i), 1)),
            pl.BlockSpec((SEQ_TB, GLA_DV), lambda b, i: (row(b, i), 2)),
            pl.BlockSpec((SEQ_TB, LANES), lambda b, i: (row(b, i), 0)),
            pl.BlockSpec((LANES, GLA_DK), lambda b, i: (0, 0)),
            pl.BlockSpec((1, GLA_DK), lambda b, i: (0, 0)),
            pl.BlockSpec((1, GLA_HEAD_V), lambda b, i: (0, 0)),
        ],
        out_specs=pl.BlockSpec((SEQ_TB, GLA_DV), lambda b, i: (row(b, i), 0)),
        out_shape=jax.ShapeDtypeStruct((TOKENS, GLA_DV), BF16),
        scratch_shapes=[pltpu.VMEM((SEQ_TB, GLA_DK), F32),
                        pltpu.VMEM((SEQ_TB, GLA_DK), F32),
                        pltpu.VMEM((GLA_HEADS, GLA_HEAD_V, GLA_HEAD_K), F32)],
        compiler_params=_cparams(("parallel", "arbitrary")),
        name="gla",
    )(z, z, z, z, gz, gate_w, gate_b, norm_g)


def _conv_kernel(ca_ref, cb_ref, ha_ref, hb_ref, w_ref, wb_ref, lg_ref, lb_ref, o_ref,
                 sh_ref, y_ref):
    H = CONV_HALO
    n_sh = SUBLANES
    keep = (pl.program_id(1) > 0).astype(F32)

    u_main = ca_ref[...].astype(F32) * jax.nn.sigmoid(cb_ref[...].astype(F32))
    u_halo = ha_ref[...].astype(F32) * jax.nn.sigmoid(hb_ref[...].astype(F32)) * keep
    for r in range(n_sh):
        sh_ref[r, 0:H - r, :] = u_halo[r:H, :]
        sh_ref[r, H - r:H - r + SEQ_TB, :] = u_main

    first = H - (CONV_WIDTH - 1)
    ct_w = 128

    groups = CONV_RC // SUBLANES

    def row_body(cs, bias, rc, carry):
        t0 = pl.multiple_of(rc * CONV_RC, CONV_RC)
        accs = [bias, None]
        for r in range(n_sh):
            offs = [o for o in range(first, first + CONV_WIDTH) if o % n_sh == r]
            lo, hi = offs[0] - r, offs[-1] - r
            slab = sh_ref[r, pl.ds(t0 + lo, CONV_RC + hi - lo), cs]
            for off in offs:
                a = off - r - lo
                win = slab[a:a + CONV_RC].reshape(groups, SUBLANES, ct_w)
                term = w_ref[off - first, :, cs][None] * win
                accs[r % 2] = term if accs[r % 2] is None else accs[r % 2] + term
        y_ref[pl.ds(t0, CONV_RC), cs] = (accs[0] + accs[1]).reshape(CONV_RC, ct_w)
        return carry

    for ct in range(CONV_CH // ct_w):
        cs = slice(ct * ct_w, (ct + 1) * ct_w)
        bias = jnp.broadcast_to(wb_ref[:, cs][None], (groups, SUBLANES, ct_w))
        lax.fori_loop(0, SEQ_TB // CONV_RC, functools.partial(row_body, cs, bias), 0)

    def ln_body(rc, carry):
        sl = pl.ds(pl.multiple_of(rc * 64, 64), 64)
        y = y_ref[sl, :]
        mu = jnp.mean(y, axis=-1, keepdims=True)
        var = jnp.mean(jnp.square(y - mu), axis=-1, keepdims=True)
        t = (y - mu) * lax.rsqrt(var + EPS) * lg_ref[...] + lb_ref[...]
        o_ref[sl, :] = (t * jax.nn.sigmoid(t)).astype(o_ref.dtype)
        return carry

    lax.fori_loop(0, SEQ_TB // 64, ln_body, 0, unroll=2)


def _conv(z, w, wb, ln_g, ln_b):
    nb = SEQ // SEQ_TB
    hb = SEQ_TB // CONV_HALO
    ca_col = (2 * GLA_DK + 2 * GLA_DV) // CONV_CH
    row = lambda b, i: b * nb + i
    halo = lambda b, i: jnp.maximum(row(b, i) * hb - 1, 0)
    return pl.pallas_call(
        _conv_kernel,
        grid=(BATCH, nb),
        in_specs=[
            pl.BlockSpec((SEQ_TB, CONV_CH), lambda b, i: (row(b, i), ca_col)),
            pl.BlockSpec((SEQ_TB, CONV_CH), lambda b, i: (row(b, i), ca_col + 1)),
            pl.BlockSpec((CONV_HALO, CONV_CH), lambda b, i: (halo(b, i), ca_col)),
            pl.BlockSpec((CONV_HALO, CONV_CH), lambda b, i: (halo(b, i), ca_col + 1)),
            pl.BlockSpec((CONV_WIDTH, SUBLANES, CONV_CH), lambda b, i: (0, 0, 0)),
            pl.BlockSpec((1, CONV_CH), lambda b, i: (0, 0)),
            pl.BlockSpec((1, CONV_CH), lambda b, i: (0, 0)),
            pl.BlockSpec((1, CONV_CH), lambda b, i: (0, 0)),
        ],
        out_specs=pl.BlockSpec((SEQ_TB, CONV_CH), lambda b, i: (row(b, i), 0)),
        out_shape=jax.ShapeDtypeStruct((TOKENS, CONV_CH), BF16),
        scratch_shapes=[pltpu.VMEM((SUBLANES, SEQ_TB + CONV_HALO, CONV_CH), F32),
                        pltpu.VMEM((SEQ_TB, CONV_CH), F32)],
        compiler_params=_cparams(("parallel", "parallel")),
        name="conv",
    )(z, z, z, z, w, wb, ln_g, ln_b)


ATT_HALF = ATT_TQ // 2


def _attn_kernel(q_ref, k0_ref, k1_ref, k2_ref, v0_ref, v1_ref, v2_ref,
                 m0_ref, ta_ref, tb_ref, o_ref):
    k_refs = (k0_ref, k1_ref, k2_ref)
    v_refs = (v0_ref, v1_ref, v2_ref)
    nt = (((1,), (1,)), ((), ()))
    tn = (((0,), (0,)), ((), ()))
    H = ATT_HALF

    def scores(j_min, h):
        hs = slice(h * ATT_HEAD_DIM, (h + 1) * ATT_HEAD_DIM)
        q = q_ref[:, hs]
        tiles = {}
        for j in range(j_min, 3):
            st = lax.dot_general(k_refs[j][:, hs], q, nt, preferred_element_type=F32)
            for kh in range(2):
                for a in range(2):
                    n = 2 * j + kh - a
                    if n < 0 or n > 4:
                        continue
                    t = st[kh * H:(kh + 1) * H, a * H:(a + 1) * H]
                    if n == 0:
                        t = t + m0_ref[...]
                    elif n == 3:
                        t = t + ta_ref[h]
                    elif n == 4:
                        t = t + tb_ref[h]
                    tiles[(j, kh, a)] = t
        maxima = []
        for a in range(2):
            mx = None
            for key in tiles:
                if key[2] == a:
                    cur = jnp.max(tiles[key], axis=0, keepdims=True)
                    mx = cur if mx is None else jnp.maximum(mx, cur)
            maxima.append(mx)
        return tiles, maxima

    def weights(tiles, maxima):
        probs = {}
        inv_l = []
        for a in range(2):
            mine = [key for key in tiles if key[2] == a]
            tot = None
            for key in mine:
                p = jnp.exp2(tiles[key] - maxima[a])
                probs[key] = p.astype(BF16)
                cur = jnp.sum(p, axis=0, keepdims=True)
                tot = cur if tot is None else tot + cur
            inv_l.append(1.0 / tot)
        return probs, inv_l

    def values(j_min, h, probs, inv_l):
        hs = slice(h * ATT_HEAD_DIM, (h + 1) * ATT_HEAD_DIM)
        zero = jnp.zeros((H, H), BF16)
        ot = None
        for j in range(j_min, 3):
            pt = jnp.concatenate(
                [jnp.concatenate([probs.get((j, kh, a), zero) for a in range(2)], axis=1)
                 for kh in range(2)], axis=0)
            cur = lax.dot_general(v_refs[j][:, hs], pt, tn, preferred_element_type=F32)
            ot = cur if ot is None else ot + cur
        ot = ot * jnp.concatenate(inv_l, axis=1)
        o_ref[:, hs] = ot.T.astype(o_ref.dtype)

    def all_heads(j_min):
        scored, weighted = {}, {}
        for t in range(ATT_HEADS + 2):
            if t < ATT_HEADS:
                scored[t] = scores(j_min, t)
            if 0 <= t - 1 < ATT_HEADS:
                weighted[t - 1] = weights(*scored.pop(t - 1))
            if 0 <= t - 2 < ATT_HEADS:
                values(j_min, t - 2, *weighted.pop(t - 2))

    i = pl.program_id(1)
    lax.cond(i >= 2, lambda: all_heads(0),
             lambda: lax.cond(i == 1, lambda: all_heads(1), lambda: all_heads(2)))


def _attention(qkv, m0, ta, tb):
    nb = SEQ // ATT_TQ
    row = lambda b, i: b * nb + i
    back = lambda d: (lambda b, i: (b * nb + jnp.maximum(i - d, 0)))
    spec = lambda rowfn, col: pl.BlockSpec((ATT_TQ, D_MODEL), lambda b, i: (rowfn(b, i), col))
    table = pl.BlockSpec((ATT_HEADS, ATT_HALF, ATT_HALF), lambda b, i: (0, 0, 0))
    return pl.pallas_call(
        _attn_kernel,
        grid=(BATCH, nb),
        in_specs=[
            spec(row, 0),
            spec(back(2), 1), spec(back(1), 1), spec(row, 1),
            spec(back(2), 2), spec(back(1), 2), spec(row, 2),
            pl.BlockSpec((ATT_HALF, ATT_HALF), lambda b, i: (0, 0)),
            table, table,
        ],
        out_specs=pl.BlockSpec((ATT_TQ, D_MODEL), lambda b, i: (row(b, i), 0)),
        out_shape=jax.ShapeDtypeStruct((TOKENS, D_MODEL), BF16),
        compiler_params=_cparams(("parallel", "parallel")),
        name="attention",
    )(qkv, qkv, qkv, qkv, qkv, qkv, qkv, m0, ta, tb)


def _attention_bias_tables(rel_bias):
    H = ATT_HALF
    assert H == REL_CLIP
    rb = rel_bias.astype(F32)
    rel = (rb - rb[:, 2 * REL_CLIP:]) * LOG2E

    def toeplitz(g):
        flat = jnp.tile(g, (1, H))[:, :H * (2 * H - 1)]
        return flat.reshape(-1, H, 2 * H - 1)[:, :, :H]

    ta = toeplitz(jnp.concatenate([jnp.zeros((ATT_HEADS, H), F32),
                                   rel[:, REL_CLIP:2 * REL_CLIP]], axis=1))
    tb = toeplitz(jnp.concatenate([rel[:, REL_CLIP:2 * REL_CLIP], rel[:, :REL_CLIP]], axis=1))
    kc = jnp.arange(H)[:, None] // ATT_CHUNK
    qc = jnp.arange(H)[None, :] // ATT_CHUNK
    tb = jnp.where((kc <= qc)[None], tb, NEG_BIG)
    m0 = jnp.where(kc >= qc, 0.0, NEG_BIG).astype(F32)
    return m0, ta, tb


def kernel(x, p, ffn_norm, ffn_w_gate, ffn_w_up, ffn_w_down, mix_norm, ab_w_in, gla_gate_w, gla_gate_b, gla_norm_g, conv_dw, conv_dw_b, conv_ln_g, conv_ln_b, ab_w_out, att_w_qkv, att_rel_bias, att_w_o, pl_norm, pl_w_gate, pl_w_proj, final_norm):
    xs = x.reshape(TOKENS, D_MODEL)
    ps = p.reshape(DEPTH, TOKENS, D_PL)
    row = lambda a: a.reshape(1, -1).astype(F32)

    def ffn(xs, i, s):
        return _ffn(xs, row(ffn_norm[i, s]), ffn_w_gate, ffn_w_up, ffn_w_down, i, s)

    for i in range(DEPTH):
        e = i // 2
        xs = ffn(xs, i, 0)
        if i % 2 == 0:
            w_in = ab_w_in[e]
            gz_lo = 2 * GLA_DK + 2 * GLA_DV
            gz_hi = gz_lo + GLA_GATE_RANK
            w_bf = w_in.astype(BF16)
            w_conv = w_bf[:, gz_hi:]
            w_gz = jnp.pad(w_bf[:, gz_lo:gz_hi], ((0, 0), (0, LANES - GLA_GATE_RANK)))
            z_all, gz = _rms_proj(
                xs, row(mix_norm[i]),
                [(w_bf, gz_lo // MM_TN), (w_conv, 2 * CONV_CH // MM_TN)],
                jnp.ones((1, AB_MAIN), F32), BF16, w_extra=w_gz)
            gate_w = jnp.pad(gla_gate_w[e], ((0, LANES - GLA_GATE_RANK), (0, 0))).astype(BF16)
            a_out = _gla(z_all, gz, gate_w, row(gla_gate_b[e]), row(gla_norm_g[e]))
            taps = jnp.broadcast_to(conv_dw[e].astype(F32)[:, None, :],
                                    (CONV_WIDTH, SUBLANES, CONV_CH))
            b_out = _conv(z_all, taps, row(conv_dw_b[e]), row(conv_ln_g[e]), row(conv_ln_b[e]))
            xs = _out_proj([a_out, b_out], ab_w_out[e], xs)
        else:
            colscale = jnp.concatenate([jnp.full((1, D_MODEL), ATT_HEAD_DIM ** -0.5 * LOG2E, F32),
                                        jnp.ones((1, 2 * D_MODEL), F32)], axis=1)
            qkv = _rms_proj(xs, row(mix_norm[i]), [(att_w_qkv[e], 3 * D_MODEL // MM_TN)],
                            colscale, BF16)
            m0, ta, tb = _attention_bias_tables(att_rel_bias[e])
            o = _attention(qkv, m0, ta, tb)
            xs = _out_proj([o], att_w_o[e], xs)
        xs = ffn(xs, i, 1)
        xs = _pl_embed(xs, ps, row(pl_norm[i]), pl_w_gate, pl_w_proj, row(final_norm),
                       i, i == DEPTH - 1)
    return xs.reshape(BATCH, SEQ, D_MODEL)
```
